```python
import jax, jax.numpy as jnp
from jax import lax
import numpy as np

D_MODEL = 1024
BATCH = 2
SEQ = 8192
DEPTH = 1

N_HEADS = 16
HEAD_DIM = 64
N_KV_GROUPS = 4
HEADS_PER_GROUP = N_HEADS // N_KV_GROUPS
ROPE_DIM = HEAD_DIM // 4
ROPE_THETA = 500000.0
CMP_BLOCK = 32
CMP_STRIDE = 16
CMP_HIDDEN = 256
SLC_BLOCK = 64
SLC_TOPK = 16
WINDOW = 512
Q_BLOCK = 128
ATTN_Q_WIDTH = N_HEADS * HEAD_DIM
KV_WIDTH = N_KV_GROUPS * HEAD_DIM
POOL_WIDTH = D_MODEL // 2
POOL_WINDOWS = (2, 4, 8, 16)
POOL_GROUP = POOL_WIDTH // len(POOL_WINDOWS)
D_FF = ((8 * D_MODEL // 3 + 255) // 256) * 256
RMS_EPS = 1e-6
NEG_INF = -1e30

IN_SIZES = (POOL_WIDTH, ATTN_Q_WIDTH, KV_WIDTH, KV_WIDTH, KV_WIDTH, KV_WIDTH, KV_WIDTH, KV_WIDTH,
            N_HEADS * 3, D_MODEL, D_MODEL)
IN_WIDTH = sum(IN_SIZES)
SPLIT_POINTS = tuple(int(v) for v in np.cumsum(IN_SIZES)[:-1])

kernel_name = "hybrid_pool_nsa_gated_block"


def rms_norm(x, w):
    xf = x.astype(jnp.float32)
    y = xf * lax.rsqrt(jnp.mean(xf * xf, axis=-1, keepdims=True) + RMS_EPS)
    return (y * w.astype(jnp.float32)).astype(x.dtype)


def partial_rope(x, positions):
    half = ROPE_DIM // 2
    inv_freq = 1.0 / (ROPE_THETA ** (jnp.arange(0, ROPE_DIM, 2, dtype=jnp.float32) / ROPE_DIM))
    ang = positions.astype(jnp.float32)[:, None] * inv_freq[None, :]
    cos = jnp.cos(ang)[None, :, None, :]
    sin = jnp.sin(ang)[None, :, None, :]
    xf = x.astype(jnp.float32)
    x1, x2 = xf[..., :half], xf[..., half:ROPE_DIM]
    out = jnp.concatenate([x1 * cos - x2 * sin, x2 * cos + x1 * sin, xf[..., ROPE_DIM:]], axis=-1)
    return out.astype(x.dtype)


def pool_mixer(u, pool_w, pool_scale):
    B, S, _ = u.shape
    uf = u.astype(jnp.float32)
    csum = jnp.concatenate([jnp.zeros((B, 1, POOL_WIDTH), jnp.float32), jnp.cumsum(uf, axis=1)], axis=1)
    t = jnp.arange(S)
    outs = []
    for g, w in enumerate(POOL_WINDOWS):
        sl = slice(g * POOL_GROUP, (g + 1) * POOL_GROUP)
        lo = jnp.maximum(t + 1 - w, 0)
        cnt = (t + 1 - lo).astype(jnp.float32)
        mean = (csum[:, 1:, sl] - csum[:, lo, sl]) / cnt[None, :, None]
        outs.append(mean - uf[:, :, sl])
    pooled = jnp.stack(outs, axis=2)
    mixed = jnp.einsum('bsgc,gcd->bsgd', pooled, pool_w.astype(jnp.float32))
    return (mixed.reshape(B, S, POOL_WIDTH) * pool_scale.astype(jnp.float32)).astype(u.dtype)


def compress(kv, pe, w1, b1, w2):
    B, S, G, dk = kv.shape
    n_cmp = (S - CMP_BLOCK) // CMP_STRIDE + 1
    idx = jnp.arange(n_cmp)[:, None] * CMP_STRIDE + jnp.arange(CMP_BLOCK)[None, :]
    blocks = kv[:, idx] + pe[None, None, :, None, :]
    flat = blocks.transpose(0, 1, 3, 2, 4).reshape(B, n_cmp, G, CMP_BLOCK * dk)
    hid = jax.nn.gelu(flat @ w1 + b1)
    return hid @ w2


def cmp_to_slc_map(n_cmp, n_slc):
    start = np.arange(n_cmp) * CMP_STRIDE
    m = np.zeros((n_cmp, n_slc), np.float32)
    m[np.arange(n_cmp), start // SLC_BLOCK] = 1.0
    m[np.arange(n_cmp), (start + CMP_BLOCK - 1) // SLC_BLOCK] = 1.0
    return jnp.asarray(m)


def nsa_attention(q_plain, q_rope, k_cmp, v_cmp, k_slc, v_slc, k_win, v_win, gates):
    B, S, H, dk = q_plain.shape
    G, Hg = N_KV_GROUPS, HEADS_PER_GROUP
    scale = dk ** -0.5
    n_cmp = k_cmp.shape[1]
    n_slc = S // SLC_BLOCK
    n_top = min(SLC_TOPK, n_slc)
    n_qblk = S // Q_BLOCK
    qp_all = q_plain.reshape(B, S, G, Hg, dk)
    qr_all = q_rope.reshape(B, S, G, Hg, dk)
    g_all = jax.nn.sigmoid(gates.astype(jnp.float32)).reshape(B, S, G, Hg, 3)
    slc_map = cmp_to_slc_map(n_cmp, n_slc)
    cmp_end = jnp.arange(n_cmp) * CMP_STRIDE + (CMP_BLOCK - 1)

    def to_blocks(a):
        return a.reshape(B, n_slc, SLC_BLOCK, G, dk).transpose(0, 3, 1, 2, 4)

    ks_blk, vs_blk = to_blocks(k_slc), to_blocks(v_slc)
    pad = ((0, 0), (WINDOW, 0), (0, 0), (0, 0))
    kw_pad, vw_pad = jnp.pad(k_win, pad), jnp.pad(v_win, pad)
    bi = jnp.arange(B)[:, None, None, None]
    gi = jnp.arange(G)[None, :, None, None]
    blk_ids = jnp.arange(n_slc)
    in_blk = jnp.arange(SLC_BLOCK)
    win_off = jnp.arange(WINDOW + Q_BLOCK)

    def masked_softmax(s, mask):
        s = jnp.where(mask, s.astype(jnp.float32) * scale, NEG_INF)
        return jnp.where(mask, jax.nn.softmax(s, axis=-1), 0.0)

    def query_block(qb):
        t0 = qb * Q_BLOCK
        tq = t0 + jnp.arange(Q_BLOCK)
        qp = lax.dynamic_slice_in_dim(qp_all, t0, Q_BLOCK, axis=1)
        qr = lax.dynamic_slice_in_dim(qr_all, t0, Q_BLOCK, axis=1)
        g = lax.dynamic_slice_in_dim(g_all, t0, Q_BLOCK, axis=1)

        cmask = cmp_end[None, :] <= tq[:, None]
        p_cmp = masked_softmax(jnp.einsum('bqghd,bngd->bghqn', qp, k_cmp), cmask)
        o_cmp = jnp.einsum('bghqn,bngd->bqghd', p_cmp, v_cmp.astype(jnp.float32))

        jt = tq // SLC_BLOCK
        imp = jnp.einsum('bghqn,nj->bgqj', p_cmp, slc_map)
        causal = blk_ids[None, :] <= jt[:, None]
        forced = (blk_ids[None, :] == 0) | (blk_ids[None, :] == jt[:, None]) | (blk_ids[None, :] == jt[:, None] - 1)
        imp = jnp.where(forced, jnp.inf, jnp.where(causal, imp, -jnp.inf))
        _, sel = lax.top_k(imp, n_top)
        k_sel = ks_blk[bi, gi, sel]
        v_sel = vs_blk[bi, gi, sel]
        kpos = sel[..., None] * SLC_BLOCK + in_blk
        smask = (kpos <= tq[None, None, :, None, None]).reshape(B, G, 1, Q_BLOCK, n_top * SLC_BLOCK)
        s = jnp.einsum('bqghd,bgqkld->bghqkl', qr, k_sel).reshape(B, G, Hg, Q_BLOCK, n_top * SLC_BLOCK)
        p = masked_softmax(s, smask).reshape(B, G, Hg, Q_BLOCK, n_top, SLC_BLOCK)
        o_slc = jnp.einsum('bghqkl,bgqkld->bqghd', p, v_sel.astype(jnp.float32))

        kwb = lax.dynamic_slice_in_dim(kw_pad, t0, WINDOW + Q_BLOCK, axis=1)
        vwb = lax.dynamic_slice_in_dim(vw_pad, t0, WINDOW + Q_BLOCK, axis=1)
        kpos_w = t0 - WINDOW + win_off
        diff = tq[:, None] - kpos_w[None, :]
        wmask = (diff >= 0) & (diff < WINDOW) & (kpos_w[None, :] >= 0)
        p = masked_softmax(jnp.einsum('bqghd,bkgd->bghqk', qr, kwb), wmask)
        o_win = jnp.einsum('bghqk,bkgd->bqghd', p, vwb.astype(jnp.float32))

        o = g[..., 0:1] * o_cmp + g[..., 1:2] * o_slc + g[..., 2:3] * o_win
        return o.reshape(B, Q_BLOCK, H * dk)

    out = lax.map(query_block, jnp.arange(n_qblk))
    return out.transpose(1, 0, 2, 3).reshape(B, S, H * dk).astype(q_plain.dtype)


def setup_inputs(seed: int = 0) -> dict:
    key = jax.random.key(seed)
    k = jax.random.split(key, 24)
    L = DEPTH

    def nrm(kk, shape, scale):
        return jax.random.normal(kk, shape, jnp.float32) * scale

    def gain(kk, shape):
        return 1.0 + 0.05 * jax.random.normal(kk, shape, jnp.float32)

    return {
        "x": nrm(k[0], (BATCH, SEQ, D_MODEL), 1.0),
        "norm1_w": gain(k[1], (L, D_MODEL)),
        "w_in": nrm(k[2], (L, D_MODEL, IN_WIDTH), D_MODEL ** -0.5),
        "pool_w": nrm(k[3], (L, len(POOL_WINDOWS), POOL_GROUP, POOL_GROUP), POOL_GROUP ** -0.5),
        "pool_scale": gain(k[4], (L, POOL_WIDTH)),
        "cmp_pe_k": nrm(k[5], (L, CMP_BLOCK, HEAD_DIM), 0.5),
        "cmp_w1_k": nrm(k[6], (L, CMP_BLOCK * HEAD_DIM, CMP_HIDDEN), (CMP_BLOCK * HEAD_DIM) ** -0.5),
        "cmp_b1_k": nrm(k[7], (L, CMP_HIDDEN), 0.02),
        "cmp_w2_k": nrm(k[8], (L, CMP_HIDDEN, HEAD_DIM), CMP_HIDDEN ** -0.5),
        "cmp_pe_v": nrm(k[9], (L, CMP_BLOCK, HEAD_DIM), 0.5),
        "cmp_w1_v": nrm(k[10], (L, CMP_BLOCK * HEAD_DIM, CMP_HIDDEN), (CMP_BLOCK * HEAD_DIM) ** -0.5),
        "cmp_b1_v": nrm(k[11], (L, CMP_HIDDEN), 0.02),
        "cmp_w2_v": nrm(k[12], (L, CMP_HIDDEN, HEAD_DIM), CMP_HIDDEN ** -0.5),
        "w_proj_pool": nrm(k[13], (L, POOL_WIDTH, D_MODEL), POOL_WIDTH ** -0.5),
        "w_proj_attn": nrm(k[14], (L, ATTN_Q_WIDTH, D_MODEL), ATTN_Q_WIDTH ** -0.5),
        "w_out": nrm(k[15], (L, D_MODEL, D_MODEL), D_MODEL ** -0.5),
        "norm2_w": gain(k[16], (L, D_MODEL)),
        "w_ffn_gate": nrm(k[17], (L, D_MODEL, D_FF), D_MODEL ** -0.5),
        "w_ffn_up": nrm(k[18], (L, D_MODEL, D_FF), D_MODEL ** -0.5),
        "w_ffn_down": nrm(k[19], (L, D_FF, D_MODEL), D_FF ** -0.5),
        "norm_f_w": gain(k[20], (D_MODEL,)),
    }


def reference(x, norm1_w, w_in, pool_w, pool_scale, cmp_pe_k, cmp_w1_k, cmp_b1_k, cmp_w2_k,
              cmp_pe_v, cmp_w1_v, cmp_b1_v, cmp_w2_v, w_proj_pool, w_proj_attn, w_out,
              norm2_w, w_ffn_gate, w_ffn_up, w_ffn_down, norm_f_w):
    B, S, _ = x.shape
    G = N_KV_GROUPS
    positions = jnp.arange(S)

    def heads(a, n):
        return a.reshape(B, S, n, HEAD_DIM)

    for l in range(DEPTH):
        h = rms_norm(x, norm1_w[l])
        proj = h @ w_in[l]
        (u_pool, q, kc_raw, vc_raw, ks_raw, vs_raw, kw_raw, vw_raw,
         g_nsa, g_pool, g_attn) = jnp.split(proj, SPLIT_POINTS, axis=-1)
        q = heads(q, N_HEADS)
        q_rope = partial_rope(q, positions)
        k_cmp = compress(heads(kc_raw, G), cmp_pe_k[l], cmp_w1_k[l], cmp_b1_k[l], cmp_w2_k[l])
        v_cmp = compress(heads(vc_raw, G), cmp_pe_v[l], cmp_w1_v[l], cmp_b1_v[l], cmp_w2_v[l])
        k_slc = partial_rope(heads(ks_raw, G), positions)
        k_win = partial_rope(heads(kw_raw, G), positions)
        attn = nsa_attention(q, q_rope, k_cmp, v_cmp, k_slc, heads(vs_raw, G), k_win, heads(vw_raw, G),
                             g_nsa.reshape(B, S, N_HEADS, 3))
        pool = pool_mixer(u_pool, pool_w[l], pool_scale[l])
        merged = (jax.nn.sigmoid(g_pool) * (pool @ w_proj_pool[l])
                  + jax.nn.sigmoid(g_attn) * (attn @ w_proj_attn[l]))
        x = x + merged @ w_out[l]
        h = rms_norm(x, norm2_w[l])
        x = x + (jax.nn.silu(h @ w_ffn_gate[l]) * (h @ w_ffn_up[l])) @ w_ffn_down[l]
    return rms_norm(x, norm_f_w)
```

```python
import functools

import numpy as np
import jax
import jax.numpy as jnp
from jax import lax
from jax.experimental import pallas as pl
from jax.experimental.pallas import tpu as pltpu

D_MODEL = 1024
N_HEADS = 16
HEAD_DIM = 64
N_KV_GROUPS = 4
HEADS_PER_GROUP = N_HEADS // N_KV_GROUPS
ROPE_DIM = HEAD_DIM // 4
ROPE_HALF = ROPE_DIM // 2
ROPE_THETA = 500000.0
CMP_BLOCK = 32
CMP_STRIDE = 16
CMP_PER_SLC = 4
SLC_BLOCK = 64
SLC_TOPK = 16
WINDOW = 512
ATTN_Q_WIDTH = N_HEADS * HEAD_DIM
KV_WIDTH = N_KV_GROUPS * HEAD_DIM
POOL_WIDTH = D_MODEL // 2
POOL_WINDOWS = (2, 4, 8, 16)
POOL_GROUP = POOL_WIDTH // len(POOL_WINDOWS)
POOL_HALO = 16
RMS_EPS = 1e-6

LANE = 128
MXU_DTYPE = jnp.bfloat16
ACT_DTYPE = jnp.bfloat16
VMEM_LIMIT_BYTES = 56 * 1024 * 1024

GATE_NSA = N_HEADS * 3
COL_POOL = 0
COL_Q = COL_POOL + POOL_WIDTH
COL_KV = COL_Q + ATTN_Q_WIDTH
COL_GPOOL = COL_KV + 6 * KV_WIDTH
COL_GATTN = COL_GPOOL + D_MODEL
COL_GNSA = COL_GATTN + D_MODEL
PROJ_WIDTH = COL_GNSA + LANE
GATE_ROWS = 16

ROW_TILE = 512
Q_TILE = 256
KPREP_TILE = 1024
FF_CHUNK = 512
SCORE_SCALE = HEAD_DIM ** -0.5
MASK_BIAS = float(2 ** 30)
NEG_BIG = -1e30
IMP_BIG = 1e30


def _log2(n):
    k = int(n).bit_length() - 1
    assert 1 << k == n, n
    return k


def _cparams(n_grid):
    return pltpu.CompilerParams(dimension_semantics=("arbitrary",) * n_grid,
                                vmem_limit_bytes=VMEM_LIMIT_BYTES)


def _const_spec(shape):
    nd = len(shape)
    return pl.BlockSpec(shape, lambda *_: (0,) * nd, pipeline_mode=pl.Buffered(1))


def _sigmoid(x):
    return 1.0 / (1.0 + jnp.exp(-x))


def _rms(x, w):
    var = jnp.mean(x * x, axis=-1, keepdims=True)
    return x * lax.rsqrt(var + RMS_EPS) * w


def _inproj_kernel(x_ref, nw_ref, w_ref, o_ref):
    h = _rms(x_ref[...], nw_ref[...]).astype(MXU_DTYPE)
    n = o_ref.shape[1]
    for c0 in range(0, n, 256):
        cw = min(256, n - c0)
        o_ref[:, c0:c0 + cw] = jnp.dot(h, w_ref[:, c0:c0 + cw],
                                       preferred_element_type=jnp.float32).astype(o_ref.dtype)


def _inproj(x2d, norm_w, w_r):
    m, d = x2d.shape
    n = w_r.shape[1]
    return pl.pallas_call(
        _inproj_kernel,
        grid=(m // ROW_TILE,),
        in_specs=[pl.BlockSpec((ROW_TILE, d), lambda i: (i, 0)),
                  _const_spec((1, d)),
                  _const_spec((d, n))],
        out_specs=pl.BlockSpec((ROW_TILE, n), lambda i: (i, 0)),
        out_shape=jax.ShapeDtypeStruct((m, n), ACT_DTYPE),
        compiler_params=_cparams(1),
        name="inproj",
    )(x2d, norm_w, w_r)


def _gelu_tanh(x):
    return 0.5 * x * (1.0 + jnp.tanh(np.sqrt(2.0 / np.pi) * (x + 0.044715 * (x * x * x))))


def _compress_one(r, pe_ref, w1_ref, b1_ref, w2_ref):
    half = w1_ref.shape[0] // 2
    n_rows = r.shape[0]
    a = jnp.dot(r, w1_ref[0:half, :], preferred_element_type=jnp.float32)
    b = jnp.dot(r, w1_ref[half:, :], preferred_element_type=jnp.float32)
    bias = jnp.dot(pe_ref[...], w1_ref[...], preferred_element_type=jnp.float32)[0:1, :] + b1_ref[...]
    hid = a + pltpu.roll(b, n_rows - 1, 0) + bias
    return jnp.dot(_gelu_tanh(hid).astype(MXU_DTYPE), w2_ref[...], preferred_element_type=jnp.float32)


def _compress_kernel(rk_ref, rv_ref, pek_ref, w1k_ref, b1k_ref, w2k_ref,
                     pev_ref, w1v_ref, b1v_ref, w2v_ref, ok_ref, ov_ref):
    ok_ref[0, 0] = _compress_one(rk_ref[0, 0], pek_ref, w1k_ref, b1k_ref, w2k_ref)
    ov_ref[0, 0] = _compress_one(rv_ref[0, 0], pev_ref, w1v_ref, b1v_ref, w2v_ref)


def _compress(rk, rv, pek, w1k, b1k, w2k, pev, w1v, b1v, w2v):
    b, g, n_rows, width = rk.shape
    dk = w2k.shape[1]
    r_spec = pl.BlockSpec((1, 1, n_rows, width), lambda i, j: (i, j, 0, 0))
    o_spec = pl.BlockSpec((1, 1, n_rows, dk), lambda i, j: (i, j, 0, 0))
    w_specs = [_const_spec(a.shape) for a in (pek, w1k, b1k, w2k)]
    o_shape = jax.ShapeDtypeStruct((b, g, n_rows, dk), jnp.float32)
    return pl.pallas_call(
        _compress_kernel,
        grid=(b, g),
        in_specs=[r_spec, r_spec] + w_specs + w_specs,
        out_specs=[o_spec, o_spec],
        out_shape=[o_shape, o_shape],
        compiler_params=_cparams(2),
        name="compress",
    )(rk, rv, pek, w1k, b1k, w2k, pev, w1v, b1v, w2v)


def _kprep_kernel(ks_ref, vs_ref, kw_ref, vw_ref, cos_ref, sin_ref, rot_ref,
                  kaug_ref, vst_ref, kwin_ref, vwt_ref):
    ts = ks_ref.shape[1]
    t0 = pl.program_id(1) * ts
    cos = cos_ref[...]
    sin = sin_ref[...]
    rot = rot_ref[...]

    def rope(k_ref):
        k = k_ref[0]
        krot = jnp.dot(k.astype(MXU_DTYPE), rot, preferred_element_type=jnp.float32)
        return (k.astype(jnp.float32) * cos + krot * sin) * SCORE_SCALE

    n_blk = kaug_ref.shape[3] - 2 * HEAD_DIM
    blk = lax.broadcasted_iota(jnp.int32, (ts, n_blk), 1)
    key_blk = (t0 + lax.broadcasted_iota(jnp.int32, (ts, n_blk), 0)) >> _log2(SLC_BLOCK)
    onehot = jnp.where(blk == key_blk, 1.0, 0.0).astype(kaug_ref.dtype)
    ks = rope(ks_ref).astype(kaug_ref.dtype)
    kw = rope(kw_ref).astype(kwin_ref.dtype)
    vst = vs_ref[0].astype(jnp.float32).T.astype(vst_ref.dtype)
    vwt = vw_ref[0].astype(jnp.float32).T.astype(vwt_ref.dtype)
    zeros = jnp.zeros((ts, HEAD_DIM), kaug_ref.dtype)
    for g in range(N_KV_GROUPS):
        sl = slice(g * HEAD_DIM, (g + 1) * HEAD_DIM)
        kaug_ref[0, g, :, 0:n_blk] = onehot
        kaug_ref[0, g, :, n_blk:n_blk + HEAD_DIM] = ks[:, sl]
        kaug_ref[0, g, :, n_blk + HEAD_DIM:] = zeros
        kwin_ref[0, g] = kw[:, sl]
        vst_ref[0, g] = vst[sl, :]
        vwt_ref[0, g] = vwt[sl, :]


def _kprep(proj3, cos_k, sin_k, rot):
    b, s, _ = proj3.shape
    n_blk = s // SLC_BLOCK
    ts = min(KPREP_TILE, s)
    kvb = COL_KV // KV_WIDTH

    def col(j):
        return pl.BlockSpec((1, ts, KV_WIDTH), lambda i, t: (i, t, kvb + j))

    tab = pl.BlockSpec((ts, KV_WIDTH), lambda i, t: (t, 0))
    aug_w = n_blk + 2 * HEAD_DIM
    return pl.pallas_call(
        _kprep_kernel,
        grid=(b, s // ts),
        in_specs=[col(2), col(3), col(4), col(5), tab, tab, _const_spec(rot.shape)],
        out_specs=[pl.BlockSpec((1, N_KV_GROUPS, ts, aug_w), lambda i, t: (i, 0, t, 0)),
                   pl.BlockSpec((1, N_KV_GROUPS, HEAD_DIM, ts), lambda i, t: (i, 0, 0, t)),
                   pl.BlockSpec((1, N_KV_GROUPS, ts, HEAD_DIM), lambda i, t: (i, 0, t, 0)),
                   pl.BlockSpec((1, N_KV_GROUPS, HEAD_DIM, ts), lambda i, t: (i, 0, 0, t))],
        out_shape=[jax.ShapeDtypeStruct((b, N_KV_GROUPS, s, aug_w), MXU_DTYPE),
                   jax.ShapeDtypeStruct((b, N_KV_GROUPS, HEAD_DIM, s), MXU_DTYPE),
                   jax.ShapeDtypeStruct((b, N_KV_GROUPS, s, HEAD_DIM), MXU_DTYPE),
                   jax.ShapeDtypeStruct((b, N_KV_GROUPS, HEAD_DIM, s), MXU_DTYPE)],
        compiler_params=_cparams(2),
        name="kprep",
    )(proj3, proj3, proj3, proj3, cos_k, sin_k, rot)


def _heads_to_lanes(xt):
    return jnp.concatenate([xt[h * HEAD_DIM:(h + 1) * HEAD_DIM, :] for h in range(HEADS_PER_GROUP)],
                           axis=1)


def _cmpsel_kernel(q_ref, kc_ref, vct_ref, ocmp_ref, bias_ref):
    tq = q_ref.shape[1]
    n_cmp_pad = kc_ref.shape[2]
    n_blk = bias_ref.shape[2]
    m_rows = HEADS_PER_GROUP * tq
    t0 = pl.program_id(2) * tq

    qt = _heads_to_lanes(q_ref[0].astype(jnp.float32).T).astype(MXU_DTYPE)
    s = jnp.dot(kc_ref[0, 0], qt, preferred_element_type=jnp.float32)
    row = lax.broadcasted_iota(jnp.int32, (n_cmp_pad, m_rows), 0)
    n_idx = (row & (n_blk - 1)) * CMP_PER_SLC + (row >> _log2(n_blk))
    t = t0 + (lax.broadcasted_iota(jnp.int32, (n_cmp_pad, m_rows), 1) & (tq - 1))
    vis = n_idx * CMP_STRIDE + (CMP_BLOCK - 1) <= t
    s = jnp.where(vis, s, NEG_BIG)
    mx = jnp.max(s, axis=0, keepdims=True)
    p = jnp.where(vis, jnp.exp(s - mx), 0.0)
    den = jnp.sum(p, axis=0, keepdims=True)
    p = p * jnp.where(den > 0.0, 1.0 / den, 0.0)
    ocmp_ref[0, 0, 0] = jnp.dot(vct_ref[0, 0], p.astype(MXU_DTYPE), preferred_element_type=jnp.float32)

    ph = p[:, 0:tq]
    for h in range(1, HEADS_PER_GROUP):
        ph = ph + p[:, h * tq:(h + 1) * tq]
    last = ph[3 * n_blk:4 * n_blk, :]
    jrow = lax.broadcasted_iota(jnp.int32, (n_blk, tq), 0)
    prev = jnp.where(jrow == 0, 0.0, pltpu.roll(last, 1, 0))
    imp = ph[0:n_blk, :] + ph[n_blk:2 * n_blk, :] + ph[2 * n_blk:3 * n_blk, :] + last + prev

    jt = (t0 + lax.broadcasted_iota(jnp.int32, (n_blk, tq), 1)) >> _log2(SLC_BLOCK)
    forced = (jrow == 0) | (jrow == jt) | (jrow == jt - 1)
    val = jnp.where(forced, IMP_BIG, jnp.where(jrow <= jt, imp, -IMP_BIG))
    jf = jrow.astype(jnp.float32)
    sel = jnp.zeros((n_blk, tq), jnp.float32)
    for _ in range(min(SLC_TOPK, n_blk)):
        top = jnp.max(val, axis=0, keepdims=True)
        first = jnp.min(jnp.where(val == top, jf, float(n_blk)), axis=0, keepdims=True)
        hit = jf == first
        sel = jnp.where(hit, 1.0, sel)
        val = jnp.where(hit, -3e38, val)
    bias_ref[0, 0] = jnp.where(sel > 0.0, 0.0, -MASK_BIAS).astype(bias_ref.dtype)


def _cmpsel(proj3, kc_r, vc_rt):
    b, s, _ = proj3.shape
    tq = min(Q_TILE, s)
    nq = s // tq
    n_cmp_pad = kc_r.shape[2]
    n_blk = s // SLC_BLOCK
    qb = COL_Q // KV_WIDTH
    m_rows = HEADS_PER_GROUP * tq
    return pl.pallas_call(
        _cmpsel_kernel,
        grid=(b, N_KV_GROUPS, nq),
        in_specs=[pl.BlockSpec((1, tq, KV_WIDTH), lambda i, g, q: (i, q, qb + g)),
                  pl.BlockSpec((1, 1, n_cmp_pad, HEAD_DIM), lambda i, g, q: (i, g, 0, 0)),
                  pl.BlockSpec((1, 1, HEAD_DIM, n_cmp_pad), lambda i, g, q: (i, g, 0, 0))],
        out_specs=[pl.BlockSpec((1, 1, 1, HEAD_DIM, m_rows), lambda i, g, q: (i, g, q, 0, 0)),
                   pl.BlockSpec((1, 1, n_blk, tq), lambda i, g, q: (i, g, 0, q))],
        out_shape=[jax.ShapeDtypeStruct((b, N_KV_GROUPS, nq, HEAD_DIM, m_rows), jnp.float32),
                   jax.ShapeDtypeStruct((b, N_KV_GROUPS, n_blk, s), MXU_DTYPE)],
        compiler_params=_cparams(3),
        name="cmpsel",
    )(proj3, kc_r, vc_rt)


def _flash_kernel(q_ref, cos_ref, sin_ref, bias_ref, kaug_ref, vst_ref, kwin_ref, vwt_ref,
                  gate_ref, ocmp_ref, o_ref, m_ref, l_ref, acc_ref):
    tq = q_ref.shape[1]
    m_rows = HEADS_PER_GROUP * tq
    grp = pl.program_id(1)
    qi = pl.program_id(2)
    t0 = qi * tq

    qt = q_ref[0].astype(jnp.float32).T
    cos = cos_ref[...]
    sin = sin_ref[...]
    parts = []
    for h in range(HEADS_PER_GROUP):
        qh = qt[h * HEAD_DIM:(h + 1) * HEAD_DIM, :]
        x1 = qh[0:ROPE_HALF, :]
        x2 = qh[ROPE_HALF:ROPE_DIM, :]
        parts.append(jnp.concatenate([x1 * cos - x2 * sin, x2 * cos + x1 * sin, qh[ROPE_DIM:, :]], axis=0))
    qr = jnp.concatenate(parts, axis=1).astype(MXU_DTYPE)
    bias4 = jnp.concatenate([bias_ref[0, 0]] * HEADS_PER_GROUP, axis=1)
    qaug = jnp.concatenate([bias4, qr, jnp.zeros((HEAD_DIM, m_rows), MXU_DTYPE)], axis=0)

    tcol = t0 + (lax.broadcasted_iota(jnp.int32, (tq, m_rows), 1) & (tq - 1))
    koff = lax.broadcasted_iota(jnp.int32, (tq, m_rows), 0)

    def start(s, vt):
        mx = jnp.max(s, axis=0, keepdims=True)
        p = jnp.exp(s - mx)
        m_ref[...] = mx
        l_ref[...] = jnp.sum(p, axis=0, keepdims=True)
        acc_ref[...] = jnp.dot(vt, p.astype(MXU_DTYPE), preferred_element_type=jnp.float32)

    def update(s, vt):
        m_old = m_ref[...]
        mx = jnp.maximum(m_old, jnp.max(s, axis=0, keepdims=True))
        alpha = jnp.exp(m_old - mx)
        p = jnp.exp(s - mx)
        m_ref[...] = mx
        l_ref[...] = alpha * l_ref[...] + jnp.sum(p, axis=0, keepdims=True)
        acc_ref[...] = alpha * acc_ref[...] + jnp.dot(vt, p.astype(MXU_DTYPE),
                                                      preferred_element_type=jnp.float32)

    kd = pl.multiple_of(t0, tq)
    s = jnp.dot(kaug_ref[0, 0, pl.ds(kd, tq), :], qaug, preferred_element_type=jnp.float32)
    s = jnp.where(t0 + koff <= tcol, s, NEG_BIG)
    start(s, vst_ref[0, 0, :, pl.ds(kd, tq)])

    def slc_body(kt, carry):
        k0 = pl.multiple_of(kt * tq, tq)
        sk = jnp.dot(kaug_ref[0, 0, pl.ds(k0, tq), :], qaug, preferred_element_type=jnp.float32)
        update(sk, vst_ref[0, 0, :, pl.ds(k0, tq)])
        return carry

    lax.fori_loop(0, qi, slc_body, 0)
    o_slc = acc_ref[...] / l_ref[...]

    for back in range(WINDOW // tq + 1):
        kbase = t0 - back * tq
        k0 = pl.multiple_of(jnp.maximum(kbase, 0), tq)
        sw = jnp.dot(kwin_ref[0, 0, pl.ds(k0, tq), :], qr, preferred_element_type=jnp.float32)
        kpos = kbase + koff
        diff = tcol - kpos
        ok = (diff >= 0) & (diff < WINDOW) & (kpos >= 0)
        sw = jnp.where(ok, sw, NEG_BIG)
        if back == 0:
            start(sw, vwt_ref[0, 0, :, pl.ds(k0, tq)])
        else:
            update(sw, vwt_ref[0, 0, :, pl.ds(k0, tq)])
    o_win = acc_ref[...] / l_ref[...]

    r_id = lax.broadcasted_iota(jnp.int32, (GATE_ROWS, LANE), 0)
    c_id = lax.broadcasted_iota(jnp.int32, (GATE_ROWS, LANE), 1)
    pick = jnp.where((c_id == grp * (HEADS_PER_GROUP * 3) + r_id) & (r_id < HEADS_PER_GROUP * 3),
                     1.0, 0.0).astype(gate_ref.dtype)
    graw = lax.dot_general(pick, gate_ref[0], (((1,), (1,)), ((), ())),
                           preferred_element_type=jnp.float32)
    gates = _sigmoid(graw)
    o_cmp = ocmp_ref[0, 0, 0]
    mixed = []
    for h in range(HEADS_PER_GROUP):
        sl = slice(h * tq, (h + 1) * tq)
        mixed.append(gates[3 * h:3 * h + 1, :] * o_cmp[:, sl]
                     + gates[3 * h + 1:3 * h + 2, :] * o_slc[:, sl]
                     + gates[3 * h + 2:3 * h + 3, :] * o_win[:, sl])
    o_ref[0] = jnp.concatenate(mixed, axis=0).T.astype(o_ref.dtype)


def _flash(proj3, cos_q, sin_q, bias, kaug, vst, kwin, vwt, ocmp):
    b, s, _ = proj3.shape
    tq = min(Q_TILE, s)
    nq = s // tq
    n_blk = s // SLC_BLOCK
    aug_w = kaug.shape[3]
    qb = COL_Q // KV_WIDTH
    m_rows = HEADS_PER_GROUP * tq
    return pl.pallas_call(
        _flash_kernel,
        grid=(b, N_KV_GROUPS, nq),
        in_specs=[pl.BlockSpec((1, tq, KV_WIDTH), lambda i, g, q: (i, q, qb + g)),
                  pl.BlockSpec((ROPE_HALF, tq), lambda i, g, q: (0, q)),
                  pl.BlockSpec((ROPE_HALF, tq), lambda i, g, q: (0, q)),
                  pl.BlockSpec((1, 1, n_blk, tq), lambda i, g, q: (i, g, 0, q)),
                  pl.BlockSpec((1, 1, s, aug_w), lambda i, g, q: (i, g, 0, 0)),
                  pl.BlockSpec((1, 1, HEAD_DIM, s), lambda i, g, q: (i, g, 0, 0)),
                  pl.BlockSpec((1, 1, s, HEAD_DIM), lambda i, g, q: (i, g, 0, 0)),
                  pl.BlockSpec((1, 1, HEAD_DIM, s), lambda i, g, q: (i, g, 0, 0)),
                  pl.BlockSpec((1, tq, LANE), lambda i, g, q: (i, q, COL_GNSA // LANE)),
                  pl.BlockSpec((1, 1, 1, HEAD_DIM, m_rows), lambda i, g, q: (i, g, q, 0, 0))],
        out_specs=pl.BlockSpec((1, tq, KV_WIDTH), lambda i, g, q: (i, q, g)),
        out_shape=jax.ShapeDtypeStruct((b, s, ATTN_Q_WIDTH), ACT_DTYPE),
        scratch_shapes=[pltpu.VMEM((1, m_rows), jnp.float32),
                        pltpu.VMEM((1, m_rows), jnp.float32),
                        pltpu.VMEM((HEAD_DIM, m_rows), jnp.float32)],
        compiler_params=_cparams(3),
        name="flash",
    )(proj3, cos_q, sin_q, bias, kaug, vst, kwin, vwt, proj3, ocmp)


def _merge_kernel(u_ref, halo_ref, gp_ref, ga_ref, attn_ref, x_ref, pw_ref, ps_ref,
                  wpp_ref, wpa_ref, wo_ref, o_ref, *, seq_len):
    tm = u_ref.shape[0]
    t0 = (pl.program_id(0) * tm) & (seq_len - 1)
    u = u_ref[...].astype(jnp.float32)
    halo = jnp.where(t0 > 0, halo_ref[...].astype(jnp.float32), 0.0)
    ext = jnp.concatenate([halo, u], axis=0)
    t = t0 + lax.broadcasted_iota(jnp.int32, (tm, POOL_GROUP), 0)
    pooled = []
    for gi, w in enumerate(POOL_WINDOWS):
        sl = slice(gi * POOL_GROUP, (gi + 1) * POOL_GROUP)
        acc = ext[:, sl]
        span = 1
        while span < w:
            acc = acc + pltpu.roll(acc, span, 0)
            span *= 2
        cnt = jnp.minimum(t + 1, w).astype(jnp.float32)
        mean = acc[POOL_HALO:, :] / cnt
        mixed = jnp.dot((mean - u[:, sl]).astype(MXU_DTYPE), pw_ref[gi],
                        preferred_element_type=jnp.float32)
        pooled.append(mixed * ps_ref[:, sl])
    pool = jnp.concatenate(pooled, axis=1).astype(MXU_DTYPE)
    pp = jnp.dot(pool, wpp_ref[...], preferred_element_type=jnp.float32)
    pa = jnp.dot(attn_ref[...], wpa_ref[...], preferred_element_type=jnp.float32)
    merged = (_sigmoid(gp_ref[...].astype(jnp.float32)) * pp
              + _sigmoid(ga_ref[...].astype(jnp.float32)) * pa)
    o_ref[...] = x_ref[...] + jnp.dot(merged.astype(MXU_DTYPE), wo_ref[...],
                                      preferred_element_type=jnp.float32)


def _merge(proj2, attn2, x2d, pool_w, pool_scale, wpp, wpa, wo, seq_len):
    m, d = x2d.shape
    tm = min(ROW_TILE, seq_len)
    halo_per_tile = tm // POOL_HALO
    return pl.pallas_call(
        functools.partial(_merge_kernel, seq_len=seq_len),
        grid=(m // tm,),
        in_specs=[pl.BlockSpec((tm, POOL_WIDTH), lambda i: (i, COL_POOL // POOL_WIDTH)),
                  pl.BlockSpec((POOL_HALO, POOL_WIDTH),
                               lambda i: (jnp.maximum(i * halo_per_tile - 1, 0), COL_POOL // POOL_WIDTH)),
                  pl.BlockSpec((tm, d), lambda i: (i, COL_GPOOL // D_MODEL)),
                  pl.BlockSpec((tm, d), lambda i: (i, COL_GATTN // D_MODEL)),
                  pl.BlockSpec((tm, ATTN_Q_WIDTH), lambda i: (i, 0)),
                  pl.BlockSpec((tm, d), lambda i: (i, 0)),
                  _const_spec(pool_w.shape), _const_spec(pool_scale.shape),
                  _const_spec(wpp.shape), _const_spec(wpa.shape), _const_spec(wo.shape)],
        out_specs=pl.BlockSpec((tm, d), lambda i: (i, 0)),
        out_shape=jax.ShapeDtypeStruct((m, d), jnp.float32),
        compiler_params=_cparams(1),
        name="merge",
    )(proj2, proj2, proj2, proj2, attn2, x2d, pool_w, pool_scale, wpp, wpa, wo)


def _ffn_kernel(x_ref, n2_ref, wg_ref, wu_ref, wd_ref, nf_ref, o_ref):
    x = x_ref[...]
    h = _rms(x, n2_ref[...]).astype(MXU_DTYPE)
    d_ff = wg_ref.shape[1]
    acc = x
    for c0 in range(0, d_ff, FF_CHUNK):
        cw = min(FF_CHUNK, d_ff - c0)
        gate = jnp.dot(h, wg_ref[:, c0:c0 + cw], preferred_element_type=jnp.float32)
        up = jnp.dot(h, wu_ref[:, c0:c0 + cw], preferred_element_type=jnp.float32)
        act = (gate * _sigmoid(gate) * up).astype(MXU_DTYPE)
        acc = acc + jnp.dot(act, wd_ref[c0:c0 + cw, :], preferred_element_type=jnp.float32)
    o_ref[...] = _rms(acc, nf_ref[...])


def _ffn(x1, norm2_w, wg, wu, wd, norm_f_w):
    m, d = x1.shape
    return pl.pallas_call(
        _ffn_kernel,
        grid=(m // ROW_TILE,),
        in_specs=[pl.BlockSpec((ROW_TILE, d), lambda i: (i, 0)),
                  _const_spec(norm2_w.shape), _const_spec(wg.shape), _const_spec(wu.shape),
                  _const_spec(wd.shape), _const_spec(norm_f_w.shape)],
        out_specs=pl.BlockSpec((ROW_TILE, d), lambda i: (i, 0)),
        out_shape=jax.ShapeDtypeStruct((m, d), jnp.float32),
        compiler_params=_cparams(1),
        name="ffn",
    )(x1, norm2_w, wg, wu, wd, norm_f_w)


def _rope_tables(seq_len):
    inv_freq = 1.0 / (ROPE_THETA ** (jnp.arange(0, ROPE_DIM, 2, dtype=jnp.float32) / ROPE_DIM))
    ang = jnp.arange(seq_len).astype(jnp.float32)[:, None] * inv_freq[None, :]
    cos, sin = jnp.cos(ang), jnp.sin(ang)
    lane = np.arange(KV_WIDTH) % HEAD_DIM
    rotated = jnp.asarray(lane < ROPE_DIM)
    cos_k = jnp.where(rotated[None, :], cos[:, lane % ROPE_HALF], 1.0)
    sin_k = jnp.where(rotated[None, :], sin[:, lane % ROPE_HALF], 0.0)
    return cos.T, sin.T, cos_k, sin_k


def _rotate_half_matrix():
    rot = np.zeros((KV_WIDTH, KV_WIDTH), np.float32)
    for l in range(KV_WIDTH):
        d = l % HEAD_DIM
        if d < ROPE_HALF:
            rot[l + ROPE_HALF, l] = -1.0
        elif d < ROPE_DIM:
            rot[l - ROPE_HALF, l] = 1.0
    return jnp.asarray(rot, MXU_DTYPE)


def _layer(x, norm1_w, w_in, pool_w, pool_scale, cmp_pe_k, cmp_w1_k, cmp_b1_k, cmp_w2_k,
           cmp_pe_v, cmp_w1_v, cmp_b1_v, cmp_w2_v, w_proj_pool, w_proj_attn, w_out):
    b, s, d = x.shape
    m = b * s
    n_blk = s // SLC_BLOCK
    assert s % Q_TILE == 0 and s % ROW_TILE == 0 and (s & (s - 1)) == 0, s
    cd = MXU_DTYPE

    g0 = COL_GPOOL
    w_r = jnp.concatenate([w_in[:, :g0], w_in[:, g0 + GATE_NSA:], w_in[:, g0:g0 + GATE_NSA],
                           jnp.zeros((d, LANE - GATE_NSA), w_in.dtype)], axis=1).astype(cd)
    x2d = x.reshape(m, d)
    proj2 = _inproj(x2d, norm1_w.reshape(1, d), w_r)
    proj3 = proj2.reshape(b, s, PROJ_WIDTH)

    def rows16(c0):
        a = proj3[:, :, c0:c0 + KV_WIDTH].reshape(b, s // CMP_STRIDE, CMP_STRIDE, N_KV_GROUPS, HEAD_DIM)
        return a.transpose(0, 3, 1, 2, 4).reshape(b, N_KV_GROUPS, s // CMP_STRIDE, CMP_STRIDE * HEAD_DIM)

    def pe8(pe):
        return jnp.broadcast_to(pe.reshape(1, -1), (8, pe.size)).astype(cd)

    kcmp, vcmp = _compress(rows16(COL_KV), rows16(COL_KV + KV_WIDTH),
                           pe8(cmp_pe_k), cmp_w1_k.astype(cd), cmp_b1_k.reshape(1, -1), cmp_w2_k.astype(cd),
                           pe8(cmp_pe_v), cmp_w1_v.astype(cd), cmp_b1_v.reshape(1, -1), cmp_w2_v.astype(cd))

    def rmajor(a):
        return a.reshape(b, N_KV_GROUPS, n_blk, CMP_PER_SLC, HEAD_DIM).transpose(0, 1, 3, 2, 4).reshape(
            b, N_KV_GROUPS, n_blk * CMP_PER_SLC, HEAD_DIM)

    kc_r = (rmajor(kcmp) * SCORE_SCALE).astype(cd)
    vc_rt = rmajor(vcmp).transpose(0, 1, 3, 2).astype(cd)

    cos_q, sin_q, cos_k, sin_k = _rope_tables(s)
    kaug, vst, kwin, vwt = _kprep(proj3, cos_k, sin_k, _rotate_half_matrix())
    ocmp, bias = _cmpsel(proj3, kc_r, vc_rt)
    attn = _flash(proj3, cos_q, sin_q, bias, kaug, vst, kwin, vwt, ocmp)
    return _merge(proj2, attn.reshape(m, ATTN_Q_WIDTH), x2d, pool_w.astype(cd),
                  pool_scale.reshape(1, -1), w_proj_pool.astype(cd), w_proj_attn.astype(cd),
                  w_out.astype(cd), s)


def kernel(x, norm1_w, w_in, pool_w, pool_scale, cmp_pe_k, cmp_w1_k, cmp_b1_k, cmp_w2_k, cmp_pe_v,
           cmp_w1_v, cmp_b1_v, cmp_w2_v, w_proj_pool, w_proj_attn, w_out, norm2_w, w_ffn_gate,
           w_ffn_up, w_ffn_down, norm_f_w):
    b, s, d = x.shape
    depth = w_in.shape[0]
    assert depth == 1, "the final norm is fused into the last layer's FFN kernel"
    cd = MXU_DTYPE
    x1 = _layer(x, norm1_w[0], w_in[0], pool_w[0], pool_scale[0], cmp_pe_k[0], cmp_w1_k[0], cmp_b1_k[0],
                cmp_w2_k[0], cmp_pe_v[0], cmp_w1_v[0], cmp_b1_v[0], cmp_w2_v[0], w_proj_pool[0],
                w_proj_attn[0], w_out[0])
    out = _ffn(x1, norm2_w[0].reshape(1, d), w_ffn_gate[0].astype(cd), w_ffn_up[0].astype(cd),
               w_ffn_down[0].astype(cd), norm_f_w.reshape(1, d))
    return out.reshape(b, s, d)
```

```python
import functools

import numpy as np
import jax
import jax.numpy as jnp
from jax import lax
from jax.experimental import pallas as pl
from jax.experimental.pallas import tpu as pltpu

D_MODEL = 1024
N_HEADS = 16
HEAD_DIM = 64
N_KV_GROUPS = 4
HEADS_PER_GROUP = N_HEADS // N_KV_GROUPS
ROPE_DIM = HEAD_DIM // 4
ROPE_HALF = ROPE_DIM // 2
ROPE_THETA = 500000.0
CMP_BLOCK = 32
CMP_STRIDE = 16
CMP_PER_SLC = 4
SLC_BLOCK = 64
SLC_TOPK = 16
WINDOW = 512
ATTN_Q_WIDTH = N_HEADS * HEAD_DIM
KV_WIDTH = N_KV_GROUPS * HEAD_DIM
POOL_WIDTH = D_MODEL // 2
POOL_WINDOWS = (2, 4, 8, 16)
POOL_GROUP = POOL_WIDTH // len(POOL_WINDOWS)
POOL_HALO = 16
RMS_EPS = 1e-6

LANE = 128
MXU_DTYPE = jnp.bfloat16
ACT_DTYPE = jnp.bfloat16
VMEM_LIMIT_BYTES = 56 * 1024 * 1024

GATE_NSA = N_HEADS * 3
COL_POOL = 0
COL_Q = COL_POOL + POOL_WIDTH
COL_KV = COL_Q + ATTN_Q_WIDTH
COL_GPOOL = COL_KV + 6 * KV_WIDTH
COL_GATTN = COL_GPOOL + D_MODEL
COL_GNSA = COL_GATTN + D_MODEL
PROJ_WIDTH = COL_GNSA + LANE
GATE_ROWS = 16

ROW_TILE = 512
Q_TILE = 256
KPREP_TILE = 1024
FF_CHUNK = 512
SCORE_SCALE = HEAD_DIM ** -0.5
LOG2E = float(np.log2(np.e))
FINITE_LIMIT = 3e38
MASK_BIAS = float(2 ** 30)
NEG_BIG = -1e30
IMP_BIG = 1e30


def _log2(n):
    k = int(n).bit_length() - 1
    assert 1 << k == n, n
    return k


def _cparams(n_grid):
    return pltpu.CompilerParams(dimension_semantics=("arbitrary",) * n_grid,
                                vmem_limit_bytes=VMEM_LIMIT_BYTES)


def _const_spec(shape):
    nd = len(shape)
    return pl.BlockSpec(shape, lambda *_: (0,) * nd, pipeline_mode=pl.Buffered(1))


def _sigmoid(x):
    return 1.0 / (1.0 + jnp.exp(-x))


def _not_finite(x):
    return jnp.logical_not(jnp.sum(x) < FINITE_LIMIT)


def _rms(x, w):
    var = jnp.mean(x * x, axis=-1, keepdims=True)
    return x * lax.rsqrt(var + RMS_EPS) * w


def _inproj_kernel(x_ref, nw_ref, w_ref, o_ref):
    h = _rms(x_ref[...], nw_ref[...]).astype(MXU_DTYPE)
    n = o_ref.shape[1]
    for c0 in range(0, n, 256):
        cw = min(256, n - c0)
        o_ref[:, c0:c0 + cw] = jnp.dot(h, w_ref[:, c0:c0 + cw],
                                       preferred_element_type=jnp.float32).astype(o_ref.dtype)


def _inproj(x2d, norm_w, w_r):
    m, d = x2d.shape
    n = w_r.shape[1]
    return pl.pallas_call(
        _inproj_kernel,
        grid=(m // ROW_TILE,),
        in_specs=[pl.BlockSpec((ROW_TILE, d), lambda i: (i, 0)),
                  _const_spec((1, d)),
                  _const_spec((d, n))],
        out_specs=pl.BlockSpec((ROW_TILE, n), lambda i: (i, 0)),
        out_shape=jax.ShapeDtypeStruct((m, n), ACT_DTYPE),
        compiler_params=_cparams(1),
        name="inproj",
    )(x2d, norm_w, w_r)


def _gelu_tanh(x):
    return 0.5 * x * (1.0 + jnp.tanh(np.sqrt(2.0 / np.pi) * (x + 0.044715 * (x * x * x))))


def _compress_one(r, pe_ref, w1_ref, b1_ref, w2_ref):
    half = w1_ref.shape[0] // 2
    n_rows = r.shape[0]
    a = jnp.dot(r, w1_ref[0:half, :], preferred_element_type=jnp.float32)
    b = jnp.dot(r, w1_ref[half:, :], preferred_element_type=jnp.float32)
    bias = jnp.dot(pe_ref[...], w1_ref[...], preferred_element_type=jnp.float32)[0:1, :] + b1_ref[...]
    hid = a + pltpu.roll(b, n_rows - 1, 0) + bias
    return jnp.dot(_gelu_tanh(hid).astype(MXU_DTYPE), w2_ref[...], preferred_element_type=jnp.float32)


def _compress_kernel(rk_ref, rv_ref, pek_ref, w1k_ref, b1k_ref, w2k_ref,
                     pev_ref, w1v_ref, b1v_ref, w2v_ref, ok_ref, ov_ref):
    ok_ref[0, 0] = _compress_one(rk_ref[0, 0], pek_ref, w1k_ref, b1k_ref, w2k_ref)
    ov_ref[0, 0] = _compress_one(rv_ref[0, 0], pev_ref, w1v_ref, b1v_ref, w2v_ref)


def _compress(rk, rv, pek, w1k, b1k, w2k, pev, w1v, b1v, w2v):
    b, g, n_rows, width = rk.shape
    dk = w2k.shape[1]
    r_spec = pl.BlockSpec((1, 1, n_rows, width), lambda i, j: (i, j, 0, 0))
    o_spec = pl.BlockSpec((1, 1, n_rows, dk), lambda i, j: (i, j, 0, 0))
    w_specs = [_const_spec(a.shape) for a in (pek, w1k, b1k, w2k)]
    o_shape = jax.ShapeDtypeStruct((b, g, n_rows, dk), jnp.float32)
    return pl.pallas_call(
        _compress_kernel,
        grid=(b, g),
        in_specs=[r_spec, r_spec] + w_specs + w_specs,
        out_specs=[o_spec, o_spec],
        out_shape=[o_shape, o_shape],
        compiler_params=_cparams(2),
        name="compress",
    )(rk, rv, pek, w1k, b1k, w2k, pev, w1v, b1v, w2v)


def _kprep_kernel(ks_ref, vs_ref, kw_ref, vw_ref, cos_ref, sin_ref, rot_ref,
                  kaug_ref, vst_ref, kwin_ref, vwt_ref):
    ts = ks_ref.shape[1]
    t0 = pl.program_id(1) * ts
    cos = cos_ref[...]
    sin = sin_ref[...]
    rot = rot_ref[...]

    def rope(k_ref):
        k = k_ref[0]
        krot = jnp.dot(k.astype(MXU_DTYPE), rot, preferred_element_type=jnp.float32)
        return (k.astype(jnp.float32) * cos + krot * sin) * (SCORE_SCALE * LOG2E)

    n_blk = kaug_ref.shape[3] - 2 * HEAD_DIM
    blk = lax.broadcasted_iota(jnp.int32, (ts, n_blk), 1)
    key_blk = (t0 + lax.broadcasted_iota(jnp.int32, (ts, n_blk), 0)) >> _log2(SLC_BLOCK)
    onehot = jnp.where(blk == key_blk, 1.0, 0.0).astype(kaug_ref.dtype)
    ks = rope(ks_ref).astype(kaug_ref.dtype)
    kw = rope(kw_ref).astype(kwin_ref.dtype)
    vst = vs_ref[0].astype(jnp.float32).T.astype(vst_ref.dtype)
    vwt = vw_ref[0].astype(jnp.float32).T.astype(vwt_ref.dtype)
    ones_col = jnp.where(lax.broadcasted_iota(jnp.int32, (ts, HEAD_DIM), 1) == 0, 1.0, 0.0
                         ).astype(kaug_ref.dtype)
    for g in range(N_KV_GROUPS):
        sl = slice(g * HEAD_DIM, (g + 1) * HEAD_DIM)
        kaug_ref[0, g, :, 0:n_blk] = onehot
        kaug_ref[0, g, :, n_blk:n_blk + HEAD_DIM] = ks[:, sl]
        kaug_ref[0, g, :, n_blk + HEAD_DIM:] = ones_col
        kwin_ref[0, g, :, 0:HEAD_DIM] = kw[:, sl]
        kwin_ref[0, g, :, HEAD_DIM:] = ones_col
        vst_ref[0, g] = vst[sl, :]
        vwt_ref[0, g] = vwt[sl, :]


def _kprep(proj3, cos_k, sin_k, rot):
    b, s, _ = proj3.shape
    n_blk = s // SLC_BLOCK
    ts = min(KPREP_TILE, s)
    kvb = COL_KV // KV_WIDTH

    def col(j):
        return pl.BlockSpec((1, ts, KV_WIDTH), lambda i, t: (i, t, kvb + j))

    tab = pl.BlockSpec((ts, KV_WIDTH), lambda i, t: (t, 0))
    aug_w = n_blk + 2 * HEAD_DIM
    return pl.pallas_call(
        _kprep_kernel,
        grid=(b, s // ts),
        in_specs=[col(2), col(3), col(4), col(5), tab, tab, _const_spec(rot.shape)],
        out_specs=[pl.BlockSpec((1, N_KV_GROUPS, ts, aug_w), lambda i, t: (i, 0, t, 0)),
                   pl.BlockSpec((1, N_KV_GROUPS, HEAD_DIM, ts), lambda i, t: (i, 0, 0, t)),
                   pl.BlockSpec((1, N_KV_GROUPS, ts, 2 * HEAD_DIM), lambda i, t: (i, 0, t, 0)),
                   pl.BlockSpec((1, N_KV_GROUPS, HEAD_DIM, ts), lambda i, t: (i, 0, 0, t))],
        out_shape=[jax.ShapeDtypeStruct((b, N_KV_GROUPS, s, aug_w), MXU_DTYPE),
                   jax.ShapeDtypeStruct((b, N_KV_GROUPS, HEAD_DIM, s), MXU_DTYPE),
                   jax.ShapeDtypeStruct((b, N_KV_GROUPS, s, 2 * HEAD_DIM), MXU_DTYPE),
                   jax.ShapeDtypeStruct((b, N_KV_GROUPS, HEAD_DIM, s), MXU_DTYPE)],
        compiler_params=_cparams(2),
        name="kprep",
    )(proj3, proj3, proj3, proj3, cos_k, sin_k, rot)


def _heads_to_lanes(xt):
    return jnp.concatenate([xt[h * HEAD_DIM:(h + 1) * HEAD_DIM, :] for h in range(HEADS_PER_GROUP)],
                           axis=1)


def _cmpsel_kernel(q_ref, kc_ref, vct_ref, ocmp_ref, bias_ref):
    tq = q_ref.shape[1]
    n_cmp_pad = kc_ref.shape[2]
    n_blk = bias_ref.shape[2]
    m_rows = HEADS_PER_GROUP * tq
    t0 = pl.program_id(2) * tq

    qt = _heads_to_lanes(q_ref[0].astype(jnp.float32).T).astype(MXU_DTYPE)
    s = jnp.dot(kc_ref[0, 0], qt, preferred_element_type=jnp.float32)
    row = lax.broadcasted_iota(jnp.int32, (n_cmp_pad, m_rows), 0)
    n_idx = (row & (n_blk - 1)) * CMP_PER_SLC + (row >> _log2(n_blk))
    t = t0 + (lax.broadcasted_iota(jnp.int32, (n_cmp_pad, m_rows), 1) & (tq - 1))
    vis = n_idx * CMP_STRIDE + (CMP_BLOCK - 1) <= t
    s = jnp.where(vis, s, NEG_BIG)
    mx = jnp.max(s, axis=0, keepdims=True)
    p = jnp.where(vis, jnp.exp2(s - mx), 0.0)
    den = jnp.sum(p, axis=0, keepdims=True)
    p = p * jnp.where(den > 0.0, 1.0 / den, 0.0)
    ocmp_ref[0, 0, 0] = jnp.dot(vct_ref[0, 0], p.astype(MXU_DTYPE), preferred_element_type=jnp.float32)

    ph = p[:, 0:tq]
    for h in range(1, HEADS_PER_GROUP):
        ph = ph + p[:, h * tq:(h + 1) * tq]
    last = ph[3 * n_blk:4 * n_blk, :]
    jrow = lax.broadcasted_iota(jnp.int32, (n_blk, tq), 0)
    prev = jnp.where(jrow == 0, 0.0, pltpu.roll(last, 1, 0))
    imp = ph[0:n_blk, :] + ph[n_blk:2 * n_blk, :] + ph[2 * n_blk:3 * n_blk, :] + last + prev

    jt = (t0 + lax.broadcasted_iota(jnp.int32, (n_blk, tq), 1)) >> _log2(SLC_BLOCK)
    forced = (jrow == 0) | (jrow == jt) | (jrow == jt - 1)
    val = jnp.where(forced, IMP_BIG, jnp.where(jrow <= jt, imp, -IMP_BIG))
    jf = jrow.astype(jnp.float32)
    sel = jnp.zeros((n_blk, tq), jnp.float32)
    for _ in range(min(SLC_TOPK, n_blk)):
        top = jnp.max(val, axis=0, keepdims=True)
        first = jnp.min(jnp.where(val == top, jf, float(n_blk)), axis=0, keepdims=True)
        hit = jf == first
        sel = jnp.where(hit, 1.0, sel)
        val = jnp.where(hit, -3e38, val)
    bias_ref[0, 0] = jnp.where(sel > 0.0, 0.0, -MASK_BIAS).astype(bias_ref.dtype)


def _cmpsel(proj3, kc_r, vc_rt):
    b, s, _ = proj3.shape
    tq = min(Q_TILE, s)
    nq = s // tq
    n_cmp_pad = kc_r.shape[2]
    n_blk = s // SLC_BLOCK
    qb = COL_Q // KV_WIDTH
    m_rows = HEADS_PER_GROUP * tq
    return pl.pallas_call(
        _cmpsel_kernel,
        grid=(b, N_KV_GROUPS, nq),
        in_specs=[pl.BlockSpec((1, tq, KV_WIDTH), lambda i, g, q: (i, q, qb + g)),
                  pl.BlockSpec((1, 1, n_cmp_pad, HEAD_DIM), lambda i, g, q: (i, g, 0, 0)),
                  pl.BlockSpec((1, 1, HEAD_DIM, n_cmp_pad), lambda i, g, q: (i, g, 0, 0))],
        out_specs=[pl.BlockSpec((1, 1, 1, HEAD_DIM, m_rows), lambda i, g, q: (i, g, q, 0, 0)),
                   pl.BlockSpec((1, 1, n_blk, tq), lambda i, g, q: (i, g, 0, q))],
        out_shape=[jax.ShapeDtypeStruct((b, N_KV_GROUPS, nq, HEAD_DIM, m_rows), jnp.float32),
                   jax.ShapeDtypeStruct((b, N_KV_GROUPS, n_blk, s), MXU_DTYPE)],
        compiler_params=_cparams(3),
        name="cmpsel",
    )(proj3, kc_r, vc_rt)


def _flash_kernel(q_ref, cos_ref, sin_ref, bias_ref, kaug_ref, vst_ref, kwin_ref, vwt_ref,
                  gate_ref, ocmp_ref, o_ref, m_ref, l_ref, acc_ref, lw_ref, accw_ref):
    tq = q_ref.shape[1]
    m_rows = HEADS_PER_GROUP * tq
    grp = pl.program_id(1)
    qi = pl.program_id(2)
    t0 = qi * tq
    kd = pl.multiple_of(t0, tq)

    qt = q_ref[0].astype(jnp.float32).T
    cos = cos_ref[...]
    sin = sin_ref[...]
    parts = []
    for h in range(HEADS_PER_GROUP):
        qh = qt[h * HEAD_DIM:(h + 1) * HEAD_DIM, :]
        x1 = qh[0:ROPE_HALF, :]
        x2 = qh[ROPE_HALF:ROPE_DIM, :]
        parts.append(jnp.concatenate([x1 * cos - x2 * sin, x2 * cos + x1 * sin, qh[ROPE_DIM:, :]], axis=0))
    qr = jnp.concatenate(parts, axis=1).astype(MXU_DTYPE)
    bias4 = jnp.concatenate([bias_ref[0, 0]] * HEADS_PER_GROUP, axis=1)
    zpad = jnp.zeros((HEAD_DIM, m_rows), MXU_DTYPE)
    pad_row = lax.broadcasted_iota(jnp.int32, (HEAD_DIM, m_rows), 0)

    def stab_rows(neg_m):
        return jnp.where(pad_row == 0, neg_m, 0.0).astype(MXU_DTYPE)

    tcol = t0 + (lax.broadcasted_iota(jnp.int32, (tq, m_rows), 1) & (tq - 1))
    koff = lax.broadcasted_iota(jnp.int32, (tq, m_rows), 0)
    n_back = WINDOW // tq

    def win_mask(back):
        diff = tcol - (t0 - back * tq + koff)
        if back == 0:
            return diff >= 0
        if back * tq + tq - 1 < WINDOW:
            return None
        return diff < WINDOW

    def fast_first(k_ref, v_ref, q0, mask, l_out, acc_out):
        s = jnp.dot(k_ref[0, 0, pl.ds(kd, tq), :], q0, preferred_element_type=jnp.float32)
        s = jnp.where(mask, s, NEG_BIG)
        neg_m = -jnp.max(s, axis=0, keepdims=True).astype(MXU_DTYPE).astype(jnp.float32)
        p = jnp.exp2(s + neg_m)
        l_out[...] = jnp.sum(p, axis=0, keepdims=True)
        acc_out[...] = jnp.dot(v_ref[0, 0, :, pl.ds(kd, tq)], p.astype(MXU_DTYPE),
                               preferred_element_type=jnp.float32)
        return neg_m

    def fast_tile(k_ref, v_ref, qs, k0, tk, mask, l_out, acc_out):
        s = jnp.dot(k_ref[0, 0, pl.ds(k0, tk), :], qs, preferred_element_type=jnp.float32)
        if mask is not None:
            s = jnp.where(mask, s, NEG_BIG)
        p = jnp.exp2(s)
        l_out[...] += jnp.sum(p, axis=0, keepdims=True)
        acc_out[...] += jnp.dot(v_ref[0, 0, :, pl.ds(k0, tk)], p.astype(MXU_DTYPE),
                                preferred_element_type=jnp.float32)

    neg_m = fast_first(kaug_ref, vst_ref, jnp.concatenate([bias4, qr, zpad], axis=0),
                       t0 + koff <= tcol, l_ref, acc_ref)
    qs_slc = jnp.concatenate([bias4, qr, stab_rows(neg_m)], axis=0)

    def pair_body(i, carry):
        fast_tile(kaug_ref, vst_ref, qs_slc, pl.multiple_of(i * (2 * tq), 2 * tq), 2 * tq, None,
                  l_ref, acc_ref)
        return carry

    lax.fori_loop(0, qi // 2, pair_body, 0)

    @pl.when(qi % 2 == 1)
    def _():
        fast_tile(kaug_ref, vst_ref, qs_slc, pl.multiple_of(t0 - tq, tq), tq, None, l_ref, acc_ref)

    neg_mw = fast_first(kwin_ref, vwt_ref, jnp.concatenate([qr, zpad], axis=0), win_mask(0),
                        lw_ref, accw_ref)
    qs_win = jnp.concatenate([qr, stab_rows(neg_mw)], axis=0)
    for back in range(1, n_back + 1):
        @pl.when(qi >= back)
        def _(back=back):
            fast_tile(kwin_ref, vwt_ref, qs_win, pl.multiple_of(t0 - back * tq, tq), tq,
                      win_mask(back), lw_ref, accw_ref)

    def exact_start(s, vt, l_out, acc_out):
        mx = jnp.max(s, axis=0, keepdims=True)
        p = jnp.exp2(s - mx)
        m_ref[...] = mx
        l_out[...] = jnp.sum(p, axis=0, keepdims=True)
        acc_out[...] = jnp.dot(vt, p.astype(MXU_DTYPE), preferred_element_type=jnp.float32)

    def exact_update(s, vt, l_out, acc_out):
        m_old = m_ref[...]
        mx = jnp.maximum(m_old, jnp.max(s, axis=0, keepdims=True))
        alpha = jnp.exp2(m_old - mx)
        p = jnp.exp2(s - mx)
        m_ref[...] = mx
        l_out[...] = alpha * l_out[...] + jnp.sum(p, axis=0, keepdims=True)
        acc_out[...] = alpha * acc_out[...] + jnp.dot(vt, p.astype(MXU_DTYPE),
                                                      preferred_element_type=jnp.float32)

    check = (l_ref[...] + lw_ref[...] + jnp.sum(jnp.abs(acc_ref[...]), axis=0, keepdims=True)
             + jnp.sum(jnp.abs(accw_ref[...]), axis=0, keepdims=True))

    @pl.when(_not_finite(check))
    def _():
        q0 = jnp.concatenate([bias4, qr, zpad], axis=0)
        s = jnp.dot(kaug_ref[0, 0, pl.ds(kd, tq), :], q0, preferred_element_type=jnp.float32)
        exact_start(jnp.where(t0 + koff <= tcol, s, NEG_BIG), vst_ref[0, 0, :, pl.ds(kd, tq)],
                    l_ref, acc_ref)

        def body(kt, carry):
            k0 = pl.multiple_of(kt * tq, tq)
            sk = jnp.dot(kaug_ref[0, 0, pl.ds(k0, tq), :], q0, preferred_element_type=jnp.float32)
            exact_update(sk, vst_ref[0, 0, :, pl.ds(k0, tq)], l_ref, acc_ref)
            return carry

        lax.fori_loop(0, qi, body, 0)
        qw0 = jnp.concatenate([qr, zpad], axis=0)
        for back in range(n_back + 1):
            kbase = t0 - back * tq
            k0 = pl.multiple_of(jnp.maximum(kbase, 0), tq)
            sw = jnp.dot(kwin_ref[0, 0, pl.ds(k0, tq), :], qw0, preferred_element_type=jnp.float32)
            diff = tcol - (kbase + koff)
            sw = jnp.where((diff >= 0) & (diff < WINDOW) & (kbase + koff >= 0), sw, NEG_BIG)
            if back == 0:
                exact_start(sw, vwt_ref[0, 0, :, pl.ds(k0, tq)], lw_ref, accw_ref)
            else:
                exact_update(sw, vwt_ref[0, 0, :, pl.ds(k0, tq)], lw_ref, accw_ref)

    o_slc = acc_ref[...] / l_ref[...]
    o_win = accw_ref[...] / lw_ref[...]

    r_id = lax.broadcasted_iota(jnp.int32, (GATE_ROWS, LANE), 0)
    c_id = lax.broadcasted_iota(jnp.int32, (GATE_ROWS, LANE), 1)
    pick = jnp.where((c_id == grp * (HEADS_PER_GROUP * 3) + r_id) & (r_id < HEADS_PER_GROUP * 3),
                     1.0, 0.0).astype(gate_ref.dtype)
    graw = lax.dot_general(pick, gate_ref[0], (((1,), (1,)), ((), ())),
                           preferred_element_type=jnp.float32)
    gates = _sigmoid(graw)
    o_cmp = ocmp_ref[0, 0, 0]
    mixed = []
    for h in range(HEADS_PER_GROUP):
        sl = slice(h * tq, (h + 1) * tq)
        mixed.append(gates[3 * h:3 * h + 1, :] * o_cmp[:, sl]
                     + gates[3 * h + 1:3 * h + 2, :] * o_slc[:, sl]
                     + gates[3 * h + 2:3 * h + 3, :] * o_win[:, sl])
    o_ref[0] = jnp.concatenate(mixed, axis=0).T.astype(o_ref.dtype)


def _flash(proj3, cos_q, sin_q, bias, kaug, vst, kwin, vwt, ocmp):
    b, s, _ = proj3.shape
    tq = min(Q_TILE, s)
    nq = s // tq
    n_blk = s // SLC_BLOCK
    aug_w = kaug.shape[3]
    qb = COL_Q // KV_WIDTH
    m_rows = HEADS_PER_GROUP * tq
    return pl.pallas_call(
        _flash_kernel,
        grid=(b, N_KV_GROUPS, nq),
        in_specs=[pl.BlockSpec((1, tq, KV_WIDTH), lambda i, g, q: (i, q, qb + g)),
                  pl.BlockSpec((ROPE_HALF, tq), lambda i, g, q: (0, q)),
                  pl.BlockSpec((ROPE_HALF, tq), lambda i, g, q: (0, q)),
                  pl.BlockSpec((1, 1, n_blk, tq), lambda i, g, q: (i, g, 0, q)),
                  pl.BlockSpec((1, 1, s, aug_w), lambda i, g, q: (i, g, 0, 0)),
                  pl.BlockSpec((1, 1, HEAD_DIM, s), lambda i, g, q: (i, g, 0, 0)),
                  pl.BlockSpec((1, 1, s, 2 * HEAD_DIM), lambda i, g, q: (i, g, 0, 0)),
                  pl.BlockSpec((1, 1, HEAD_DIM, s), lambda i, g, q: (i, g, 0, 0)),
                  pl.BlockSpec((1, tq, LANE), lambda i, g, q: (i, q, COL_GNSA // LANE)),
                  pl.BlockSpec((1, 1, 1, HEAD_DIM, m_rows), lambda i, g, q: (i, g, q, 0, 0))],
        out_specs=pl.BlockSpec((1, tq, KV_WIDTH), lambda i, g, q: (i, q, g)),
        out_shape=jax.ShapeDtypeStruct((b, s, ATTN_Q_WIDTH), ACT_DTYPE),
        scratch_shapes=[pltpu.VMEM((1, m_rows), jnp.float32),
                        pltpu.VMEM((1, m_rows), jnp.float32),
                        pltpu.VMEM((HEAD_DIM, m_rows), jnp.float32),
                        pltpu.VMEM((1, m_rows), jnp.float32),
                        pltpu.VMEM((HEAD_DIM, m_rows), jnp.float32)],
        compiler_params=_cparams(3),
        name="flash",
    )(proj3, cos_q, sin_q, bias, kaug, vst, kwin, vwt, proj3, ocmp)


def _merge_kernel(u_ref, halo_ref, gp_ref, ga_ref, attn_ref, x_ref, pw_ref, ps_ref,
                  wpp_ref, wpa_ref, wo_ref, o_ref, *, seq_len):
    tm = u_ref.shape[0]
    t0 = (pl.program_id(0) * tm) & (seq_len - 1)
    u = u_ref[...].astype(jnp.float32)
    halo = jnp.where(t0 > 0, halo_ref[...].astype(jnp.float32), 0.0)
    ext = jnp.concatenate([halo, u], axis=0)
    t = t0 + lax.broadcasted_iota(jnp.int32, (tm, POOL_GROUP), 0)
    pooled = []
    for gi, w in enumerate(POOL_WINDOWS):
        sl = slice(gi * POOL_GROUP, (gi + 1) * POOL_GROUP)
        acc = ext[:, sl]
        span = 1
        while span < w:
            acc = acc + pltpu.roll(acc, span, 0)
            span *= 2
        cnt = jnp.minimum(t + 1, w).astype(jnp.float32)
        mean = acc[POOL_HALO:, :] / cnt
        mixed = jnp.dot((mean - u[:, sl]).astype(MXU_DTYPE), pw_ref[gi],
                        preferred_element_type=jnp.float32)
        pooled.append(mixed * ps_ref[:, sl])
    pool = jnp.concatenate(pooled, axis=1).astype(MXU_DTYPE)
    pp = jnp.dot(pool, wpp_ref[...], preferred_element_type=jnp.float32)
    pa = jnp.dot(attn_ref[...], wpa_ref[...], preferred_element_type=jnp.float32)
    merged = (_sigmoid(gp_ref[...].astype(jnp.float32)) * pp
              + _sigmoid(ga_ref[...].astype(jnp.float32)) * pa)
    o_ref[...] = x_ref[...] + jnp.dot(merged.astype(MXU_DTYPE), wo_ref[...],
                                      preferred_element_type=jnp.float32)


def _merge(proj2, attn2, x2d, pool_w, pool_scale, wpp, wpa, wo, seq_len):
    m, d = x2d.shape
    tm = min(ROW_TILE, seq_len)
    halo_per_tile = tm // POOL_HALO
    return pl.pallas_call(
        functools.partial(_merge_kernel, seq_len=seq_len),
        grid=(m // tm,),
        in_specs=[pl.BlockSpec((tm, POOL_WIDTH), lambda i: (i, COL_POOL // POOL_WIDTH)),
                  pl.BlockSpec((POOL_HALO, POOL_WIDTH),
                               lambda i: (jnp.maximum(i * halo_per_tile - 1, 0), COL_POOL // POOL_WIDTH)),
                  pl.BlockSpec((tm, d), lambda i: (i, COL_GPOOL // D_MODEL)),
                  pl.BlockSpec((tm, d), lambda i: (i, COL_GATTN // D_MODEL)),
                  pl.BlockSpec((tm, ATTN_Q_WIDTH), lambda i: (i, 0)),
                  pl.BlockSpec((tm, d), lambda i: (i, 0)),
                  _const_spec(pool_w.shape), _const_spec(pool_scale.shape),
                  _const_spec(wpp.shape), _const_spec(wpa.shape), _const_spec(wo.shape)],
        out_specs=pl.BlockSpec((tm, d), lambda i: (i, 0)),
        out_shape=jax.ShapeDtypeStruct((m, d), jnp.float32),
        compiler_params=_cparams(1),
        name="merge",
    )(proj2, proj2, proj2, proj2, attn2, x2d, pool_w, pool_scale, wpp, wpa, wo)


def _ffn_kernel(x_ref, n2_ref, wg_ref, wu_ref, wd_ref, nf_ref, o_ref):
    x = x_ref[...]
    h = _rms(x, n2_ref[...]).astype(MXU_DTYPE)
    d_ff = wg_ref.shape[1]
    acc = x
    for c0 in range(0, d_ff, FF_CHUNK):
        cw = min(FF_CHUNK, d_ff - c0)
        gate = jnp.dot(h, wg_ref[:, c0:c0 + cw], preferred_element_type=jnp.float32)
        up = jnp.dot(h, wu_ref[:, c0:c0 + cw], preferred_element_type=jnp.float32)
        act = (gate * _sigmoid(gate) * up).astype(MXU_DTYPE)
        acc = acc + jnp.dot(act, wd_ref[c0:c0 + cw, :], preferred_element_type=jnp.float32)
    o_ref[...] = _rms(acc, nf_ref[...])


def _ffn(x1, norm2_w, wg, wu, wd, norm_f_w):
    m, d = x1.shape
    return pl.pallas_call(
        _ffn_kernel,
        grid=(m // ROW_TILE,),
        in_specs=[pl.BlockSpec((ROW_TILE, d), lambda i: (i, 0)),
                  _const_spec(norm2_w.shape), _const_spec(wg.shape), _const_spec(wu.shape),
                  _const_spec(wd.shape), _const_spec(norm_f_w.shape)],
        out_specs=pl.BlockSpec((ROW_TILE, d), lambda i: (i, 0)),
        out_shape=jax.ShapeDtypeStruct((m, d), jnp.float32),
        compiler_params=_cparams(1),
        name="ffn",
    )(x1, norm2_w, wg, wu, wd, norm_f_w)


def _rope_tables(seq_len):
    inv_freq = 1.0 / (ROPE_THETA ** (jnp.arange(0, ROPE_DIM, 2, dtype=jnp.float32) / ROPE_DIM))
    ang = jnp.arange(seq_len).astype(jnp.float32)[:, None] * inv_freq[None, :]
    cos, sin = jnp.cos(ang), jnp.sin(ang)
    lane = np.arange(KV_WIDTH) % HEAD_DIM
    rotated = jnp.asarray(lane < ROPE_DIM)
    cos_k = jnp.where(rotated[None, :], cos[:, lane % ROPE_HALF], 1.0)
    sin_k = jnp.where(rotated[None, :], sin[:, lane % ROPE_HALF], 0.0)
    return cos.T, sin.T, cos_k, sin_k


def _rotate_half_matrix():
    rot = np.zeros((KV_WIDTH, KV_WIDTH), np.float32)
    for l in range(KV_WIDTH):
        d = l % HEAD_DIM
        if d < ROPE_HALF:
            rot[l + ROPE_HALF, l] = -1.0
        elif d < ROPE_DIM:
            rot[l - ROPE_HALF, l] = 1.0
    return jnp.asarray(rot, MXU_DTYPE)


def _layer(x, norm1_w, w_in, pool_w, pool_scale, cmp_pe_k, cmp_w1_k, cmp_b1_k, cmp_w2_k,
           cmp_pe_v, cmp_w1_v, cmp_b1_v, cmp_w2_v, w_proj_pool, w_proj_attn, w_out):
    b, s, d = x.shape
    m = b * s
    n_blk = s // SLC_BLOCK
    assert s % Q_TILE == 0 and s % ROW_TILE == 0 and (s & (s - 1)) == 0, s
    cd = MXU_DTYPE

    g0 = COL_GPOOL
    w_r = jnp.concatenate([w_in[:, :g0], w_in[:, g0 + GATE_NSA:], w_in[:, g0:g0 + GATE_NSA],
                           jnp.zeros((d, LANE - GATE_NSA), w_in.dtype)], axis=1).astype(cd)
    x2d = x.reshape(m, d)
    proj2 = _inproj(x2d, norm1_w.reshape(1, d), w_r)
    proj3 = proj2.reshape(b, s, PROJ_WIDTH)

    def rows16(c0):
        a = proj3[:, :, c0:c0 + KV_WIDTH].reshape(b, s // CMP_STRIDE, CMP_STRIDE, N_KV_GROUPS, HEAD_DIM)
        return a.transpose(0, 3, 1, 2, 4).reshape(b, N_KV_GROUPS, s // CMP_STRIDE, CMP_STRIDE * HEAD_DIM)

    def pe8(pe):
        return jnp.broadcast_to(pe.reshape(1, -1), (8, pe.size)).astype(cd)

    kcmp, vcmp = _compress(rows16(COL_KV), rows16(COL_KV + KV_WIDTH),
                           pe8(cmp_pe_k), cmp_w1_k.astype(cd), cmp_b1_k.reshape(1, -1), cmp_w2_k.astype(cd),
                           pe8(cmp_pe_v), cmp_w1_v.astype(cd), cmp_b1_v.reshape(1, -1), cmp_w2_v.astype(cd))

    def rmajor(a):
        return a.reshape(b, N_KV_GROUPS, n_blk, CMP_PER_SLC, HEAD_DIM).transpose(0, 1, 3, 2, 4).reshape(
            b, N_KV_GROUPS, n_blk * CMP_PER_SLC, HEAD_DIM)

    kc_r = (rmajor(kcmp) * (SCORE_SCALE * LOG2E)).astype(cd)
    vc_rt = rmajor(vcmp).transpose(0, 1, 3, 2).astype(cd)

    cos_q, sin_q, cos_k, sin_k = _rope_tables(s)
    kaug, vst, kwin, vwt = _kprep(proj3, cos_k, sin_k, _rotate_half_matrix())
    ocmp, bias = _cmpsel(proj3, kc_r, vc_rt)
    attn = _flash(proj3, cos_q, sin_q, bias, kaug, vst, kwin, vwt, ocmp)
    return _merge(proj2, attn.reshape(m, ATTN_Q_WIDTH), x2d, pool_w.astype(cd),
                  pool_scale.reshape(1, -1), w_proj_pool.astype(cd), w_proj_attn.astype(cd),
                  w_out.astype(cd), s)


def kernel(x, norm1_w, w_in, pool_w, pool_scale, cmp_pe_k, cmp_w1_k, cmp_b1_k, cmp_w2_k, cmp_pe_v,
           cmp_w1_v, cmp_b1_v, cmp_w2_v, w_proj_pool, w_proj_attn, w_out, norm2_w, w_ffn_gate,
           w_ffn_up, w_ffn_down, norm_f_w):
    b, s, d = x.shape
    depth = w_in.shape[0]
    assert depth == 1, "the final norm is fused into the last layer's FFN kernel"
    cd = MXU_DTYPE
    x1 = _layer(x, norm1_w[0], w_in[0], pool_w[0], pool_scale[0], cmp_pe_k[0], cmp_w1_k[0], cmp_b1_k[0],
                cmp_w2_k[0], cmp_pe_v[0], cmp_w1_v[0], cmp_b1_v[0], cmp_w2_v[0], w_proj_pool[0],
                w_proj_attn[0], w_out[0])
    out = _ffn(x1, norm2_w[0].reshape(1, d), w_ffn_gate[0].astype(cd), w_ffn_up[0].astype(cd),
               w_ffn_down[0].astype(cd), norm_f_w.reshape(1, d))
    return out.reshape(b, s, d)
```

```python
import functools

import numpy as np
import jax
import jax.numpy as jnp
from jax import lax
from jax.experimental import pallas as pl
from jax.experimental.pallas import tpu as pltpu

D_MODEL = 1024
N_HEADS = 16
HEAD_DIM = 64
N_KV_GROUPS = 4
HEADS_PER_GROUP = N_HEADS // N_KV_GROUPS
ROPE_DIM = HEAD_DIM // 4
ROPE_HALF = ROPE_DIM // 2
ROPE_THETA = 500000.0
CMP_BLOCK = 32
CMP_STRIDE = 16
CMP_PER_SLC = 4
SLC_BLOCK = 64
SLC_TOPK = 16
WINDOW = 512
ATTN_Q_WIDTH = N_HEADS * HEAD_DIM
KV_WIDTH = N_KV_GROUPS * HEAD_DIM
POOL_WIDTH = D_MODEL // 2
POOL_WINDOWS = (2, 4, 8, 16)
POOL_GROUP = POOL_WIDTH // len(POOL_WINDOWS)
POOL_HALO = 16
RMS_EPS = 1e-6

LANE = 128
MXU_DTYPE = jnp.bfloat16
ACT_DTYPE = jnp.bfloat16
VMEM_LIMIT_BYTES = 56 * 1024 * 1024

GATE_NSA = N_HEADS * 3
COL_POOL = 0
COL_Q = COL_POOL + POOL_WIDTH
COL_KV = COL_Q + ATTN_Q_WIDTH
COL_GPOOL = COL_KV + 6 * KV_WIDTH
COL_GATTN = COL_GPOOL + D_MODEL
COL_GNSA = COL_GATTN + D_MODEL
PROJ_WIDTH = COL_GNSA + LANE
GATE_ROWS = 16

ROW_TILE = 512
Q_TILE = 256
SLC_SPAN = 2048
SLC_SUBTILE = 512
KPREP_TILE = 1024
FF_CHUNK = 512
SCORE_SCALE = HEAD_DIM ** -0.5
LOG2E = float(np.log2(np.e))
FINITE_LIMIT = 3e38
MASK_BIAS = float(2 ** 30)
NEG_BIG = -1e30
IMP_BIG = 1e30
CHOSEN = -3e38
N_FORCED = 3


def _log2(n):
    k = int(n).bit_length() - 1
    assert 1 << k == n, n
    return k


def _cparams(n_grid):
    return pltpu.CompilerParams(dimension_semantics=("arbitrary",) * n_grid,
                                vmem_limit_bytes=VMEM_LIMIT_BYTES)


def _const_spec(shape):
    nd = len(shape)
    return pl.BlockSpec(shape, lambda *_: (0,) * nd, pipeline_mode=pl.Buffered(1))


def _sigmoid(x):
    return 1.0 / (1.0 + jnp.exp(-x))


def _not_finite(x):
    return jnp.logical_not(jnp.sum(x) < FINITE_LIMIT)


def _rms(x, w):
    var = jnp.mean(x * x, axis=-1, keepdims=True)
    return x * lax.rsqrt(var + RMS_EPS) * w


def _inproj_kernel(x_ref, nw_ref, w_ref, o_ref, kvc_ref):
    h = _rms(x_ref[...], nw_ref[...]).astype(MXU_DTYPE)
    n = o_ref.shape[1]
    for c0 in range(0, n, KV_WIDTH):
        cw = min(KV_WIDTH, n - c0)
        res = jnp.dot(h, w_ref[:, c0:c0 + cw], preferred_element_type=jnp.float32).astype(o_ref.dtype)
        o_ref[:, c0:c0 + cw] = res
        which = (c0 - COL_KV) // KV_WIDTH
        if c0 >= COL_KV and which < 2:
            for g in range(N_KV_GROUPS):
                kvc_ref[which * N_KV_GROUPS + g] = res[:, g * HEAD_DIM:(g + 1) * HEAD_DIM]


def _inproj(x2d, norm_w, w_r):
    m, d = x2d.shape
    n = w_r.shape[1]
    return pl.pallas_call(
        _inproj_kernel,
        grid=(m // ROW_TILE,),
        in_specs=[pl.BlockSpec((ROW_TILE, d), lambda i: (i, 0)),
                  _const_spec((1, d)),
                  _const_spec((d, n))],
        out_specs=[pl.BlockSpec((ROW_TILE, n), lambda i: (i, 0)),
                   pl.BlockSpec((2 * N_KV_GROUPS, ROW_TILE, HEAD_DIM), lambda i: (0, i, 0))],
        out_shape=[jax.ShapeDtypeStruct((m, n), ACT_DTYPE),
                   jax.ShapeDtypeStruct((2 * N_KV_GROUPS, m, HEAD_DIM), ACT_DTYPE)],
        compiler_params=_cparams(1),
        name="inproj",
    )(x2d, norm_w, w_r)


def _gelu_tanh(x):
    return 0.5 * x * (1.0 + jnp.tanh(np.sqrt(2.0 / np.pi) * (x + 0.044715 * (x * x * x))))


def _compress_one(r, pe_ref, w1_ref, b1_ref, w2_ref):
    half = w1_ref.shape[0] // 2
    n_rows = r.shape[0]
    a = jnp.dot(r, w1_ref[0:half, :], preferred_element_type=jnp.float32)
    b = jnp.dot(r, w1_ref[half:, :], preferred_element_type=jnp.float32)
    bias = jnp.dot(pe_ref[...], w1_ref[...], preferred_element_type=jnp.float32)[0:1, :] + b1_ref[...]
    hid = a + pltpu.roll(b, n_rows - 1, 0) + bias
    return jnp.dot(_gelu_tanh(hid).astype(MXU_DTYPE), w2_ref[...], preferred_element_type=jnp.float32)


def _compress_kernel(rk_ref, rv_ref, pek_ref, w1k_ref, b1k_ref, w2k_ref,
                     pev_ref, w1v_ref, b1v_ref, w2v_ref, ok_ref, ovt_ref, nat_ref):
    n_rows = nat_ref.shape[0]
    n_blk = n_rows // CMP_PER_SLC
    kc = _compress_one(rk_ref[0, 0], pek_ref, w1k_ref, b1k_ref, w2k_ref)
    vc = _compress_one(rv_ref[0, 0], pev_ref, w1v_ref, b1v_ref, w2v_ref)
    nat_ref[...] = jnp.concatenate([kc, vc], axis=1)
    rmaj = jnp.concatenate([nat_ref[pl.ds(r, n_blk, stride=CMP_PER_SLC), :] for r in range(CMP_PER_SLC)],
                           axis=0)
    ok_ref[0, 0] = (rmaj[:, 0:HEAD_DIM] * (SCORE_SCALE * LOG2E)).astype(ok_ref.dtype)
    ovt_ref[0, 0] = rmaj.T[HEAD_DIM:, :].astype(ovt_ref.dtype)


def _compress(kvc_rows, pek, w1k, b1k, w2k, pev, w1v, b1v, w2v):
    _, b, n_rows, width = kvc_rows.shape
    dk = w2k.shape[1]
    w_specs = [_const_spec(a.shape) for a in (pek, w1k, b1k, w2k)]
    return pl.pallas_call(
        _compress_kernel,
        grid=(b, N_KV_GROUPS),
        in_specs=[pl.BlockSpec((1, 1, n_rows, width), lambda i, j: (j, i, 0, 0)),
                  pl.BlockSpec((1, 1, n_rows, width), lambda i, j: (N_KV_GROUPS + j, i, 0, 0))]
        + w_specs + w_specs,
        out_specs=[pl.BlockSpec((1, 1, n_rows, dk), lambda i, j: (i, j, 0, 0)),
                   pl.BlockSpec((1, 1, dk, n_rows), lambda i, j: (i, j, 0, 0))],
        out_shape=[jax.ShapeDtypeStruct((b, N_KV_GROUPS, n_rows, dk), MXU_DTYPE),
                   jax.ShapeDtypeStruct((b, N_KV_GROUPS, dk, n_rows), MXU_DTYPE)],
        scratch_shapes=[pltpu.VMEM((n_rows, 2 * dk), jnp.float32)],
        compiler_params=_cparams(2),
        name="compress",
    )(kvc_rows, kvc_rows, pek, w1k, b1k, w2k, pev, w1v, b1v, w2v)


def _kprep_kernel(ks_ref, vs_ref, kw_ref, vw_ref, cos_ref, sin_ref, rot_ref,
                  kaug_ref, vst_ref, kwin_ref, vwt_ref):
    ts = ks_ref.shape[1]
    t0 = pl.program_id(1) * ts
    cos = cos_ref[...]
    sin = sin_ref[...]
    rot = rot_ref[...]

    def rope(k_ref):
        k = k_ref[0]
        krot = jnp.dot(k.astype(MXU_DTYPE), rot, preferred_element_type=jnp.float32)
        return (k.astype(jnp.float32) * cos + krot * sin) * (SCORE_SCALE * LOG2E)

    n_blk = kaug_ref.shape[3] - 2 * HEAD_DIM
    blk = lax.broadcasted_iota(jnp.int32, (ts, n_blk), 1)
    key_blk = (t0 + lax.broadcasted_iota(jnp.int32, (ts, n_blk), 0)) >> _log2(SLC_BLOCK)
    onehot = jnp.where(blk == key_blk, 1.0, 0.0).astype(kaug_ref.dtype)
    ks = rope(ks_ref).astype(kaug_ref.dtype)
    kw = rope(kw_ref).astype(kwin_ref.dtype)
    vst = vs_ref[0].astype(jnp.float32).T.astype(vst_ref.dtype)
    vwt = vw_ref[0].astype(jnp.float32).T.astype(vwt_ref.dtype)
    ones_col = jnp.where(lax.broadcasted_iota(jnp.int32, (ts, HEAD_DIM), 1) == 0, 1.0, 0.0
                         ).astype(kaug_ref.dtype)
    for g in range(N_KV_GROUPS):
        sl = slice(g * HEAD_DIM, (g + 1) * HEAD_DIM)
        kaug_ref[0, g, :, 0:n_blk] = onehot
        kaug_ref[0, g, :, n_blk:n_blk + HEAD_DIM] = ks[:, sl]
        kaug_ref[0, g, :, n_blk + HEAD_DIM:] = ones_col
        kwin_ref[0, g, :, 0:HEAD_DIM] = kw[:, sl]
        kwin_ref[0, g, :, HEAD_DIM:] = ones_col
        vst_ref[0, g] = vst[sl, :]
        vwt_ref[0, g] = vwt[sl, :]


def _kprep(proj3, cos_k, sin_k, rot):
    b, s, _ = proj3.shape
    n_blk = s // SLC_BLOCK
    ts = min(KPREP_TILE, s)
    kvb = COL_KV // KV_WIDTH

    def col(j):
        return pl.BlockSpec((1, ts, KV_WIDTH), lambda i, t: (i, t, kvb + j))

    tab = pl.BlockSpec((ts, KV_WIDTH), lambda i, t: (t, 0))
    aug_w = n_blk + 2 * HEAD_DIM
    return pl.pallas_call(
        _kprep_kernel,
        grid=(b, s // ts),
        in_specs=[col(2), col(3), col(4), col(5), tab, tab, _const_spec(rot.shape)],
        out_specs=[pl.BlockSpec((1, N_KV_GROUPS, ts, aug_w), lambda i, t: (i, 0, t, 0)),
                   pl.BlockSpec((1, N_KV_GROUPS, HEAD_DIM, ts), lambda i, t: (i, 0, 0, t)),
                   pl.BlockSpec((1, N_KV_GROUPS, ts, 2 * HEAD_DIM), lambda i, t: (i, 0, t, 0)),
                   pl.BlockSpec((1, N_KV_GROUPS, HEAD_DIM, ts), lambda i, t: (i, 0, 0, t))],
        out_shape=[jax.ShapeDtypeStruct((b, N_KV_GROUPS, s, aug_w), MXU_DTYPE),
                   jax.ShapeDtypeStruct((b, N_KV_GROUPS, HEAD_DIM, s), MXU_DTYPE),
                   jax.ShapeDtypeStruct((b, N_KV_GROUPS, s, 2 * HEAD_DIM), MXU_DTYPE),
                   jax.ShapeDtypeStruct((b, N_KV_GROUPS, HEAD_DIM, s), MXU_DTYPE)],
        compiler_params=_cparams(2),
        name="kprep",
    )(proj3, proj3, proj3, proj3, cos_k, sin_k, rot)


def _heads_to_lanes(xt):
    return jnp.concatenate([xt[h * HEAD_DIM:(h + 1) * HEAD_DIM, :] for h in range(HEADS_PER_GROUP)],
                           axis=1)


def _cmpsel_kernel(q_ref, kc_ref, vct_ref, ocmp_ref, bias_ref):
    tq = q_ref.shape[1]
    n_cmp_pad = kc_ref.shape[2]
    n_blk = bias_ref.shape[2]
    m_rows = HEADS_PER_GROUP * tq
    t0 = pl.program_id(2) * tq

    qt = _heads_to_lanes(q_ref[0].astype(jnp.float32).T).astype(MXU_DTYPE)
    s = jnp.dot(kc_ref[0, 0], qt, preferred_element_type=jnp.float32)
    row = lax.broadcasted_iota(jnp.int32, (n_cmp_pad, tq), 0)
    n_idx = (row & (n_blk - 1)) * CMP_PER_SLC + (row >> _log2(n_blk))
    t = t0 + lax.broadcasted_iota(jnp.int32, (n_cmp_pad, tq), 1)
    vbias = jnp.where(n_idx * CMP_STRIDE + (CMP_BLOCK - 1) <= t, 0.0, NEG_BIG)
    s = s + jnp.concatenate([vbias] * HEADS_PER_GROUP, axis=1)
    mx = jnp.max(s, axis=0, keepdims=True)
    p = jnp.exp2(s - mx)
    den = jnp.sum(p, axis=0, keepdims=True)
    t_row = t0 + (lax.broadcasted_iota(jnp.int32, (1, m_rows), 1) & (tq - 1))
    inv = jnp.where(t_row >= CMP_BLOCK - 1, 1.0 / den, 0.0)
    ocmp_ref[0, 0, 0] = inv * jnp.dot(vct_ref[0, 0], p.astype(MXU_DTYPE),
                                      preferred_element_type=jnp.float32)

    jrow = lax.broadcasted_iota(jnp.int32, (n_blk, tq), 0)
    imp = None
    for h in range(HEADS_PER_GROUP):
        ph = p[:, h * tq:(h + 1) * tq]
        last = ph[3 * n_blk:4 * n_blk, :]
        prev = jnp.where(jrow == 0, 0.0, pltpu.roll(last, 1, 0))
        taps = ph[0:n_blk, :] + ph[n_blk:2 * n_blk, :] + ph[2 * n_blk:3 * n_blk, :] + last + prev
        imp_h = taps * inv[:, h * tq:(h + 1) * tq]
        imp = imp_h if imp is None else imp + imp_h

    jt = (t0 + lax.broadcasted_iota(jnp.int32, (n_blk, tq), 1)) >> _log2(SLC_BLOCK)
    forced = (jrow == 0) | (jrow == jt) | (jrow == jt - 1)
    val = jnp.where(forced, CHOSEN, jnp.where(jrow <= jt, imp, -IMP_BIG))
    jf = jrow.astype(jnp.float32)
    for _ in range(min(SLC_TOPK, n_blk) - N_FORCED):
        top = jnp.max(val, axis=0, keepdims=True)
        first = jnp.min(jnp.where(val == top, jf, float(n_blk)), axis=0, keepdims=True)
        val = jnp.where(jf == first, CHOSEN, val)
    bias_ref[0, 0] = jnp.where(val == CHOSEN, 0.0, -MASK_BIAS).astype(bias_ref.dtype)


def _cmpsel(proj3, kc_r, vc_rt):
    b, s, _ = proj3.shape
    tq = min(Q_TILE, s)
    nq = s // tq
    n_cmp_pad = kc_r.shape[2]
    n_blk = s // SLC_BLOCK
    qb = COL_Q // KV_WIDTH
    m_rows = HEADS_PER_GROUP * tq
    return pl.pallas_call(
        _cmpsel_kernel,
        grid=(b, N_KV_GROUPS, nq),
        in_specs=[pl.BlockSpec((1, tq, KV_WIDTH), lambda i, g, q: (i, q, qb + g)),
                  pl.BlockSpec((1, 1, n_cmp_pad, HEAD_DIM), lambda i, g, q: (i, g, 0, 0)),
                  pl.BlockSpec((1, 1, HEAD_DIM, n_cmp_pad), lambda i, g, q: (i, g, 0, 0))],
        out_specs=[pl.BlockSpec((1, 1, 1, HEAD_DIM, m_rows), lambda i, g, q: (i, g, q, 0, 0)),
                   pl.BlockSpec((1, 1, n_blk, tq), lambda i, g, q: (i, g, 0, q))],
        out_shape=[jax.ShapeDtypeStruct((b, N_KV_GROUPS, nq, HEAD_DIM, m_rows), jnp.float32),
                   jax.ShapeDtypeStruct((b, N_KV_GROUPS, n_blk, s), MXU_DTYPE)],
        compiler_params=_cparams(3),
        name="cmpsel",
    )(proj3, kc_r, vc_rt)


def _flash_kernel(q_ref, cos_ref, sin_ref, bias_ref, kaug_ref, vst_ref, kwin_ref, vwt_ref,
                  gate_ref, ocmp_ref, o_ref, m_ref, l_ref, acc_ref, lw_ref, accw_ref):
    tq = q_ref.shape[1]
    m_rows = HEADS_PER_GROUP * tq
    grp = pl.program_id(1)
    qi = pl.program_id(2)
    t0 = qi * tq
    kd = pl.multiple_of(t0, tq)

    qt = q_ref[0].astype(jnp.float32).T
    cos = cos_ref[...]
    sin = sin_ref[...]
    parts = []
    for h in range(HEADS_PER_GROUP):
        qh = qt[h * HEAD_DIM:(h + 1) * HEAD_DIM, :]
        x1 = qh[0:ROPE_HALF, :]
        x2 = qh[ROPE_HALF:ROPE_DIM, :]
        parts.append(jnp.concatenate([x1 * cos - x2 * sin, x2 * cos + x1 * sin, qh[ROPE_DIM:, :]], axis=0))
    qr = jnp.concatenate(parts, axis=1).astype(MXU_DTYPE)
    bias4 = jnp.concatenate([bias_ref[0, 0]] * HEADS_PER_GROUP, axis=1)
    zpad = jnp.zeros((HEAD_DIM, m_rows), MXU_DTYPE)
    pad_row = lax.broadcasted_iota(jnp.int32, (HEAD_DIM, m_rows), 0)

    def stab_rows(neg_m):
        return jnp.where(pad_row == 0, neg_m, 0.0).astype(MXU_DTYPE)

    tcol = t0 + (lax.broadcasted_iota(jnp.int32, (tq, m_rows), 1) & (tq - 1))
    koff = lax.broadcasted_iota(jnp.int32, (tq, m_rows), 0)
    n_back = WINDOW // tq

    def win_mask(back):
        diff = tcol - (t0 - back * tq + koff)
        if back == 0:
            return diff >= 0
        if back * tq + tq - 1 < WINDOW:
            return None
        return diff < WINDOW

    def fast_first(k_ref, v_ref, q0, mask, l_out, acc_out):
        s = jnp.dot(k_ref[0, 0, pl.ds(kd, tq), :], q0, preferred_element_type=jnp.float32)
        s = jnp.where(mask, s, NEG_BIG)
        neg_m = -jnp.max(s, axis=0, keepdims=True).astype(MXU_DTYPE).astype(jnp.float32)
        p = jnp.exp2(s + neg_m)
        l_out[...] = jnp.sum(p, axis=0, keepdims=True)
        acc_out[...] = jnp.dot(v_ref[0, 0, :, pl.ds(kd, tq)], p.astype(MXU_DTYPE),
                               preferred_element_type=jnp.float32)
        return neg_m

    def fast_tiles(k_ref, v_ref, qs, tiles, l_out, acc_out):
        l_add = acc_add = None
        for k0, tk, mask in tiles:
            s = jnp.dot(k_ref[0, 0, pl.ds(k0, tk), :], qs, preferred_element_type=jnp.float32)
            if mask is not None:
                s = jnp.where(mask, s, NEG_BIG)
            p = jnp.exp2(s)
            l_t = jnp.sum(p, axis=0, keepdims=True)
            acc_t = jnp.dot(v_ref[0, 0, :, pl.ds(k0, tk)], p.astype(MXU_DTYPE),
                            preferred_element_type=jnp.float32)
            l_add = l_t if l_add is None else l_add + l_t
            acc_add = acc_t if acc_add is None else acc_add + acc_t
        l_out[...] += l_add
        acc_out[...] += acc_add

    neg_m = fast_first(kaug_ref, vst_ref, jnp.concatenate([bias4, qr, zpad], axis=0),
                       t0 + koff <= tcol, l_ref, acc_ref)
    neg_mw = fast_first(kwin_ref, vwt_ref, jnp.concatenate([qr, zpad], axis=0), win_mask(0),
                        lw_ref, accw_ref)
    qs_slc = jnp.concatenate([bias4, qr, stab_rows(neg_m)], axis=0)
    qs_win = jnp.concatenate([qr, stab_rows(neg_mw)], axis=0)

    sub = SLC_SUBTILE
    n_sub = SLC_SPAN // sub

    def span_body(i, carry):
        base = i * SLC_SPAN
        fast_tiles(kaug_ref, vst_ref, qs_slc,
                   [(pl.multiple_of(base + j * sub, sub), sub, None) for j in range(n_sub)],
                   l_ref, acc_ref)
        return carry

    n_span = t0 // SLC_SPAN
    lax.fori_loop(0, n_span, span_body, 0)
    done = n_span * SLC_SPAN
    size = SLC_SPAN // 2
    while size >= tq:
        take = ((t0 // size) % 2) == 1

        @pl.when(take)
        def _(done=done, size=size):
            piece = min(size, sub)
            fast_tiles(kaug_ref, vst_ref, qs_slc,
                       [(pl.multiple_of(done + j * piece, piece), piece, None) for j in range(size // piece)],
                       l_ref, acc_ref)

        done = done + jnp.where(take, size, 0)
        size //= 2

    def win_tiles(n):
        return [(pl.multiple_of(t0 - back * tq, tq), tq, win_mask(back)) for back in range(1, n + 1)]

    @pl.when(qi >= n_back)
    def _():
        fast_tiles(kwin_ref, vwt_ref, qs_win, win_tiles(n_back), lw_ref, accw_ref)

    for n in range(1, n_back):
        @pl.when(qi == n)
        def _(n=n):
            fast_tiles(kwin_ref, vwt_ref, qs_win, win_tiles(n), lw_ref, accw_ref)

    def exact_start(s, vt, l_out, acc_out):
        mx = jnp.max(s, axis=0, keepdims=True)
        p = jnp.exp2(s - mx)
        m_ref[...] = mx
        l_out[...] = jnp.sum(p, axis=0, keepdims=True)
        acc_out[...] = jnp.dot(vt, p.astype(MXU_DTYPE), preferred_element_type=jnp.float32)

    def exact_update(s, vt, l_out, acc_out):
        m_old = m_ref[...]
        mx = jnp.maximum(m_old, jnp.max(s, axis=0, keepdims=True))
        alpha = jnp.exp2(m_old - mx)
        p = jnp.exp2(s - mx)
        m_ref[...] = mx
        l_out[...] = alpha * l_out[...] + jnp.sum(p, axis=0, keepdims=True)
        acc_out[...] = alpha * acc_out[...] + jnp.dot(vt, p.astype(MXU_DTYPE),
                                                      preferred_element_type=jnp.float32)

    check = (l_ref[...] + lw_ref[...] + jnp.sum(jnp.abs(acc_ref[...]), axis=0, keepdims=True)
             + jnp.sum(jnp.abs(accw_ref[...]), axis=0, keepdims=True))

    @pl.when(_not_finite(check))
    def _():
        q0 = jnp.concatenate([bias4, qr, zpad], axis=0)
        s = jnp.dot(kaug_ref[0, 0, pl.ds(kd, tq), :], q0, preferred_element_type=jnp.float32)
        exact_start(jnp.where(t0 + koff <= tcol, s, NEG_BIG), vst_ref[0, 0, :, pl.ds(kd, tq)],
                    l_ref, acc_ref)

        def body(kt, carry):
            k0 = pl.multiple_of(kt * tq, tq)
            sk = jnp.dot(kaug_ref[0, 0, pl.ds(k0, tq), :], q0, preferred_element_type=jnp.float32)
            exact_update(sk, vst_ref[0, 0, :, pl.ds(k0, tq)], l_ref, acc_ref)
            return carry

        lax.fori_loop(0, qi, body, 0)
        qw0 = jnp.concatenate([qr, zpad], axis=0)
        for back in range(n_back + 1):
            kbase = t0 - back * tq
            k0 = pl.multiple_of(jnp.maximum(kbase, 0), tq)
            sw = jnp.dot(kwin_ref[0, 0, pl.ds(k0, tq), :], qw0, preferred_element_type=jnp.float32)
            diff = tcol - (kbase + koff)
            sw = jnp.where((diff >= 0) & (diff < WINDOW) & (kbase + koff >= 0), sw, NEG_BIG)
            if back == 0:
                exact_start(sw, vwt_ref[0, 0, :, pl.ds(k0, tq)], lw_ref, accw_ref)
            else:
                exact_update(sw, vwt_ref[0, 0, :, pl.ds(k0, tq)], lw_ref, accw_ref)

    o_slc = acc_ref[...] / l_ref[...]
    o_win = accw_ref[...] / lw_ref[...]

    r_id = lax.broadcasted_iota(jnp.int32, (GATE_ROWS, LANE), 0)
    c_id = lax.broadcasted_iota(jnp.int32, (GATE_ROWS, LANE), 1)
    pick = jnp.where((c_id == grp * (HEADS_PER_GROUP * 3) + r_id) & (r_id < HEADS_PER_GROUP * 3),
                     1.0, 0.0).astype(gate_ref.dtype)
    graw = lax.dot_general(pick, gate_ref[0], (((1,), (1,)), ((), ())),
                           preferred_element_type=jnp.float32)
    gates = _sigmoid(graw)
    o_cmp = ocmp_ref[0, 0, 0]
    mixed = []
    for h in range(HEADS_PER_GROUP):
        sl = slice(h * tq, (h + 1) * tq)
        mixed.append(gates[3 * h:3 * h + 1, :] * o_cmp[:, sl]
                     + gates[3 * h + 1:3 * h + 2, :] * o_slc[:, sl]
                     + gates[3 * h + 2:3 * h + 3, :] * o_win[:, sl])
    o_ref[0] = jnp.concatenate(mixed, axis=0).T.astype(o_ref.dtype)


def _flash(proj3, cos_q, sin_q, bias, kaug, vst, kwin, vwt, ocmp):
    b, s, _ = proj3.shape
    tq = min(Q_TILE, s)
    nq = s // tq
    n_blk = s // SLC_BLOCK
    aug_w = kaug.shape[3]
    qb = COL_Q // KV_WIDTH
    m_rows = HEADS_PER_GROUP * tq
    return pl.pallas_call(
        _flash_kernel,
        grid=(b, N_KV_GROUPS, nq),
        in_specs=[pl.BlockSpec((1, tq, KV_WIDTH), lambda i, g, q: (i, q, qb + g)),
                  pl.BlockSpec((ROPE_HALF, tq), lambda i, g, q: (0, q)),
                  pl.BlockSpec((ROPE_HALF, tq), lambda i, g, q: (0, q)),
                  pl.BlockSpec((1, 1, n_blk, tq), lambda i, g, q: (i, g, 0, q)),
                  pl.BlockSpec((1, 1, s, aug_w), lambda i, g, q: (i, g, 0, 0)),
                  pl.BlockSpec((1, 1, HEAD_DIM, s), lambda i, g, q: (i, g, 0, 0)),
                  pl.BlockSpec((1, 1, s, 2 * HEAD_DIM), lambda i, g, q: (i, g, 0, 0)),
                  pl.BlockSpec((1, 1, HEAD_DIM, s), lambda i, g, q: (i, g, 0, 0)),
                  pl.BlockSpec((1, tq, LANE), lambda i, g, q: (i, q, COL_GNSA // LANE)),
                  pl.BlockSpec((1, 1, 1, HEAD_DIM, m_rows), lambda i, g, q: (i, g, q, 0, 0))],
        out_specs=pl.BlockSpec((1, tq, KV_WIDTH), lambda i, g, q: (i, q, g)),
        out_shape=jax.ShapeDtypeStruct((b, s, ATTN_Q_WIDTH), ACT_DTYPE),
        scratch_shapes=[pltpu.VMEM((1, m_rows), jnp.float32),
                        pltpu.VMEM((1, m_rows), jnp.float32),
                        pltpu.VMEM((HEAD_DIM, m_rows), jnp.float32),
                        pltpu.VMEM((1, m_rows), jnp.float32),
                        pltpu.VMEM((HEAD_DIM, m_rows), jnp.float32)],
        compiler_params=_cparams(3),
        name="flash",
    )(proj3, cos_q, sin_q, bias, kaug, vst, kwin, vwt, proj3, ocmp)


def _merge_kernel(u_ref, halo_ref, gp_ref, ga_ref, attn_ref, x_ref, pw_ref, ps_ref,
                  wpp_ref, wpa_ref, wo_ref, o_ref, *, seq_len):
    tm = u_ref.shape[0]
    t0 = (pl.program_id(0) * tm) & (seq_len - 1)
    u = u_ref[...].astype(jnp.float32)
    halo = jnp.where(t0 > 0, halo_ref[...].astype(jnp.float32), 0.0)
    ext = jnp.concatenate([halo, u], axis=0)
    t = t0 + lax.broadcasted_iota(jnp.int32, (tm, POOL_GROUP), 0)
    pooled = []
    for gi, w in enumerate(POOL_WINDOWS):
        sl = slice(gi * POOL_GROUP, (gi + 1) * POOL_GROUP)
        acc = ext[:, sl]
        span = 1
        while span < w:
            acc = acc + pltpu.roll(acc, span, 0)
            span *= 2
        cnt = jnp.minimum(t + 1, w).astype(jnp.float32)
        mean = acc[POOL_HALO:, :] / cnt
        mixed = jnp.dot((mean - u[:, sl]).astype(MXU_DTYPE), pw_ref[gi],
                        preferred_element_type=jnp.float32)
        pooled.append(mixed * ps_ref[:, sl])
    pool = jnp.concatenate(pooled, axis=1).astype(MXU_DTYPE)
    pp = jnp.dot(pool, wpp_ref[...], preferred_element_type=jnp.float32)
    pa = jnp.dot(attn_ref[...], wpa_ref[...], preferred_element_type=jnp.float32)
    merged = (_sigmoid(gp_ref[...].astype(jnp.float32)) * pp
              + _sigmoid(ga_ref[...].astype(jnp.float32)) * pa)
    o_ref[...] = x_ref[...] + jnp.dot(merged.astype(MXU_DTYPE), wo_ref[...],
                                      preferred_element_type=jnp.float32)


def _merge(proj2, attn2, x2d, pool_w, pool_scale, wpp, wpa, wo, seq_len):
    m, d = x2d.shape
    tm = min(ROW_TILE, seq_len)
    halo_per_tile = tm // POOL_HALO
    return pl.pallas_call(
        functools.partial(_merge_kernel, seq_len=seq_len),
        grid=(m // tm,),
        in_specs=[pl.BlockSpec((tm, POOL_WIDTH), lambda i: (i, COL_POOL // POOL_WIDTH)),
                  pl.BlockSpec((POOL_HALO, POOL_WIDTH),
                               lambda i: (jnp.maximum(i * halo_per_tile - 1, 0), COL_POOL // POOL_WIDTH)),
                  pl.BlockSpec((tm, d), lambda i: (i, COL_GPOOL // D_MODEL)),
                  pl.BlockSpec((tm, d), lambda i: (i, COL_GATTN // D_MODEL)),
                  pl.BlockSpec((tm, ATTN_Q_WIDTH), lambda i: (i, 0)),
                  pl.BlockSpec((tm, d), lambda i: (i, 0)),
                  _const_spec(pool_w.shape), _const_spec(pool_scale.shape),
                  _const_spec(wpp.shape), _const_spec(wpa.shape), _const_spec(wo.shape)],
        out_specs=pl.BlockSpec((tm, d), lambda i: (i, 0)),
        out_shape=jax.ShapeDtypeStruct((m, d), jnp.float32),
        compiler_params=_cparams(1),
        name="merge",
    )(proj2, proj2, proj2, proj2, attn2, x2d, pool_w, pool_scale, wpp, wpa, wo)


def _ffn_kernel(x_ref, n2_ref, wg_ref, wu_ref, wd_ref, nf_ref, o_ref):
    x = x_ref[...]
    h = _rms(x, n2_ref[...]).astype(MXU_DTYPE)
    d_ff = wg_ref.shape[1]
    acc = x
    for c0 in range(0, d_ff, FF_CHUNK):
        cw = min(FF_CHUNK, d_ff - c0)
        gate = jnp.dot(h, wg_ref[:, c0:c0 + cw], preferred_element_type=jnp.float32)
        up = jnp.dot(h, wu_ref[:, c0:c0 + cw], preferred_element_type=jnp.float32)
        act = (gate * _sigmoid(gate) * up).astype(MXU_DTYPE)
        acc = acc + jnp.dot(act, wd_ref[c0:c0 + cw, :], preferred_element_type=jnp.float32)
    o_ref[...] = _rms(acc, nf_ref[...])


def _ffn(x1, norm2_w, wg, wu, wd, norm_f_w):
    m, d = x1.shape
    return pl.pallas_call(
        _ffn_kernel,
        grid=(m // ROW_TILE,),
        in_specs=[pl.BlockSpec((ROW_TILE, d), lambda i: (i, 0)),
                  _const_spec(norm2_w.shape), _const_spec(wg.shape), _const_spec(wu.shape),
                  _const_spec(wd.shape), _const_spec(norm_f_w.shape)],
        out_specs=pl.BlockSpec((ROW_TILE, d), lambda i: (i, 0)),
        out_shape=jax.ShapeDtypeStruct((m, d), jnp.float32),
        compiler_params=_cparams(1),
        name="ffn",
    )(x1, norm2_w, wg, wu, wd, norm_f_w)


def _rope_tables(seq_len):
    inv_freq = 1.0 / (ROPE_THETA ** (jnp.arange(0, ROPE_DIM, 2, dtype=jnp.float32) / ROPE_DIM))
    ang = jnp.arange(seq_len).astype(jnp.float32)[:, None] * inv_freq[None, :]
    cos, sin = jnp.cos(ang), jnp.sin(ang)
    rest = HEAD_DIM - ROPE_DIM
    cos_h = jnp.concatenate([cos, cos, jnp.ones((seq_len, rest), jnp.float32)], axis=1)
    sin_h = jnp.concatenate([sin, sin, jnp.zeros((seq_len, rest), jnp.float32)], axis=1)
    return cos.T, sin.T, jnp.tile(cos_h, (1, N_KV_GROUPS)), jnp.tile(sin_h, (1, N_KV_GROUPS))


def _rotate_half_matrix():
    rot = np.zeros((KV_WIDTH, KV_WIDTH), np.float32)
    for l in range(KV_WIDTH):
        d = l % HEAD_DIM
        if d < ROPE_HALF:
            rot[l + ROPE_HALF, l] = -1.0
        elif d < ROPE_DIM:
            rot[l - ROPE_HALF, l] = 1.0
    return jnp.asarray(rot, MXU_DTYPE)


def _layer(x, norm1_w, w_in, pool_w, pool_scale, cmp_pe_k, cmp_w1_k, cmp_b1_k, cmp_w2_k,
           cmp_pe_v, cmp_w1_v, cmp_b1_v, cmp_w2_v, w_proj_pool, w_proj_attn, w_out):
    b, s, d = x.shape
    m = b * s
    n_blk = s // SLC_BLOCK
    assert s % Q_TILE == 0 and s % ROW_TILE == 0 and (s & (s - 1)) == 0, s
    cd = MXU_DTYPE

    g0 = COL_GPOOL
    w_r = jnp.concatenate([w_in[:, :g0], w_in[:, g0 + GATE_NSA:], w_in[:, g0:g0 + GATE_NSA],
                           jnp.zeros((d, LANE - GATE_NSA), w_in.dtype)], axis=1).astype(cd)
    x2d = x.reshape(m, d)
    proj2, kvc = _inproj(x2d, norm1_w.reshape(1, d), w_r)
    proj3 = proj2.reshape(b, s, PROJ_WIDTH)

    def pe8(pe):
        return jnp.broadcast_to(pe.reshape(1, -1), (8, pe.size)).astype(cd)

    kvc_rows = kvc.reshape(2 * N_KV_GROUPS, b, s // CMP_STRIDE, CMP_STRIDE * HEAD_DIM)
    kc_r, vc_rt = _compress(kvc_rows,
                            pe8(cmp_pe_k), cmp_w1_k.astype(cd), cmp_b1_k.reshape(1, -1), cmp_w2_k.astype(cd),
                            pe8(cmp_pe_v), cmp_w1_v.astype(cd), cmp_b1_v.reshape(1, -1), cmp_w2_v.astype(cd))

    cos_q, sin_q, cos_k, sin_k = _rope_tables(s)
    kaug, vst, kwin, vwt = _kprep(proj3, cos_k, sin_k, _rotate_half_matrix())
    ocmp, bias = _cmpsel(proj3, kc_r, vc_rt)
    attn = _flash(proj3, cos_q, sin_q, bias, kaug, vst, kwin, vwt, ocmp)
    return _merge(proj2, attn.reshape(m, ATTN_Q_WIDTH), x2d, pool_w.astype(cd),
                  pool_scale.reshape(1, -1), w_proj_pool.astype(cd), w_proj_attn.astype(cd),
                  w_out.astype(cd), s)


def kernel(x, norm1_w, w_in, pool_w, pool_scale, cmp_pe_k, cmp_w1_k, cmp_b1_k, cmp_w2_k, cmp_pe_v,
           cmp_w1_v, cmp_b1_v, cmp_w2_v, w_proj_pool, w_proj_attn, w_out, norm2_w, w_ffn_gate,
           w_ffn_up, w_ffn_down, norm_f_w):
    b, s, d = x.shape
    depth = w_in.shape[0]
    assert depth == 1, "the final norm is fused into the last layer's FFN kernel"
    cd = MXU_DTYPE
    x1 = _layer(x, norm1_w[0], w_in[0], pool_w[0], pool_scale[0], cmp_pe_k[0], cmp_w1_k[0], cmp_b1_k[0],
                cmp_w2_k[0], cmp_pe_v[0], cmp_w1_v[0], cmp_b1_v[0], cmp_w2_v[0], w_proj_pool[0],
                w_proj_attn[0], w_out[0])
    out = _ffn(x1, norm2_w[0].reshape(1, d), w_ffn_gate[0].astype(cd), w_ffn_up[0].astype(cd),
               w_ffn_down[0].astype(cd), norm_f_w.reshape(1, d))
    return out.reshape(b, s, d)
```

```python
import functools

import numpy as np
import jax
import jax.numpy as jnp
from jax import lax
from jax.experimental import pallas as pl
from jax.experimental.pallas import tpu as pltpu

D_MODEL = 1024
N_HEADS = 16
HEAD_DIM = 64
N_KV_GROUPS = 4
HEADS_PER_GROUP = N_HEADS // N_KV_GROUPS
ROPE_DIM = HEAD_DIM // 4
ROPE_HALF = ROPE_DIM // 2
ROPE_THETA = 500000.0
CMP_BLOCK = 32
CMP_STRIDE = 16
CMP_PER_SLC = 4
SLC_BLOCK = 64
SLC_TOPK = 16
WINDOW = 512
ATTN_Q_WIDTH = N_HEADS * HEAD_DIM
KV_WIDTH = N_KV_GROUPS * HEAD_DIM
POOL_WIDTH = D_MODEL // 2
POOL_WINDOWS = (2, 4, 8, 16)
POOL_GROUP = POOL_WIDTH // len(POOL_WINDOWS)
POOL_HALO = 16
RMS_EPS = 1e-6

LANE = 128
MXU_DTYPE = jnp.bfloat16
ACT_DTYPE = jnp.bfloat16
VMEM_LIMIT_BYTES = 56 * 1024 * 1024

GATE_NSA = N_HEADS * 3
COL_POOL = 0
COL_Q = COL_POOL + POOL_WIDTH
COL_KV = COL_Q + ATTN_Q_WIDTH
COL_GPOOL = COL_KV + 6 * KV_WIDTH
COL_GATTN = COL_GPOOL + D_MODEL
COL_GNSA = COL_GATTN + D_MODEL
PROJ_WIDTH = COL_GNSA + LANE
GATE_ROWS = 16

ROW_TILE = 512
Q_TILE = 256
SLC_SPAN = 2048
SLC_SUBTILE = 512
KPREP_TILE = 1024
FF_CHUNK = 512
SCORE_SCALE = HEAD_DIM ** -0.5
LOG2E = float(np.log2(np.e))
FINITE_LIMIT = 3e38
MASK_BIAS = float(2 ** 30)
NEG_BIG = -1e30
IMP_BIG = 1e30
CHOSEN = -3e38
CHOSEN_LIMIT = -2e38
N_FORCED = 3


def _log2(n):
    k = int(n).bit_length() - 1
    assert 1 << k == n, n
    return k


def _cparams(n_grid):
    return pltpu.CompilerParams(dimension_semantics=("arbitrary",) * n_grid,
                                vmem_limit_bytes=VMEM_LIMIT_BYTES)


def _const_spec(shape):
    nd = len(shape)
    return pl.BlockSpec(shape, lambda *_: (0,) * nd, pipeline_mode=pl.Buffered(1))


def _sigmoid(x):
    return 1.0 / (1.0 + jnp.exp(-x))


def _not_finite(x):
    return jnp.logical_not(jnp.sum(x) < FINITE_LIMIT)


def _rms(x, w):
    var = jnp.mean(x * x, axis=-1, keepdims=True)
    return x * lax.rsqrt(var + RMS_EPS) * w


def _inproj_kernel(x_ref, nw_ref, w_ref, o_ref, kvc_ref):
    h = _rms(x_ref[...], nw_ref[...]).astype(MXU_DTYPE)
    n = o_ref.shape[1]
    for c0 in range(0, n, KV_WIDTH):
        cw = min(KV_WIDTH, n - c0)
        res = jnp.dot(h, w_ref[:, c0:c0 + cw], preferred_element_type=jnp.float32).astype(o_ref.dtype)
        o_ref[:, c0:c0 + cw] = res
        which = (c0 - COL_KV) // KV_WIDTH
        if c0 >= COL_KV and which < 2:
            for g in range(N_KV_GROUPS):
                kvc_ref[which * N_KV_GROUPS + g] = res[:, g * HEAD_DIM:(g + 1) * HEAD_DIM]


def _inproj(x2d, norm_w, w_r):
    m, d = x2d.shape
    n = w_r.shape[1]
    return pl.pallas_call(
        _inproj_kernel,
        grid=(m // ROW_TILE,),
        in_specs=[pl.BlockSpec((ROW_TILE, d), lambda i: (i, 0)),
                  _const_spec((1, d)),
                  _const_spec((d, n))],
        out_specs=[pl.BlockSpec((ROW_TILE, n), lambda i: (i, 0)),
                   pl.BlockSpec((2 * N_KV_GROUPS, ROW_TILE, HEAD_DIM), lambda i: (0, i, 0))],
        out_shape=[jax.ShapeDtypeStruct((m, n), ACT_DTYPE),
                   jax.ShapeDtypeStruct((2 * N_KV_GROUPS, m, HEAD_DIM), ACT_DTYPE)],
        compiler_params=_cparams(1),
        name="inproj",
    )(x2d, norm_w, w_r)


def _gelu_tanh(x):
    return 0.5 * x * (1.0 + jnp.tanh(np.sqrt(2.0 / np.pi) * (x + 0.044715 * (x * x * x))))


def _compress_one(r_ref, pe_ref, w1_ref, b1_ref, w2_ref):
    half = w1_ref.shape[0] // 2
    n_blk = r_ref.shape[2]
    rows = [r_ref[0, 0, :, r * half:(r + 1) * half] for r in range(CMP_PER_SLC)]
    top = [jnp.dot(x, w1_ref[0:half, :], preferred_element_type=jnp.float32) for x in rows]
    bot = [jnp.dot(x, w1_ref[half:, :], preferred_element_type=jnp.float32) for x in rows]
    bias = jnp.dot(pe_ref[...], w1_ref[...], preferred_element_type=jnp.float32)[0:1, :] + b1_ref[...]
    nxt = bot[1:] + [pltpu.roll(bot[0], n_blk - 1, 0)]
    hid = jnp.concatenate([top[r] + nxt[r] for r in range(CMP_PER_SLC)], axis=0) + bias
    return jnp.dot(_gelu_tanh(hid).astype(MXU_DTYPE), w2_ref[...], preferred_element_type=jnp.float32)


def _compress_kernel(rk_ref, rv_ref, pek_ref, w1k_ref, b1k_ref, w2k_ref,
                     pev_ref, w1v_ref, b1v_ref, w2v_ref, ok_ref, ovt_ref):
    kc = _compress_one(rk_ref, pek_ref, w1k_ref, b1k_ref, w2k_ref)
    vc = _compress_one(rv_ref, pev_ref, w1v_ref, b1v_ref, w2v_ref)
    ok_ref[0, 0] = (kc * (SCORE_SCALE * LOG2E)).astype(ok_ref.dtype)
    ovt_ref[0, 0] = jnp.concatenate([kc, vc], axis=1).T[HEAD_DIM:, :].astype(ovt_ref.dtype)


def _compress(kvc_rows, pek, w1k, b1k, w2k, pev, w1v, b1v, w2v):
    _, b, n_blk, width = kvc_rows.shape
    dk = w2k.shape[1]
    n_rows = n_blk * CMP_PER_SLC
    w_specs = [_const_spec(a.shape) for a in (pek, w1k, b1k, w2k)]
    return pl.pallas_call(
        _compress_kernel,
        grid=(b, N_KV_GROUPS),
        in_specs=[pl.BlockSpec((1, 1, n_blk, width), lambda i, j: (j, i, 0, 0)),
                  pl.BlockSpec((1, 1, n_blk, width), lambda i, j: (N_KV_GROUPS + j, i, 0, 0))]
        + w_specs + w_specs,
        out_specs=[pl.BlockSpec((1, 1, n_rows, dk), lambda i, j: (i, j, 0, 0)),
                   pl.BlockSpec((1, 1, dk, n_rows), lambda i, j: (i, j, 0, 0))],
        out_shape=[jax.ShapeDtypeStruct((b, N_KV_GROUPS, n_rows, dk), MXU_DTYPE),
                   jax.ShapeDtypeStruct((b, N_KV_GROUPS, dk, n_rows), MXU_DTYPE)],
        compiler_params=_cparams(2),
        name="compress",
    )(kvc_rows, kvc_rows, pek, w1k, b1k, w2k, pev, w1v, b1v, w2v)


def _kprep_kernel(ks_ref, vs_ref, kw_ref, vw_ref, cos_ref, sin_ref, rot_ref,
                  kaug_ref, vst_ref, kwin_ref, vwt_ref):
    ts = ks_ref.shape[1]
    t0 = pl.program_id(1) * ts
    cos = cos_ref[...]
    sin = sin_ref[...]
    rot = rot_ref[...]

    def rope(k_ref):
        k = k_ref[0]
        krot = jnp.dot(k.astype(MXU_DTYPE), rot, preferred_element_type=jnp.float32)
        return (k.astype(jnp.float32) * cos + krot * sin) * (SCORE_SCALE * LOG2E)

    n_blk = kaug_ref.shape[3] - 2 * HEAD_DIM
    blk = lax.broadcasted_iota(jnp.int32, (ts, n_blk), 1)
    key_blk = (t0 + lax.broadcasted_iota(jnp.int32, (ts, n_blk), 0)) >> _log2(SLC_BLOCK)
    onehot = jnp.where(blk == key_blk, 1.0, 0.0).astype(kaug_ref.dtype)
    ks = rope(ks_ref).astype(kaug_ref.dtype)
    kw = rope(kw_ref).astype(kwin_ref.dtype)
    vst = vs_ref[0].astype(jnp.float32).T.astype(vst_ref.dtype)
    vwt = vw_ref[0].astype(jnp.float32).T.astype(vwt_ref.dtype)
    ones_col = jnp.where(lax.broadcasted_iota(jnp.int32, (ts, HEAD_DIM), 1) == 0, 1.0, 0.0
                         ).astype(kaug_ref.dtype)
    for g in range(N_KV_GROUPS):
        sl = slice(g * HEAD_DIM, (g + 1) * HEAD_DIM)
        kaug_ref[0, g, :, 0:n_blk] = onehot
        kaug_ref[0, g, :, n_blk:n_blk + HEAD_DIM] = ks[:, sl]
        kaug_ref[0, g, :, n_blk + HEAD_DIM:] = ones_col
        kwin_ref[0, g, :, 0:HEAD_DIM] = kw[:, sl]
        kwin_ref[0, g, :, HEAD_DIM:] = ones_col
        vst_ref[0, g] = vst[sl, :]
        vwt_ref[0, g] = vwt[sl, :]


def _kprep(proj3, cos_k, sin_k, rot):
    b, s, _ = proj3.shape
    n_blk = s // SLC_BLOCK
    ts = min(KPREP_TILE, s)
    kvb = COL_KV // KV_WIDTH

    def col(j):
        return pl.BlockSpec((1, ts, KV_WIDTH), lambda i, t: (i, t, kvb + j))

    tab = pl.BlockSpec((ts, KV_WIDTH), lambda i, t: (t, 0))
    aug_w = n_blk + 2 * HEAD_DIM
    return pl.pallas_call(
        _kprep_kernel,
        grid=(b, s // ts),
        in_specs=[col(2), col(3), col(4), col(5), tab, tab, _const_spec(rot.shape)],
        out_specs=[pl.BlockSpec((1, N_KV_GROUPS, ts, aug_w), lambda i, t: (i, 0, t, 0)),
                   pl.BlockSpec((1, N_KV_GROUPS, HEAD_DIM, ts), lambda i, t: (i, 0, 0, t)),
                   pl.BlockSpec((1, N_KV_GROUPS, ts, 2 * HEAD_DIM), lambda i, t: (i, 0, t, 0)),
                   pl.BlockSpec((1, N_KV_GROUPS, HEAD_DIM, ts), lambda i, t: (i, 0, 0, t))],
        out_shape=[jax.ShapeDtypeStruct((b, N_KV_GROUPS, s, aug_w), MXU_DTYPE),
                   jax.ShapeDtypeStruct((b, N_KV_GROUPS, HEAD_DIM, s), MXU_DTYPE),
                   jax.ShapeDtypeStruct((b, N_KV_GROUPS, s, 2 * HEAD_DIM), MXU_DTYPE),
                   jax.ShapeDtypeStruct((b, N_KV_GROUPS, HEAD_DIM, s), MXU_DTYPE)],
        compiler_params=_cparams(2),
        name="kprep",
    )(proj3, proj3, proj3, proj3, cos_k, sin_k, rot)


def _heads_to_lanes(xt):
    return jnp.concatenate([xt[h * HEAD_DIM:(h + 1) * HEAD_DIM, :] for h in range(HEADS_PER_GROUP)],
                           axis=1)


def _cmpsel_kernel(q_ref, kc_ref, vct_ref, ocmp_ref, bias_ref):
    tq = q_ref.shape[1]
    n_cmp_pad = kc_ref.shape[2]
    n_blk = bias_ref.shape[2]
    m_rows = HEADS_PER_GROUP * tq
    t0 = pl.program_id(1) * tq

    row = lax.broadcasted_iota(jnp.int32, (n_cmp_pad, tq), 0)
    n_idx = (row & (n_blk - 1)) * CMP_PER_SLC + (row >> _log2(n_blk))
    t = t0 + lax.broadcasted_iota(jnp.int32, (n_cmp_pad, tq), 1)
    vbias = jnp.where(n_idx * CMP_STRIDE + (CMP_BLOCK - 1) <= t, 0.0, NEG_BIG)
    vbias = jnp.concatenate([vbias] * HEADS_PER_GROUP, axis=1)
    t_row = t0 + (lax.broadcasted_iota(jnp.int32, (1, m_rows), 1) & (tq - 1))
    jrow = lax.broadcasted_iota(jnp.int32, (n_blk, tq), 0)
    jt = (t0 + lax.broadcasted_iota(jnp.int32, (n_blk, tq), 1)) >> _log2(SLC_BLOCK)
    forced = (jrow == 0) | (jrow == jt) | (jrow == jt - 1)
    jf = jrow.astype(jnp.float32)

    for pb in range(q_ref.shape[0]):
        qt = _heads_to_lanes(q_ref[pb].astype(jnp.float32).T).astype(MXU_DTYPE)
        s = jnp.dot(kc_ref[pb, 0], qt, preferred_element_type=jnp.float32) + vbias
        mx = jnp.max(s, axis=0, keepdims=True)
        p = jnp.exp2(s - mx)
        den = jnp.sum(p, axis=0, keepdims=True)
        inv = jnp.where(t_row >= CMP_BLOCK - 1, 1.0 / den, 0.0)
        ocmp_ref[pb, 0, 0] = inv * jnp.dot(vct_ref[pb, 0], p.astype(MXU_DTYPE),
                                           preferred_element_type=jnp.float32)

        imp = None
        for h in range(HEADS_PER_GROUP):
            ph = p[:, h * tq:(h + 1) * tq]
            last = ph[3 * n_blk:4 * n_blk, :]
            prev = jnp.where(jrow == 0, 0.0, pltpu.roll(last, 1, 0))
            taps = ph[0:n_blk, :] + ph[n_blk:2 * n_blk, :] + ph[2 * n_blk:3 * n_blk, :] + last + prev
            imp_h = taps * inv[:, h * tq:(h + 1) * tq]
            imp = imp_h if imp is None else imp + imp_h

        val = jnp.where(forced, CHOSEN, jnp.where(jrow <= jt, imp, -IMP_BIG))
        for _ in range(min(SLC_TOPK, n_blk) - N_FORCED):
            top = jnp.max(val, axis=0, keepdims=True)
            first = jnp.min(jnp.where(val == top, jf, float(n_blk)), axis=0, keepdims=True)
            val = jnp.where(jf == first, CHOSEN, val)
        bias_ref[pb, 0] = jnp.where(val < CHOSEN_LIMIT, 0.0, -MASK_BIAS).astype(bias_ref.dtype)


def _cmpsel(proj3, kc_r, vc_rt):
    b, s, _ = proj3.shape
    tq = min(Q_TILE, s)
    nq = s // tq
    n_cmp_pad = kc_r.shape[2]
    n_blk = s // SLC_BLOCK
    qb = COL_Q // KV_WIDTH
    m_rows = HEADS_PER_GROUP * tq
    return pl.pallas_call(
        _cmpsel_kernel,
        grid=(N_KV_GROUPS, nq),
        in_specs=[pl.BlockSpec((b, tq, KV_WIDTH), lambda g, q: (0, q, qb + g)),
                  pl.BlockSpec((b, 1, n_cmp_pad, HEAD_DIM), lambda g, q: (0, g, 0, 0)),
                  pl.BlockSpec((b, 1, HEAD_DIM, n_cmp_pad), lambda g, q: (0, g, 0, 0))],
        out_specs=[pl.BlockSpec((b, 1, 1, HEAD_DIM, m_rows), lambda g, q: (0, g, q, 0, 0)),
                   pl.BlockSpec((b, 1, n_blk, tq), lambda g, q: (0, g, 0, q))],
        out_shape=[jax.ShapeDtypeStruct((b, N_KV_GROUPS, nq, HEAD_DIM, m_rows), jnp.float32),
                   jax.ShapeDtypeStruct((b, N_KV_GROUPS, n_blk, s), MXU_DTYPE)],
        compiler_params=_cparams(2),
        name="cmpsel",
    )(proj3, kc_r, vc_rt)


def _flash_kernel(q_ref, cos_ref, sin_ref, bias_ref, kaug_ref, vst_ref, kwin_ref, vwt_ref,
                  gate_ref, ocmp_ref, o_ref, m_ref, l_ref, acc_ref, lw_ref, accw_ref):
    nb = q_ref.shape[0]
    tq = q_ref.shape[1]
    m_rows = HEADS_PER_GROUP * tq
    grp = pl.program_id(0)
    qi = pl.program_id(1)
    t0 = qi * tq
    kd = pl.multiple_of(t0, tq)
    probs = range(nb)

    cos = cos_ref[...]
    sin = sin_ref[...]
    qr, bias4 = [], []
    for pb in probs:
        qt = q_ref[pb].astype(jnp.float32).T
        parts = []
        for h in range(HEADS_PER_GROUP):
            qh = qt[h * HEAD_DIM:(h + 1) * HEAD_DIM, :]
            x1 = qh[0:ROPE_HALF, :]
            x2 = qh[ROPE_HALF:ROPE_DIM, :]
            parts.append(jnp.concatenate([x1 * cos - x2 * sin, x2 * cos + x1 * sin, qh[ROPE_DIM:, :]],
                                         axis=0))
        qr.append(jnp.concatenate(parts, axis=1).astype(MXU_DTYPE))
        bias4.append(jnp.concatenate([bias_ref[pb, 0]] * HEADS_PER_GROUP, axis=1))
    zpad = jnp.zeros((HEAD_DIM, m_rows), MXU_DTYPE)
    pad_row = lax.broadcasted_iota(jnp.int32, (HEAD_DIM, m_rows), 0)

    def stab_rows(neg_m):
        return jnp.where(pad_row == 0, neg_m, 0.0).astype(MXU_DTYPE)

    tcol = t0 + (lax.broadcasted_iota(jnp.int32, (tq, m_rows), 1) & (tq - 1))
    koff = lax.broadcasted_iota(jnp.int32, (tq, m_rows), 0)
    n_back = WINDOW // tq

    def win_mask(back):
        diff = tcol - (t0 - back * tq + koff)
        if back == 0:
            return diff >= 0
        if back * tq + tq - 1 < WINDOW:
            return None
        return diff < WINDOW

    def fast_first(pb, k_ref, v_ref, q0, mask, l_out, acc_out):
        s = jnp.dot(k_ref[pb, 0, pl.ds(kd, tq), :], q0, preferred_element_type=jnp.float32)
        s = jnp.where(mask, s, NEG_BIG)
        neg_m = -jnp.max(s, axis=0, keepdims=True).astype(MXU_DTYPE).astype(jnp.float32)
        p = jnp.exp2(s + neg_m)
        l_out[pb] = jnp.sum(p, axis=0, keepdims=True)
        acc_out[pb] = jnp.dot(v_ref[pb, 0, :, pl.ds(kd, tq)], p.astype(MXU_DTYPE),
                              preferred_element_type=jnp.float32)
        return neg_m

    def fast_tiles(k_ref, v_ref, qs, tiles, l_out, acc_out):
        for pb in probs:
            l_add = acc_add = None
            for k0, tk, mask in tiles:
                s = jnp.dot(k_ref[pb, 0, pl.ds(k0, tk), :], qs[pb], preferred_element_type=jnp.float32)
                if mask is not None:
                    s = jnp.where(mask, s, NEG_BIG)
                p = jnp.exp2(s)
                l_t = jnp.sum(p, axis=0, keepdims=True)
                acc_t = jnp.dot(v_ref[pb, 0, :, pl.ds(k0, tk)], p.astype(MXU_DTYPE),
                                preferred_element_type=jnp.float32)
                l_add = l_t if l_add is None else l_add + l_t
                acc_add = acc_t if acc_add is None else acc_add + acc_t
            l_out[pb] += l_add
            acc_out[pb] += acc_add

    causal = t0 + koff <= tcol
    qs_slc, qs_win = [], []
    for pb in probs:
        neg_m = fast_first(pb, kaug_ref, vst_ref, jnp.concatenate([bias4[pb], qr[pb], zpad], axis=0),
                           causal, l_ref, acc_ref)
        neg_mw = fast_first(pb, kwin_ref, vwt_ref, jnp.concatenate([qr[pb], zpad], axis=0), win_mask(0),
                            lw_ref, accw_ref)
        qs_slc.append(jnp.concatenate([bias4[pb], qr[pb], stab_rows(neg_m)], axis=0))
        qs_win.append(jnp.concatenate([qr[pb], stab_rows(neg_mw)], axis=0))

    sub = SLC_SUBTILE
    n_sub = SLC_SPAN // sub

    def span_body(i, carry):
        base = i * SLC_SPAN
        fast_tiles(kaug_ref, vst_ref, qs_slc,
                   [(pl.multiple_of(base + j * sub, sub), sub, None) for j in range(n_sub)],
                   l_ref, acc_ref)
        return carry

    n_span = t0 // SLC_SPAN
    lax.fori_loop(0, n_span, span_body, 0)
    done = n_span * SLC_SPAN
    size = SLC_SPAN // 2
    while size >= tq:
        take = ((t0 // size) % 2) == 1

        @pl.when(take)
        def _(done=done, size=size):
            piece = min(size, sub)
            fast_tiles(kaug_ref, vst_ref, qs_slc,
                       [(pl.multiple_of(done + j * piece, piece), piece, None) for j in range(size // piece)],
                       l_ref, acc_ref)

        done = done + jnp.where(take, size, 0)
        size //= 2

    def win_tiles(n):
        return [(pl.multiple_of(t0 - back * tq, tq), tq, win_mask(back)) for back in range(1, n + 1)]

    @pl.when(qi >= n_back)
    def _():
        fast_tiles(kwin_ref, vwt_ref, qs_win, win_tiles(n_back), lw_ref, accw_ref)

    for n in range(1, n_back):
        @pl.when(qi == n)
        def _(n=n):
            fast_tiles(kwin_ref, vwt_ref, qs_win, win_tiles(n), lw_ref, accw_ref)

    def exact_start(pb, s, vt, l_out, acc_out):
        mx = jnp.max(s, axis=0, keepdims=True)
        p = jnp.exp2(s - mx)
        m_ref[pb] = mx
        l_out[pb] = jnp.sum(p, axis=0, keepdims=True)
        acc_out[pb] = jnp.dot(vt, p.astype(MXU_DTYPE), preferred_element_type=jnp.float32)

    def exact_update(pb, s, vt, l_out, acc_out):
        m_old = m_ref[pb]
        mx = jnp.maximum(m_old, jnp.max(s, axis=0, keepdims=True))
        alpha = jnp.exp2(m_old - mx)
        p = jnp.exp2(s - mx)
        m_ref[pb] = mx
        l_out[pb] = alpha * l_out[pb] + jnp.sum(p, axis=0, keepdims=True)
        acc_out[pb] = alpha * acc_out[pb] + jnp.dot(vt, p.astype(MXU_DTYPE),
                                                    preferred_element_type=jnp.float32)

    check = None
    for pb in probs:
        c = (l_ref[pb] + lw_ref[pb] + jnp.sum(jnp.abs(acc_ref[pb]), axis=0, keepdims=True)
             + jnp.sum(jnp.abs(accw_ref[pb]), axis=0, keepdims=True))
        check = c if check is None else check + c

    @pl.when(_not_finite(check))
    def _():
        for pb in probs:
            q0 = jnp.concatenate([bias4[pb], qr[pb], zpad], axis=0)
            s = jnp.dot(kaug_ref[pb, 0, pl.ds(kd, tq), :], q0, preferred_element_type=jnp.float32)
            exact_start(pb, jnp.where(causal, s, NEG_BIG), vst_ref[pb, 0, :, pl.ds(kd, tq)], l_ref, acc_ref)

            def body(kt, carry, pb=pb, q0=q0):
                k0 = pl.multiple_of(kt * tq, tq)
                sk = jnp.dot(kaug_ref[pb, 0, pl.ds(k0, tq), :], q0, preferred_element_type=jnp.float32)
                exact_update(pb, sk, vst_ref[pb, 0, :, pl.ds(k0, tq)], l_ref, acc_ref)
                return carry

            lax.fori_loop(0, qi, body, 0)
            qw0 = jnp.concatenate([qr[pb], zpad], axis=0)
            for back in range(n_back + 1):
                kbase = t0 - back * tq
                k0 = pl.multiple_of(jnp.maximum(kbase, 0), tq)
                sw = jnp.dot(kwin_ref[pb, 0, pl.ds(k0, tq), :], qw0, preferred_element_type=jnp.float32)
                diff = tcol - (kbase + koff)
                sw = jnp.where((diff >= 0) & (diff < WINDOW) & (kbase + koff >= 0), sw, NEG_BIG)
                if back == 0:
                    exact_start(pb, sw, vwt_ref[pb, 0, :, pl.ds(k0, tq)], lw_ref, accw_ref)
                else:
                    exact_update(pb, sw, vwt_ref[pb, 0, :, pl.ds(k0, tq)], lw_ref, accw_ref)

    r_id = lax.broadcasted_iota(jnp.int32, (GATE_ROWS, LANE), 0)
    c_id = lax.broadcasted_iota(jnp.int32, (GATE_ROWS, LANE), 1)
    pick = jnp.where((c_id == grp * (HEADS_PER_GROUP * 3) + r_id) & (r_id < HEADS_PER_GROUP * 3),
                     1.0, 0.0).astype(gate_ref.dtype)
    for pb in probs:
        o_slc = acc_ref[pb] / l_ref[pb]
        o_win = accw_ref[pb] / lw_ref[pb]
        graw = lax.dot_general(pick, gate_ref[pb], (((1,), (1,)), ((), ())),
                               preferred_element_type=jnp.float32)
        gates = _sigmoid(graw)
        o_cmp = ocmp_ref[pb, 0, 0]
        mixed = []
        for h in range(HEADS_PER_GROUP):
            sl = slice(h * tq, (h + 1) * tq)
            mixed.append(gates[3 * h:3 * h + 1, :] * o_cmp[:, sl]
                         + gates[3 * h + 1:3 * h + 2, :] * o_slc[:, sl]
                         + gates[3 * h + 2:3 * h + 3, :] * o_win[:, sl])
        o_ref[pb] = jnp.concatenate(mixed, axis=0).T.astype(o_ref.dtype)


def _flash(proj3, cos_q, sin_q, bias, kaug, vst, kwin, vwt, ocmp):
    b, s, _ = proj3.shape
    tq = min(Q_TILE, s)
    nq = s // tq
    n_blk = s // SLC_BLOCK
    aug_w = kaug.shape[3]
    qb = COL_Q // KV_WIDTH
    m_rows = HEADS_PER_GROUP * tq

    def resident(shape):
        return pl.BlockSpec((b, 1) + shape, lambda g, q: (0, g, 0, 0), pipeline_mode=pl.Buffered(1))

    return pl.pallas_call(
        _flash_kernel,
        grid=(N_KV_GROUPS, nq),
        in_specs=[pl.BlockSpec((b, tq, KV_WIDTH), lambda g, q: (0, q, qb + g)),
                  pl.BlockSpec((ROPE_HALF, tq), lambda g, q: (0, q)),
                  pl.BlockSpec((ROPE_HALF, tq), lambda g, q: (0, q)),
                  pl.BlockSpec((b, 1, n_blk, tq), lambda g, q: (0, g, 0, q)),
                  resident((s, aug_w)),
                  resident((HEAD_DIM, s)),
                  resident((s, 2 * HEAD_DIM)),
                  resident((HEAD_DIM, s)),
                  pl.BlockSpec((b, tq, LANE), lambda g, q: (0, q, COL_GNSA // LANE)),
                  pl.BlockSpec((b, 1, 1, HEAD_DIM, m_rows), lambda g, q: (0, g, q, 0, 0))],
        out_specs=pl.BlockSpec((b, tq, KV_WIDTH), lambda g, q: (0, q, g)),
        out_shape=jax.ShapeDtypeStruct((b, s, ATTN_Q_WIDTH), ACT_DTYPE),
        scratch_shapes=[pltpu.VMEM((b, 1, m_rows), jnp.float32),
                        pltpu.VMEM((b, 1, m_rows), jnp.float32),
                        pltpu.VMEM((b, HEAD_DIM, m_rows), jnp.float32),
                        pltpu.VMEM((b, 1, m_rows), jnp.float32),
                        pltpu.VMEM((b, HEAD_DIM, m_rows), jnp.float32)],
        compiler_params=_cparams(2),
        name="flash",
    )(proj3, cos_q, sin_q, bias, kaug, vst, kwin, vwt, proj3, ocmp)


def _merge_kernel(u_ref, halo_ref, gp_ref, ga_ref, attn_ref, x_ref, pw_ref, ps_ref,
                  wpp_ref, wpa_ref, wo_ref, o_ref, *, seq_len):
    tm = u_ref.shape[0]
    t0 = (pl.program_id(0) * tm) & (seq_len - 1)
    u = u_ref[...].astype(jnp.float32)
    halo = jnp.where(t0 > 0, halo_ref[...].astype(jnp.float32), 0.0)
    ext = jnp.concatenate([halo, u], axis=0)
    t = t0 + lax.broadcasted_iota(jnp.int32, (tm, POOL_GROUP), 0)
    pooled = []
    for gi, w in enumerate(POOL_WINDOWS):
        sl = slice(gi * POOL_GROUP, (gi + 1) * POOL_GROUP)
        acc = ext[:, sl]
        span = 1
        while span < w:
            acc = acc + pltpu.roll(acc, span, 0)
            span *= 2
        cnt = jnp.minimum(t + 1, w).astype(jnp.float32)
        mean = acc[POOL_HALO:, :] / cnt
        mixed = jnp.dot((mean - u[:, sl]).astype(MXU_DTYPE), pw_ref[gi],
                        preferred_element_type=jnp.float32)
        pooled.append(mixed * ps_ref[:, sl])
    pool = jnp.concatenate(pooled, axis=1).astype(MXU_DTYPE)
    pp = jnp.dot(pool, wpp_ref[...], preferred_element_type=jnp.float32)
    pa = jnp.dot(attn_ref[...], wpa_ref[...], preferred_element_type=jnp.float32)
    merged = (_sigmoid(gp_ref[...].astype(jnp.float32)) * pp
              + _sigmoid(ga_ref[...].astype(jnp.float32)) * pa)
    o_ref[...] = x_ref[...] + jnp.dot(merged.astype(MXU_DTYPE), wo_ref[...],
                                      preferred_element_type=jnp.float32)


def _merge(proj2, attn2, x2d, pool_w, pool_scale, wpp, wpa, wo, seq_len):
    m, d = x2d.shape
    tm = min(ROW_TILE, seq_len)
    halo_per_tile = tm // POOL_HALO
    return pl.pallas_call(
        functools.partial(_merge_kernel, seq_len=seq_len),
        grid=(m // tm,),
        in_specs=[pl.BlockSpec((tm, POOL_WIDTH), lambda i: (i, COL_POOL // POOL_WIDTH)),
                  pl.BlockSpec((POOL_HALO, POOL_WIDTH),
                               lambda i: (jnp.maximum(i * halo_per_tile - 1, 0), COL_POOL // POOL_WIDTH)),
                  pl.BlockSpec((tm, d), lambda i: (i, COL_GPOOL // D_MODEL)),
                  pl.BlockSpec((tm, d), lambda i: (i, COL_GATTN // D_MODEL)),
                  pl.BlockSpec((tm, ATTN_Q_WIDTH), lambda i: (i, 0)),
                  pl.BlockSpec((tm, d), lambda i: (i, 0)),
                  _const_spec(pool_w.shape), _const_spec(pool_scale.shape),
                  _const_spec(wpp.shape), _const_spec(wpa.shape), _const_spec(wo.shape)],
        out_specs=pl.BlockSpec((tm, d), lambda i: (i, 0)),
        out_shape=jax.ShapeDtypeStruct((m, d), jnp.float32),
        compiler_params=_cparams(1),
        name="merge",
    )(proj2, proj2, proj2, proj2, attn2, x2d, pool_w, pool_scale, wpp, wpa, wo)


def _ffn_kernel(x_ref, n2_ref, wg_ref, wu_ref, wd_ref, nf_ref, o_ref):
    x = x_ref[...]
    h = _rms(x, n2_ref[...]).astype(MXU_DTYPE)
    d_ff = wg_ref.shape[1]
    acc = x
    for c0 in range(0, d_ff, FF_CHUNK):
        cw = min(FF_CHUNK, d_ff - c0)
        gate = jnp.dot(h, wg_ref[:, c0:c0 + cw], preferred_element_type=jnp.float32)
        up = jnp.dot(h, wu_ref[:, c0:c0 + cw], preferred_element_type=jnp.float32)
        act = (gate * _sigmoid(gate) * up).astype(MXU_DTYPE)
        acc = acc + jnp.dot(act, wd_ref[c0:c0 + cw, :], preferred_element_type=jnp.float32)
    o_ref[...] = _rms(acc, nf_ref[...])


def _ffn(x1, norm2_w, wg, wu, wd, norm_f_w):
    m, d = x1.shape
    return pl.pallas_call(
        _ffn_kernel,
        grid=(m // ROW_TILE,),
        in_specs=[pl.BlockSpec((ROW_TILE, d), lambda i: (i, 0)),
                  _const_spec(norm2_w.shape), _const_spec(wg.shape), _const_spec(wu.shape),
                  _const_spec(wd.shape), _const_spec(norm_f_w.shape)],
        out_specs=pl.BlockSpec((ROW_TILE, d), lambda i: (i, 0)),
        out_shape=jax.ShapeDtypeStruct((m, d), jnp.float32),
        compiler_params=_cparams(1),
        name="ffn",
    )(x1, norm2_w, wg, wu, wd, norm_f_w)


def _rope_tables(seq_len):
    inv_freq = 1.0 / (ROPE_THETA ** (jnp.arange(0, ROPE_DIM, 2, dtype=jnp.float32) / ROPE_DIM))
    ang = jnp.arange(seq_len).astype(jnp.float32)[:, None] * inv_freq[None, :]
    cos, sin = jnp.cos(ang), jnp.sin(ang)
    rest = HEAD_DIM - ROPE_DIM
    cos_h = jnp.concatenate([cos, cos, jnp.ones((seq_len, rest), jnp.float32)], axis=1)
    sin_h = jnp.concatenate([sin, sin, jnp.zeros((seq_len, rest), jnp.float32)], axis=1)
    return cos.T, sin.T, jnp.tile(cos_h, (1, N_KV_GROUPS)), jnp.tile(sin_h, (1, N_KV_GROUPS))


def _rotate_half_matrix():
    rot = np.zeros((KV_WIDTH, KV_WIDTH), np.float32)
    for l in range(KV_WIDTH):
        d = l % HEAD_DIM
        if d < ROPE_HALF:
            rot[l + ROPE_HALF, l] = -1.0
        elif d < ROPE_DIM:
            rot[l - ROPE_HALF, l] = 1.0
    return jnp.asarray(rot, MXU_DTYPE)


def _layer(x, norm1_w, w_in, pool_w, pool_scale, cmp_pe_k, cmp_w1_k, cmp_b1_k, cmp_w2_k,
           cmp_pe_v, cmp_w1_v, cmp_b1_v, cmp_w2_v, w_proj_pool, w_proj_attn, w_out):
    b, s, d = x.shape
    m = b * s
    n_blk = s // SLC_BLOCK
    assert s % Q_TILE == 0 and s % ROW_TILE == 0 and (s & (s - 1)) == 0, s
    cd = MXU_DTYPE

    g0 = COL_GPOOL
    w_r = jnp.concatenate([w_in[:, :g0], w_in[:, g0 + GATE_NSA:], w_in[:, g0:g0 + GATE_NSA],
                           jnp.zeros((d, LANE - GATE_NSA), w_in.dtype)], axis=1).astype(cd)
    x2d = x.reshape(m, d)
    proj2, kvc = _inproj(x2d, norm1_w.reshape(1, d), w_r)
    proj3 = proj2.reshape(b, s, PROJ_WIDTH)

    def pe8(pe):
        return jnp.broadcast_to(pe.reshape(1, -1), (8, pe.size)).astype(cd)

    kvc_rows = kvc.reshape(2 * N_KV_GROUPS, b, n_blk, SLC_BLOCK * HEAD_DIM)
    kc_r, vc_rt = _compress(kvc_rows,
                            pe8(cmp_pe_k), cmp_w1_k.astype(cd), cmp_b1_k.reshape(1, -1), cmp_w2_k.astype(cd),
                            pe8(cmp_pe_v), cmp_w1_v.astype(cd), cmp_b1_v.reshape(1, -1), cmp_w2_v.astype(cd))

    cos_q, sin_q, cos_k, sin_k = _rope_tables(s)
    kaug, vst, kwin, vwt = _kprep(proj3, cos_k, sin_k, _rotate_half_matrix())
    ocmp, bias = _cmpsel(proj3, kc_r, vc_rt)
    attn = _flash(proj3, cos_q, sin_q, bias, kaug, vst, kwin, vwt, ocmp)
    return _merge(proj2, attn.reshape(m, ATTN_Q_WIDTH), x2d, pool_w.astype(cd),
                  pool_scale.reshape(1, -1), w_proj_pool.astype(cd), w_proj_attn.astype(cd),
                  w_out.astype(cd), s)


def kernel(x, norm1_w, w_in, pool_w, pool_scale, cmp_pe_k, cmp_w1_k, cmp_b1_k, cmp_w2_k, cmp_pe_v,
           cmp_w1_v, cmp_b1_v, cmp_w2_v, w_proj_pool, w_proj_attn, w_out, norm2_w, w_ffn_gate,
           w_ffn_up, w_ffn_down, norm_f_w):
    b, s, d = x.shape
    depth = w_in.shape[0]
    assert depth == 1, "the final norm is fused into the last layer's FFN kernel"
    cd = MXU_DTYPE
    x1 = _layer(x, norm1_w[0], w_in[0], pool_w[0], pool_scale[0], cmp_pe_k[0], cmp_w1_k[0], cmp_b1_k[0],
                cmp_w2_k[0], cmp_pe_v[0], cmp_w1_v[0], cmp_b1_v[0], cmp_w2_v[0], w_proj_pool[0],
                w_proj_attn[0], w_out[0])
    out = _ffn(x1, norm2_w[0].reshape(1, d), w_ffn_gate[0].astype(cd), w_ffn_up[0].astype(cd),
               w_ffn_down[0].astype(cd), norm_f_w.reshape(1, d))
    return out.reshape(b, s, d)
```

```python
import functools

import numpy as np
import jax
import jax.numpy as jnp
from jax import lax
from jax.experimental import pallas as pl
from jax.experimental.pallas import tpu as pltpu

D_MODEL = 1024
N_HEADS = 16
HEAD_DIM = 64
N_KV_GROUPS = 4
HEADS_PER_GROUP = N_HEADS // N_KV_GROUPS
ROPE_DIM = HEAD_DIM // 4
ROPE_HALF = ROPE_DIM // 2
ROPE_THETA = 500000.0
CMP_BLOCK = 32
CMP_STRIDE = 16
CMP_PER_SLC = 4
SLC_BLOCK = 64
SLC_TOPK = 16
WINDOW = 512
ATTN_Q_WIDTH = N_HEADS * HEAD_DIM
KV_WIDTH = N_KV_GROUPS * HEAD_DIM
POOL_WIDTH = D_MODEL // 2
POOL_WINDOWS = (2, 4, 8, 16)
POOL_GROUP = POOL_WIDTH // len(POOL_WINDOWS)
POOL_HALO = 16
RMS_EPS = 1e-6

LANE = 128
SUBLANE = 8
MXU_DTYPE = jnp.bfloat16
ACT_DTYPE = jnp.bfloat16
VMEM_LIMIT_BYTES = 56 * 1024 * 1024

GATE_NSA = N_HEADS * 3
COL_POOL = 0
COL_Q = COL_POOL + POOL_WIDTH
COL_KV = COL_Q + ATTN_Q_WIDTH
COL_GPOOL = COL_KV + 6 * KV_WIDTH
COL_GATTN = COL_GPOOL + D_MODEL
COL_GNSA = COL_GATTN + D_MODEL
PROJ_WIDTH = COL_GNSA + LANE
GATE_ROWS = 16

ROW_TILE = 512
Q_TILE = 256
SLC_SPAN = 2048
SLC_SUBTILE = 512
KPREP_TILE = 1024
FF_CHUNK = 512
SCORE_SCALE = HEAD_DIM ** -0.5
LOG2E = float(np.log2(np.e))
FINITE_LIMIT = 3e38
STAB_MARGIN = 1.03
L_MIN = 2.0 ** -64
MASK_BIAS = float(2 ** 30)
NEG_BIG = -1e30
IMP_BIG = 1e30
CHOSEN = -3e38
CHOSEN_LIMIT = -2e38
N_FORCED = 3


def _log2(n):
    k = int(n).bit_length() - 1
    assert 1 << k == n, n
    return k


def _cparams(n_grid):
    return pltpu.CompilerParams(dimension_semantics=("arbitrary",) * n_grid,
                                vmem_limit_bytes=VMEM_LIMIT_BYTES)


def _const_spec(shape):
    nd = len(shape)
    return pl.BlockSpec(shape, lambda *_: (0,) * nd, pipeline_mode=pl.Buffered(1))


def _sigmoid(x):
    return 1.0 / (1.0 + jnp.exp(-x))


def _lost_range(total, smallest):
    return jnp.logical_not(jnp.sum(total) < FINITE_LIMIT) | (jnp.min(smallest) < L_MIN)


def _rms(x, w):
    var = jnp.mean(x * x, axis=-1, keepdims=True)
    return x * lax.rsqrt(var + RMS_EPS) * w


def _inproj_kernel(x_ref, nw_ref, w_ref, o_ref, kvc_ref):
    h = _rms(x_ref[...], nw_ref[...]).astype(MXU_DTYPE)
    n = o_ref.shape[1]
    for c0 in range(0, n, KV_WIDTH):
        cw = min(KV_WIDTH, n - c0)
        res = jnp.dot(h, w_ref[:, c0:c0 + cw], preferred_element_type=jnp.float32).astype(o_ref.dtype)
        o_ref[:, c0:c0 + cw] = res
        which = (c0 - COL_KV) // KV_WIDTH
        if c0 >= COL_KV and which < 2:
            for g in range(N_KV_GROUPS):
                kvc_ref[which * N_KV_GROUPS + g] = res[:, g * HEAD_DIM:(g + 1) * HEAD_DIM]


def _inproj(x2d, norm_w, w_r):
    m, d = x2d.shape
    n = w_r.shape[1]
    return pl.pallas_call(
        _inproj_kernel,
        grid=(m // ROW_TILE,),
        in_specs=[pl.BlockSpec((ROW_TILE, d), lambda i: (i, 0)),
                  _const_spec((1, d)),
                  _const_spec((d, n))],
        out_specs=[pl.BlockSpec((ROW_TILE, n), lambda i: (i, 0)),
                   pl.BlockSpec((2 * N_KV_GROUPS, ROW_TILE, HEAD_DIM), lambda i: (0, i, 0))],
        out_shape=[jax.ShapeDtypeStruct((m, n), ACT_DTYPE),
                   jax.ShapeDtypeStruct((2 * N_KV_GROUPS, m, HEAD_DIM), ACT_DTYPE)],
        compiler_params=_cparams(1),
        name="inproj",
    )(x2d, norm_w, w_r)


def _gelu_tanh(x):
    return 0.5 * x * (1.0 + jnp.tanh(np.sqrt(2.0 / np.pi) * (x + 0.044715 * (x * x * x))))


def _compress_one(r_ref, pe_ref, w1_ref, b1_ref, w2_ref):
    half = w1_ref.shape[0] // 2
    n_blk = r_ref.shape[2]
    rows = [r_ref[0, 0, :, r * half:(r + 1) * half] for r in range(CMP_PER_SLC)]
    top = [jnp.dot(x, w1_ref[0:half, :], preferred_element_type=jnp.float32) for x in rows]
    bot = [jnp.dot(x, w1_ref[half:, :], preferred_element_type=jnp.float32) for x in rows]
    bias = jnp.dot(pe_ref[...], w1_ref[...], preferred_element_type=jnp.float32)[0:1, :] + b1_ref[...]
    nxt = bot[1:] + [pltpu.roll(bot[0], n_blk - 1, 0)]
    hid = jnp.concatenate([top[r] + nxt[r] for r in range(CMP_PER_SLC)], axis=0) + bias
    return jnp.dot(_gelu_tanh(hid).astype(MXU_DTYPE), w2_ref[...], preferred_element_type=jnp.float32)


def _compress_kernel(rk_ref, rv_ref, pek_ref, w1k_ref, b1k_ref, w2k_ref,
                     pev_ref, w1v_ref, b1v_ref, w2v_ref, ok_ref, ovt_ref):
    kc = _compress_one(rk_ref, pek_ref, w1k_ref, b1k_ref, w2k_ref)
    vc = _compress_one(rv_ref, pev_ref, w1v_ref, b1v_ref, w2v_ref)
    ok_ref[0, 0] = (kc * (SCORE_SCALE * LOG2E)).astype(ok_ref.dtype)
    ovt_ref[0, 0] = jnp.concatenate([kc, vc], axis=1).T[HEAD_DIM:, :].astype(ovt_ref.dtype)


def _compress(kvc_rows, pek, w1k, b1k, w2k, pev, w1v, b1v, w2v):
    _, b, n_blk, width = kvc_rows.shape
    dk = w2k.shape[1]
    n_rows = n_blk * CMP_PER_SLC
    w_specs = [_const_spec(a.shape) for a in (pek, w1k, b1k, w2k)]
    return pl.pallas_call(
        _compress_kernel,
        grid=(b, N_KV_GROUPS),
        in_specs=[pl.BlockSpec((1, 1, n_blk, width), lambda i, j: (j, i, 0, 0)),
                  pl.BlockSpec((1, 1, n_blk, width), lambda i, j: (N_KV_GROUPS + j, i, 0, 0))]
        + w_specs + w_specs,
        out_specs=[pl.BlockSpec((1, 1, n_rows, dk), lambda i, j: (i, j, 0, 0)),
                   pl.BlockSpec((1, 1, dk, n_rows), lambda i, j: (i, j, 0, 0))],
        out_shape=[jax.ShapeDtypeStruct((b, N_KV_GROUPS, n_rows, dk), MXU_DTYPE),
                   jax.ShapeDtypeStruct((b, N_KV_GROUPS, dk, n_rows), MXU_DTYPE)],
        compiler_params=_cparams(2),
        name="compress",
    )(kvc_rows, kvc_rows, pek, w1k, b1k, w2k, pev, w1v, b1v, w2v)


def _kprep_kernel(ks_ref, vs_ref, kw_ref, vw_ref, cos_ref, sin_ref, rot_ref,
                  kaug_ref, vst_ref, kwin_ref, vwt_ref, knorm_ref):
    ts = ks_ref.shape[1]
    t0 = pl.program_id(1) * ts
    cos = cos_ref[...]
    sin = sin_ref[...]
    rot = rot_ref[...]

    def rope(k_ref):
        k = k_ref[0]
        krot = jnp.dot(k.astype(MXU_DTYPE), rot, preferred_element_type=jnp.float32)
        return (k.astype(jnp.float32) * cos + krot * sin) * (SCORE_SCALE * LOG2E)

    n_blk = kaug_ref.shape[3] - 2 * HEAD_DIM
    blk = lax.broadcasted_iota(jnp.int32, (ts, n_blk), 1)
    key_blk = (t0 + lax.broadcasted_iota(jnp.int32, (ts, n_blk), 0)) >> _log2(SLC_BLOCK)
    onehot = jnp.where(blk == key_blk, 1.0, 0.0).astype(kaug_ref.dtype)
    ks_f = rope(ks_ref)
    kw_f = rope(kw_ref)
    ks = ks_f.astype(kaug_ref.dtype)
    kw = kw_f.astype(kwin_ref.dtype)

    seg_l = lax.broadcasted_iota(jnp.int32, (KV_WIDTH, LANE), 0) >> _log2(HEAD_DIM)
    seg_c = lax.broadcasted_iota(jnp.int32, (KV_WIDTH, LANE), 1)
    sq = (jnp.dot(ks_f * ks_f, jnp.where(seg_c == seg_l, 1.0, 0.0), preferred_element_type=jnp.float32)
          + jnp.dot(kw_f * kw_f, jnp.where(seg_c == seg_l + N_KV_GROUPS, 1.0, 0.0),
                    preferred_element_type=jnp.float32))
    tile_max = jnp.broadcast_to(jnp.max(sq, axis=0, keepdims=True), knorm_ref.shape[1:])

    @pl.when(pl.program_id(1) == 0)
    def _():
        knorm_ref[0] = tile_max

    @pl.when(pl.program_id(1) > 0)
    def _():
        knorm_ref[0] = jnp.maximum(knorm_ref[0], tile_max)

    vst = vs_ref[0].astype(jnp.float32).T.astype(vst_ref.dtype)
    vwt = vw_ref[0].astype(jnp.float32).T.astype(vwt_ref.dtype)
    ones_col = jnp.where(lax.broadcasted_iota(jnp.int32, (ts, HEAD_DIM), 1) == 0, 1.0, 0.0
                         ).astype(kaug_ref.dtype)
    for g in range(N_KV_GROUPS):
        sl = slice(g * HEAD_DIM, (g + 1) * HEAD_DIM)
        kaug_ref[0, g, :, 0:n_blk] = onehot
        kaug_ref[0, g, :, n_blk:n_blk + HEAD_DIM] = ks[:, sl]
        kaug_ref[0, g, :, n_blk + HEAD_DIM:] = ones_col
        kwin_ref[0, g, :, 0:HEAD_DIM] = kw[:, sl]
        kwin_ref[0, g, :, HEAD_DIM:] = ones_col
        vst_ref[0, g] = vst[sl, :]
        vwt_ref[0, g] = vwt[sl, :]


def _kprep(proj3, cos_k, sin_k, rot):
    b, s, _ = proj3.shape
    n_blk = s // SLC_BLOCK
    ts = min(KPREP_TILE, s)
    kvb = COL_KV // KV_WIDTH

    def col(j):
        return pl.BlockSpec((1, ts, KV_WIDTH), lambda i, t: (i, t, kvb + j))

    tab = pl.BlockSpec((ts, KV_WIDTH), lambda i, t: (t, 0))
    aug_w = n_blk + 2 * HEAD_DIM
    return pl.pallas_call(
        _kprep_kernel,
        grid=(b, s // ts),
        in_specs=[col(2), col(3), col(4), col(5), tab, tab, _const_spec(rot.shape)],
        out_specs=[pl.BlockSpec((1, N_KV_GROUPS, ts, aug_w), lambda i, t: (i, 0, t, 0)),
                   pl.BlockSpec((1, N_KV_GROUPS, HEAD_DIM, ts), lambda i, t: (i, 0, 0, t)),
                   pl.BlockSpec((1, N_KV_GROUPS, ts, 2 * HEAD_DIM), lambda i, t: (i, 0, t, 0)),
                   pl.BlockSpec((1, N_KV_GROUPS, HEAD_DIM, ts), lambda i, t: (i, 0, 0, t)),
                   pl.BlockSpec((1, SUBLANE, LANE), lambda i, t: (i, 0, 0))],
        out_shape=[jax.ShapeDtypeStruct((b, N_KV_GROUPS, s, aug_w), MXU_DTYPE),
                   jax.ShapeDtypeStruct((b, N_KV_GROUPS, HEAD_DIM, s), MXU_DTYPE),
                   jax.ShapeDtypeStruct((b, N_KV_GROUPS, s, 2 * HEAD_DIM), MXU_DTYPE),
                   jax.ShapeDtypeStruct((b, N_KV_GROUPS, HEAD_DIM, s), MXU_DTYPE),
                   jax.ShapeDtypeStruct((b, SUBLANE, LANE), jnp.float32)],
        compiler_params=_cparams(2),
        name="kprep",
    )(proj3, proj3, proj3, proj3, cos_k, sin_k, rot)


def _heads_to_lanes(xt):
    return jnp.concatenate([xt[h * HEAD_DIM:(h + 1) * HEAD_DIM, :] for h in range(HEADS_PER_GROUP)],
                           axis=1)


def _cmpsel_kernel(q_ref, kc_ref, vct_ref, ocmp_ref, bias_ref):
    tq = q_ref.shape[1]
    n_cmp_pad = kc_ref.shape[2]
    n_blk = bias_ref.shape[2]
    m_rows = HEADS_PER_GROUP * tq
    t0 = pl.program_id(1) * tq

    row = lax.broadcasted_iota(jnp.int32, (n_cmp_pad, tq), 0)
    n_idx = (row & (n_blk - 1)) * CMP_PER_SLC + (row >> _log2(n_blk))
    t = t0 + lax.broadcasted_iota(jnp.int32, (n_cmp_pad, tq), 1)
    vbias = jnp.where(n_idx * CMP_STRIDE + (CMP_BLOCK - 1) <= t, 0.0, NEG_BIG)
    vbias = jnp.concatenate([vbias] * HEADS_PER_GROUP, axis=1)
    t_row = t0 + (lax.broadcasted_iota(jnp.int32, (1, m_rows), 1) & (tq - 1))
    jrow = lax.broadcasted_iota(jnp.int32, (n_blk, tq), 0)
    jt = (t0 + lax.broadcasted_iota(jnp.int32, (n_blk, tq), 1)) >> _log2(SLC_BLOCK)
    forced = (jrow == 0) | (jrow == jt) | (jrow == jt - 1)
    jf = jrow.astype(jnp.float32)

    for pb in range(q_ref.shape[0]):
        qt = _heads_to_lanes(q_ref[pb].astype(jnp.float32).T).astype(MXU_DTYPE)
        s = jnp.dot(kc_ref[pb, 0], qt, preferred_element_type=jnp.float32) + vbias
        mx = jnp.max(s, axis=0, keepdims=True)
        p = jnp.exp2(s - mx)
        den = jnp.sum(p, axis=0, keepdims=True)
        inv = jnp.where(t_row >= CMP_BLOCK - 1, 1.0 / den, 0.0)
        ocmp_ref[pb, 0, 0] = inv * jnp.dot(vct_ref[pb, 0], p.astype(MXU_DTYPE),
                                           preferred_element_type=jnp.float32)

        imp = None
        for h in range(HEADS_PER_GROUP):
            ph = p[:, h * tq:(h + 1) * tq]
            last = ph[3 * n_blk:4 * n_blk, :]
            prev = jnp.where(jrow == 0, 0.0, pltpu.roll(last, 1, 0))
            taps = ph[0:n_blk, :] + ph[n_blk:2 * n_blk, :] + ph[2 * n_blk:3 * n_blk, :] + last + prev
            imp_h = taps * inv[:, h * tq:(h + 1) * tq]
            imp = imp_h if imp is None else imp + imp_h

        val = jnp.where(forced, CHOSEN, jnp.where(jrow <= jt, imp, -IMP_BIG))
        for _ in range(min(SLC_TOPK, n_blk) - N_FORCED):
            top = jnp.max(val, axis=0, keepdims=True)
            first = jnp.min(jnp.where(val == top, jf, float(n_blk)), axis=0, keepdims=True)
            val = jnp.where(jf == first, CHOSEN, val)
        bias_ref[pb, 0] = jnp.where(val < CHOSEN_LIMIT, 0.0, -MASK_BIAS).astype(bias_ref.dtype)


def _cmpsel(proj3, kc_r, vc_rt):
    b, s, _ = proj3.shape
    tq = min(Q_TILE, s)
    nq = s // tq
    n_cmp_pad = kc_r.shape[2]
    n_blk = s // SLC_BLOCK
    qb = COL_Q // KV_WIDTH
    m_rows = HEADS_PER_GROUP * tq
    return pl.pallas_call(
        _cmpsel_kernel,
        grid=(N_KV_GROUPS, nq),
        in_specs=[pl.BlockSpec((b, tq, KV_WIDTH), lambda g, q: (0, q, qb + g)),
                  pl.BlockSpec((b, 1, n_cmp_pad, HEAD_DIM), lambda g, q: (0, g, 0, 0)),
                  pl.BlockSpec((b, 1, HEAD_DIM, n_cmp_pad), lambda g, q: (0, g, 0, 0))],
        out_specs=[pl.BlockSpec((b, 1, 1, HEAD_DIM, m_rows), lambda g, q: (0, g, q, 0, 0)),
                   pl.BlockSpec((b, 1, n_blk, tq), lambda g, q: (0, g, 0, q))],
        out_shape=[jax.ShapeDtypeStruct((b, N_KV_GROUPS, nq, HEAD_DIM, m_rows), jnp.float32),
                   jax.ShapeDtypeStruct((b, N_KV_GROUPS, n_blk, s), MXU_DTYPE)],
        compiler_params=_cparams(2),
        name="cmpsel",
    )(proj3, kc_r, vc_rt)


def _flash_kernel(q_ref, cos_ref, sin_ref, bias_ref, kaug_ref, vst_ref, kwin_ref, vwt_ref, knorm_ref,
                  gate_ref, ocmp_ref, o_ref, m_ref, l_ref, acc_ref, lw_ref, accw_ref):
    nb = q_ref.shape[0]
    tq = q_ref.shape[1]
    m_rows = HEADS_PER_GROUP * tq
    grp = pl.program_id(0)
    qi = pl.program_id(1)
    t0 = qi * tq
    kd = pl.multiple_of(t0, tq)
    probs = range(nb)
    zpad = jnp.zeros((HEAD_DIM, m_rows), MXU_DTYPE)
    pad_row = lax.broadcasted_iota(jnp.int32, (HEAD_DIM, m_rows), 0)
    lane = lax.broadcasted_iota(jnp.int32, (1, LANE), 1)

    def stab_rows(neg_m):
        return jnp.where(pad_row == 0, neg_m, 0.0).astype(MXU_DTYPE)

    cos = cos_ref[...]
    sin = sin_ref[...]
    qr, bias4, qs_slc, qs_win = [], [], [], []
    for pb in probs:
        qt = q_ref[pb].astype(jnp.float32).T
        parts = []
        for h in range(HEADS_PER_GROUP):
            qh = qt[h * HEAD_DIM:(h + 1) * HEAD_DIM, :]
            x1 = qh[0:ROPE_HALF, :]
            x2 = qh[ROPE_HALF:ROPE_DIM, :]
            parts.append(jnp.concatenate([x1 * cos - x2 * sin, x2 * cos + x1 * sin, qh[ROPE_DIM:, :]],
                                         axis=0))
        qf = jnp.concatenate(parts, axis=1)
        q_sq = jnp.sum(qf * qf, axis=0, keepdims=True)
        k_sq = knorm_ref[pb, 0:1, :]
        ks_sq = jnp.max(jnp.where(lane == grp, k_sq, 0.0), axis=1, keepdims=True)
        kw_sq = jnp.max(jnp.where(lane == grp + N_KV_GROUPS, k_sq, 0.0), axis=1, keepdims=True)
        qr.append(qf.astype(MXU_DTYPE))
        bias4.append(jnp.concatenate([bias_ref[pb, 0]] * HEADS_PER_GROUP, axis=1))
        qs_slc.append(jnp.concatenate([bias4[pb], qr[pb], stab_rows(-STAB_MARGIN * jnp.sqrt(q_sq * ks_sq))],
                                      axis=0))
        qs_win.append(jnp.concatenate([qr[pb], stab_rows(-STAB_MARGIN * jnp.sqrt(q_sq * kw_sq))], axis=0))

    tcol = t0 + (lax.broadcasted_iota(jnp.int32, (tq, m_rows), 1) & (tq - 1))
    koff = lax.broadcasted_iota(jnp.int32, (tq, m_rows), 0)
    n_back = WINDOW // tq
    causal = t0 + koff <= tcol

    def win_tile(back):
        diff = tcol - (t0 - back * tq + koff)
        if back == 0:
            return kd, tq, diff >= 0
        exists = qi >= back
        mask = exists if back * tq + tq - 1 < WINDOW else (diff < WINDOW) & exists
        return pl.multiple_of(jnp.maximum(t0 - back * tq, 0), tq), tq, mask

    def fast_tiles(k_ref, v_ref, qs, tiles, l_out, acc_out, first=False):
        for pb in probs:
            l_add = acc_add = None
            for k0, tk, mask in tiles:
                s = jnp.dot(k_ref[pb, 0, pl.ds(k0, tk), :], qs[pb], preferred_element_type=jnp.float32)
                if mask is not None:
                    s = jnp.where(mask, s, NEG_BIG)
                p = jnp.exp2(s)
                l_t = jnp.sum(p, axis=0, keepdims=True)
                acc_t = jnp.dot(v_ref[pb, 0, :, pl.ds(k0, tk)], p.astype(MXU_DTYPE),
                                preferred_element_type=jnp.float32)
                l_add = l_t if l_add is None else l_add + l_t
                acc_add = acc_t if acc_add is None else acc_add + acc_t
            if first:
                l_out[pb] = l_add
                acc_out[pb] = acc_add
            else:
                l_out[pb] += l_add
                acc_out[pb] += acc_add

    fast_tiles(kaug_ref, vst_ref, qs_slc, [(kd, tq, causal)], l_ref, acc_ref, first=True)
    fast_tiles(kwin_ref, vwt_ref, qs_win, [win_tile(back) for back in range(n_back + 1)],
               lw_ref, accw_ref, first=True)

    sub = SLC_SUBTILE
    n_sub = SLC_SPAN // sub

    def span_body(i, carry):
        base = i * SLC_SPAN
        fast_tiles(kaug_ref, vst_ref, qs_slc,
                   [(pl.multiple_of(base + j * sub, sub), sub, None) for j in range(n_sub)],
                   l_ref, acc_ref)
        return carry

    n_span = t0 // SLC_SPAN
    lax.fori_loop(0, n_span, span_body, 0)
    done = n_span * SLC_SPAN
    size = SLC_SPAN // 2
    while size >= tq:
        take = ((t0 // size) % 2) == 1

        @pl.when(take)
        def _(done=done, size=size):
            piece = min(size, sub)
            fast_tiles(kaug_ref, vst_ref, qs_slc,
                       [(pl.multiple_of(done + j * piece, piece), piece, None) for j in range(size // piece)],
                       l_ref, acc_ref)

        done = done + jnp.where(take, size, 0)
        size //= 2

    def exact_start(pb, s, vt, l_out, acc_out):
        mx = jnp.max(s, axis=0, keepdims=True)
        p = jnp.exp2(s - mx)
        m_ref[pb] = mx
        l_out[pb] = jnp.sum(p, axis=0, keepdims=True)
        acc_out[pb] = jnp.dot(vt, p.astype(MXU_DTYPE), preferred_element_type=jnp.float32)

    def exact_update(pb, s, vt, l_out, acc_out):
        m_old = m_ref[pb]
        mx = jnp.maximum(m_old, jnp.max(s, axis=0, keepdims=True))
        alpha = jnp.exp2(m_old - mx)
        p = jnp.exp2(s - mx)
        m_ref[pb] = mx
        l_out[pb] = alpha * l_out[pb] + jnp.sum(p, axis=0, keepdims=True)
        acc_out[pb] = alpha * acc_out[pb] + jnp.dot(vt, p.astype(MXU_DTYPE),
                                                    preferred_element_type=jnp.float32)

    check = l_low = None
    for pb in probs:
        c = (l_ref[pb] + lw_ref[pb] + jnp.sum(jnp.abs(acc_ref[pb]), axis=0, keepdims=True)
             + jnp.sum(jnp.abs(accw_ref[pb]), axis=0, keepdims=True))
        lo = jnp.minimum(l_ref[pb], lw_ref[pb])
        check = c if check is None else check + c
        l_low = lo if l_low is None else jnp.minimum(l_low, lo)

    @pl.when(_lost_range(check, l_low))
    def _():
        for pb in probs:
            q0 = jnp.concatenate([bias4[pb], qr[pb], zpad], axis=0)
            s = jnp.dot(kaug_ref[pb, 0, pl.ds(kd, tq), :], q0, preferred_element_type=jnp.float32)
            exact_start(pb, jnp.where(causal, s, NEG_BIG), vst_ref[pb, 0, :, pl.ds(kd, tq)], l_ref, acc_ref)

            def body(kt, carry, pb=pb, q0=q0):
                k0 = pl.multiple_of(kt * tq, tq)
                sk = jnp.dot(kaug_ref[pb, 0, pl.ds(k0, tq), :], q0, preferred_element_type=jnp.float32)
                exact_update(pb, sk, vst_ref[pb, 0, :, pl.ds(k0, tq)], l_ref, acc_ref)
                return carry

            lax.fori_loop(0, qi, body, 0)
            qw0 = jnp.concatenate([qr[pb], zpad], axis=0)
            for back in range(n_back + 1):
                kbase = t0 - back * tq
                k0 = pl.multiple_of(jnp.maximum(kbase, 0), tq)
                sw = jnp.dot(kwin_ref[pb, 0, pl.ds(k0, tq), :], qw0, preferred_element_type=jnp.float32)
                diff = tcol - (kbase + koff)
                sw = jnp.where((diff >= 0) & (diff < WINDOW) & (kbase + koff >= 0), sw, NEG_BIG)
                if back == 0:
                    exact_start(pb, sw, vwt_ref[pb, 0, :, pl.ds(k0, tq)], lw_ref, accw_ref)
                else:
                    exact_update(pb, sw, vwt_ref[pb, 0, :, pl.ds(k0, tq)], lw_ref, accw_ref)

    r_id = lax.broadcasted_iota(jnp.int32, (GATE_ROWS, LANE), 0)
    c_id = lax.broadcasted_iota(jnp.int32, (GATE_ROWS, LANE), 1)
    pick = jnp.where((c_id == grp * (HEADS_PER_GROUP * 3) + r_id) & (r_id < HEADS_PER_GROUP * 3),
                     1.0, 0.0).astype(gate_ref.dtype)
    for pb in probs:
        o_slc = acc_ref[pb] / l_ref[pb]
        o_win = accw_ref[pb] / lw_ref[pb]
        graw = lax.dot_general(pick, gate_ref[pb], (((1,), (1,)), ((), ())),
                               preferred_element_type=jnp.float32)
        gates = _sigmoid(graw)
        o_cmp = ocmp_ref[pb, 0, 0]
        mixed = []
        for h in range(HEADS_PER_GROUP):
            sl = slice(h * tq, (h + 1) * tq)
            mixed.append(gates[3 * h:3 * h + 1, :] * o_cmp[:, sl]
                         + gates[3 * h + 1:3 * h + 2, :] * o_slc[:, sl]
                         + gates[3 * h + 2:3 * h + 3, :] * o_win[:, sl])
        o_ref[pb] = jnp.concatenate(mixed, axis=0).T.astype(o_ref.dtype)


def _flash(proj3, cos_q, sin_q, bias, kaug, vst, kwin, vwt, knorm, ocmp):
    b, s, _ = proj3.shape
    tq = min(Q_TILE, s)
    nq = s // tq
    n_blk = s // SLC_BLOCK
    aug_w = kaug.shape[3]
    qb = COL_Q // KV_WIDTH
    m_rows = HEADS_PER_GROUP * tq

    def resident(shape):
        return pl.BlockSpec((b, 1) + shape, lambda g, q: (0, g, 0, 0), pipeline_mode=pl.Buffered(1))

    return pl.pallas_call(
        _flash_kernel,
        grid=(N_KV_GROUPS, nq),
        in_specs=[pl.BlockSpec((b, tq, KV_WIDTH), lambda g, q: (0, q, qb + g)),
                  pl.BlockSpec((ROPE_HALF, tq), lambda g, q: (0, q)),
                  pl.BlockSpec((ROPE_HALF, tq), lambda g, q: (0, q)),
                  pl.BlockSpec((b, 1, n_blk, tq), lambda g, q: (0, g, 0, q)),
                  resident((s, aug_w)),
                  resident((HEAD_DIM, s)),
                  resident((s, 2 * HEAD_DIM)),
                  resident((HEAD_DIM, s)),
                  _const_spec((b, SUBLANE, LANE)),
                  pl.BlockSpec((b, tq, LANE), lambda g, q: (0, q, COL_GNSA // LANE)),
                  pl.BlockSpec((b, 1, 1, HEAD_DIM, m_rows), lambda g, q: (0, g, q, 0, 0))],
        out_specs=pl.BlockSpec((b, tq, KV_WIDTH), lambda g, q: (0, q, g)),
        out_shape=jax.ShapeDtypeStruct((b, s, ATTN_Q_WIDTH), ACT_DTYPE),
        scratch_shapes=[pltpu.VMEM((b, 1, m_rows), jnp.float32),
                        pltpu.VMEM((b, 1, m_rows), jnp.float32),
                        pltpu.VMEM((b, HEAD_DIM, m_rows), jnp.float32),
                        pltpu.VMEM((b, 1, m_rows), jnp.float32),
                        pltpu.VMEM((b, HEAD_DIM, m_rows), jnp.float32)],
        compiler_params=_cparams(2),
        name="flash",
    )(proj3, cos_q, sin_q, bias, kaug, vst, kwin, vwt, knorm, proj3, ocmp)


def _merge_kernel(u_ref, halo_ref, gp_ref, ga_ref, attn_ref, x_ref, pw_ref, ps_ref,
                  wpp_ref, wpa_ref, wo_ref, o_ref, *, seq_len):
    tm = u_ref.shape[0]
    t0 = (pl.program_id(0) * tm) & (seq_len - 1)
    u = u_ref[...].astype(jnp.float32)
    halo = jnp.where(t0 > 0, halo_ref[...].astype(jnp.float32), 0.0)
    ext = jnp.concatenate([halo, u], axis=0)
    t = t0 + lax.broadcasted_iota(jnp.int32, (tm, POOL_GROUP), 0)
    pooled = []
    for gi, w in enumerate(POOL_WINDOWS):
        sl = slice(gi * POOL_GROUP, (gi + 1) * POOL_GROUP)
        acc = ext[:, sl]
        span = 1
        while span < w:
            acc = acc + pltpu.roll(acc, span, 0)
            span *= 2
        cnt = jnp.minimum(t + 1, w).astype(jnp.float32)
        mean = acc[POOL_HALO:, :] / cnt
        mixed = jnp.dot((mean - u[:, sl]).astype(MXU_DTYPE), pw_ref[gi],
                        preferred_element_type=jnp.float32)
        pooled.append(mixed * ps_ref[:, sl])
    pool = jnp.concatenate(pooled, axis=1).astype(MXU_DTYPE)
    pp = jnp.dot(pool, wpp_ref[...], preferred_element_type=jnp.float32)
    pa = jnp.dot(attn_ref[...], wpa_ref[...], preferred_element_type=jnp.float32)
    merged = (_sigmoid(gp_ref[...].astype(jnp.float32)) * pp
              + _sigmoid(ga_ref[...].astype(jnp.float32)) * pa)
    o_ref[...] = x_ref[...] + jnp.dot(merged.astype(MXU_DTYPE), wo_ref[...],
                                      preferred_element_type=jnp.float32)


def _merge(proj2, attn2, x2d, pool_w, pool_scale, wpp, wpa, wo, seq_len):
    m, d = x2d.shape
    tm = min(ROW_TILE, seq_len)
    halo_per_tile = tm // POOL_HALO
    return pl.pallas_call(
        functools.partial(_merge_kernel, seq_len=seq_len),
        grid=(m // tm,),
        in_specs=[pl.BlockSpec((tm, POOL_WIDTH), lambda i: (i, COL_POOL // POOL_WIDTH)),
                  pl.BlockSpec((POOL_HALO, POOL_WIDTH),
                               lambda i: (jnp.maximum(i * halo_per_tile - 1, 0), COL_POOL // POOL_WIDTH)),
                  pl.BlockSpec((tm, d), lambda i: (i, COL_GPOOL // D_MODEL)),
                  pl.BlockSpec((tm, d), lambda i: (i, COL_GATTN // D_MODEL)),
                  pl.BlockSpec((tm, ATTN_Q_WIDTH), lambda i: (i, 0)),
                  pl.BlockSpec((tm, d), lambda i: (i, 0)),
                  _const_spec(pool_w.shape), _const_spec(pool_scale.shape),
                  _const_spec(wpp.shape), _const_spec(wpa.shape), _const_spec(wo.shape)],
        out_specs=pl.BlockSpec((tm, d), lambda i: (i, 0)),
        out_shape=jax.ShapeDtypeStruct((m, d), jnp.float32),
        compiler_params=_cparams(1),
        name="merge",
    )(proj2, proj2, proj2, proj2, attn2, x2d, pool_w, pool_scale, wpp, wpa, wo)


def _ffn_kernel(x_ref, n2_ref, wg_ref, wu_ref, wd_ref, nf_ref, o_ref):
    x = x_ref[...]
    h = _rms(x, n2_ref[...]).astype(MXU_DTYPE)
    d_ff = wg_ref.shape[1]
    acc = x
    for c0 in range(0, d_ff, FF_CHUNK):
        cw = min(FF_CHUNK, d_ff - c0)
        gate = jnp.dot(h, wg_ref[:, c0:c0 + cw], preferred_element_type=jnp.float32)
        up = jnp.dot(h, wu_ref[:, c0:c0 + cw], preferred_element_type=jnp.float32)
        act = (gate * _sigmoid(gate) * up).astype(MXU_DTYPE)
        acc = acc + jnp.dot(act, wd_ref[c0:c0 + cw, :], preferred_element_type=jnp.float32)
    o_ref[...] = _rms(acc, nf_ref[...])


def _ffn(x1, norm2_w, wg, wu, wd, norm_f_w):
    m, d = x1.shape
    return pl.pallas_call(
        _ffn_kernel,
        grid=(m // ROW_TILE,),
        in_specs=[pl.BlockSpec((ROW_TILE, d), lambda i: (i, 0)),
                  _const_spec(norm2_w.shape), _const_spec(wg.shape), _const_spec(wu.shape),
                  _const_spec(wd.shape), _const_spec(norm_f_w.shape)],
        out_specs=pl.BlockSpec((ROW_TILE, d), lambda i: (i, 0)),
        out_shape=jax.ShapeDtypeStruct((m, d), jnp.float32),
        compiler_params=_cparams(1),
        name="ffn",
    )(x1, norm2_w, wg, wu, wd, norm_f_w)


def _rope_tables(seq_len):
    inv_freq = 1.0 / (ROPE_THETA ** (jnp.arange(0, ROPE_DIM, 2, dtype=jnp.float32) / ROPE_DIM))
    ang = jnp.arange(seq_len).astype(jnp.float32)[:, None] * inv_freq[None, :]
    cos, sin = jnp.cos(ang), jnp.sin(ang)
    rest = HEAD_DIM - ROPE_DIM
    cos_h = jnp.concatenate([cos, cos, jnp.ones((seq_len, rest), jnp.float32)], axis=1)
    sin_h = jnp.concatenate([sin, sin, jnp.zeros((seq_len, rest), jnp.float32)], axis=1)
    return cos.T, sin.T, jnp.tile(cos_h, (1, N_KV_GROUPS)), jnp.tile(sin_h, (1, N_KV_GROUPS))


def _rotate_half_matrix():
    rot = np.zeros((KV_WIDTH, KV_WIDTH), np.float32)
    for l in range(KV_WIDTH):
        d = l % HEAD_DIM
        if d < ROPE_HALF:
            rot[l + ROPE_HALF, l] = -1.0
        elif d < ROPE_DIM:
            rot[l - ROPE_HALF, l] = 1.0
    return jnp.asarray(rot, MXU_DTYPE)


def _layer(x, norm1_w, w_in, pool_w, pool_scale, cmp_pe_k, cmp_w1_k, cmp_b1_k, cmp_w2_k,
           cmp_pe_v, cmp_w1_v, cmp_b1_v, cmp_w2_v, w_proj_pool, w_proj_attn, w_out):
    b, s, d = x.shape
    m = b * s
    n_blk = s // SLC_BLOCK
    assert s % Q_TILE == 0 and s % ROW_TILE == 0 and (s & (s - 1)) == 0, s
    cd = MXU_DTYPE

    g0 = COL_GPOOL
    w_r = jnp.concatenate([w_in[:, :g0], w_in[:, g0 + GATE_NSA:], w_in[:, g0:g0 + GATE_NSA],
                           jnp.zeros((d, LANE - GATE_NSA), w_in.dtype)], axis=1).astype(cd)
    x2d = x.reshape(m, d)
    proj2, kvc = _inproj(x2d, norm1_w.reshape(1, d), w_r)
    proj3 = proj2.reshape(b, s, PROJ_WIDTH)

    def pe8(pe):
        return jnp.broadcast_to(pe.reshape(1, -1), (8, pe.size)).astype(cd)

    kvc_rows = kvc.reshape(2 * N_KV_GROUPS, b, n_blk, SLC_BLOCK * HEAD_DIM)
    kc_r, vc_rt = _compress(kvc_rows,
                            pe8(cmp_pe_k), cmp_w1_k.astype(cd), cmp_b1_k.reshape(1, -1), cmp_w2_k.astype(cd),
                            pe8(cmp_pe_v), cmp_w1_v.astype(cd), cmp_b1_v.reshape(1, -1), cmp_w2_v.astype(cd))

    cos_q, sin_q, cos_k, sin_k = _rope_tables(s)
    kaug, vst, kwin, vwt, knorm = _kprep(proj3, cos_k, sin_k, _rotate_half_matrix())
    ocmp, bias = _cmpsel(proj3, kc_r, vc_rt)
    attn = _flash(proj3, cos_q, sin_q, bias, kaug, vst, kwin, vwt, knorm, ocmp)
    return _merge(proj2, attn.reshape(m, ATTN_Q_WIDTH), x2d, pool_w.astype(cd),
                  pool_scale.reshape(1, -1), w_proj_pool.astype(cd), w_proj_attn.astype(cd),
                  w_out.astype(cd), s)


def kernel(x, norm1_w, w_in, pool_w, pool_scale, cmp_pe_k, cmp_w1_k, cmp_b1_k, cmp_w2_k, cmp_pe_v,
           cmp_w1_v, cmp_b1_v, cmp_w2_v, w_proj_pool, w_proj_attn, w_out, norm2_w, w_ffn_gate,
           w_ffn_up, w_ffn_down, norm_f_w):
    b, s, d = x.shape
    depth = w_in.shape[0]
    assert depth == 1, "the final norm is fused into the last layer's FFN kernel"
    cd = MXU_DTYPE
    x1 = _layer(x, norm1_w[0], w_in[0], pool_w[0], pool_scale[0], cmp_pe_k[0], cmp_w1_k[0], cmp_b1_k[0],
                cmp_w2_k[0], cmp_pe_v[0], cmp_w1_v[0], cmp_b1_v[0], cmp_w2_v[0], w_proj_pool[0],
                w_proj_attn[0], w_out[0])
    out = _ffn(x1, norm2_w[0].reshape(1, d), w_ffn_gate[0].astype(cd), w_ffn_up[0].astype(cd),
               w_ffn_down[0].astype(cd), norm_f_w.reshape(1, d))
    return out.reshape(b, s, d)
```

```python
import functools

import numpy as np
import jax
import jax.numpy as jnp
from jax import lax
from jax.experimental import pallas as pl
from jax.experimental.pallas import tpu as pltpu

D_MODEL = 1024
N_HEADS = 16
HEAD_DIM = 64
N_KV_GROUPS = 4
HEADS_PER_GROUP = N_HEADS // N_KV_GROUPS
ROPE_DIM = HEAD_DIM // 4
ROPE_HALF = ROPE_DIM // 2
ROPE_THETA = 500000.0
CMP_BLOCK = 32
CMP_STRIDE = 16
CMP_PER_SLC = 4
SLC_BLOCK = 64
SLC_TOPK = 16
WINDOW = 512
ATTN_Q_WIDTH = N_HEADS * HEAD_DIM
KV_WIDTH = N_KV_GROUPS * HEAD_DIM
POOL_WIDTH = D_MODEL // 2
POOL_WINDOWS = (2, 4, 8, 16)
POOL_GROUP = POOL_WIDTH // len(POOL_WINDOWS)
POOL_HALO = 16
RMS_EPS = 1e-6

LANE = 128
SUBLANE = 8
MXU_DTYPE = jnp.bfloat16
ACT_DTYPE = jnp.bfloat16
VMEM_LIMIT_BYTES = 56 * 1024 * 1024

GATE_NSA = N_HEADS * 3
COL_POOL = 0
COL_Q = COL_POOL + POOL_WIDTH
COL_KV = COL_Q + ATTN_Q_WIDTH
COL_GPOOL = COL_KV + 6 * KV_WIDTH
COL_GATTN = COL_GPOOL + D_MODEL
COL_GNSA = COL_GATTN + D_MODEL
PROJ_WIDTH = COL_GNSA + LANE
GATE_ROWS = 16

ROW_TILE = 512
Q_TILE = 256
SLC_SPAN = 2048
SLC_SUBTILE = 512
KPREP_TILE = 1024
FF_CHUNK = 512
SCORE_SCALE = HEAD_DIM ** -0.5
LOG2E = float(np.log2(np.e))
FINITE_LIMIT = 3e38
STAB_MARGIN = 1.03
L_MIN = 2.0 ** -64
MASK_BIAS = float(2 ** 30)
NEG_BIG = -1e30
IMP_BIG = 1e30
CHOSEN = -3e38
CHOSEN_LIMIT = -2e38
N_FORCED = 3


def _log2(n):
    k = int(n).bit_length() - 1
    assert 1 << k == n, n
    return k


def _cparams(n_grid):
    return pltpu.CompilerParams(dimension_semantics=("arbitrary",) * n_grid,
                                vmem_limit_bytes=VMEM_LIMIT_BYTES)


def _const_spec(shape):
    nd = len(shape)
    return pl.BlockSpec(shape, lambda *_: (0,) * nd, pipeline_mode=pl.Buffered(1))


def _sigmoid(x):
    return 1.0 / (1.0 + jnp.exp(-x))


def _lost_range(total, smallest):
    return jnp.logical_not(jnp.sum(total) < FINITE_LIMIT) | (jnp.min(smallest) < L_MIN)


def _rms(x, w):
    var = jnp.mean(x * x, axis=-1, keepdims=True)
    return x * lax.rsqrt(var + RMS_EPS) * w


def _inproj_kernel(x_ref, nw_ref, wa_ref, wb_ref, wc_ref, o_ref, kvc_ref):
    h = _rms(x_ref[...], nw_ref[...]).astype(MXU_DTYPE)
    n = o_ref.shape[1]
    for c0 in range(0, n, KV_WIDTH):
        cw = min(KV_WIDTH, n - c0)
        w_ref, w0 = ((wa_ref, 0) if c0 < COL_GPOOL else (wb_ref, COL_GPOOL) if c0 < COL_GNSA
                     else (wc_ref, COL_GNSA))
        res = jnp.dot(h, w_ref[:, c0 - w0:c0 - w0 + cw],
                      preferred_element_type=jnp.float32).astype(o_ref.dtype)
        o_ref[:, c0:c0 + cw] = res
        which = (c0 - COL_KV) // KV_WIDTH
        if c0 >= COL_KV and which < 2:
            for g in range(N_KV_GROUPS):
                kvc_ref[which * N_KV_GROUPS + g] = res[:, g * HEAD_DIM:(g + 1) * HEAD_DIM]


def _inproj(x2d, norm_w, w_parts):
    m, d = x2d.shape
    n = sum(w.shape[1] for w in w_parts)
    assert n == PROJ_WIDTH
    return pl.pallas_call(
        _inproj_kernel,
        grid=(m // ROW_TILE,),
        in_specs=[pl.BlockSpec((ROW_TILE, d), lambda i: (i, 0)),
                  _const_spec((1, d))] + [_const_spec(w.shape) for w in w_parts],
        out_specs=[pl.BlockSpec((ROW_TILE, n), lambda i: (i, 0)),
                   pl.BlockSpec((2 * N_KV_GROUPS, ROW_TILE, HEAD_DIM), lambda i: (0, i, 0))],
        out_shape=[jax.ShapeDtypeStruct((m, n), ACT_DTYPE),
                   jax.ShapeDtypeStruct((2 * N_KV_GROUPS, m, HEAD_DIM), ACT_DTYPE)],
        compiler_params=_cparams(1),
        name="inproj",
    )(x2d, norm_w, *w_parts)


def _gelu_tanh(x):
    return 0.5 * x * (1.0 + jnp.tanh(np.sqrt(2.0 / np.pi) * (x + 0.044715 * (x * x * x))))


def _compress_one(r_ref, pe_ref, w1_ref, b1_ref, w2_ref):
    half = w1_ref.shape[0] // 2
    n_blk = r_ref.shape[2]
    rows = [r_ref[0, 0, :, r * half:(r + 1) * half] for r in range(CMP_PER_SLC)]
    top = [jnp.dot(x, w1_ref[0:half, :], preferred_element_type=jnp.float32) for x in rows]
    bot = [jnp.dot(x, w1_ref[half:, :], preferred_element_type=jnp.float32) for x in rows]
    bias = jnp.dot(pe_ref[...], w1_ref[...], preferred_element_type=jnp.float32)[0:1, :] + b1_ref[...]
    nxt = bot[1:] + [pltpu.roll(bot[0], n_blk - 1, 0)]
    hid = jnp.concatenate([top[r] + nxt[r] for r in range(CMP_PER_SLC)], axis=0) + bias
    return jnp.dot(_gelu_tanh(hid).astype(MXU_DTYPE), w2_ref[...], preferred_element_type=jnp.float32)


def _compress_kernel(rk_ref, rv_ref, pek_ref, w1k_ref, b1k_ref, w2k_ref,
                     pev_ref, w1v_ref, b1v_ref, w2v_ref, ok_ref, ovt_ref):
    kc = _compress_one(rk_ref, pek_ref, w1k_ref, b1k_ref, w2k_ref)
    vc = _compress_one(rv_ref, pev_ref, w1v_ref, b1v_ref, w2v_ref)
    ok_ref[0, 0] = jnp.concatenate([kc * (SCORE_SCALE * LOG2E), jnp.zeros_like(kc)], axis=1).astype(ok_ref.dtype)
    ovt_ref[0, 0] = jnp.concatenate([kc, vc], axis=1).T[HEAD_DIM:, :].astype(ovt_ref.dtype)


def _compress(kvc_rows, pek, w1k, b1k, w2k, pev, w1v, b1v, w2v):
    _, b, n_blk, width = kvc_rows.shape
    dk = w2k.shape[1]
    n_rows = n_blk * CMP_PER_SLC
    w_specs = [_const_spec(a.shape) for a in (pek, w1k, b1k, w2k)]
    return pl.pallas_call(
        _compress_kernel,
        grid=(b, N_KV_GROUPS),
        in_specs=[pl.BlockSpec((1, 1, n_blk, width), lambda i, j: (j, i, 0, 0)),
                  pl.BlockSpec((1, 1, n_blk, width), lambda i, j: (N_KV_GROUPS + j, i, 0, 0))]
        + w_specs + w_specs,
        out_specs=[pl.BlockSpec((1, 1, n_rows, 2 * dk), lambda i, j: (i, j, 0, 0)),
                   pl.BlockSpec((1, 1, dk, n_rows), lambda i, j: (i, j, 0, 0))],
        out_shape=[jax.ShapeDtypeStruct((b, N_KV_GROUPS, n_rows, 2 * dk), MXU_DTYPE),
                   jax.ShapeDtypeStruct((b, N_KV_GROUPS, dk, n_rows), MXU_DTYPE)],
        compiler_params=_cparams(2),
        name="compress",
    )(kvc_rows, kvc_rows, pek, w1k, b1k, w2k, pev, w1v, b1v, w2v)


def _kprep_kernel(ks_ref, vs_ref, kw_ref, vw_ref, cos_ref, sin_ref, rot_ref,
                  kaug_ref, vst_ref, kwin_ref, vwt_ref, knorm_ref):
    ts = ks_ref.shape[1]
    t0 = pl.program_id(1) * ts
    cos = jnp.concatenate([cos_ref[...]] * N_KV_GROUPS, axis=1)
    sin = jnp.concatenate([sin_ref[...]] * N_KV_GROUPS, axis=1)
    rot = rot_ref[...]

    def rope(k_ref):
        k = k_ref[0]
        krot = jnp.dot(k.astype(MXU_DTYPE), rot, preferred_element_type=jnp.float32)
        return (k.astype(jnp.float32) * cos + krot * sin) * (SCORE_SCALE * LOG2E)

    n_blk = kaug_ref.shape[3] - 2 * HEAD_DIM
    blk = lax.broadcasted_iota(jnp.int32, (ts, n_blk), 1)
    key_blk = (t0 + lax.broadcasted_iota(jnp.int32, (ts, n_blk), 0)) >> _log2(SLC_BLOCK)
    onehot = jnp.where(blk == key_blk, 1.0, 0.0).astype(kaug_ref.dtype)
    ks_f = rope(ks_ref)
    kw_f = rope(kw_ref)
    ks = ks_f.astype(kaug_ref.dtype)
    kw = kw_f.astype(kwin_ref.dtype)

    seg_l = lax.broadcasted_iota(jnp.int32, (KV_WIDTH, LANE), 0) >> _log2(HEAD_DIM)
    seg_c = lax.broadcasted_iota(jnp.int32, (KV_WIDTH, LANE), 1)
    sq = (jnp.dot(ks_f * ks_f, jnp.where(seg_c == seg_l, 1.0, 0.0), preferred_element_type=jnp.float32)
          + jnp.dot(kw_f * kw_f, jnp.where(seg_c == seg_l + N_KV_GROUPS, 1.0, 0.0),
                    preferred_element_type=jnp.float32))
    tile_max = jnp.broadcast_to(jnp.max(sq, axis=0, keepdims=True), knorm_ref.shape[1:])

    @pl.when(pl.program_id(1) == 0)
    def _():
        knorm_ref[0] = tile_max

    @pl.when(pl.program_id(1) > 0)
    def _():
        knorm_ref[0] = jnp.maximum(knorm_ref[0], tile_max)

    vst = vs_ref[0].astype(jnp.float32).T.astype(vst_ref.dtype)
    vwt = vw_ref[0].astype(jnp.float32).T.astype(vwt_ref.dtype)
    ones_col = jnp.where(lax.broadcasted_iota(jnp.int32, (ts, HEAD_DIM), 1) == 0, 1.0, 0.0
                         ).astype(kaug_ref.dtype)
    for g in range(N_KV_GROUPS):
        sl = slice(g * HEAD_DIM, (g + 1) * HEAD_DIM)
        kaug_ref[0, g, :, 0:n_blk] = onehot
        kaug_ref[0, g, :, n_blk:n_blk + HEAD_DIM] = ks[:, sl]
        kaug_ref[0, g, :, n_blk + HEAD_DIM:] = ones_col
        kwin_ref[0, g, :, 0:HEAD_DIM] = kw[:, sl]
        kwin_ref[0, g, :, HEAD_DIM:] = ones_col
        vst_ref[0, g] = vst[sl, :]
        vwt_ref[0, g] = vwt[sl, :]


def _kprep(proj3, cos_k, sin_k, rot):
    b, s, _ = proj3.shape
    n_blk = s // SLC_BLOCK
    ts = min(KPREP_TILE, s)
    kvb = COL_KV // KV_WIDTH

    def col(j):
        return pl.BlockSpec((1, ts, KV_WIDTH), lambda i, t: (i, t, kvb + j))

    tab = pl.BlockSpec((ts, HEAD_DIM), lambda i, t: (t, 0))
    aug_w = n_blk + 2 * HEAD_DIM
    return pl.pallas_call(
        _kprep_kernel,
        grid=(b, s // ts),
        in_specs=[col(2), col(3), col(4), col(5), tab, tab, _const_spec(rot.shape)],
        out_specs=[pl.BlockSpec((1, N_KV_GROUPS, ts, aug_w), lambda i, t: (i, 0, t, 0)),
                   pl.BlockSpec((1, N_KV_GROUPS, HEAD_DIM, ts), lambda i, t: (i, 0, 0, t)),
                   pl.BlockSpec((1, N_KV_GROUPS, ts, 2 * HEAD_DIM), lambda i, t: (i, 0, t, 0)),
                   pl.BlockSpec((1, N_KV_GROUPS, HEAD_DIM, ts), lambda i, t: (i, 0, 0, t)),
                   pl.BlockSpec((1, SUBLANE, LANE), lambda i, t: (i, 0, 0))],
        out_shape=[jax.ShapeDtypeStruct((b, N_KV_GROUPS, s, aug_w), MXU_DTYPE),
                   jax.ShapeDtypeStruct((b, N_KV_GROUPS, HEAD_DIM, s), MXU_DTYPE),
                   jax.ShapeDtypeStruct((b, N_KV_GROUPS, s, 2 * HEAD_DIM), MXU_DTYPE),
                   jax.ShapeDtypeStruct((b, N_KV_GROUPS, HEAD_DIM, s), MXU_DTYPE),
                   jax.ShapeDtypeStruct((b, SUBLANE, LANE), jnp.float32)],
        compiler_params=_cparams(2),
        name="kprep",
    )(proj3, proj3, proj3, proj3, cos_k, sin_k, rot)


def _heads_to_lanes(xt):
    return jnp.concatenate([xt[h * HEAD_DIM:(h + 1) * HEAD_DIM, :] for h in range(HEADS_PER_GROUP)],
                           axis=1)


def _vis_rows(tq):
    levels = tq // CMP_STRIDE
    dt = np.arange(tq)
    v = (dt - (CMP_BLOCK - 1)) // CMP_STRIDE - (-(CMP_BLOCK - 1)) // CMP_STRIDE
    rows = np.zeros((HEAD_DIM, tq), np.float32)
    for c in range(1, levels + 1):
        rows[c] = np.where(v < c, -MASK_BIAS, 0.0)
    rows[levels + 1] = -MASK_BIAS
    return jnp.asarray(np.tile(rows, (1, HEADS_PER_GROUP)), MXU_DTYPE)


def _cmpsel_kernel(q_ref, kc_ref, vct_ref, vis_ref, ocmp_ref, bias_ref):
    tq = q_ref.shape[1]
    n_cmp_pad = kc_ref.shape[2]
    n_blk = bias_ref.shape[2]
    m_rows = HEADS_PER_GROUP * tq
    t0 = pl.program_id(1) * tq
    levels = tq // CMP_STRIDE

    row = lax.broadcasted_iota(jnp.int32, (n_cmp_pad, LANE), 0)
    n_idx = (row & (n_blk - 1)) * CMP_PER_SLC + (row >> _log2(n_blk))
    n_first = (t0 - (CMP_BLOCK - 1)) >> _log2(CMP_STRIDE)
    u = jnp.clip(n_idx - n_first, 0, levels + 1)
    vis_cols = jnp.where(lax.broadcasted_iota(jnp.int32, (n_cmp_pad, LANE), 1) - HEAD_DIM == u,
                         1.0, 0.0).astype(MXU_DTYPE)
    vis_rows = vis_ref[...]
    t_row = t0 + (lax.broadcasted_iota(jnp.int32, (1, m_rows), 1) & (tq - 1))
    jrow = lax.broadcasted_iota(jnp.int32, (n_blk, tq), 0)
    jt = (t0 + lax.broadcasted_iota(jnp.int32, (n_blk, tq), 1)) >> _log2(SLC_BLOCK)
    causal_blk = jrow <= jt
    forced = (jrow == 0) | (jrow == jt) | (jrow == jt - 1)
    jf = jrow.astype(jnp.float32)
    n_extract = min(SLC_TOPK, n_blk) - N_FORCED
    want = jnp.minimum(jt[0:1, :] + 1, SLC_TOPK).astype(jnp.float32)

    def to_bias(marks):
        return jnp.where(marks < CHOSEN_LIMIT, 0.0, -MASK_BIAS).astype(bias_ref.dtype)

    pending = []
    for pb in range(q_ref.shape[0]):
        qt = _heads_to_lanes(q_ref[pb].astype(jnp.float32).T).astype(MXU_DTYPE)
        s = jnp.dot(kc_ref[pb, 0] + vis_cols, jnp.concatenate([qt, vis_rows], axis=0),
                    preferred_element_type=jnp.float32)
        mx = jnp.max(s, axis=0, keepdims=True)
        p = jnp.exp2(s - mx)
        den = jnp.sum(p, axis=0, keepdims=True)
        inv = jnp.where(t_row >= CMP_BLOCK - 1, 1.0 / den, 0.0)
        ocmp_ref[pb, 0, 0] = inv * jnp.dot(vct_ref[pb, 0], p.astype(MXU_DTYPE),
                                           preferred_element_type=jnp.float32)

        imp = None
        for h in range(HEADS_PER_GROUP):
            ph = p[:, h * tq:(h + 1) * tq]
            last = ph[3 * n_blk:4 * n_blk, :]
            prev = jnp.where(jrow == 0, 0.0, pltpu.roll(last, 1, 0))
            taps = ph[0:n_blk, :] + ph[n_blk:2 * n_blk, :] + ph[2 * n_blk:3 * n_blk, :] + last + prev
            imp_h = taps * inv[:, h * tq:(h + 1) * tq]
            imp = imp_h if imp is None else imp + imp_h

        val = jnp.where(forced, CHOSEN, jnp.where(causal_blk, imp, -IMP_BIG))
        marks = val
        for _ in range(n_extract):
            marks = jnp.where(marks == jnp.max(marks, axis=0, keepdims=True), CHOSEN, marks)
        bias_ref[pb, 0] = to_bias(marks)
        count = jnp.sum(jnp.where((marks < CHOSEN_LIMIT) & causal_blk, 1.0, 0.0), axis=0, keepdims=True)
        pending.append((pb, val, jnp.max(jnp.abs(count - want)) > 0.5))

    for pb, val, tied in pending:
        @pl.when(tied)
        def _(val=val, pb=pb):
            v = val
            for _ in range(n_extract):
                top = jnp.max(v, axis=0, keepdims=True)
                first = jnp.min(jnp.where(v == top, jf, float(n_blk)), axis=0, keepdims=True)
                v = jnp.where(jf == first, CHOSEN, v)
            bias_ref[pb, 0] = to_bias(v)


def _cmpsel(proj3, kc_r, vc_rt):
    b, s, _ = proj3.shape
    tq = min(Q_TILE, s)
    nq = s // tq
    n_cmp_pad = kc_r.shape[2]
    n_blk = s // SLC_BLOCK
    qb = COL_Q // KV_WIDTH
    m_rows = HEADS_PER_GROUP * tq
    return pl.pallas_call(
        _cmpsel_kernel,
        grid=(N_KV_GROUPS, nq),
        in_specs=[pl.BlockSpec((b, tq, KV_WIDTH), lambda g, q: (0, q, qb + g)),
                  pl.BlockSpec((b, 1, n_cmp_pad, LANE), lambda g, q: (0, g, 0, 0)),
                  pl.BlockSpec((b, 1, HEAD_DIM, n_cmp_pad), lambda g, q: (0, g, 0, 0)),
                  _const_spec((HEAD_DIM, m_rows))],
        out_specs=[pl.BlockSpec((b, 1, 1, HEAD_DIM, m_rows), lambda g, q: (0, g, q, 0, 0)),
                   pl.BlockSpec((b, 1, n_blk, tq), lambda g, q: (0, g, 0, q))],
        out_shape=[jax.ShapeDtypeStruct((b, N_KV_GROUPS, nq, HEAD_DIM, m_rows), jnp.float32),
                   jax.ShapeDtypeStruct((b, N_KV_GROUPS, n_blk, s), MXU_DTYPE)],
        compiler_params=_cparams(2),
        name="cmpsel",
    )(proj3, kc_r, vc_rt, _vis_rows(tq))


def _flash_kernel(q_ref, cos_ref, sin_ref, bias_ref, kaug_ref, vst_ref, kwin_ref, vwt_ref, knorm_ref,
                  gate_ref, ocmp_ref, o_ref, m_ref, l_ref, acc_ref, lw_ref, accw_ref):
    nb = q_ref.shape[0]
    tq = q_ref.shape[1]
    m_rows = HEADS_PER_GROUP * tq
    grp = pl.program_id(0)
    qi = pl.program_id(1)
    t0 = qi * tq
    kd = pl.multiple_of(t0, tq)
    probs = range(nb)
    zpad = jnp.zeros((HEAD_DIM, m_rows), MXU_DTYPE)
    pad_row = lax.broadcasted_iota(jnp.int32, (HEAD_DIM, m_rows), 0)
    lane = lax.broadcasted_iota(jnp.int32, (1, LANE), 1)

    def stab_rows(neg_m):
        return jnp.where(pad_row == 0, neg_m, 0.0).astype(MXU_DTYPE)

    cos = cos_ref[...]
    sin = sin_ref[...]
    qr, bias4, qs_slc, qs_win = [], [], [], []
    for pb in probs:
        qt = q_ref[pb].astype(jnp.float32).T
        parts = []
        for h in range(HEADS_PER_GROUP):
            qh = qt[h * HEAD_DIM:(h + 1) * HEAD_DIM, :]
            x1 = qh[0:ROPE_HALF, :]
            x2 = qh[ROPE_HALF:ROPE_DIM, :]
            parts.append(jnp.concatenate([x1 * cos - x2 * sin, x2 * cos + x1 * sin, qh[ROPE_DIM:, :]],
                                         axis=0))
        qf = jnp.concatenate(parts, axis=1)
        q_sq = jnp.sum(qf * qf, axis=0, keepdims=True)
        k_sq = knorm_ref[pb, 0:1, :]
        ks_sq = jnp.max(jnp.where(lane == grp, k_sq, 0.0), axis=1, keepdims=True)
        kw_sq = jnp.max(jnp.where(lane == grp + N_KV_GROUPS, k_sq, 0.0), axis=1, keepdims=True)
        qr.append(qf.astype(MXU_DTYPE))
        bias4.append(jnp.concatenate([bias_ref[pb, 0]] * HEADS_PER_GROUP, axis=1))
        qs_slc.append(jnp.concatenate([bias4[pb], qr[pb], stab_rows(-STAB_MARGIN * jnp.sqrt(q_sq * ks_sq))],
                                      axis=0))
        qs_win.append(jnp.concatenate([qr[pb], stab_rows(-STAB_MARGIN * jnp.sqrt(q_sq * kw_sq))], axis=0))

    tcol = t0 + (lax.broadcasted_iota(jnp.int32, (tq, m_rows), 1) & (tq - 1))
    koff = lax.broadcasted_iota(jnp.int32, (tq, m_rows), 0)
    n_back = WINDOW // tq
    causal = t0 + koff <= tcol

    def win_tile(back):
        diff = tcol - (t0 - back * tq + koff)
        if back == 0:
            return kd, tq, diff >= 0
        exists = qi >= back
        mask = exists if back * tq + tq - 1 < WINDOW else (diff < WINDOW) & exists
        return pl.multiple_of(jnp.maximum(t0 - back * tq, 0), tq), tq, mask

    def fast_tiles(k_ref, v_ref, qs, tiles, l_out, acc_out, first=False):
        for pb in probs:
            l_add = acc_add = None
            for k0, tk, mask in tiles:
                s = jnp.dot(k_ref[pb, 0, pl.ds(k0, tk), :], qs[pb], preferred_element_type=jnp.float32)
                if mask is not None:
                    s = jnp.where(mask, s, NEG_BIG)
                p = jnp.exp2(s)
                l_t = jnp.sum(p, axis=0, keepdims=True)
                acc_t = jnp.dot(v_ref[pb, 0, :, pl.ds(k0, tk)], p.astype(MXU_DTYPE),
                                preferred_element_type=jnp.float32)
                l_add = l_t if l_add is None else l_add + l_t
                acc_add = acc_t if acc_add is None else acc_add + acc_t
            if first:
                l_out[pb] = l_add
                acc_out[pb] = acc_add
            else:
                l_out[pb] += l_add
                acc_out[pb] += acc_add

    fast_tiles(kaug_ref, vst_ref, qs_slc, [(kd, tq, causal)], l_ref, acc_ref, first=True)
    fast_tiles(kwin_ref, vwt_ref, qs_win, [win_tile(back) for back in range(n_back + 1)],
               lw_ref, accw_ref, first=True)

    sub = SLC_SUBTILE
    n_sub = SLC_SPAN // sub

    def span_body(i, carry):
        base = i * SLC_SPAN
        fast_tiles(kaug_ref, vst_ref, qs_slc,
                   [(pl.multiple_of(base + j * sub, sub), sub, None) for j in range(n_sub)],
                   l_ref, acc_ref)
        return carry

    n_span = t0 // SLC_SPAN
    lax.fori_loop(0, n_span, span_body, 0)
    done = n_span * SLC_SPAN
    size = SLC_SPAN // 2
    while size >= tq:
        take = ((t0 // size) % 2) == 1

        @pl.when(take)
        def _(done=done, size=size):
            piece = min(size, sub)
            fast_tiles(kaug_ref, vst_ref, qs_slc,
                       [(pl.multiple_of(done + j * piece, piece), piece, None) for j in range(size // piece)],
                       l_ref, acc_ref)

        done = done + jnp.where(take, size, 0)
        size //= 2

    def exact_start(pb, s, vt, l_out, acc_out):
        mx = jnp.max(s, axis=0, keepdims=True)
        p = jnp.exp2(s - mx)
        m_ref[pb] = mx
        l_out[pb] = jnp.sum(p, axis=0, keepdims=True)
        acc_out[pb] = jnp.dot(vt, p.astype(MXU_DTYPE), preferred_element_type=jnp.float32)

    def exact_update(pb, s, vt, l_out, acc_out):
        m_old = m_ref[pb]
        mx = jnp.maximum(m_old, jnp.max(s, axis=0, keepdims=True))
        alpha = jnp.exp2(m_old - mx)
        p = jnp.exp2(s - mx)
        m_ref[pb] = mx
        l_out[pb] = alpha * l_out[pb] + jnp.sum(p, axis=0, keepdims=True)
        acc_out[pb] = alpha * acc_out[pb] + jnp.dot(vt, p.astype(MXU_DTYPE),
                                                    preferred_element_type=jnp.float32)

    check = l_low = None
    for pb in probs:
        c = (l_ref[pb] + lw_ref[pb] + jnp.sum(jnp.abs(acc_ref[pb]), axis=0, keepdims=True)
             + jnp.sum(jnp.abs(accw_ref[pb]), axis=0, keepdims=True))
        lo = jnp.minimum(l_ref[pb], lw_ref[pb])
        check = c if check is None else check + c
        l_low = lo if l_low is None else jnp.minimum(l_low, lo)

    @pl.when(_lost_range(check, l_low))
    def _():
        for pb in probs:
            q0 = jnp.concatenate([bias4[pb], qr[pb], zpad], axis=0)
            s = jnp.dot(kaug_ref[pb, 0, pl.ds(kd, tq), :], q0, preferred_element_type=jnp.float32)
            exact_start(pb, jnp.where(causal, s, NEG_BIG), vst_ref[pb, 0, :, pl.ds(kd, tq)], l_ref, acc_ref)

            def body(kt, carry, pb=pb, q0=q0):
                k0 = pl.multiple_of(kt * tq, tq)
                sk = jnp.dot(kaug_ref[pb, 0, pl.ds(k0, tq), :], q0, preferred_element_type=jnp.float32)
                exact_update(pb, sk, vst_ref[pb, 0, :, pl.ds(k0, tq)], l_ref, acc_ref)
                return carry

            lax.fori_loop(0, qi, body, 0)
            qw0 = jnp.concatenate([qr[pb], zpad], axis=0)
            for back in range(n_back + 1):
                kbase = t0 - back * tq
                k0 = pl.multiple_of(jnp.maximum(kbase, 0), tq)
                sw = jnp.dot(kwin_ref[pb, 0, pl.ds(k0, tq), :], qw0, preferred_element_type=jnp.float32)
                diff = tcol - (kbase + koff)
                sw = jnp.where((diff >= 0) & (diff < WINDOW) & (kbase + koff >= 0), sw, NEG_BIG)
                if back == 0:
                    exact_start(pb, sw, vwt_ref[pb, 0, :, pl.ds(k0, tq)], lw_ref, accw_ref)
                else:
                    exact_update(pb, sw, vwt_ref[pb, 0, :, pl.ds(k0, tq)], lw_ref, accw_ref)

    r_id = lax.broadcasted_iota(jnp.int32, (GATE_ROWS, LANE), 0)
    c_id = lax.broadcasted_iota(jnp.int32, (GATE_ROWS, LANE), 1)
    pick = jnp.where((c_id == grp * (HEADS_PER_GROUP * 3) + r_id) & (r_id < HEADS_PER_GROUP * 3),
                     1.0, 0.0).astype(gate_ref.dtype)
    for pb in probs:
        o_slc = acc_ref[pb] / l_ref[pb]
        o_win = accw_ref[pb] / lw_ref[pb]
        graw = lax.dot_general(pick, gate_ref[pb], (((1,), (1,)), ((), ())),
                               preferred_element_type=jnp.float32)
        gates = _sigmoid(graw)
        o_cmp = ocmp_ref[pb, 0, 0]
        mixed = []
        for h in range(HEADS_PER_GROUP):
            sl = slice(h * tq, (h + 1) * tq)
            mixed.append(gates[3 * h:3 * h + 1, :] * o_cmp[:, sl]
                         + gates[3 * h + 1:3 * h + 2, :] * o_slc[:, sl]
                         + gates[3 * h + 2:3 * h + 3, :] * o_win[:, sl])
        o_ref[pb] = jnp.concatenate(mixed, axis=0).T.astype(o_ref.dtype)


def _flash(proj3, cos_q, sin_q, bias, kaug, vst, kwin, vwt, knorm, ocmp):
    b, s, _ = proj3.shape
    tq = min(Q_TILE, s)
    nq = s // tq
    n_blk = s // SLC_BLOCK
    aug_w = kaug.shape[3]
    qb = COL_Q // KV_WIDTH
    m_rows = HEADS_PER_GROUP * tq

    def resident(shape):
        return pl.BlockSpec((b, 1) + shape, lambda g, q: (0, g, 0, 0), pipeline_mode=pl.Buffered(1))

    return pl.pallas_call(
        _flash_kernel,
        grid=(N_KV_GROUPS, nq),
        in_specs=[pl.BlockSpec((b, tq, KV_WIDTH), lambda g, q: (0, q, qb + g)),
                  pl.BlockSpec((ROPE_HALF, tq), lambda g, q: (0, q)),
                  pl.BlockSpec((ROPE_HALF, tq), lambda g, q: (0, q)),
                  pl.BlockSpec((b, 1, n_blk, tq), lambda g, q: (0, g, 0, q)),
                  resident((s, aug_w)),
                  resident((HEAD_DIM, s)),
                  resident((s, 2 * HEAD_DIM)),
                  resident((HEAD_DIM, s)),
                  _const_spec((b, SUBLANE, LANE)),
                  pl.BlockSpec((b, tq, LANE), lambda g, q: (0, q, COL_GNSA // LANE)),
                  pl.BlockSpec((b, 1, 1, HEAD_DIM, m_rows), lambda g, q: (0, g, q, 0, 0))],
        out_specs=pl.BlockSpec((b, tq, KV_WIDTH), lambda g, q: (0, q, g)),
        out_shape=jax.ShapeDtypeStruct((b, s, ATTN_Q_WIDTH), ACT_DTYPE),
        scratch_shapes=[pltpu.VMEM((b, 1, m_rows), jnp.float32),
                        pltpu.VMEM((b, 1, m_rows), jnp.float32),
                        pltpu.VMEM((b, HEAD_DIM, m_rows), jnp.float32),
                        pltpu.VMEM((b, 1, m_rows), jnp.float32),
                        pltpu.VMEM((b, HEAD_DIM, m_rows), jnp.float32)],
        compiler_params=_cparams(2),
        name="flash",
    )(proj3, cos_q, sin_q, bias, kaug, vst, kwin, vwt, knorm, proj3, ocmp)


def _merge_kernel(u_ref, halo_ref, gp_ref, ga_ref, attn_ref, x_ref, pw_ref, ps_ref,
                  wpp_ref, wpa_ref, wo_ref, o_ref, *, seq_len):
    tm = u_ref.shape[0]
    t0 = (pl.program_id(0) * tm) & (seq_len - 1)
    u = u_ref[...].astype(jnp.float32)
    halo = jnp.where(t0 > 0, halo_ref[...].astype(jnp.float32), 0.0)
    ext = jnp.concatenate([halo, u], axis=0)
    t = t0 + lax.broadcasted_iota(jnp.int32, (tm, POOL_GROUP), 0)
    pooled = []
    for gi, w in enumerate(POOL_WINDOWS):
        sl = slice(gi * POOL_GROUP, (gi + 1) * POOL_GROUP)
        acc = ext[:, sl]
        span = 1
        while span < w:
            acc = acc + pltpu.roll(acc, span, 0)
            span *= 2
        cnt = jnp.minimum(t + 1, w).astype(jnp.float32)
        mean = acc[POOL_HALO:, :] / cnt
        mixed = jnp.dot((mean - u[:, sl]).astype(MXU_DTYPE), pw_ref[gi],
                        preferred_element_type=jnp.float32)
        pooled.append(mixed * ps_ref[:, sl])
    pool = jnp.concatenate(pooled, axis=1).astype(MXU_DTYPE)
    pp = jnp.dot(pool, wpp_ref[...], preferred_element_type=jnp.float32)
    pa = jnp.dot(attn_ref[...], wpa_ref[...], preferred_element_type=jnp.float32)
    merged = (_sigmoid(gp_ref[...].astype(jnp.float32)) * pp
              + _sigmoid(ga_ref[...].astype(jnp.float32)) * pa)
    o_ref[...] = x_ref[...] + jnp.dot(merged.astype(MXU_DTYPE), wo_ref[...],
                                      preferred_element_type=jnp.float32)


def _merge(proj2, attn2, x2d, pool_w, pool_scale, wpp, wpa, wo, seq_len):
    m, d = x2d.shape
    tm = min(ROW_TILE, seq_len)
    halo_per_tile = tm // POOL_HALO
    return pl.pallas_call(
        functools.partial(_merge_kernel, seq_len=seq_len),
        grid=(m // tm,),
        in_specs=[pl.BlockSpec((tm, POOL_WIDTH), lambda i: (i, COL_POOL // POOL_WIDTH)),
                  pl.BlockSpec((POOL_HALO, POOL_WIDTH),
                               lambda i: (jnp.maximum(i * halo_per_tile - 1, 0), COL_POOL // POOL_WIDTH)),
                  pl.BlockSpec((tm, d), lambda i: (i, COL_GPOOL // D_MODEL)),
                  pl.BlockSpec((tm, d), lambda i: (i, COL_GATTN // D_MODEL)),
                  pl.BlockSpec((tm, ATTN_Q_WIDTH), lambda i: (i, 0)),
                  pl.BlockSpec((tm, d), lambda i: (i, 0)),
                  _const_spec(pool_w.shape), _const_spec(pool_scale.shape),
                  _const_spec(wpp.shape), _const_spec(wpa.shape), _const_spec(wo.shape)],
        out_specs=pl.BlockSpec((tm, d), lambda i: (i, 0)),
        out_shape=jax.ShapeDtypeStruct((m, d), jnp.float32),
        compiler_params=_cparams(1),
        name="merge",
    )(proj2, proj2, proj2, proj2, attn2, x2d, pool_w, pool_scale, wpp, wpa, wo)


def _ffn_kernel(x_ref, n2_ref, wg_ref, wu_ref, wd_ref, nf_ref, o_ref):
    x = x_ref[...]
    h = _rms(x, n2_ref[...]).astype(MXU_DTYPE)
    d_ff = wg_ref.shape[1]
    acc = x
    for c0 in range(0, d_ff, FF_CHUNK):
        cw = min(FF_CHUNK, d_ff - c0)
        gate = jnp.dot(h, wg_ref[:, c0:c0 + cw], preferred_element_type=jnp.float32)
        up = jnp.dot(h, wu_ref[:, c0:c0 + cw], preferred_element_type=jnp.float32)
        act = (gate * _sigmoid(gate) * up).astype(MXU_DTYPE)
        acc = acc + jnp.dot(act, wd_ref[c0:c0 + cw, :], preferred_element_type=jnp.float32)
    o_ref[...] = _rms(acc, nf_ref[...])


def _ffn(x1, norm2_w, wg, wu, wd, norm_f_w):
    m, d = x1.shape
    return pl.pallas_call(
        _ffn_kernel,
        grid=(m // ROW_TILE,),
        in_specs=[pl.BlockSpec((ROW_TILE, d), lambda i: (i, 0)),
                  _const_spec(norm2_w.shape), _const_spec(wg.shape), _const_spec(wu.shape),
                  _const_spec(wd.shape), _const_spec(norm_f_w.shape)],
        out_specs=pl.BlockSpec((ROW_TILE, d), lambda i: (i, 0)),
        out_shape=jax.ShapeDtypeStruct((m, d), jnp.float32),
        compiler_params=_cparams(1),
        name="ffn",
    )(x1, norm2_w, wg, wu, wd, norm_f_w)


def _rope_tables(seq_len):
    inv_freq = 1.0 / (ROPE_THETA ** (jnp.arange(0, ROPE_DIM, 2, dtype=jnp.float32) / ROPE_DIM))
    ang = jnp.arange(seq_len).astype(jnp.float32)[:, None] * inv_freq[None, :]
    cos, sin = jnp.cos(ang), jnp.sin(ang)
    rest = HEAD_DIM - ROPE_DIM
    cos_h = jnp.concatenate([cos, cos, jnp.ones((seq_len, rest), jnp.float32)], axis=1)
    sin_h = jnp.concatenate([sin, sin, jnp.zeros((seq_len, rest), jnp.float32)], axis=1)
    return cos.T, sin.T, cos_h, sin_h


def _rotate_half_matrix():
    rot = np.zeros((KV_WIDTH, KV_WIDTH), np.float32)
    for l in range(KV_WIDTH):
        d = l % HEAD_DIM
        if d < ROPE_HALF:
            rot[l + ROPE_HALF, l] = -1.0
        elif d < ROPE_DIM:
            rot[l - ROPE_HALF, l] = 1.0
    return jnp.asarray(rot, MXU_DTYPE)


def _layer(x, norm1_w, w_in, pool_w, pool_scale, cmp_pe_k, cmp_w1_k, cmp_b1_k, cmp_w2_k,
           cmp_pe_v, cmp_w1_v, cmp_b1_v, cmp_w2_v, w_proj_pool, w_proj_attn, w_out):
    b, s, d = x.shape
    m = b * s
    n_blk = s // SLC_BLOCK
    assert s % Q_TILE == 0 and s % ROW_TILE == 0 and (s & (s - 1)) == 0, s
    cd = MXU_DTYPE

    g0 = COL_GPOOL
    w_parts = (w_in[:, :g0].astype(cd), w_in[:, g0 + GATE_NSA:].astype(cd),
               jnp.pad(w_in[:, g0:g0 + GATE_NSA], ((0, 0), (0, LANE - GATE_NSA))).astype(cd))
    x2d = x.reshape(m, d)
    proj2, kvc = _inproj(x2d, norm1_w.reshape(1, d), w_parts)
    proj3 = proj2.reshape(b, s, PROJ_WIDTH)

    def pe8(pe):
        return jnp.broadcast_to(pe.reshape(1, -1), (8, pe.size)).astype(cd)

    kvc_rows = kvc.reshape(2 * N_KV_GROUPS, b, n_blk, SLC_BLOCK * HEAD_DIM)
    kc_r, vc_rt = _compress(kvc_rows,
                            pe8(cmp_pe_k), cmp_w1_k.astype(cd), cmp_b1_k.reshape(1, -1), cmp_w2_k.astype(cd),
                            pe8(cmp_pe_v), cmp_w1_v.astype(cd), cmp_b1_v.reshape(1, -1), cmp_w2_v.astype(cd))

    cos_q, sin_q, cos_k, sin_k = _rope_tables(s)
    kaug, vst, kwin, vwt, knorm = _kprep(proj3, cos_k, sin_k, _rotate_half_matrix())
    ocmp, bias = _cmpsel(proj3, kc_r, vc_rt)
    attn = _flash(proj3, cos_q, sin_q, bias, kaug, vst, kwin, vwt, knorm, ocmp)
    return _merge(proj2, attn.reshape(m, ATTN_Q_WIDTH), x2d, pool_w.astype(cd),
                  pool_scale.reshape(1, -1), w_proj_pool.astype(cd), w_proj_attn.astype(cd),
                  w_out.astype(cd), s)


def kernel(x, norm1_w, w_in, pool_w, pool_scale, cmp_pe_k, cmp_w1_k, cmp_b1_k, cmp_w2_k, cmp_pe_v,
           cmp_w1_v, cmp_b1_v, cmp_w2_v, w_proj_pool, w_proj_attn, w_out, norm2_w, w_ffn_gate,
           w_ffn_up, w_ffn_down, norm_f_w):
    b, s, d = x.shape
    depth = w_in.shape[0]
    assert depth == 1, "the final norm is fused into the last layer's FFN kernel"
    cd = MXU_DTYPE
    x1 = _layer(x, norm1_w[0], w_in[0], pool_w[0], pool_scale[0], cmp_pe_k[0], cmp_w1_k[0], cmp_b1_k[0],
                cmp_w2_k[0], cmp_pe_v[0], cmp_w1_v[0], cmp_b1_v[0], cmp_w2_v[0], w_proj_pool[0],
                w_proj_attn[0], w_out[0])
    out = _ffn(x1, norm2_w[0].reshape(1, d), w_ffn_gate[0].astype(cd), w_ffn_up[0].astype(cd),
               w_ffn_down[0].astype(cd), norm_f_w.reshape(1, d))
    return out.reshape(b, s, d)
```

```python
import functools

import numpy as np
import jax
import jax.numpy as jnp
from jax import lax
from jax.experimental import pallas as pl
from jax.experimental.pallas import tpu as pltpu

D_MODEL = 1024
N_HEADS = 16
HEAD_DIM = 64
N_KV_GROUPS = 4
HEADS_PER_GROUP = N_HEADS // N_KV_GROUPS
ROPE_DIM = HEAD_DIM // 4
ROPE_HALF = ROPE_DIM // 2
ROPE_THETA = 500000.0
CMP_BLOCK = 32
CMP_STRIDE = 16
CMP_PER_SLC = 4
SLC_BLOCK = 64
SLC_TOPK = 16
WINDOW = 512
ATTN_Q_WIDTH = N_HEADS * HEAD_DIM
KV_WIDTH = N_KV_GROUPS * HEAD_DIM
POOL_WIDTH = D_MODEL // 2
POOL_WINDOWS = (2, 4, 8, 16)
POOL_GROUP = POOL_WIDTH // len(POOL_WINDOWS)
POOL_HALO = 16
RMS_EPS = 1e-6

LANE = 128
SUBLANE = 8
MXU_DTYPE = jnp.bfloat16
ACT_DTYPE = jnp.bfloat16
VMEM_LIMIT_BYTES = 56 * 1024 * 1024

GATE_NSA = N_HEADS * 3
COL_POOL = 0
COL_Q = COL_POOL + POOL_WIDTH
COL_KV = COL_Q + ATTN_Q_WIDTH
COL_GPOOL = COL_KV + 6 * KV_WIDTH
COL_GATTN = COL_GPOOL + D_MODEL
COL_GNSA = COL_GATTN + D_MODEL
PROJ_WIDTH = COL_GNSA + LANE
GATE_ROWS = 16

ROW_TILE = 512
Q_TILE = 256
SLC_SPAN = 2048
SLC_SUBTILE = 512
KPREP_TILE = 1024
FF_CHUNK = 512
SCORE_SCALE = HEAD_DIM ** -0.5
LOG2E = float(np.log2(np.e))
FINITE_LIMIT = 3e38
STAB_MARGIN = 1.03
L_MIN = 2.0 ** -64
MASK_BIAS = float(2 ** 30)
NEG_BIG = -1e30
IMP_BIG = 1e30
CHOSEN = -3e38
CHOSEN_LIMIT = -2e38
N_FORCED = 3


def _log2(n):
    k = int(n).bit_length() - 1
    assert 1 << k == n, n
    return k


def _cparams(n_grid):
    return pltpu.CompilerParams(dimension_semantics=("arbitrary",) * n_grid,
                                vmem_limit_bytes=VMEM_LIMIT_BYTES)


def _const_spec(shape):
    nd = len(shape)
    return pl.BlockSpec(shape, lambda *_: (0,) * nd, pipeline_mode=pl.Buffered(1))


def _sigmoid(x):
    return 1.0 / (1.0 + jnp.exp(-x))


def _lost_range(total, smallest):
    return jnp.logical_not(jnp.sum(total) < FINITE_LIMIT) | (jnp.min(smallest) < L_MIN)


def _rms(x, w):
    var = jnp.mean(x * x, axis=-1, keepdims=True)
    return x * lax.rsqrt(var + RMS_EPS) * w


def _inproj_kernel(x_ref, nw_ref, wa_ref, wb_ref, wc_ref, o_ref, kvc_ref):
    h = _rms(x_ref[...], nw_ref[...]).astype(MXU_DTYPE)
    n = o_ref.shape[1]
    for c0 in range(0, n, KV_WIDTH):
        cw = min(KV_WIDTH, n - c0)
        w_ref, w0 = ((wa_ref, 0) if c0 < COL_GPOOL else (wb_ref, COL_GPOOL) if c0 < COL_GNSA
                     else (wc_ref, COL_GNSA))
        res = jnp.dot(h, w_ref[:, c0 - w0:c0 - w0 + cw],
                      preferred_element_type=jnp.float32).astype(o_ref.dtype)
        o_ref[:, c0:c0 + cw] = res
        which = (c0 - COL_KV) // KV_WIDTH
        if c0 >= COL_KV and which < 2:
            for g in range(N_KV_GROUPS):
                kvc_ref[which * N_KV_GROUPS + g] = res[:, g * HEAD_DIM:(g + 1) * HEAD_DIM]


def _inproj(x2d, norm_w, w_parts):
    m, d = x2d.shape
    n = sum(w.shape[1] for w in w_parts)
    assert n == PROJ_WIDTH
    return pl.pallas_call(
        _inproj_kernel,
        grid=(m // ROW_TILE,),
        in_specs=[pl.BlockSpec((ROW_TILE, d), lambda i: (i, 0)),
                  _const_spec((1, d))] + [_const_spec(w.shape) for w in w_parts],
        out_specs=[pl.BlockSpec((ROW_TILE, n), lambda i: (i, 0)),
                   pl.BlockSpec((2 * N_KV_GROUPS, ROW_TILE, HEAD_DIM), lambda i: (0, i, 0))],
        out_shape=[jax.ShapeDtypeStruct((m, n), ACT_DTYPE),
                   jax.ShapeDtypeStruct((2 * N_KV_GROUPS, m, HEAD_DIM), ACT_DTYPE)],
        compiler_params=_cparams(1),
        name="inproj",
    )(x2d, norm_w, *w_parts)


def _gelu_tanh(x):
    return 0.5 * x * (1.0 + jnp.tanh(np.sqrt(2.0 / np.pi) * (x + 0.044715 * (x * x * x))))


def _compress_one(r_ref, pe_ref, w1_ref, b1_ref, w2_ref):
    half = w1_ref.shape[0] // 2
    n_blk = r_ref.shape[2]
    rows = [r_ref[0, 0, :, r * half:(r + 1) * half] for r in range(CMP_PER_SLC)]
    top = [jnp.dot(x, w1_ref[0:half, :], preferred_element_type=jnp.float32) for x in rows]
    bot = [jnp.dot(x, w1_ref[half:, :], preferred_element_type=jnp.float32) for x in rows]
    bias = jnp.dot(pe_ref[...], w1_ref[...], preferred_element_type=jnp.float32)[0:1, :] + b1_ref[...]
    nxt = bot[1:] + [pltpu.roll(bot[0], n_blk - 1, 0)]
    hid = jnp.concatenate([top[r] + nxt[r] for r in range(CMP_PER_SLC)], axis=0) + bias
    return jnp.dot(_gelu_tanh(hid).astype(MXU_DTYPE), w2_ref[...], preferred_element_type=jnp.float32)


def _compress_kernel(rk_ref, rv_ref, pek_ref, w1k_ref, b1k_ref, w2k_ref,
                     pev_ref, w1v_ref, b1v_ref, w2v_ref, ok_ref, ovt_ref):
    kc = _compress_one(rk_ref, pek_ref, w1k_ref, b1k_ref, w2k_ref)
    vc = _compress_one(rv_ref, pev_ref, w1v_ref, b1v_ref, w2v_ref)
    ok_ref[0, 0] = jnp.concatenate([kc * (SCORE_SCALE * LOG2E), jnp.zeros_like(kc)], axis=1).astype(ok_ref.dtype)
    ovt_ref[0, 0] = jnp.concatenate([kc, vc], axis=1).T[HEAD_DIM:, :].astype(ovt_ref.dtype)


def _compress(kvc_rows, pek, w1k, b1k, w2k, pev, w1v, b1v, w2v):
    _, b, n_blk, width = kvc_rows.shape
    dk = w2k.shape[1]
    n_rows = n_blk * CMP_PER_SLC
    w_specs = [_const_spec(a.shape) for a in (pek, w1k, b1k, w2k)]
    return pl.pallas_call(
        _compress_kernel,
        grid=(b, N_KV_GROUPS),
        in_specs=[pl.BlockSpec((1, 1, n_blk, width), lambda i, j: (j, i, 0, 0)),
                  pl.BlockSpec((1, 1, n_blk, width), lambda i, j: (N_KV_GROUPS + j, i, 0, 0))]
        + w_specs + w_specs,
        out_specs=[pl.BlockSpec((1, 1, n_rows, 2 * dk), lambda i, j: (i, j, 0, 0)),
                   pl.BlockSpec((1, 1, dk, n_rows), lambda i, j: (i, j, 0, 0))],
        out_shape=[jax.ShapeDtypeStruct((b, N_KV_GROUPS, n_rows, 2 * dk), MXU_DTYPE),
                   jax.ShapeDtypeStruct((b, N_KV_GROUPS, dk, n_rows), MXU_DTYPE)],
        compiler_params=_cparams(2),
        name="compress",
    )(kvc_rows, kvc_rows, pek, w1k, b1k, w2k, pev, w1v, b1v, w2v)


def _kprep_kernel(ks_ref, vs_ref, kw_ref, vw_ref, cos_ref, sin_ref, rot_ref,
                  kaug_ref, vst_ref, kwin_ref, vwt_ref, knorm_ref):
    ts = ks_ref.shape[1]
    t0 = pl.program_id(1) * ts
    cos = jnp.concatenate([cos_ref[...]] * N_KV_GROUPS, axis=1)
    sin = jnp.concatenate([sin_ref[...]] * N_KV_GROUPS, axis=1)
    rot = rot_ref[...]

    def rope(k_ref):
        k = k_ref[0]
        krot = jnp.dot(k.astype(MXU_DTYPE), rot, preferred_element_type=jnp.float32)
        return (k.astype(jnp.float32) * cos + krot * sin) * (SCORE_SCALE * LOG2E)

    n_blk = kaug_ref.shape[3] - 2 * HEAD_DIM
    blk = lax.broadcasted_iota(jnp.int32, (ts, n_blk), 1)
    key_blk = (t0 + lax.broadcasted_iota(jnp.int32, (ts, n_blk), 0)) >> _log2(SLC_BLOCK)
    onehot = jnp.where(blk == key_blk, 1.0, 0.0).astype(kaug_ref.dtype)
    ks_f = rope(ks_ref)
    kw_f = rope(kw_ref)
    ks = ks_f.astype(kaug_ref.dtype)
    kw = kw_f.astype(kwin_ref.dtype)

    seg_l = lax.broadcasted_iota(jnp.int32, (KV_WIDTH, LANE), 0) >> _log2(HEAD_DIM)
    seg_c = lax.broadcasted_iota(jnp.int32, (KV_WIDTH, LANE), 1)
    sq = (jnp.dot(ks_f * ks_f, jnp.where(seg_c == seg_l, 1.0, 0.0), preferred_element_type=jnp.float32)
          + jnp.dot(kw_f * kw_f, jnp.where(seg_c == seg_l + N_KV_GROUPS, 1.0, 0.0),
                    preferred_element_type=jnp.float32))
    tile_max = jnp.broadcast_to(jnp.max(sq, axis=0, keepdims=True), knorm_ref.shape[1:])

    @pl.when(pl.program_id(1) == 0)
    def _():
        knorm_ref[0] = tile_max

    @pl.when(pl.program_id(1) > 0)
    def _():
        knorm_ref[0] = jnp.maximum(knorm_ref[0], tile_max)

    vst = vs_ref[0].astype(jnp.float32).T.astype(vst_ref.dtype)
    vwt = vw_ref[0].astype(jnp.float32).T.astype(vwt_ref.dtype)
    ones_col = jnp.where(lax.broadcasted_iota(jnp.int32, (ts, HEAD_DIM), 1) == 0, 1.0, 0.0
                         ).astype(kaug_ref.dtype)
    for g in range(N_KV_GROUPS):
        sl = slice(g * HEAD_DIM, (g + 1) * HEAD_DIM)
        kaug_ref[0, g, :, 0:n_blk] = onehot
        kaug_ref[0, g, :, n_blk:n_blk + HEAD_DIM] = ks[:, sl]
        kaug_ref[0, g, :, n_blk + HEAD_DIM:] = ones_col
        kwin_ref[0, g, :, 0:HEAD_DIM] = kw[:, sl]
        kwin_ref[0, g, :, HEAD_DIM:] = ones_col
        vst_ref[0, g] = vst[sl, :]
        vwt_ref[0, g] = vwt[sl, :]


def _kprep(proj3, cos_k, sin_k, rot):
    b, s, _ = proj3.shape
    n_blk = s // SLC_BLOCK
    ts = min(KPREP_TILE, s)
    kvb = COL_KV // KV_WIDTH

    def col(j):
        return pl.BlockSpec((1, ts, KV_WIDTH), lambda i, t: (i, t, kvb + j))

    tab = pl.BlockSpec((ts, HEAD_DIM), lambda i, t: (t, 0))
    aug_w = n_blk + 2 * HEAD_DIM
    return pl.pallas_call(
        _kprep_kernel,
        grid=(b, s // ts),
        in_specs=[col(2), col(3), col(4), col(5), tab, tab, _const_spec(rot.shape)],
        out_specs=[pl.BlockSpec((1, N_KV_GROUPS, ts, aug_w), lambda i, t: (i, 0, t, 0)),
                   pl.BlockSpec((1, N_KV_GROUPS, HEAD_DIM, ts), lambda i, t: (i, 0, 0, t)),
                   pl.BlockSpec((1, N_KV_GROUPS, ts, 2 * HEAD_DIM), lambda i, t: (i, 0, t, 0)),
                   pl.BlockSpec((1, N_KV_GROUPS, HEAD_DIM, ts), lambda i, t: (i, 0, 0, t)),
                   pl.BlockSpec((1, SUBLANE, LANE), lambda i, t: (i, 0, 0))],
        out_shape=[jax.ShapeDtypeStruct((b, N_KV_GROUPS, s, aug_w), MXU_DTYPE),
                   jax.ShapeDtypeStruct((b, N_KV_GROUPS, HEAD_DIM, s), MXU_DTYPE),
                   jax.ShapeDtypeStruct((b, N_KV_GROUPS, s, 2 * HEAD_DIM), MXU_DTYPE),
                   jax.ShapeDtypeStruct((b, N_KV_GROUPS, HEAD_DIM, s), MXU_DTYPE),
                   jax.ShapeDtypeStruct((b, SUBLANE, LANE), jnp.float32)],
        compiler_params=_cparams(2),
        name="kprep",
    )(proj3, proj3, proj3, proj3, cos_k, sin_k, rot)


def _heads_to_lanes(xt):
    return jnp.concatenate([xt[h * HEAD_DIM:(h + 1) * HEAD_DIM, :] for h in range(HEADS_PER_GROUP)],
                           axis=1)


def _vis_rows(tq):
    levels = tq // CMP_STRIDE
    dt = np.arange(tq)
    v = (dt - (CMP_BLOCK - 1)) // CMP_STRIDE - (-(CMP_BLOCK - 1)) // CMP_STRIDE
    rows = np.zeros((HEAD_DIM, tq), np.float32)
    for c in range(1, levels + 1):
        rows[c] = np.where(v < c, -MASK_BIAS, 0.0)
    rows[levels + 1] = -MASK_BIAS
    return jnp.asarray(np.tile(rows, (1, HEADS_PER_GROUP)), MXU_DTYPE)


def _cmpsel_kernel(q_ref, kc_ref, vct_ref, vis_ref, ocmp_ref, bias_ref):
    tq = q_ref.shape[1]
    n_cmp_pad = kc_ref.shape[2]
    n_blk = bias_ref.shape[2]
    m_rows = HEADS_PER_GROUP * tq
    t0 = pl.program_id(1) * tq
    levels = tq // CMP_STRIDE

    row = lax.broadcasted_iota(jnp.int32, (n_cmp_pad, LANE), 0)
    n_idx = (row & (n_blk - 1)) * CMP_PER_SLC + (row >> _log2(n_blk))
    n_first = (t0 - (CMP_BLOCK - 1)) >> _log2(CMP_STRIDE)
    u = jnp.clip(n_idx - n_first, 0, levels + 1)
    vis_cols = jnp.where(lax.broadcasted_iota(jnp.int32, (n_cmp_pad, LANE), 1) - HEAD_DIM == u,
                         1.0, 0.0).astype(MXU_DTYPE)
    vis_rows = vis_ref[...]
    t_row = t0 + (lax.broadcasted_iota(jnp.int32, (1, m_rows), 1) & (tq - 1))
    jrow = lax.broadcasted_iota(jnp.int32, (n_blk, tq), 0)
    jt = (t0 + lax.broadcasted_iota(jnp.int32, (n_blk, tq), 1)) >> _log2(SLC_BLOCK)
    causal_blk = jrow <= jt
    forced = (jrow == 0) | (jrow == jt) | (jrow == jt - 1)
    jf = jrow.astype(jnp.float32)
    n_extract = min(SLC_TOPK, n_blk) - N_FORCED
    want = jnp.minimum(jt[0:1, :] + 1, SLC_TOPK).astype(jnp.float32)

    def to_bias(marks):
        return jnp.where(marks < CHOSEN_LIMIT, 0.0, -MASK_BIAS).astype(bias_ref.dtype)

    pending = []
    for pb in range(q_ref.shape[0]):
        qt = _heads_to_lanes(q_ref[pb].astype(jnp.float32).T).astype(MXU_DTYPE)
        s = jnp.dot(kc_ref[pb, 0] + vis_cols, jnp.concatenate([qt, vis_rows], axis=0),
                    preferred_element_type=jnp.float32)
        mx = jnp.max(s, axis=0, keepdims=True)
        p = jnp.exp2(s - mx)
        den = jnp.sum(p, axis=0, keepdims=True)
        inv = jnp.where(t_row >= CMP_BLOCK - 1, 1.0 / den, 0.0)
        ocmp_ref[pb, 0, 0] = inv * jnp.dot(vct_ref[pb, 0], p.astype(MXU_DTYPE),
                                           preferred_element_type=jnp.float32)

        imp = None
        for h in range(HEADS_PER_GROUP):
            ph = p[:, h * tq:(h + 1) * tq]
            last = ph[3 * n_blk:4 * n_blk, :]
            prev = jnp.where(jrow == 0, 0.0, pltpu.roll(last, 1, 0))
            taps = ph[0:n_blk, :] + ph[n_blk:2 * n_blk, :] + ph[2 * n_blk:3 * n_blk, :] + last + prev
            imp_h = taps * inv[:, h * tq:(h + 1) * tq]
            imp = imp_h if imp is None else imp + imp_h

        val = jnp.where(forced, CHOSEN, jnp.where(causal_blk, imp, -IMP_BIG))
        marks = val
        for _ in range(n_extract):
            marks = jnp.where(marks == jnp.max(marks, axis=0, keepdims=True), CHOSEN, marks)
        bias_ref[pb, 0] = to_bias(marks)
        count = jnp.sum(jnp.where((marks < CHOSEN_LIMIT) & causal_blk, 1.0, 0.0), axis=0, keepdims=True)
        pending.append((pb, val, jnp.max(jnp.abs(count - want)) > 0.5))

    for pb, val, tied in pending:
        @pl.when(tied)
        def _(val=val, pb=pb):
            v = val
            for _ in range(n_extract):
                top = jnp.max(v, axis=0, keepdims=True)
                first = jnp.min(jnp.where(v == top, jf, float(n_blk)), axis=0, keepdims=True)
                v = jnp.where(jf == first, CHOSEN, v)
            bias_ref[pb, 0] = to_bias(v)


def _cmpsel(proj3, kc_r, vc_rt):
    b, s, _ = proj3.shape
    tq = min(Q_TILE, s)
    nq = s // tq
    n_cmp_pad = kc_r.shape[2]
    n_blk = s // SLC_BLOCK
    qb = COL_Q // KV_WIDTH
    m_rows = HEADS_PER_GROUP * tq
    return pl.pallas_call(
        _cmpsel_kernel,
        grid=(N_KV_GROUPS, nq),
        in_specs=[pl.BlockSpec((b, tq, KV_WIDTH), lambda g, q: (0, q, qb + g)),
                  pl.BlockSpec((b, 1, n_cmp_pad, LANE), lambda g, q: (0, g, 0, 0)),
                  pl.BlockSpec((b, 1, HEAD_DIM, n_cmp_pad), lambda g, q: (0, g, 0, 0)),
                  _const_spec((HEAD_DIM, m_rows))],
        out_specs=[pl.BlockSpec((b, 1, 1, HEAD_DIM, m_rows), lambda g, q: (0, g, q, 0, 0)),
                   pl.BlockSpec((b, 1, n_blk, tq), lambda g, q: (0, g, 0, q))],
        out_shape=[jax.ShapeDtypeStruct((b, N_KV_GROUPS, nq, HEAD_DIM, m_rows), jnp.float32),
                   jax.ShapeDtypeStruct((b, N_KV_GROUPS, n_blk, s), MXU_DTYPE)],
        compiler_params=_cparams(2),
        name="cmpsel",
    )(proj3, kc_r, vc_rt, _vis_rows(tq))


def _flash_kernel(q_ref, cos_ref, sin_ref, bias_ref, kaug_ref, vst_ref, kwin_ref, vwt_ref, knorm_ref,
                  wmask_ref, gate_ref, ocmp_ref, o_ref, m_ref, l_ref, acc_ref, lw_ref, accw_ref):
    nb = q_ref.shape[0]
    tq = q_ref.shape[1]
    m_rows = HEADS_PER_GROUP * tq
    grp = pl.program_id(0)
    qi = pl.program_id(1)
    t0 = qi * tq
    kd = pl.multiple_of(t0, tq)
    probs = range(nb)
    zpad = jnp.zeros((HEAD_DIM, m_rows), MXU_DTYPE)
    pad_row = lax.broadcasted_iota(jnp.int32, (HEAD_DIM, m_rows), 0)
    lane = lax.broadcasted_iota(jnp.int32, (1, LANE), 1)

    def stab_rows(neg_m):
        return jnp.where(pad_row == 0, neg_m, 0.0).astype(MXU_DTYPE)

    cos = cos_ref[...]
    sin = sin_ref[...]
    qr, bias4, qs_slc, qs_win = [], [], [], []
    for pb in probs:
        qt = q_ref[pb].astype(jnp.float32).T
        parts = []
        for h in range(HEADS_PER_GROUP):
            qh = qt[h * HEAD_DIM:(h + 1) * HEAD_DIM, :]
            x1 = qh[0:ROPE_HALF, :]
            x2 = qh[ROPE_HALF:ROPE_DIM, :]
            parts.append(jnp.concatenate([x1 * cos - x2 * sin, x2 * cos + x1 * sin, qh[ROPE_DIM:, :]],
                                         axis=0))
        qf = jnp.concatenate(parts, axis=1)
        q_sq = jnp.sum(qf * qf, axis=0, keepdims=True)
        k_sq = knorm_ref[pb, 0:1, :]
        ks_sq = jnp.max(jnp.where(lane == grp, k_sq, 0.0), axis=1, keepdims=True)
        kw_sq = jnp.max(jnp.where(lane == grp + N_KV_GROUPS, k_sq, 0.0), axis=1, keepdims=True)
        qr.append(qf.astype(MXU_DTYPE))
        bias4.append(jnp.concatenate([bias_ref[pb, 0]] * HEADS_PER_GROUP, axis=1))
        qs_slc.append(jnp.concatenate([bias4[pb], qr[pb], stab_rows(-STAB_MARGIN * jnp.sqrt(q_sq * ks_sq))],
                                      axis=0))
        qs_win.append(jnp.concatenate([qr[pb], stab_rows(-STAB_MARGIN * jnp.sqrt(q_sq * kw_sq))], axis=0))

    n_back = WINDOW // tq

    def win_tile(back):
        if back == 0:
            return kd, tq, wmask_ref[0]
        inside = wmask_ref[back] if back * tq + tq - 1 >= WINDOW else 0.0
        return (pl.multiple_of(jnp.maximum(t0 - back * tq, 0), tq), tq,
                jnp.where(qi >= back, inside, NEG_BIG))

    def fast_tiles(k_ref, v_ref, qs, tiles, l_out, acc_out, first=False):
        for pb in probs:
            l_add = acc_add = None
            for k0, tk, mask in tiles:
                s = jnp.dot(k_ref[pb, 0, pl.ds(k0, tk), :], qs[pb], preferred_element_type=jnp.float32)
                if mask is not None:
                    s = s + mask
                p = jnp.exp2(s)
                l_t = jnp.sum(p, axis=0, keepdims=True)
                acc_t = jnp.dot(v_ref[pb, 0, :, pl.ds(k0, tk)], p.astype(MXU_DTYPE),
                                preferred_element_type=jnp.float32)
                l_add = l_t if l_add is None else l_add + l_t
                acc_add = acc_t if acc_add is None else acc_add + acc_t
            if first:
                l_out[pb] = l_add
                acc_out[pb] = acc_add
            else:
                l_out[pb] += l_add
                acc_out[pb] += acc_add

    fast_tiles(kaug_ref, vst_ref, qs_slc, [(kd, tq, wmask_ref[0])], l_ref, acc_ref, first=True)
    fast_tiles(kwin_ref, vwt_ref, qs_win, [win_tile(back) for back in range(n_back + 1)],
               lw_ref, accw_ref, first=True)

    sub = SLC_SUBTILE
    n_sub = SLC_SPAN // sub

    def span_body(i, carry):
        base = i * SLC_SPAN
        fast_tiles(kaug_ref, vst_ref, qs_slc,
                   [(pl.multiple_of(base + j * sub, sub), sub, None) for j in range(n_sub)],
                   l_ref, acc_ref)
        return carry

    n_span = t0 // SLC_SPAN
    lax.fori_loop(0, n_span, span_body, 0)
    done = n_span * SLC_SPAN
    size = SLC_SPAN // 2
    while size >= tq:
        take = ((t0 // size) % 2) == 1

        @pl.when(take)
        def _(done=done, size=size):
            piece = min(size, sub)
            fast_tiles(kaug_ref, vst_ref, qs_slc,
                       [(pl.multiple_of(done + j * piece, piece), piece, None) for j in range(size // piece)],
                       l_ref, acc_ref)

        done = done + jnp.where(take, size, 0)
        size //= 2

    def exact_start(pb, s, vt, l_out, acc_out):
        mx = jnp.max(s, axis=0, keepdims=True)
        p = jnp.exp2(s - mx)
        m_ref[pb] = mx
        l_out[pb] = jnp.sum(p, axis=0, keepdims=True)
        acc_out[pb] = jnp.dot(vt, p.astype(MXU_DTYPE), preferred_element_type=jnp.float32)

    def exact_update(pb, s, vt, l_out, acc_out):
        m_old = m_ref[pb]
        mx = jnp.maximum(m_old, jnp.max(s, axis=0, keepdims=True))
        alpha = jnp.exp2(m_old - mx)
        p = jnp.exp2(s - mx)
        m_ref[pb] = mx
        l_out[pb] = alpha * l_out[pb] + jnp.sum(p, axis=0, keepdims=True)
        acc_out[pb] = alpha * acc_out[pb] + jnp.dot(vt, p.astype(MXU_DTYPE),
                                                    preferred_element_type=jnp.float32)

    check = l_low = None
    for pb in probs:
        c = (l_ref[pb] + lw_ref[pb] + jnp.sum(jnp.abs(acc_ref[pb]), axis=0, keepdims=True)
             + jnp.sum(jnp.abs(accw_ref[pb]), axis=0, keepdims=True))
        lo = jnp.minimum(l_ref[pb], lw_ref[pb])
        check = c if check is None else check + c
        l_low = lo if l_low is None else jnp.minimum(l_low, lo)

    @pl.when(_lost_range(check, l_low))
    def _():
        tcol = t0 + (lax.broadcasted_iota(jnp.int32, (tq, m_rows), 1) & (tq - 1))
        koff = lax.broadcasted_iota(jnp.int32, (tq, m_rows), 0)
        causal = t0 + koff <= tcol
        for pb in probs:
            q0 = jnp.concatenate([bias4[pb], qr[pb], zpad], axis=0)
            s = jnp.dot(kaug_ref[pb, 0, pl.ds(kd, tq), :], q0, preferred_element_type=jnp.float32)
            exact_start(pb, jnp.where(causal, s, NEG_BIG), vst_ref[pb, 0, :, pl.ds(kd, tq)], l_ref, acc_ref)

            def body(kt, carry, pb=pb, q0=q0):
                k0 = pl.multiple_of(kt * tq, tq)
                sk = jnp.dot(kaug_ref[pb, 0, pl.ds(k0, tq), :], q0, preferred_element_type=jnp.float32)
                exact_update(pb, sk, vst_ref[pb, 0, :, pl.ds(k0, tq)], l_ref, acc_ref)
                return carry

            lax.fori_loop(0, qi, body, 0)
            qw0 = jnp.concatenate([qr[pb], zpad], axis=0)
            for back in range(n_back + 1):
                kbase = t0 - back * tq
                k0 = pl.multiple_of(jnp.maximum(kbase, 0), tq)
                sw = jnp.dot(kwin_ref[pb, 0, pl.ds(k0, tq), :], qw0, preferred_element_type=jnp.float32)
                diff = tcol - (kbase + koff)
                sw = jnp.where((diff >= 0) & (diff < WINDOW) & (kbase + koff >= 0), sw, NEG_BIG)
                if back == 0:
                    exact_start(pb, sw, vwt_ref[pb, 0, :, pl.ds(k0, tq)], lw_ref, accw_ref)
                else:
                    exact_update(pb, sw, vwt_ref[pb, 0, :, pl.ds(k0, tq)], lw_ref, accw_ref)

    r_id = lax.broadcasted_iota(jnp.int32, (GATE_ROWS, LANE), 0)
    c_id = lax.broadcasted_iota(jnp.int32, (GATE_ROWS, LANE), 1)
    pick = jnp.where((c_id == grp * (HEADS_PER_GROUP * 3) + r_id) & (r_id < HEADS_PER_GROUP * 3),
                     1.0, 0.0).astype(gate_ref.dtype)
    for pb in probs:
        o_slc = acc_ref[pb] / l_ref[pb]
        o_win = accw_ref[pb] / lw_ref[pb]
        graw = lax.dot_general(pick, gate_ref[pb], (((1,), (1,)), ((), ())),
                               preferred_element_type=jnp.float32)
        gates = _sigmoid(graw)
        o_cmp = ocmp_ref[pb, 0, 0]
        mixed = []
        for h in range(HEADS_PER_GROUP):
            sl = slice(h * tq, (h + 1) * tq)
            mixed.append(gates[3 * h:3 * h + 1, :] * o_cmp[:, sl]
                         + gates[3 * h + 1:3 * h + 2, :] * o_slc[:, sl]
                         + gates[3 * h + 2:3 * h + 3, :] * o_win[:, sl])
        o_ref[pb] = jnp.concatenate(mixed, axis=0).T.astype(o_ref.dtype)


def _window_masks(tq):
    k = np.arange(tq)[:, None]
    t = np.arange(tq)[None, :]
    tiles = []
    for back in range(WINDOW // tq + 1):
        diff = t + back * tq - k
        tiles.append(np.tile(np.where((diff >= 0) & (diff < WINDOW), 0.0, NEG_BIG), (1, HEADS_PER_GROUP)))
    return jnp.asarray(np.stack(tiles), jnp.float32)


def _flash(proj3, cos_q, sin_q, bias, kaug, vst, kwin, vwt, knorm, ocmp):
    b, s, _ = proj3.shape
    tq = min(Q_TILE, s)
    nq = s // tq
    n_blk = s // SLC_BLOCK
    aug_w = kaug.shape[3]
    qb = COL_Q // KV_WIDTH
    m_rows = HEADS_PER_GROUP * tq

    wmask = _window_masks(tq)

    def resident(shape):
        return pl.BlockSpec((b, 1) + shape, lambda g, q: (0, g, 0, 0), pipeline_mode=pl.Buffered(1))

    return pl.pallas_call(
        _flash_kernel,
        grid=(N_KV_GROUPS, nq),
        in_specs=[pl.BlockSpec((b, tq, KV_WIDTH), lambda g, q: (0, q, qb + g)),
                  pl.BlockSpec((ROPE_HALF, tq), lambda g, q: (0, q)),
                  pl.BlockSpec((ROPE_HALF, tq), lambda g, q: (0, q)),
                  pl.BlockSpec((b, 1, n_blk, tq), lambda g, q: (0, g, 0, q)),
                  resident((s, aug_w)),
                  resident((HEAD_DIM, s)),
                  resident((s, 2 * HEAD_DIM)),
                  resident((HEAD_DIM, s)),
                  _const_spec((b, SUBLANE, LANE)),
                  _const_spec(wmask.shape),
                  pl.BlockSpec((b, tq, LANE), lambda g, q: (0, q, COL_GNSA // LANE)),
                  pl.BlockSpec((b, 1, 1, HEAD_DIM, m_rows), lambda g, q: (0, g, q, 0, 0))],
        out_specs=pl.BlockSpec((b, tq, KV_WIDTH), lambda g, q: (0, q, g)),
        out_shape=jax.ShapeDtypeStruct((b, s, ATTN_Q_WIDTH), ACT_DTYPE),
        scratch_shapes=[pltpu.VMEM((b, 1, m_rows), jnp.float32),
                        pltpu.VMEM((b, 1, m_rows), jnp.float32),
                        pltpu.VMEM((b, HEAD_DIM, m_rows), jnp.float32),
                        pltpu.VMEM((b, 1, m_rows), jnp.float32),
                        pltpu.VMEM((b, HEAD_DIM, m_rows), jnp.float32)],
        compiler_params=_cparams(2),
        name="flash",
    )(proj3, cos_q, sin_q, bias, kaug, vst, kwin, vwt, knorm, wmask, proj3, ocmp)


def _tail_kernel(u_ref, halo_ref, gp_ref, ga_ref, attn_ref, x_ref, pw_ref, ps_ref, wpp_ref, wpa_ref, wo_ref,
                 n2_ref, wg_ref, wu_ref, wd_ref, nf_ref, o_ref, *, seq_len):
    tm = u_ref.shape[0]
    t0 = (pl.program_id(0) * tm) & (seq_len - 1)

    u = u_ref[...].astype(jnp.float32)
    halo = jnp.where(t0 > 0, halo_ref[...].astype(jnp.float32), 0.0)
    ext = jnp.concatenate([halo, u], axis=0)
    t = t0 + lax.broadcasted_iota(jnp.int32, (tm, POOL_GROUP), 0)
    pooled = []
    for gi, w in enumerate(POOL_WINDOWS):
        sl = slice(gi * POOL_GROUP, (gi + 1) * POOL_GROUP)
        acc = ext[:, sl]
        span = 1
        while span < w:
            acc = acc + pltpu.roll(acc, span, 0)
            span *= 2
        cnt = jnp.minimum(t + 1, w).astype(jnp.float32)
        mean = acc[POOL_HALO:, :] / cnt
        mixed = jnp.dot((mean - u[:, sl]).astype(MXU_DTYPE), pw_ref[gi],
                        preferred_element_type=jnp.float32)
        pooled.append(mixed * ps_ref[:, sl])
    pool = jnp.concatenate(pooled, axis=1).astype(MXU_DTYPE)

    pp = jnp.dot(pool, wpp_ref[...], preferred_element_type=jnp.float32)
    pa = jnp.dot(attn_ref[...], wpa_ref[...], preferred_element_type=jnp.float32)
    merged = (_sigmoid(gp_ref[...].astype(jnp.float32)) * pp
              + _sigmoid(ga_ref[...].astype(jnp.float32)) * pa)
    x1 = x_ref[...] + jnp.dot(merged.astype(MXU_DTYPE), wo_ref[...], preferred_element_type=jnp.float32)

    h = _rms(x1, n2_ref[...]).astype(MXU_DTYPE)
    d_ff = wg_ref.shape[1]
    out = x1
    for c0 in range(0, d_ff, FF_CHUNK):
        cw = min(FF_CHUNK, d_ff - c0)
        gate = jnp.dot(h, wg_ref[:, c0:c0 + cw], preferred_element_type=jnp.float32)
        up = jnp.dot(h, wu_ref[:, c0:c0 + cw], preferred_element_type=jnp.float32)
        act = (gate * _sigmoid(gate) * up).astype(MXU_DTYPE)
        out = out + jnp.dot(act, wd_ref[c0:c0 + cw, :], preferred_element_type=jnp.float32)
    o_ref[...] = _rms(out, nf_ref[...])


def _tail(proj2, attn2, x2d, pool_w, pool_scale, wpp, wpa, wo, norm2_w, wg, wu, wd, norm_f_w, seq_len):
    m, d = x2d.shape
    tm = min(ROW_TILE, seq_len)
    halo_per_tile = tm // POOL_HALO
    weights = (pool_w, pool_scale, wpp, wpa, wo, norm2_w, wg, wu, wd, norm_f_w)
    return pl.pallas_call(
        functools.partial(_tail_kernel, seq_len=seq_len),
        grid=(m // tm,),
        in_specs=[pl.BlockSpec((tm, POOL_WIDTH), lambda i: (i, COL_POOL // POOL_WIDTH)),
                  pl.BlockSpec((POOL_HALO, POOL_WIDTH),
                               lambda i: (jnp.maximum(i * halo_per_tile - 1, 0), COL_POOL // POOL_WIDTH)),
                  pl.BlockSpec((tm, d), lambda i: (i, COL_GPOOL // D_MODEL)),
                  pl.BlockSpec((tm, d), lambda i: (i, COL_GATTN // D_MODEL)),
                  pl.BlockSpec((tm, ATTN_Q_WIDTH), lambda i: (i, 0)),
                  pl.BlockSpec((tm, d), lambda i: (i, 0))] + [_const_spec(w.shape) for w in weights],
        out_specs=pl.BlockSpec((tm, d), lambda i: (i, 0)),
        out_shape=jax.ShapeDtypeStruct((m, d), jnp.float32),
        compiler_params=_cparams(1),
        name="tail",
    )(proj2, proj2, proj2, proj2, attn2, x2d, *weights)


def _rope_tables(seq_len):
    inv_freq = 1.0 / (ROPE_THETA ** (jnp.arange(0, ROPE_DIM, 2, dtype=jnp.float32) / ROPE_DIM))
    ang = jnp.arange(seq_len).astype(jnp.float32)[:, None] * inv_freq[None, :]
    cos, sin = jnp.cos(ang), jnp.sin(ang)
    rest = HEAD_DIM - ROPE_DIM
    cos_h = jnp.concatenate([cos, cos, jnp.ones((seq_len, rest), jnp.float32)], axis=1)
    sin_h = jnp.concatenate([sin, sin, jnp.zeros((seq_len, rest), jnp.float32)], axis=1)
    return cos.T, sin.T, cos_h, sin_h


def _rotate_half_matrix():
    rot = np.zeros((KV_WIDTH, KV_WIDTH), np.float32)
    for l in range(KV_WIDTH):
        d = l % HEAD_DIM
        if d < ROPE_HALF:
            rot[l + ROPE_HALF, l] = -1.0
        elif d < ROPE_DIM:
            rot[l - ROPE_HALF, l] = 1.0
    return jnp.asarray(rot, MXU_DTYPE)


def _layer(x, norm1_w, w_in, pool_w, pool_scale, cmp_pe_k, cmp_w1_k, cmp_b1_k, cmp_w2_k,
           cmp_pe_v, cmp_w1_v, cmp_b1_v, cmp_w2_v, w_proj_pool, w_proj_attn, w_out,
           norm2_w, w_ffn_gate, w_ffn_up, w_ffn_down, norm_f_w):
    b, s, d = x.shape
    m = b * s
    n_blk = s // SLC_BLOCK
    assert s % Q_TILE == 0 and s % ROW_TILE == 0 and (s & (s - 1)) == 0, s
    cd = MXU_DTYPE

    g0 = COL_GPOOL
    w_parts = (w_in[:, :g0].astype(cd), w_in[:, g0 + GATE_NSA:].astype(cd),
               jnp.pad(w_in[:, g0:g0 + GATE_NSA], ((0, 0), (0, LANE - GATE_NSA))).astype(cd))
    x2d = x.reshape(m, d)
    proj2, kvc = _inproj(x2d, norm1_w.reshape(1, d), w_parts)
    proj3 = proj2.reshape(b, s, PROJ_WIDTH)

    def pe8(pe):
        return jnp.broadcast_to(pe.reshape(1, -1), (8, pe.size)).astype(cd)

    kvc_rows = kvc.reshape(2 * N_KV_GROUPS, b, n_blk, SLC_BLOCK * HEAD_DIM)
    kc_r, vc_rt = _compress(kvc_rows,
                            pe8(cmp_pe_k), cmp_w1_k.astype(cd), cmp_b1_k.reshape(1, -1), cmp_w2_k.astype(cd),
                            pe8(cmp_pe_v), cmp_w1_v.astype(cd), cmp_b1_v.reshape(1, -1), cmp_w2_v.astype(cd))

    cos_q, sin_q, cos_k, sin_k = _rope_tables(s)
    kaug, vst, kwin, vwt, knorm = _kprep(proj3, cos_k, sin_k, _rotate_half_matrix())
    ocmp, bias = _cmpsel(proj3, kc_r, vc_rt)
    attn = _flash(proj3, cos_q, sin_q, bias, kaug, vst, kwin, vwt, knorm, ocmp)
    return _tail(proj2, attn.reshape(m, ATTN_Q_WIDTH), x2d, pool_w.astype(cd),
                 pool_scale.reshape(1, -1), w_proj_pool.astype(cd), w_proj_attn.astype(cd),
                 w_out.astype(cd), norm2_w.reshape(1, d), w_ffn_gate.astype(cd), w_ffn_up.astype(cd),
                 w_ffn_down.astype(cd), norm_f_w.reshape(1, d), s)


def kernel(x, norm1_w, w_in, pool_w, pool_scale, cmp_pe_k, cmp_w1_k, cmp_b1_k, cmp_w2_k, cmp_pe_v,
           cmp_w1_v, cmp_b1_v, cmp_w2_v, w_proj_pool, w_proj_attn, w_out, norm2_w, w_ffn_gate,
           w_ffn_up, w_ffn_down, norm_f_w):
    b, s, d = x.shape
    depth = w_in.shape[0]
    assert depth == 1, "the final norm is fused into the last layer's FFN kernel"
    out = _layer(x, norm1_w[0], w_in[0], pool_w[0], pool_scale[0], cmp_pe_k[0], cmp_w1_k[0], cmp_b1_k[0],
                 cmp_w2_k[0], cmp_pe_v[0], cmp_w1_v[0], cmp_b1_v[0], cmp_w2_v[0], w_proj_pool[0],
                 w_proj_attn[0], w_out[0], norm2_w[0], w_ffn_gate[0], w_ffn_up[0], w_ffn_down[0], norm_f_w)
    return out.reshape(b, s, d)
```

```python
import functools

import numpy as np
import jax
import jax.numpy as jnp
from jax import lax
from jax.experimental import pallas as pl
from jax.experimental.pallas import tpu as pltpu

D_MODEL = 1024
N_HEADS = 16
HEAD_DIM = 64
N_KV_GROUPS = 4
HEADS_PER_GROUP = N_HEADS // N_KV_GROUPS
ROPE_DIM = HEAD_DIM // 4
ROPE_HALF = ROPE_DIM // 2
ROPE_THETA = 500000.0
CMP_BLOCK = 32
CMP_STRIDE = 16
CMP_PER_SLC = 4
SLC_BLOCK = 64
SLC_TOPK = 16
WINDOW = 512
ATTN_Q_WIDTH = N_HEADS * HEAD_DIM
KV_WIDTH = N_KV_GROUPS * HEAD_DIM
POOL_WIDTH = D_MODEL // 2
POOL_WINDOWS = (2, 4, 8, 16)
POOL_GROUP = POOL_WIDTH // len(POOL_WINDOWS)
POOL_HALO = 16
RMS_EPS = 1e-6

LANE = 128
SUBLANE = 8
MXU_DTYPE = jnp.bfloat16
ACT_DTYPE = jnp.bfloat16
VMEM_LIMIT_BYTES = 56 * 1024 * 1024

GATE_NSA = N_HEADS * 3
COL_POOL = 0
COL_Q = COL_POOL + POOL_WIDTH
COL_KV = COL_Q + ATTN_Q_WIDTH
COL_GPOOL = COL_KV + 6 * KV_WIDTH
COL_GATTN = COL_GPOOL + D_MODEL
COL_GNSA = COL_GATTN + D_MODEL
PROJ_WIDTH = COL_GNSA + LANE
GATE_ROWS = 16

ROW_TILE = 512
Q_TILE = 256
SLC_SPAN = 2048
SLC_SUBTILE = 512
KPREP_TILE = 1024
FF_CHUNK = 512
SCORE_SCALE = HEAD_DIM ** -0.5
LOG2E = float(np.log2(np.e))
FINITE_LIMIT = 3e38
STAB_MARGIN = 1.03
L_MIN = 2.0 ** -64
MASK_BIAS = float(2 ** 30)
NEG_BIG = -1e30
IMP_BIG = 1e30
CHOSEN = -3e38
CHOSEN_LIMIT = -2e38
N_FORCED = 3


def _log2(n):
    k = int(n).bit_length() - 1
    assert 1 << k == n, n
    return k


def _cparams(n_grid):
    return pltpu.CompilerParams(dimension_semantics=("arbitrary",) * n_grid,
                                vmem_limit_bytes=VMEM_LIMIT_BYTES)


def _const_spec(shape):
    nd = len(shape)
    return pl.BlockSpec(shape, lambda *_: (0,) * nd, pipeline_mode=pl.Buffered(1))


def _sigmoid(x):
    return 1.0 / (1.0 + jnp.exp(-x))


def _lost_range(total, smallest):
    return jnp.logical_not(jnp.sum(total) < FINITE_LIMIT) | (jnp.min(smallest) < L_MIN)


def _rms(x, w):
    var = jnp.mean(x * x, axis=-1, keepdims=True)
    return x * lax.rsqrt(var + RMS_EPS) * w


def _inproj_kernel(x_ref, nw_ref, wa_ref, wb_ref, wc_ref, o_ref, kvc_ref):
    h = _rms(x_ref[...], nw_ref[...]).astype(MXU_DTYPE)
    n = o_ref.shape[1]
    for c0 in range(0, n, KV_WIDTH):
        cw = min(KV_WIDTH, n - c0)
        w_ref, w0 = ((wa_ref, 0) if c0 < COL_GPOOL else (wb_ref, COL_GPOOL) if c0 < COL_GNSA
                     else (wc_ref, COL_GNSA))
        res = jnp.dot(h, w_ref[:, c0 - w0:c0 - w0 + cw],
                      preferred_element_type=jnp.float32).astype(o_ref.dtype)
        o_ref[:, c0:c0 + cw] = res
        which = (c0 - COL_KV) // KV_WIDTH
        if c0 >= COL_KV and which < 2:
            for g in range(N_KV_GROUPS):
                kvc_ref[which * N_KV_GROUPS + g] = res[:, g * HEAD_DIM:(g + 1) * HEAD_DIM]


def _inproj(x2d, norm_w, w_parts):
    m, d = x2d.shape
    n = sum(w.shape[1] for w in w_parts)
    assert n == PROJ_WIDTH
    return pl.pallas_call(
        _inproj_kernel,
        grid=(m // ROW_TILE,),
        in_specs=[pl.BlockSpec((ROW_TILE, d), lambda i: (i, 0)),
                  _const_spec((1, d))] + [_const_spec(w.shape) for w in w_parts],
        out_specs=[pl.BlockSpec((ROW_TILE, n), lambda i: (i, 0)),
                   pl.BlockSpec((2 * N_KV_GROUPS, ROW_TILE, HEAD_DIM), lambda i: (0, i, 0))],
        out_shape=[jax.ShapeDtypeStruct((m, n), ACT_DTYPE),
                   jax.ShapeDtypeStruct((2 * N_KV_GROUPS, m, HEAD_DIM), ACT_DTYPE)],
        compiler_params=_cparams(1),
        name="inproj",
    )(x2d, norm_w, *w_parts)


def _gelu_tanh(x):
    return 0.5 * x * (1.0 + jnp.tanh(np.sqrt(2.0 / np.pi) * (x + 0.044715 * (x * x * x))))


def _compress_one(r_ref, pe_ref, w1_ref, b1_ref, w2_ref):
    half = w1_ref.shape[0] // 2
    n_blk = r_ref.shape[2]
    rows = [r_ref[0, 0, :, r * half:(r + 1) * half] for r in range(CMP_PER_SLC)]
    top = [jnp.dot(x, w1_ref[0:half, :], preferred_element_type=jnp.float32) for x in rows]
    bot = [jnp.dot(x, w1_ref[half:, :], preferred_element_type=jnp.float32) for x in rows]
    bias = jnp.dot(pe_ref[...], w1_ref[...], preferred_element_type=jnp.float32)[0:1, :] + b1_ref[...]
    nxt = bot[1:] + [pltpu.roll(bot[0], n_blk - 1, 0)]
    hid = jnp.concatenate([top[r] + nxt[r] for r in range(CMP_PER_SLC)], axis=0) + bias
    return jnp.dot(_gelu_tanh(hid).astype(MXU_DTYPE), w2_ref[...], preferred_element_type=jnp.float32)


def _compress_kernel(rk_ref, rv_ref, pek_ref, w1k_ref, b1k_ref, w2k_ref,
                     pev_ref, w1v_ref, b1v_ref, w2v_ref, ok_ref, ovt_ref):
    kc = _compress_one(rk_ref, pek_ref, w1k_ref, b1k_ref, w2k_ref)
    vc = _compress_one(rv_ref, pev_ref, w1v_ref, b1v_ref, w2v_ref)
    ok_ref[0, 0] = jnp.concatenate([kc * (SCORE_SCALE * LOG2E), jnp.zeros_like(kc)], axis=1).astype(ok_ref.dtype)
    ovt_ref[0, 0] = jnp.concatenate([kc, vc], axis=1).T[HEAD_DIM:, :].astype(ovt_ref.dtype)


def _compress(kvc_rows, pek, w1k, b1k, w2k, pev, w1v, b1v, w2v):
    _, b, n_blk, width = kvc_rows.shape
    dk = w2k.shape[1]
    n_rows = n_blk * CMP_PER_SLC
    w_specs = [_const_spec(a.shape) for a in (pek, w1k, b1k, w2k)]
    return pl.pallas_call(
        _compress_kernel,
        grid=(b, N_KV_GROUPS),
        in_specs=[pl.BlockSpec((1, 1, n_blk, width), lambda i, j: (j, i, 0, 0)),
                  pl.BlockSpec((1, 1, n_blk, width), lambda i, j: (N_KV_GROUPS + j, i, 0, 0))]
        + w_specs + w_specs,
        out_specs=[pl.BlockSpec((1, 1, n_rows, 2 * dk), lambda i, j: (i, j, 0, 0)),
                   pl.BlockSpec((1, 1, dk, n_rows), lambda i, j: (i, j, 0, 0))],
        out_shape=[jax.ShapeDtypeStruct((b, N_KV_GROUPS, n_rows, 2 * dk), MXU_DTYPE),
                   jax.ShapeDtypeStruct((b, N_KV_GROUPS, dk, n_rows), MXU_DTYPE)],
        compiler_params=_cparams(2),
        name="compress",
    )(kvc_rows, kvc_rows, pek, w1k, b1k, w2k, pev, w1v, b1v, w2v)


def _kprep_kernel(ks_ref, vs_ref, kw_ref, vw_ref, cos_ref, sin_ref, rot_ref,
                  kaug_ref, vst_ref, kwin_ref, vwt_ref, knorm_ref):
    ts = ks_ref.shape[1]
    t0 = pl.program_id(1) * ts
    cos = jnp.concatenate([cos_ref[...]] * N_KV_GROUPS, axis=1)
    sin = jnp.concatenate([sin_ref[...]] * N_KV_GROUPS, axis=1)
    rot = rot_ref[...]

    def rope(k_ref):
        k = k_ref[0]
        krot = jnp.dot(k.astype(MXU_DTYPE), rot, preferred_element_type=jnp.float32)
        return (k.astype(jnp.float32) * cos + krot * sin) * (SCORE_SCALE * LOG2E)

    n_blk = kaug_ref.shape[3] - 2 * HEAD_DIM
    blk = lax.broadcasted_iota(jnp.int32, (ts, n_blk), 1)
    key_blk = (t0 + lax.broadcasted_iota(jnp.int32, (ts, n_blk), 0)) >> _log2(SLC_BLOCK)
    onehot = jnp.where(blk == key_blk, 1.0, 0.0).astype(kaug_ref.dtype)
    ks_f = rope(ks_ref)
    kw_f = rope(kw_ref)
    ks = ks_f.astype(kaug_ref.dtype)
    kw = kw_f.astype(kwin_ref.dtype)

    seg_l = lax.broadcasted_iota(jnp.int32, (KV_WIDTH, LANE), 0) >> _log2(HEAD_DIM)
    seg_c = lax.broadcasted_iota(jnp.int32, (KV_WIDTH, LANE), 1)
    sq = (jnp.dot(ks_f * ks_f, jnp.where(seg_c == seg_l, 1.0, 0.0), preferred_element_type=jnp.float32)
          + jnp.dot(kw_f * kw_f, jnp.where(seg_c == seg_l + N_KV_GROUPS, 1.0, 0.0),
                    preferred_element_type=jnp.float32))
    tile_max = jnp.broadcast_to(jnp.max(sq, axis=0, keepdims=True), knorm_ref.shape[1:])

    @pl.when(pl.program_id(1) == 0)
    def _():
        knorm_ref[0] = tile_max

    @pl.when(pl.program_id(1) > 0)
    def _():
        knorm_ref[0] = jnp.maximum(knorm_ref[0], tile_max)

    vst = vs_ref[0].astype(jnp.float32).T.astype(vst_ref.dtype)
    vwt = vw_ref[0].astype(jnp.float32).T.astype(vwt_ref.dtype)
    ones_col = jnp.where(lax.broadcasted_iota(jnp.int32, (ts, HEAD_DIM), 1) == 0, 1.0, 0.0
                         ).astype(kaug_ref.dtype)
    for g in range(N_KV_GROUPS):
        sl = slice(g * HEAD_DIM, (g + 1) * HEAD_DIM)
        kaug_ref[0, g, :, 0:n_blk] = onehot
        kaug_ref[0, g, :, n_blk:n_blk + HEAD_DIM] = ks[:, sl]
        kaug_ref[0, g, :, n_blk + HEAD_DIM:] = ones_col
        kwin_ref[0, g, :, 0:HEAD_DIM] = kw[:, sl]
        kwin_ref[0, g, :, HEAD_DIM:] = ones_col
        vst_ref[0, g] = vst[sl, :]
        vwt_ref[0, g] = vwt[sl, :]


def _kprep(proj3, cos_k, sin_k, rot):
    b, s, _ = proj3.shape
    n_blk = s // SLC_BLOCK
    ts = min(KPREP_TILE, s)
    kvb = COL_KV // KV_WIDTH

    def col(j):
        return pl.BlockSpec((1, ts, KV_WIDTH), lambda i, t: (i, t, kvb + j))

    tab = pl.BlockSpec((ts, HEAD_DIM), lambda i, t: (t, 0))
    aug_w = n_blk + 2 * HEAD_DIM
    return pl.pallas_call(
        _kprep_kernel,
        grid=(b, s // ts),
        in_specs=[col(2), col(3), col(4), col(5), tab, tab, _const_spec(rot.shape)],
        out_specs=[pl.BlockSpec((1, N_KV_GROUPS, ts, aug_w), lambda i, t: (i, 0, t, 0)),
                   pl.BlockSpec((1, N_KV_GROUPS, HEAD_DIM, ts), lambda i, t: (i, 0, 0, t)),
                   pl.BlockSpec((1, N_KV_GROUPS, ts, 2 * HEAD_DIM), lambda i, t: (i, 0, t, 0)),
                   pl.BlockSpec((1, N_KV_GROUPS, HEAD_DIM, ts), lambda i, t: (i, 0, 0, t)),
                   pl.BlockSpec((1, SUBLANE, LANE), lambda i, t: (i, 0, 0))],
        out_shape=[jax.ShapeDtypeStruct((b, N_KV_GROUPS, s, aug_w), MXU_DTYPE),
                   jax.ShapeDtypeStruct((b, N_KV_GROUPS, HEAD_DIM, s), MXU_DTYPE),
                   jax.ShapeDtypeStruct((b, N_KV_GROUPS, s, 2 * HEAD_DIM), MXU_DTYPE),
                   jax.ShapeDtypeStruct((b, N_KV_GROUPS, HEAD_DIM, s), MXU_DTYPE),
                   jax.ShapeDtypeStruct((b, SUBLANE, LANE), jnp.float32)],
        compiler_params=_cparams(2),
        name="kprep",
    )(proj3, proj3, proj3, proj3, cos_k, sin_k, rot)


def _heads_to_lanes(xt):
    return jnp.concatenate([xt[h * HEAD_DIM:(h + 1) * HEAD_DIM, :] for h in range(HEADS_PER_GROUP)],
                           axis=1)


def _vis_rows(tq):
    levels = tq // CMP_STRIDE
    dt = np.arange(tq)
    v = (dt - (CMP_BLOCK - 1)) // CMP_STRIDE - (-(CMP_BLOCK - 1)) // CMP_STRIDE
    rows = np.zeros((HEAD_DIM, tq), np.float32)
    for c in range(1, levels + 1):
        rows[c] = np.where(v < c, -MASK_BIAS, 0.0)
    rows[levels + 1] = -MASK_BIAS
    return jnp.asarray(np.tile(rows, (1, HEADS_PER_GROUP)), MXU_DTYPE)


def _cmpsel_kernel(q_ref, kc_ref, vct_ref, vis_ref, ocmp_ref, bias_ref):
    tq = q_ref.shape[1]
    n_blk = bias_ref.shape[2]
    m_rows = HEADS_PER_GROUP * tq
    t0 = pl.program_id(1) * tq
    levels = tq // CMP_STRIDE

    def body(nr):
        n_tok = CMP_PER_SLC * nr
        row = lax.broadcasted_iota(jnp.int32, (n_tok, LANE), 0)
        n_idx = (row & (nr - 1)) * CMP_PER_SLC + (row >> _log2(nr))
        n_first = (t0 - (CMP_BLOCK - 1)) >> _log2(CMP_STRIDE)
        u = jnp.clip(n_idx - n_first, 0, levels + 1)
        vis_cols = jnp.where(lax.broadcasted_iota(jnp.int32, (n_tok, LANE), 1) - HEAD_DIM == u,
                             1.0, 0.0).astype(MXU_DTYPE)
        vis_rows = vis_ref[...]
        t_row = t0 + (lax.broadcasted_iota(jnp.int32, (1, m_rows), 1) & (tq - 1))
        jrow = lax.broadcasted_iota(jnp.int32, (nr, tq), 0)
        jt = (t0 + lax.broadcasted_iota(jnp.int32, (nr, tq), 1)) >> _log2(SLC_BLOCK)
        causal_blk = jrow <= jt
        forced = (jrow == 0) | (jrow == jt) | (jrow == jt - 1)
        jf = jrow.astype(jnp.float32)
        n_extract = min(SLC_TOPK, n_blk) - N_FORCED
        jt_row = (t0 + lax.broadcasted_iota(jnp.int32, (1, tq), 1)) >> _log2(SLC_BLOCK)
        want = jnp.minimum(jt_row + 1, SLC_TOPK).astype(jnp.float32)

        def store_bias(pb, marks):
            bias_ref[pb, 0, 0:nr, :] = jnp.where(marks < CHOSEN_LIMIT, 0.0, -MASK_BIAS).astype(bias_ref.dtype)

        pending = []
        for pb in range(q_ref.shape[0]):
            kc = jnp.concatenate([kc_ref[pb, 0, r * n_blk:r * n_blk + nr, :] for r in range(CMP_PER_SLC)],
                                 axis=0)
            vct = jnp.concatenate([vct_ref[pb, 0, :, r * n_blk:r * n_blk + nr] for r in range(CMP_PER_SLC)],
                                  axis=1)
            qt = _heads_to_lanes(q_ref[pb].astype(jnp.float32).T).astype(MXU_DTYPE)
            s = jnp.dot(kc + vis_cols, jnp.concatenate([qt, vis_rows], axis=0),
                        preferred_element_type=jnp.float32)
            mx = jnp.max(s, axis=0, keepdims=True)
            p = jnp.exp2(s - mx)
            den = jnp.sum(p, axis=0, keepdims=True)
            inv = jnp.where(t_row >= CMP_BLOCK - 1, 1.0 / den, 0.0)
            ocmp_ref[pb, 0, 0] = inv * jnp.dot(vct, p.astype(MXU_DTYPE), preferred_element_type=jnp.float32)

            imp = None
            for h in range(HEADS_PER_GROUP):
                ph = p[:, h * tq:(h + 1) * tq]
                last = ph[3 * nr:4 * nr, :]
                prev = jnp.where(jrow == 0, 0.0, pltpu.roll(last, 1, 0))
                taps = ph[0:nr, :] + ph[nr:2 * nr, :] + ph[2 * nr:3 * nr, :] + last + prev
                imp_h = taps * inv[:, h * tq:(h + 1) * tq]
                imp = imp_h if imp is None else imp + imp_h

            val = jnp.where(forced, CHOSEN, jnp.where(causal_blk, imp, -IMP_BIG))
            marks = val
            for _ in range(n_extract):
                marks = jnp.where(marks == jnp.max(marks, axis=0, keepdims=True), CHOSEN, marks)
            store_bias(pb, marks)
            if nr < n_blk:
                bias_ref[pb, 0, nr:, :] = jnp.full((n_blk - nr, tq), -MASK_BIAS, bias_ref.dtype)
            count = jnp.sum(jnp.where((marks < CHOSEN_LIMIT) & causal_blk, 1.0, 0.0), axis=0, keepdims=True)
            pending.append((pb, val, jnp.max(jnp.abs(count - want)) > 0.5))

        for pb, val, tied in pending:
            @pl.when(tied)
            def _(val=val, pb=pb):
                v = val
                for _ in range(n_extract):
                    top = jnp.max(v, axis=0, keepdims=True)
                    first = jnp.min(jnp.where(v == top, jf, float(n_blk)), axis=0, keepdims=True)
                    v = jnp.where(jf == first, CHOSEN, v)
                store_bias(pb, v)

    sizes = [n_blk // f for f in (4, 2, 1) if (n_blk // f) % 16 == 0]
    lo = 0
    for nr in sizes:
        hi = nr * SLC_BLOCK if nr < n_blk else None
        end = t0 + tq
        cond = (end > lo) if hi is None else (end > lo) & (end <= hi)
        pl.when(cond)(functools.partial(body, nr))
        lo = hi


def _cmpsel(proj3, kc_r, vc_rt):
    b, s, _ = proj3.shape
    tq = min(Q_TILE, s)
    nq = s // tq
    n_cmp_pad = kc_r.shape[2]
    n_blk = s // SLC_BLOCK
    qb = COL_Q // KV_WIDTH
    m_rows = HEADS_PER_GROUP * tq
    return pl.pallas_call(
        _cmpsel_kernel,
        grid=(N_KV_GROUPS, nq),
        in_specs=[pl.BlockSpec((b, tq, KV_WIDTH), lambda g, q: (0, q, qb + g)),
                  pl.BlockSpec((b, 1, n_cmp_pad, LANE), lambda g, q: (0, g, 0, 0)),
                  pl.BlockSpec((b, 1, HEAD_DIM, n_cmp_pad), lambda g, q: (0, g, 0, 0)),
                  _const_spec((HEAD_DIM, m_rows))],
        out_specs=[pl.BlockSpec((b, 1, 1, HEAD_DIM, m_rows), lambda g, q: (0, g, q, 0, 0)),
                   pl.BlockSpec((b, 1, n_blk, tq), lambda g, q: (0, g, 0, q))],
        out_shape=[jax.ShapeDtypeStruct((b, N_KV_GROUPS, nq, HEAD_DIM, m_rows), jnp.float32),
                   jax.ShapeDtypeStruct((b, N_KV_GROUPS, n_blk, s), MXU_DTYPE)],
        compiler_params=_cparams(2),
        name="cmpsel",
    )(proj3, kc_r, vc_rt, _vis_rows(tq))


def _flash_kernel(q_ref, cos_ref, sin_ref, bias_ref, kaug_ref, vst_ref, kwin_ref, vwt_ref, knorm_ref,
                  wmask_ref, gate_ref, ocmp_ref, o_ref, m_ref, l_ref, acc_ref, lw_ref, accw_ref):
    nb = q_ref.shape[0]
    tq = q_ref.shape[1]
    m_rows = HEADS_PER_GROUP * tq
    grp = pl.program_id(0)
    qi = pl.program_id(1)
    t0 = qi * tq
    kd = pl.multiple_of(t0, tq)
    probs = range(nb)
    zpad = jnp.zeros((HEAD_DIM, m_rows), MXU_DTYPE)
    pad_row = lax.broadcasted_iota(jnp.int32, (HEAD_DIM, m_rows), 0)
    lane = lax.broadcasted_iota(jnp.int32, (1, LANE), 1)

    def stab_rows(neg_m):
        return jnp.where(pad_row == 0, neg_m, 0.0).astype(MXU_DTYPE)

    cos = cos_ref[...]
    sin = sin_ref[...]
    qr, bias4, qs_slc, qs_win = [], [], [], []
    for pb in probs:
        qt = q_ref[pb].astype(jnp.float32).T
        parts = []
        for h in range(HEADS_PER_GROUP):
            qh = qt[h * HEAD_DIM:(h + 1) * HEAD_DIM, :]
            x1 = qh[0:ROPE_HALF, :]
            x2 = qh[ROPE_HALF:ROPE_DIM, :]
            parts.append(jnp.concatenate([x1 * cos - x2 * sin, x2 * cos + x1 * sin, qh[ROPE_DIM:, :]],
                                         axis=0))
        qf = jnp.concatenate(parts, axis=1)
        q_sq = jnp.sum(qf * qf, axis=0, keepdims=True)
        k_sq = knorm_ref[pb, 0:1, :]
        ks_sq = jnp.max(jnp.where(lane == grp, k_sq, 0.0), axis=1, keepdims=True)
        kw_sq = jnp.max(jnp.where(lane == grp + N_KV_GROUPS, k_sq, 0.0), axis=1, keepdims=True)
        qr.append(qf.astype(MXU_DTYPE))
        bias4.append(jnp.concatenate([bias_ref[pb, 0]] * HEADS_PER_GROUP, axis=1))
        qs_slc.append(jnp.concatenate([bias4[pb], qr[pb], stab_rows(-STAB_MARGIN * jnp.sqrt(q_sq * ks_sq))],
                                      axis=0))
        qs_win.append(jnp.concatenate([qr[pb], stab_rows(-STAB_MARGIN * jnp.sqrt(q_sq * kw_sq))], axis=0))

    n_back = WINDOW // tq

    def win_tile(back):
        if back == 0:
            return kd, tq, wmask_ref[0]
        inside = wmask_ref[back] if back * tq + tq - 1 >= WINDOW else 0.0
        return (pl.multiple_of(jnp.maximum(t0 - back * tq, 0), tq), tq,
                jnp.where(qi >= back, inside, NEG_BIG))

    def fast_tiles(k_ref, v_ref, qs, tiles, l_out, acc_out, first=False):
        for pb in probs:
            l_add = acc_add = None
            for k0, tk, mask in tiles:
                s = jnp.dot(k_ref[pb, 0, pl.ds(k0, tk), :], qs[pb], preferred_element_type=jnp.float32)
                if mask is not None:
                    s = s + mask
                p = jnp.exp2(s)
                l_t = jnp.sum(p, axis=0, keepdims=True)
                acc_t = jnp.dot(v_ref[pb, 0, :, pl.ds(k0, tk)], p.astype(MXU_DTYPE),
                                preferred_element_type=jnp.float32)
                l_add = l_t if l_add is None else l_add + l_t
                acc_add = acc_t if acc_add is None else acc_add + acc_t
            if first:
                l_out[pb] = l_add
                acc_out[pb] = acc_add
            else:
                l_out[pb] += l_add
                acc_out[pb] += acc_add

    fast_tiles(kaug_ref, vst_ref, qs_slc, [(kd, tq, wmask_ref[0])], l_ref, acc_ref, first=True)
    fast_tiles(kwin_ref, vwt_ref, qs_win, [win_tile(back) for back in range(n_back + 1)],
               lw_ref, accw_ref, first=True)

    sub = SLC_SUBTILE
    n_sub = SLC_SPAN // sub

    def span_body(i, carry):
        base = i * SLC_SPAN
        fast_tiles(kaug_ref, vst_ref, qs_slc,
                   [(pl.multiple_of(base + j * sub, sub), sub, None) for j in range(n_sub)],
                   l_ref, acc_ref)
        return carry

    n_span = t0 // SLC_SPAN
    lax.fori_loop(0, n_span, span_body, 0)
    done = n_span * SLC_SPAN
    size = SLC_SPAN // 2
    while size >= tq:
        take = ((t0 // size) % 2) == 1

        @pl.when(take)
        def _(done=done, size=size):
            piece = min(size, sub)
            fast_tiles(kaug_ref, vst_ref, qs_slc,
                       [(pl.multiple_of(done + j * piece, piece), piece, None) for j in range(size // piece)],
                       l_ref, acc_ref)

        done = done + jnp.where(take, size, 0)
        size //= 2

    def exact_start(pb, s, vt, l_out, acc_out):
        mx = jnp.max(s, axis=0, keepdims=True)
        p = jnp.exp2(s - mx)
        m_ref[pb] = mx
        l_out[pb] = jnp.sum(p, axis=0, keepdims=True)
        acc_out[pb] = jnp.dot(vt, p.astype(MXU_DTYPE), preferred_element_type=jnp.float32)

    def exact_update(pb, s, vt, l_out, acc_out):
        m_old = m_ref[pb]
        mx = jnp.maximum(m_old, jnp.max(s, axis=0, keepdims=True))
        alpha = jnp.exp2(m_old - mx)
        p = jnp.exp2(s - mx)
        m_ref[pb] = mx
        l_out[pb] = alpha * l_out[pb] + jnp.sum(p, axis=0, keepdims=True)
        acc_out[pb] = alpha * acc_out[pb] + jnp.dot(vt, p.astype(MXU_DTYPE),
                                                    preferred_element_type=jnp.float32)

    check = l_low = None
    for pb in probs:
        c = (l_ref[pb] + lw_ref[pb] + jnp.sum(jnp.abs(acc_ref[pb]), axis=0, keepdims=True)
             + jnp.sum(jnp.abs(accw_ref[pb]), axis=0, keepdims=True))
        lo = jnp.minimum(l_ref[pb], lw_ref[pb])
        check = c if check is None else check + c
        l_low = lo if l_low is None else jnp.minimum(l_low, lo)

    @pl.when(_lost_range(check, l_low))
    def _():
        tcol = t0 + (lax.broadcasted_iota(jnp.int32, (tq, m_rows), 1) & (tq - 1))
        koff = lax.broadcasted_iota(jnp.int32, (tq, m_rows), 0)
        causal = t0 + koff <= tcol
        for pb in probs:
            q0 = jnp.concatenate([bias4[pb], qr[pb], zpad], axis=0)
            s = jnp.dot(kaug_ref[pb, 0, pl.ds(kd, tq), :], q0, preferred_element_type=jnp.float32)
            exact_start(pb, jnp.where(causal, s, NEG_BIG), vst_ref[pb, 0, :, pl.ds(kd, tq)], l_ref, acc_ref)

            def body(kt, carry, pb=pb, q0=q0):
                k0 = pl.multiple_of(kt * tq, tq)
                sk = jnp.dot(kaug_ref[pb, 0, pl.ds(k0, tq), :], q0, preferred_element_type=jnp.float32)
                exact_update(pb, sk, vst_ref[pb, 0, :, pl.ds(k0, tq)], l_ref, acc_ref)
                return carry

            lax.fori_loop(0, qi, body, 0)
            qw0 = jnp.concatenate([qr[pb], zpad], axis=0)
            for back in range(n_back + 1):
                kbase = t0 - back * tq
                k0 = pl.multiple_of(jnp.maximum(kbase, 0), tq)
                sw = jnp.dot(kwin_ref[pb, 0, pl.ds(k0, tq), :], qw0, preferred_element_type=jnp.float32)
                diff = tcol - (kbase + koff)
                sw = jnp.where((diff >= 0) & (diff < WINDOW) & (kbase + koff >= 0), sw, NEG_BIG)
                if back == 0:
                    exact_start(pb, sw, vwt_ref[pb, 0, :, pl.ds(k0, tq)], lw_ref, accw_ref)
                else:
                    exact_update(pb, sw, vwt_ref[pb, 0, :, pl.ds(k0, tq)], lw_ref, accw_ref)

    r_id = lax.broadcasted_iota(jnp.int32, (GATE_ROWS, LANE), 0)
    c_id = lax.broadcasted_iota(jnp.int32, (GATE_ROWS, LANE), 1)
    pick = jnp.where((c_id == grp * (HEADS_PER_GROUP * 3) + r_id) & (r_id < HEADS_PER_GROUP * 3),
                     1.0, 0.0).astype(gate_ref.dtype)
    for pb in probs:
        o_slc = acc_ref[pb] / l_ref[pb]
        o_win = accw_ref[pb] / lw_ref[pb]
        graw = lax.dot_general(pick, gate_ref[pb], (((1,), (1,)), ((), ())),
                               preferred_element_type=jnp.float32)
        gates = _sigmoid(graw)
        o_cmp = ocmp_ref[pb, 0, 0]
        mixed = []
        for h in range(HEADS_PER_GROUP):
            sl = slice(h * tq, (h + 1) * tq)
            mixed.append(gates[3 * h:3 * h + 1, :] * o_cmp[:, sl]
                         + gates[3 * h + 1:3 * h + 2, :] * o_slc[:, sl]
                         + gates[3 * h + 2:3 * h + 3, :] * o_win[:, sl])
        o_ref[pb] = jnp.concatenate(mixed, axis=0).T.astype(o_ref.dtype)


def _window_masks(tq):
    k = np.arange(tq)[:, None]
    t = np.arange(tq)[None, :]
    tiles = []
    for back in range(WINDOW // tq + 1):
        diff = t + back * tq - k
        tiles.append(np.tile(np.where((diff >= 0) & (diff < WINDOW), 0.0, NEG_BIG), (1, HEADS_PER_GROUP)))
    return jnp.asarray(np.stack(tiles), jnp.float32)


def _flash(proj3, cos_q, sin_q, bias, kaug, vst, kwin, vwt, knorm, ocmp):
    b, s, _ = proj3.shape
    tq = min(Q_TILE, s)
    nq = s // tq
    n_blk = s // SLC_BLOCK
    aug_w = kaug.shape[3]
    qb = COL_Q // KV_WIDTH
    m_rows = HEADS_PER_GROUP * tq

    wmask = _window_masks(tq)

    def resident(shape):
        return pl.BlockSpec((b, 1) + shape, lambda g, q: (0, g, 0, 0), pipeline_mode=pl.Buffered(1))

    return pl.pallas_call(
        _flash_kernel,
        grid=(N_KV_GROUPS, nq),
        in_specs=[pl.BlockSpec((b, tq, KV_WIDTH), lambda g, q: (0, q, qb + g)),
                  pl.BlockSpec((ROPE_HALF, tq), lambda g, q: (0, q)),
                  pl.BlockSpec((ROPE_HALF, tq), lambda g, q: (0, q)),
                  pl.BlockSpec((b, 1, n_blk, tq), lambda g, q: (0, g, 0, q)),
                  resident((s, aug_w)),
                  resident((HEAD_DIM, s)),
                  resident((s, 2 * HEAD_DIM)),
                  resident((HEAD_DIM, s)),
                  _const_spec((b, SUBLANE, LANE)),
                  _const_spec(wmask.shape),
                  pl.BlockSpec((b, tq, LANE), lambda g, q: (0, q, COL_GNSA // LANE)),
                  pl.BlockSpec((b, 1, 1, HEAD_DIM, m_rows), lambda g, q: (0, g, q, 0, 0))],
        out_specs=pl.BlockSpec((b, tq, KV_WIDTH), lambda g, q: (0, q, g)),
        out_shape=jax.ShapeDtypeStruct((b, s, ATTN_Q_WIDTH), ACT_DTYPE),
        scratch_shapes=[pltpu.VMEM((b, 1, m_rows), jnp.float32),
                        pltpu.VMEM((b, 1, m_rows), jnp.float32),
                        pltpu.VMEM((b, HEAD_DIM, m_rows), jnp.float32),
                        pltpu.VMEM((b, 1, m_rows), jnp.float32),
                        pltpu.VMEM((b, HEAD_DIM, m_rows), jnp.float32)],
        compiler_params=_cparams(2),
        name="flash",
    )(proj3, cos_q, sin_q, bias, kaug, vst, kwin, vwt, knorm, wmask, proj3, ocmp)


def _tail_kernel(u_ref, halo_ref, gp_ref, ga_ref, attn_ref, x_ref, pw_ref, ps_ref, wpp_ref, wpa_ref, wo_ref,
                 n2_ref, wg_ref, wu_ref, wd_ref, nf_ref, o_ref, *, seq_len):
    tm = u_ref.shape[0]
    t0 = (pl.program_id(0) * tm) & (seq_len - 1)

    u = u_ref[...].astype(jnp.float32)
    halo = jnp.where(t0 > 0, halo_ref[...].astype(jnp.float32), 0.0)
    ext = jnp.concatenate([halo, u], axis=0)
    t = t0 + lax.broadcasted_iota(jnp.int32, (tm, POOL_GROUP), 0)
    pooled = []
    for gi, w in enumerate(POOL_WINDOWS):
        sl = slice(gi * POOL_GROUP, (gi + 1) * POOL_GROUP)
        acc = ext[:, sl]
        span = 1
        while span < w:
            acc = acc + pltpu.roll(acc, span, 0)
            span *= 2
        cnt = jnp.minimum(t + 1, w).astype(jnp.float32)
        mean = acc[POOL_HALO:, :] / cnt
        mixed = jnp.dot((mean - u[:, sl]).astype(MXU_DTYPE), pw_ref[gi],
                        preferred_element_type=jnp.float32)
        pooled.append(mixed * ps_ref[:, sl])
    pool = jnp.concatenate(pooled, axis=1).astype(MXU_DTYPE)

    pp = jnp.dot(pool, wpp_ref[...], preferred_element_type=jnp.float32)
    pa = jnp.dot(attn_ref[...], wpa_ref[...], preferred_element_type=jnp.float32)
    merged = (_sigmoid(gp_ref[...].astype(jnp.float32)) * pp
              + _sigmoid(ga_ref[...].astype(jnp.float32)) * pa)
    x1 = x_ref[...] + jnp.dot(merged.astype(MXU_DTYPE), wo_ref[...], preferred_element_type=jnp.float32)

    h = _rms(x1, n2_ref[...]).astype(MXU_DTYPE)
    d_ff = wg_ref.shape[1]
    out = x1
    for c0 in range(0, d_ff, FF_CHUNK):
        cw = min(FF_CHUNK, d_ff - c0)
        gate = jnp.dot(h, wg_ref[:, c0:c0 + cw], preferred_element_type=jnp.float32)
        up = jnp.dot(h, wu_ref[:, c0:c0 + cw], preferred_element_type=jnp.float32)
        act = (gate * _sigmoid(gate) * up).astype(MXU_DTYPE)
        out = out + jnp.dot(act, wd_ref[c0:c0 + cw, :], preferred_element_type=jnp.float32)
    o_ref[...] = _rms(out, nf_ref[...])


def _tail(proj2, attn2, x2d, pool_w, pool_scale, wpp, wpa, wo, norm2_w, wg, wu, wd, norm_f_w, seq_len):
    m, d = x2d.shape
    tm = min(ROW_TILE, seq_len)
    halo_per_tile = tm // POOL_HALO
    weights = (pool_w, pool_scale, wpp, wpa, wo, norm2_w, wg, wu, wd, norm_f_w)
    return pl.pallas_call(
        functools.partial(_tail_kernel, seq_len=seq_len),
        grid=(m // tm,),
        in_specs=[pl.BlockSpec((tm, POOL_WIDTH), lambda i: (i, COL_POOL // POOL_WIDTH)),
                  pl.BlockSpec((POOL_HALO, POOL_WIDTH),
                               lambda i: (jnp.maximum(i * halo_per_tile - 1, 0), COL_POOL // POOL_WIDTH)),
                  pl.BlockSpec((tm, d), lambda i: (i, COL_GPOOL // D_MODEL)),
                  pl.BlockSpec((tm, d), lambda i: (i, COL_GATTN // D_MODEL)),
                  pl.BlockSpec((tm, ATTN_Q_WIDTH), lambda i: (i, 0)),
                  pl.BlockSpec((tm, d), lambda i: (i, 0))] + [_const_spec(w.shape) for w in weights],
        out_specs=pl.BlockSpec((tm, d), lambda i: (i, 0)),
        out_shape=jax.ShapeDtypeStruct((m, d), jnp.float32),
        compiler_params=_cparams(1),
        name="tail",
    )(proj2, proj2, proj2, proj2, attn2, x2d, *weights)


def _rope_tables(seq_len):
    inv_freq = 1.0 / (ROPE_THETA ** (jnp.arange(0, ROPE_DIM, 2, dtype=jnp.float32) / ROPE_DIM))
    ang = jnp.arange(seq_len).astype(jnp.float32)[:, None] * inv_freq[None, :]
    cos, sin = jnp.cos(ang), jnp.sin(ang)
    rest = HEAD_DIM - ROPE_DIM
    cos_h = jnp.concatenate([cos, cos, jnp.ones((seq_len, rest), jnp.float32)], axis=1)
    sin_h = jnp.concatenate([sin, sin, jnp.zeros((seq_len, rest), jnp.float32)], axis=1)
    return cos.T, sin.T, cos_h, sin_h


def _rotate_half_matrix():
    rot = np.zeros((KV_WIDTH, KV_WIDTH), np.float32)
    for l in range(KV_WIDTH):
        d = l % HEAD_DIM
        if d < ROPE_HALF:
            rot[l + ROPE_HALF, l] = -1.0
        elif d < ROPE_DIM:
            rot[l - ROPE_HALF, l] = 1.0
    return jnp.asarray(rot, MXU_DTYPE)


def _layer(x, norm1_w, w_in, pool_w, pool_scale, cmp_pe_k, cmp_w1_k, cmp_b1_k, cmp_w2_k,
           cmp_pe_v, cmp_w1_v, cmp_b1_v, cmp_w2_v, w_proj_pool, w_proj_attn, w_out,
           norm2_w, w_ffn_gate, w_ffn_up, w_ffn_down, norm_f_w):
    b, s, d = x.shape
    m = b * s
    n_blk = s // SLC_BLOCK
    assert s % Q_TILE == 0 and s % ROW_TILE == 0 and (s & (s - 1)) == 0, s
    cd = MXU_DTYPE

    g0 = COL_GPOOL
    w_parts = (w_in[:, :g0].astype(cd), w_in[:, g0 + GATE_NSA:].astype(cd),
               jnp.pad(w_in[:, g0:g0 + GATE_NSA], ((0, 0), (0, LANE - GATE_NSA))).astype(cd))
    x2d = x.reshape(m, d)
    proj2, kvc = _inproj(x2d, norm1_w.reshape(1, d), w_parts)
    proj3 = proj2.reshape(b, s, PROJ_WIDTH)

    def pe8(pe):
        return jnp.broadcast_to(pe.reshape(1, -1), (8, pe.size)).astype(cd)

    kvc_rows = kvc.reshape(2 * N_KV_GROUPS, b, n_blk, SLC_BLOCK * HEAD_DIM)
    kc_r, vc_rt = _compress(kvc_rows,
                            pe8(cmp_pe_k), cmp_w1_k.astype(cd), cmp_b1_k.reshape(1, -1), cmp_w2_k.astype(cd),
                            pe8(cmp_pe_v), cmp_w1_v.astype(cd), cmp_b1_v.reshape(1, -1), cmp_w2_v.astype(cd))

    cos_q, sin_q, cos_k, sin_k = _rope_tables(s)
    kaug, vst, kwin, vwt, knorm = _kprep(proj3, cos_k, sin_k, _rotate_half_matrix())
    ocmp, bias = _cmpsel(proj3, kc_r, vc_rt)
    attn = _flash(proj3, cos_q, sin_q, bias, kaug, vst, kwin, vwt, knorm, ocmp)
    return _tail(proj2, attn.reshape(m, ATTN_Q_WIDTH), x2d, pool_w.astype(cd),
                 pool_scale.reshape(1, -1), w_proj_pool.astype(cd), w_proj_attn.astype(cd),
                 w_out.astype(cd), norm2_w.reshape(1, d), w_ffn_gate.astype(cd), w_ffn_up.astype(cd),
                 w_ffn_down.astype(cd), norm_f_w.reshape(1, d), s)


def kernel(x, norm1_w, w_in, pool_w, pool_scale, cmp_pe_k, cmp_w1_k, cmp_b1_k, cmp_w2_k, cmp_pe_v,
           cmp_w1_v, cmp_b1_v, cmp_w2_v, w_proj_pool, w_proj_attn, w_out, norm2_w, w_ffn_gate,
           w_ffn_up, w_ffn_down, norm_f_w):
    b, s, d = x.shape
    depth = w_in.shape[0]
    assert depth == 1, "the final norm is fused into the last layer's FFN kernel"
    out = _layer(x, norm1_w[0], w_in[0], pool_w[0], pool_scale[0], cmp_pe_k[0], cmp_w1_k[0], cmp_b1_k[0],
                 cmp_w2_k[0], cmp_pe_v[0], cmp_w1_v[0], cmp_b1_v[0], cmp_w2_v[0], w_proj_pool[0],
                 w_proj_attn[0], w_out[0], norm2_w[0], w_ffn_gate[0], w_ffn_up[0], w_ffn_down[0], norm_f_w)
    return out.reshape(b, s, d)
```

```python
import functools

import numpy as np
import jax
import jax.numpy as jnp
from jax import lax
from jax.experimental import pallas as pl
from jax.experimental.pallas import tpu as pltpu

D_MODEL = 1024
N_HEADS = 16
HEAD_DIM = 64
N_KV_GROUPS = 4
HEADS_PER_GROUP = N_HEADS // N_KV_GROUPS
ROPE_DIM = HEAD_DIM // 4
ROPE_HALF = ROPE_DIM // 2
ROPE_THETA = 500000.0
CMP_BLOCK = 32
CMP_STRIDE = 16
CMP_PER_SLC = 4
SLC_BLOCK = 64
SLC_TOPK = 16
WINDOW = 512
ATTN_Q_WIDTH = N_HEADS * HEAD_DIM
KV_WIDTH = N_KV_GROUPS * HEAD_DIM
POOL_WIDTH = D_MODEL // 2
POOL_WINDOWS = (2, 4, 8, 16)
POOL_GROUP = POOL_WIDTH // len(POOL_WINDOWS)
POOL_HALO = 16
RMS_EPS = 1e-6

LANE = 128
SUBLANE = 8
MXU_DTYPE = jnp.bfloat16
ACT_DTYPE = jnp.bfloat16
VMEM_LIMIT_BYTES = 56 * 1024 * 1024

GATE_NSA = N_HEADS * 3
COL_POOL = 0
COL_Q = COL_POOL + POOL_WIDTH
COL_KV = COL_Q + ATTN_Q_WIDTH
COL_GPOOL = COL_KV + 6 * KV_WIDTH
COL_GATTN = COL_GPOOL + D_MODEL
COL_GNSA = COL_GATTN + D_MODEL
PROJ_WIDTH = COL_GNSA + LANE
GATE_ROWS = 16

ROW_TILE = 512
Q_TILE = 256
SLC_SPAN = 2048
SLC_SUBTILE = 512
KPREP_TILE = 1024
FF_CHUNK = 512
SCORE_SCALE = HEAD_DIM ** -0.5
LOG2E = float(np.log2(np.e))
FINITE_LIMIT = 3e38
STAB_MARGIN = 1.03
L_MIN = 2.0 ** -64
MASK_BIAS = float(2 ** 30)
NEG_BIG = -1e30
IMP_BIG = 1e30
CHOSEN = -3e38
CHOSEN_LIMIT = -2e38
N_FORCED = 3


def _log2(n):
    k = int(n).bit_length() - 1
    assert 1 << k == n, n
    return k


def _cparams(n_grid):
    return pltpu.CompilerParams(dimension_semantics=("arbitrary",) * n_grid,
                                vmem_limit_bytes=VMEM_LIMIT_BYTES)


def _const_spec(shape):
    nd = len(shape)
    return pl.BlockSpec(shape, lambda *_: (0,) * nd, pipeline_mode=pl.Buffered(1))


def _sigmoid(x):
    return 1.0 / (1.0 + jnp.exp(-x))


def _lost_range(total, smallest):
    return jnp.logical_not(jnp.sum(total) < FINITE_LIMIT) | (jnp.min(smallest) < L_MIN)


def _rms(x, w):
    var = jnp.mean(x * x, axis=-1, keepdims=True)
    return x * lax.rsqrt(var + RMS_EPS) * w


def _inproj_kernel(x_ref, nw_ref, wa_ref, wb_ref, wc_ref, o_ref, kvc_ref):
    h = _rms(x_ref[...], nw_ref[...]).astype(MXU_DTYPE)
    n = o_ref.shape[1]
    for c0 in range(0, n, KV_WIDTH):
        cw = min(KV_WIDTH, n - c0)
        w_ref, w0 = ((wa_ref, 0) if c0 < COL_GPOOL else (wb_ref, COL_GPOOL) if c0 < COL_GNSA
                     else (wc_ref, COL_GNSA))
        res = jnp.dot(h, w_ref[:, c0 - w0:c0 - w0 + cw],
                      preferred_element_type=jnp.float32).astype(o_ref.dtype)
        o_ref[:, c0:c0 + cw] = res
        which = (c0 - COL_KV) // KV_WIDTH
        if c0 >= COL_KV and which < 2:
            for g in range(N_KV_GROUPS):
                kvc_ref[which * N_KV_GROUPS + g] = res[:, g * HEAD_DIM:(g + 1) * HEAD_DIM]


def _inproj(x2d, norm_w, w_parts):
    m, d = x2d.shape
    n = sum(w.shape[1] for w in w_parts)
    assert n == PROJ_WIDTH
    return pl.pallas_call(
        _inproj_kernel,
        grid=(m // ROW_TILE,),
        in_specs=[pl.BlockSpec((ROW_TILE, d), lambda i: (i, 0)),
                  _const_spec((1, d))] + [_const_spec(w.shape) for w in w_parts],
        out_specs=[pl.BlockSpec((ROW_TILE, n), lambda i: (i, 0)),
                   pl.BlockSpec((2 * N_KV_GROUPS, ROW_TILE, HEAD_DIM), lambda i: (0, i, 0))],
        out_shape=[jax.ShapeDtypeStruct((m, n), ACT_DTYPE),
                   jax.ShapeDtypeStruct((2 * N_KV_GROUPS, m, HEAD_DIM), ACT_DTYPE)],
        compiler_params=_cparams(1),
        name="inproj",
    )(x2d, norm_w, *w_parts)


def _gelu_tanh(x):
    return 0.5 * x * (1.0 + jnp.tanh(np.sqrt(2.0 / np.pi) * (x + 0.044715 * (x * x * x))))


def _compress_one(r_ref, pe_ref, w1_ref, b1_ref, w2_ref):
    half = w1_ref.shape[0] // 2
    n_blk = r_ref.shape[2]
    rows = [r_ref[0, 0, :, r * half:(r + 1) * half] for r in range(CMP_PER_SLC)]
    top = [jnp.dot(x, w1_ref[0:half, :], preferred_element_type=jnp.float32) for x in rows]
    bot = [jnp.dot(x, w1_ref[half:, :], preferred_element_type=jnp.float32) for x in rows]
    bias = jnp.dot(pe_ref[...], w1_ref[...], preferred_element_type=jnp.float32)[0:1, :] + b1_ref[...]
    nxt = bot[1:] + [pltpu.roll(bot[0], n_blk - 1, 0)]
    hid = jnp.concatenate([top[r] + nxt[r] for r in range(CMP_PER_SLC)], axis=0) + bias
    return jnp.dot(_gelu_tanh(hid).astype(MXU_DTYPE), w2_ref[...], preferred_element_type=jnp.float32)


def _compress_kernel(rk_ref, rv_ref, pek_ref, w1k_ref, b1k_ref, w2k_ref,
                     pev_ref, w1v_ref, b1v_ref, w2v_ref, ok_ref, ovt_ref):
    kc = _compress_one(rk_ref, pek_ref, w1k_ref, b1k_ref, w2k_ref)
    vc = _compress_one(rv_ref, pev_ref, w1v_ref, b1v_ref, w2v_ref)
    ok_ref[0, 0] = jnp.concatenate([kc * (SCORE_SCALE * LOG2E), jnp.zeros_like(kc)], axis=1).astype(ok_ref.dtype)
    ovt_ref[0, 0] = jnp.concatenate([kc, vc], axis=1).T[HEAD_DIM:, :].astype(ovt_ref.dtype)


def _compress(kvc_rows, pek, w1k, b1k, w2k, pev, w1v, b1v, w2v):
    _, b, n_blk, width = kvc_rows.shape
    dk = w2k.shape[1]
    n_rows = n_blk * CMP_PER_SLC
    w_specs = [_const_spec(a.shape) for a in (pek, w1k, b1k, w2k)]
    return pl.pallas_call(
        _compress_kernel,
        grid=(b, N_KV_GROUPS),
        in_specs=[pl.BlockSpec((1, 1, n_blk, width), lambda i, j: (j, i, 0, 0)),
                  pl.BlockSpec((1, 1, n_blk, width), lambda i, j: (N_KV_GROUPS + j, i, 0, 0))]
        + w_specs + w_specs,
        out_specs=[pl.BlockSpec((1, 1, n_rows, 2 * dk), lambda i, j: (i, j, 0, 0)),
                   pl.BlockSpec((1, 1, dk, n_rows), lambda i, j: (i, j, 0, 0))],
        out_shape=[jax.ShapeDtypeStruct((b, N_KV_GROUPS, n_rows, 2 * dk), MXU_DTYPE),
                   jax.ShapeDtypeStruct((b, N_KV_GROUPS, dk, n_rows), MXU_DTYPE)],
        compiler_params=_cparams(2),
        name="compress",
    )(kvc_rows, kvc_rows, pek, w1k, b1k, w2k, pev, w1v, b1v, w2v)


def _kprep_kernel(ks_ref, vs_ref, kw_ref, vw_ref, cos_ref, sin_ref, rot_ref,
                  kaug_ref, vst_ref, kwin_ref, vwt_ref, knorm_ref):
    ts = ks_ref.shape[1]
    t0 = pl.program_id(1) * ts
    cos = jnp.concatenate([cos_ref[...]] * N_KV_GROUPS, axis=1)
    sin = jnp.concatenate([sin_ref[...]] * N_KV_GROUPS, axis=1)
    rot = rot_ref[...]

    def rope(k_ref):
        k = k_ref[0]
        krot = jnp.dot(k.astype(MXU_DTYPE), rot, preferred_element_type=jnp.float32)
        return (k.astype(jnp.float32) * cos + krot * sin) * (SCORE_SCALE * LOG2E)

    n_blk = kaug_ref.shape[3] - 2 * HEAD_DIM
    blk = lax.broadcasted_iota(jnp.int32, (ts, n_blk), 1)
    key_blk = (t0 + lax.broadcasted_iota(jnp.int32, (ts, n_blk), 0)) >> _log2(SLC_BLOCK)
    onehot = jnp.where(blk == key_blk, 1.0, 0.0).astype(kaug_ref.dtype)
    ks_f = rope(ks_ref)
    kw_f = rope(kw_ref)
    ks = ks_f.astype(kaug_ref.dtype)
    kw = kw_f.astype(kwin_ref.dtype)

    seg_l = lax.broadcasted_iota(jnp.int32, (KV_WIDTH, LANE), 0) >> _log2(HEAD_DIM)
    seg_c = lax.broadcasted_iota(jnp.int32, (KV_WIDTH, LANE), 1)
    sq = (jnp.dot(ks_f * ks_f, jnp.where(seg_c == seg_l, 1.0, 0.0), preferred_element_type=jnp.float32)
          + jnp.dot(kw_f * kw_f, jnp.where(seg_c == seg_l + N_KV_GROUPS, 1.0, 0.0),
                    preferred_element_type=jnp.float32))
    tile_max = jnp.broadcast_to(jnp.max(sq, axis=0, keepdims=True), knorm_ref.shape[1:])

    @pl.when(pl.program_id(1) == 0)
    def _():
        knorm_ref[0] = tile_max

    @pl.when(pl.program_id(1) > 0)
    def _():
        knorm_ref[0] = jnp.maximum(knorm_ref[0], tile_max)

    vst = vs_ref[0].astype(jnp.float32).T.astype(vst_ref.dtype)
    vwt = vw_ref[0].astype(jnp.float32).T.astype(vwt_ref.dtype)
    ones_col = jnp.where(lax.broadcasted_iota(jnp.int32, (ts, HEAD_DIM), 1) == 0, 1.0, 0.0
                         ).astype(kaug_ref.dtype)
    for g in range(N_KV_GROUPS):
        sl = slice(g * HEAD_DIM, (g + 1) * HEAD_DIM)
        kaug_ref[0, g, :, 0:n_blk] = onehot
        kaug_ref[0, g, :, n_blk:n_blk + HEAD_DIM] = ks[:, sl]
        kaug_ref[0, g, :, n_blk + HEAD_DIM:] = ones_col
        kwin_ref[0, g, :, 0:HEAD_DIM] = kw[:, sl]
        kwin_ref[0, g, :, HEAD_DIM:] = ones_col
        vst_ref[0, g] = vst[sl, :]
        vwt_ref[0, g] = vwt[sl, :]


def _kprep(proj3, cos_k, sin_k, rot):
    b, s, _ = proj3.shape
    n_blk = s // SLC_BLOCK
    ts = min(KPREP_TILE, s)
    kvb = COL_KV // KV_WIDTH

    def col(j):
        return pl.BlockSpec((1, ts, KV_WIDTH), lambda i, t: (i, t, kvb + j))

    tab = pl.BlockSpec((ts, HEAD_DIM), lambda i, t: (t, 0))
    aug_w = n_blk + 2 * HEAD_DIM
    return pl.pallas_call(
        _kprep_kernel,
        grid=(b, s // ts),
        in_specs=[col(2), col(3), col(4), col(5), tab, tab, _const_spec(rot.shape)],
        out_specs=[pl.BlockSpec((1, N_KV_GROUPS, ts, aug_w), lambda i, t: (i, 0, t, 0)),
                   pl.BlockSpec((1, N_KV_GROUPS, HEAD_DIM, ts), lambda i, t: (i, 0, 0, t)),
                   pl.BlockSpec((1, N_KV_GROUPS, ts, 2 * HEAD_DIM), lambda i, t: (i, 0, t, 0)),
                   pl.BlockSpec((1, N_KV_GROUPS, HEAD_DIM, ts), lambda i, t: (i, 0, 0, t)),
                   pl.BlockSpec((1, SUBLANE, LANE), lambda i, t: (i, 0, 0))],
        out_shape=[jax.ShapeDtypeStruct((b, N_KV_GROUPS, s, aug_w), MXU_DTYPE),
                   jax.ShapeDtypeStruct((b, N_KV_GROUPS, HEAD_DIM, s), MXU_DTYPE),
                   jax.ShapeDtypeStruct((b, N_KV_GROUPS, s, 2 * HEAD_DIM), MXU_DTYPE),
                   jax.ShapeDtypeStruct((b, N_KV_GROUPS, HEAD_DIM, s), MXU_DTYPE),
                   jax.ShapeDtypeStruct((b, SUBLANE, LANE), jnp.float32)],
        compiler_params=_cparams(2),
        name="kprep",
    )(proj3, proj3, proj3, proj3, cos_k, sin_k, rot)


def _heads_to_lanes(xt):
    return jnp.concatenate([xt[h * HEAD_DIM:(h + 1) * HEAD_DIM, :] for h in range(HEADS_PER_GROUP)],
                           axis=1)


def _vis_rows(tq):
    levels = tq // CMP_STRIDE
    dt = np.arange(tq)
    v = (dt - (CMP_BLOCK - 1)) // CMP_STRIDE - (-(CMP_BLOCK - 1)) // CMP_STRIDE
    rows = np.zeros((HEAD_DIM, tq), np.float32)
    for c in range(1, levels + 1):
        rows[c] = np.where(v < c, -MASK_BIAS, 0.0)
    rows[levels + 1] = -MASK_BIAS
    return jnp.asarray(np.tile(rows, (1, HEADS_PER_GROUP)), MXU_DTYPE)


def _cmpsel_kernel(q_ref, kc_ref, vct_ref, vis_ref, ocmp_ref, bias_ref):
    tq = q_ref.shape[1]
    n_blk = bias_ref.shape[2]
    m_rows = HEADS_PER_GROUP * tq
    t0 = pl.program_id(1) * tq
    levels = tq // CMP_STRIDE

    def body(nr):
        n_tok = CMP_PER_SLC * nr
        j_idx = lax.broadcasted_iota(jnp.int32, (nr, LANE), 0)
        n_idx = jnp.concatenate([j_idx * CMP_PER_SLC + r for r in range(CMP_PER_SLC)], axis=0)
        n_first = (t0 - (CMP_BLOCK - 1)) >> _log2(CMP_STRIDE)
        u = jnp.clip(n_idx - n_first, 0, levels + 1)
        vis_cols = jnp.where(lax.broadcasted_iota(jnp.int32, (n_tok, LANE), 1) - HEAD_DIM == u,
                             1.0, 0.0).astype(MXU_DTYPE)
        vis_rows = vis_ref[...]
        t_row = t0 + (lax.broadcasted_iota(jnp.int32, (1, m_rows), 1) & (tq - 1))
        jrow = lax.broadcasted_iota(jnp.int32, (nr, tq), 0)
        jt = (t0 + lax.broadcasted_iota(jnp.int32, (nr, tq), 1)) >> _log2(SLC_BLOCK)
        causal_blk = jrow <= jt
        forced = (jrow == 0) | (jrow == jt) | (jrow == jt - 1)
        jf = jrow.astype(jnp.float32)
        n_extract = min(SLC_TOPK, n_blk) - N_FORCED
        jt_row = (t0 + lax.broadcasted_iota(jnp.int32, (1, tq), 1)) >> _log2(SLC_BLOCK)
        want = jnp.minimum(jt_row + 1, SLC_TOPK).astype(jnp.float32)

        def store_bias(pb, marks):
            bias_ref[pb, 0, 0:nr, :] = jnp.where(marks < CHOSEN_LIMIT, 0.0, -MASK_BIAS).astype(bias_ref.dtype)

        pending = []
        for pb in range(q_ref.shape[0]):
            kc = jnp.concatenate([kc_ref[pb, 0, r * n_blk:r * n_blk + nr, :] for r in range(CMP_PER_SLC)],
                                 axis=0)
            vct = jnp.concatenate([vct_ref[pb, 0, :, r * n_blk:r * n_blk + nr] for r in range(CMP_PER_SLC)],
                                  axis=1)
            qt = _heads_to_lanes(q_ref[pb].astype(jnp.float32).T).astype(MXU_DTYPE)
            s = jnp.dot(kc + vis_cols, jnp.concatenate([qt, vis_rows], axis=0),
                        preferred_element_type=jnp.float32)
            mx = jnp.max(s, axis=0, keepdims=True)
            p = jnp.exp2(s - mx)
            den = jnp.sum(p, axis=0, keepdims=True)
            inv = jnp.where(t_row >= CMP_BLOCK - 1, 1.0 / den, 0.0)
            ocmp_ref[pb, 0, 0] = inv * jnp.dot(vct, p.astype(MXU_DTYPE), preferred_element_type=jnp.float32)

            imp = None
            for h in range(HEADS_PER_GROUP):
                ph = p[:, h * tq:(h + 1) * tq]
                last = ph[3 * nr:4 * nr, :]
                prev = jnp.where(jrow == 0, 0.0, pltpu.roll(last, 1, 0))
                taps = ph[0:nr, :] + ph[nr:2 * nr, :] + ph[2 * nr:3 * nr, :] + last + prev
                imp_h = taps * inv[:, h * tq:(h + 1) * tq]
                imp = imp_h if imp is None else imp + imp_h

            val = jnp.where(forced, CHOSEN, jnp.where(causal_blk, imp, -IMP_BIG))
            marks = val
            for _ in range(n_extract):
                marks = jnp.where(marks == jnp.max(marks, axis=0, keepdims=True), CHOSEN, marks)
            store_bias(pb, marks)
            if nr < n_blk:
                bias_ref[pb, 0, nr:, :] = jnp.full((n_blk - nr, tq), -MASK_BIAS, bias_ref.dtype)
            count = jnp.sum(jnp.where((marks < CHOSEN_LIMIT) & causal_blk, 1.0, 0.0), axis=0, keepdims=True)
            pending.append((pb, val, jnp.max(jnp.abs(count - want)) > 0.5))

        for pb, val, tied in pending:
            @pl.when(tied)
            def _(val=val, pb=pb):
                v = val
                for _ in range(n_extract):
                    top = jnp.max(v, axis=0, keepdims=True)
                    first = jnp.min(jnp.where(v == top, jf, float(n_blk)), axis=0, keepdims=True)
                    v = jnp.where(jf == first, CHOSEN, v)
                store_bias(pb, v)

    sizes = [n_blk * f // 4 for f in (1, 2, 3, 4) if (n_blk * f // 4) % 16 == 0]
    lo = 0
    for nr in sizes:
        hi = nr * SLC_BLOCK if nr < n_blk else None
        end = t0 + tq
        cond = (end > lo) if hi is None else (end > lo) & (end <= hi)
        pl.when(cond)(functools.partial(body, nr))
        lo = hi


def _cmpsel(proj3, kc_r, vc_rt):
    b, s, _ = proj3.shape
    tq = min(Q_TILE, s)
    nq = s // tq
    n_cmp_pad = kc_r.shape[2]
    n_blk = s // SLC_BLOCK
    qb = COL_Q // KV_WIDTH
    m_rows = HEADS_PER_GROUP * tq
    return pl.pallas_call(
        _cmpsel_kernel,
        grid=(N_KV_GROUPS, nq),
        in_specs=[pl.BlockSpec((b, tq, KV_WIDTH), lambda g, q: (0, q, qb + g)),
                  pl.BlockSpec((b, 1, n_cmp_pad, LANE), lambda g, q: (0, g, 0, 0)),
                  pl.BlockSpec((b, 1, HEAD_DIM, n_cmp_pad), lambda g, q: (0, g, 0, 0)),
                  _const_spec((HEAD_DIM, m_rows))],
        out_specs=[pl.BlockSpec((b, 1, 1, HEAD_DIM, m_rows), lambda g, q: (0, g, q, 0, 0)),
                   pl.BlockSpec((b, 1, n_blk, tq), lambda g, q: (0, g, 0, q))],
        out_shape=[jax.ShapeDtypeStruct((b, N_KV_GROUPS, nq, HEAD_DIM, m_rows), jnp.float32),
                   jax.ShapeDtypeStruct((b, N_KV_GROUPS, n_blk, s), MXU_DTYPE)],
        compiler_params=_cparams(2),
        name="cmpsel",
    )(proj3, kc_r, vc_rt, _vis_rows(tq))


def _flash_kernel(q_ref, cos_ref, sin_ref, bias_ref, kaug_ref, vst_ref, kwin_ref, vwt_ref, knorm_ref,
                  wmask_ref, gate_ref, ocmp_ref, o_ref, m_ref, l_ref, acc_ref, lw_ref, accw_ref):
    nb = q_ref.shape[0]
    tq = q_ref.shape[1]
    m_rows = HEADS_PER_GROUP * tq
    grp = pl.program_id(0)
    qi = pl.program_id(1)
    t0 = qi * tq
    kd = pl.multiple_of(t0, tq)
    probs = range(nb)
    zpad = jnp.zeros((HEAD_DIM, m_rows), MXU_DTYPE)
    pad_row = lax.broadcasted_iota(jnp.int32, (HEAD_DIM, m_rows), 0)
    lane = lax.broadcasted_iota(jnp.int32, (1, LANE), 1)

    def stab_rows(neg_m):
        return jnp.where(pad_row == 0, neg_m, 0.0).astype(MXU_DTYPE)

    cos = cos_ref[...]
    sin = sin_ref[...]
    qr, bias4, qs_slc, qs_win = [], [], [], []
    for pb in probs:
        qt = q_ref[pb].astype(jnp.float32).T
        parts = []
        for h in range(HEADS_PER_GROUP):
            qh = qt[h * HEAD_DIM:(h + 1) * HEAD_DIM, :]
            x1 = qh[0:ROPE_HALF, :]
            x2 = qh[ROPE_HALF:ROPE_DIM, :]
            parts.append(jnp.concatenate([x1 * cos - x2 * sin, x2 * cos + x1 * sin, qh[ROPE_DIM:, :]],
                                         axis=0))
        qf = jnp.concatenate(parts, axis=1)
        q_sq = jnp.sum(qf * qf, axis=0, keepdims=True)
        k_sq = knorm_ref[pb, 0:1, :]
        ks_sq = jnp.max(jnp.where(lane == grp, k_sq, 0.0), axis=1, keepdims=True)
        kw_sq = jnp.max(jnp.where(lane == grp + N_KV_GROUPS, k_sq, 0.0), axis=1, keepdims=True)
        qr.append(qf.astype(MXU_DTYPE))
        bias4.append(jnp.concatenate([bias_ref[pb, 0]] * HEADS_PER_GROUP, axis=1))
        qs_slc.append(jnp.concatenate([bias4[pb], qr[pb], stab_rows(-STAB_MARGIN * jnp.sqrt(q_sq * ks_sq))],
                                      axis=0))
        qs_win.append(jnp.concatenate([qr[pb], stab_rows(-STAB_MARGIN * jnp.sqrt(q_sq * kw_sq))], axis=0))

    n_back = WINDOW // tq

    def win_tile(back):
        if back == 0:
            return kd, tq, wmask_ref[0]
        inside = wmask_ref[back] if back * tq + tq - 1 >= WINDOW else 0.0
        return (pl.multiple_of(jnp.maximum(t0 - back * tq, 0), tq), tq,
                jnp.where(qi >= back, inside, NEG_BIG))

    def fast_tiles(k_ref, v_ref, qs, tiles, l_out, acc_out, first=False):
        for pb in probs:
            l_add = acc_add = None
            for k0, tk, mask in tiles:
                s = jnp.dot(k_ref[pb, 0, pl.ds(k0, tk), :], qs[pb], preferred_element_type=jnp.float32)
                if mask is not None:
                    s = s + mask
                p = jnp.exp2(s)
                l_t = jnp.sum(p, axis=0, keepdims=True)
                acc_t = jnp.dot(v_ref[pb, 0, :, pl.ds(k0, tk)], p.astype(MXU_DTYPE),
                                preferred_element_type=jnp.float32)
                l_add = l_t if l_add is None else l_add + l_t
                acc_add = acc_t if acc_add is None else acc_add + acc_t
            if first:
                l_out[pb] = l_add
                acc_out[pb] = acc_add
            else:
                l_out[pb] += l_add
                acc_out[pb] += acc_add

    fast_tiles(kaug_ref, vst_ref, qs_slc, [(kd, tq, wmask_ref[0])], l_ref, acc_ref, first=True)
    fast_tiles(kwin_ref, vwt_ref, qs_win, [win_tile(back) for back in range(n_back + 1)],
               lw_ref, accw_ref, first=True)

    sub = SLC_SUBTILE
    n_sub = SLC_SPAN // sub

    def span_body(i, carry):
        base = i * SLC_SPAN
        fast_tiles(kaug_ref, vst_ref, qs_slc,
                   [(pl.multiple_of(base + j * sub, sub), sub, None) for j in range(n_sub)],
                   l_ref, acc_ref)
        return carry

    n_span = t0 // SLC_SPAN
    lax.fori_loop(0, n_span, span_body, 0)
    done = n_span * SLC_SPAN
    size = SLC_SPAN // 2
    while size >= tq:
        take = ((t0 // size) % 2) == 1

        @pl.when(take)
        def _(done=done, size=size):
            piece = min(size, sub)
            fast_tiles(kaug_ref, vst_ref, qs_slc,
                       [(pl.multiple_of(done + j * piece, piece), piece, None) for j in range(size // piece)],
                       l_ref, acc_ref)

        done = done + jnp.where(take, size, 0)
        size //= 2

    def exact_start(pb, s, vt, l_out, acc_out):
        mx = jnp.max(s, axis=0, keepdims=True)
        p = jnp.exp2(s - mx)
        m_ref[pb] = mx
        l_out[pb] = jnp.sum(p, axis=0, keepdims=True)
        acc_out[pb] = jnp.dot(vt, p.astype(MXU_DTYPE), preferred_element_type=jnp.float32)

    def exact_update(pb, s, vt, l_out, acc_out):
        m_old = m_ref[pb]
        mx = jnp.maximum(m_old, jnp.max(s, axis=0, keepdims=True))
        alpha = jnp.exp2(m_old - mx)
        p = jnp.exp2(s - mx)
        m_ref[pb] = mx
        l_out[pb] = alpha * l_out[pb] + jnp.sum(p, axis=0, keepdims=True)
        acc_out[pb] = alpha * acc_out[pb] + jnp.dot(vt, p.astype(MXU_DTYPE),
                                                    preferred_element_type=jnp.float32)

    check = l_low = None
    for pb in probs:
        c = (l_ref[pb] + lw_ref[pb] + jnp.sum(jnp.abs(acc_ref[pb]), axis=0, keepdims=True)
             + jnp.sum(jnp.abs(accw_ref[pb]), axis=0, keepdims=True))
        lo = jnp.minimum(l_ref[pb], lw_ref[pb])
        check = c if check is None else check + c
        l_low = lo if l_low is None else jnp.minimum(l_low, lo)

    @pl.when(_lost_range(check, l_low))
    def _():
        tcol = t0 + (lax.broadcasted_iota(jnp.int32, (tq, m_rows), 1) & (tq - 1))
        koff = lax.broadcasted_iota(jnp.int32, (tq, m_rows), 0)
        causal = t0 + koff <= tcol
        for pb in probs:
            q0 = jnp.concatenate([bias4[pb], qr[pb], zpad], axis=0)
            s = jnp.dot(kaug_ref[pb, 0, pl.ds(kd, tq), :], q0, preferred_element_type=jnp.float32)
            exact_start(pb, jnp.where(causal, s, NEG_BIG), vst_ref[pb, 0, :, pl.ds(kd, tq)], l_ref, acc_ref)

            def body(kt, carry, pb=pb, q0=q0):
                k0 = pl.multiple_of(kt * tq, tq)
                sk = jnp.dot(kaug_ref[pb, 0, pl.ds(k0, tq), :], q0, preferred_element_type=jnp.float32)
                exact_update(pb, sk, vst_ref[pb, 0, :, pl.ds(k0, tq)], l_ref, acc_ref)
                return carry

            lax.fori_loop(0, qi, body, 0)
            qw0 = jnp.concatenate([qr[pb], zpad], axis=0)
            for back in range(n_back + 1):
                kbase = t0 - back * tq
                k0 = pl.multiple_of(jnp.maximum(kbase, 0), tq)
                sw = jnp.dot(kwin_ref[pb, 0, pl.ds(k0, tq), :], qw0, preferred_element_type=jnp.float32)
                diff = tcol - (kbase + koff)
                sw = jnp.where((diff >= 0) & (diff < WINDOW) & (kbase + koff >= 0), sw, NEG_BIG)
                if back == 0:
                    exact_start(pb, sw, vwt_ref[pb, 0, :, pl.ds(k0, tq)], lw_ref, accw_ref)
                else:
                    exact_update(pb, sw, vwt_ref[pb, 0, :, pl.ds(k0, tq)], lw_ref, accw_ref)

    r_id = lax.broadcasted_iota(jnp.int32, (GATE_ROWS, LANE), 0)
    c_id = lax.broadcasted_iota(jnp.int32, (GATE_ROWS, LANE), 1)
    pick = jnp.where((c_id == grp * (HEADS_PER_GROUP * 3) + r_id) & (r_id < HEADS_PER_GROUP * 3),
                     1.0, 0.0).astype(gate_ref.dtype)
    for pb in probs:
        o_slc = acc_ref[pb] / l_ref[pb]
        o_win = accw_ref[pb] / lw_ref[pb]
        graw = lax.dot_general(pick, gate_ref[pb], (((1,), (1,)), ((), ())),
                               preferred_element_type=jnp.float32)
        gates = _sigmoid(graw)
        o_cmp = ocmp_ref[pb, 0, 0]
        mixed = []
        for h in range(HEADS_PER_GROUP):
            sl = slice(h * tq, (h + 1) * tq)
            mixed.append(gates[3 * h:3 * h + 1, :] * o_cmp[:, sl]
                         + gates[3 * h + 1:3 * h + 2, :] * o_slc[:, sl]
                         + gates[3 * h + 2:3 * h + 3, :] * o_win[:, sl])
        o_ref[pb] = jnp.concatenate(mixed, axis=0).T.astype(o_ref.dtype)


def _window_masks(tq):
    k = np.arange(tq)[:, None]
    t = np.arange(tq)[None, :]
    tiles = []
    for back in range(WINDOW // tq + 1):
        diff = t + back * tq - k
        tiles.append(np.tile(np.where((diff >= 0) & (diff < WINDOW), 0.0, NEG_BIG), (1, HEADS_PER_GROUP)))
    return jnp.asarray(np.stack(tiles), jnp.float32)


def _flash(proj3, cos_q, sin_q, bias, kaug, vst, kwin, vwt, knorm, ocmp):
    b, s, _ = proj3.shape
    tq = min(Q_TILE, s)
    nq = s // tq
    n_blk = s // SLC_BLOCK
    aug_w = kaug.shape[3]
    qb = COL_Q // KV_WIDTH
    m_rows = HEADS_PER_GROUP * tq

    wmask = _window_masks(tq)

    def resident(shape):
        return pl.BlockSpec((b, 1) + shape, lambda g, q: (0, g, 0, 0), pipeline_mode=pl.Buffered(1))

    return pl.pallas_call(
        _flash_kernel,
        grid=(N_KV_GROUPS, nq),
        in_specs=[pl.BlockSpec((b, tq, KV_WIDTH), lambda g, q: (0, q, qb + g)),
                  pl.BlockSpec((ROPE_HALF, tq), lambda g, q: (0, q)),
                  pl.BlockSpec((ROPE_HALF, tq), lambda g, q: (0, q)),
                  pl.BlockSpec((b, 1, n_blk, tq), lambda g, q: (0, g, 0, q)),
                  resident((s, aug_w)),
                  resident((HEAD_DIM, s)),
                  resident((s, 2 * HEAD_DIM)),
                  resident((HEAD_DIM, s)),
                  _const_spec((b, SUBLANE, LANE)),
                  _const_spec(wmask.shape),
                  pl.BlockSpec((b, tq, LANE), lambda g, q: (0, q, COL_GNSA // LANE)),
                  pl.BlockSpec((b, 1, 1, HEAD_DIM, m_rows), lambda g, q: (0, g, q, 0, 0))],
        out_specs=pl.BlockSpec((b, tq, KV_WIDTH), lambda g, q: (0, q, g)),
        out_shape=jax.ShapeDtypeStruct((b, s, ATTN_Q_WIDTH), ACT_DTYPE),
        scratch_shapes=[pltpu.VMEM((b, 1, m_rows), jnp.float32),
                        pltpu.VMEM((b, 1, m_rows), jnp.float32),
                        pltpu.VMEM((b, HEAD_DIM, m_rows), jnp.float32),
                        pltpu.VMEM((b, 1, m_rows), jnp.float32),
                        pltpu.VMEM((b, HEAD_DIM, m_rows), jnp.float32)],
        compiler_params=_cparams(2),
        name="flash",
    )(proj3, cos_q, sin_q, bias, kaug, vst, kwin, vwt, knorm, wmask, proj3, ocmp)


def _tail_kernel(u_ref, halo_ref, gp_ref, ga_ref, attn_ref, x_ref, pw_ref, ps_ref, wpp_ref, wpa_ref, wo_ref,
                 n2_ref, wg_ref, wu_ref, wd_ref, nf_ref, o_ref, *, seq_len):
    tm = u_ref.shape[0]
    t0 = (pl.program_id(0) * tm) & (seq_len - 1)

    u = u_ref[...].astype(jnp.float32)
    halo = jnp.where(t0 > 0, halo_ref[...].astype(jnp.float32), 0.0)
    ext = jnp.concatenate([halo, u], axis=0)
    t = t0 + lax.broadcasted_iota(jnp.int32, (tm, POOL_GROUP), 0)
    pooled = []
    for gi, w in enumerate(POOL_WINDOWS):
        sl = slice(gi * POOL_GROUP, (gi + 1) * POOL_GROUP)
        acc = ext[:, sl]
        span = 1
        while span < w:
            acc = acc + pltpu.roll(acc, span, 0)
            span *= 2
        cnt = jnp.minimum(t + 1, w).astype(jnp.float32)
        mean = acc[POOL_HALO:, :] / cnt
        mixed = jnp.dot((mean - u[:, sl]).astype(MXU_DTYPE), pw_ref[gi],
                        preferred_element_type=jnp.float32)
        pooled.append(mixed * ps_ref[:, sl])
    pool = jnp.concatenate(pooled, axis=1).astype(MXU_DTYPE)

    pp = jnp.dot(pool, wpp_ref[...], preferred_element_type=jnp.float32)
    pa = jnp.dot(attn_ref[...], wpa_ref[...], preferred_element_type=jnp.float32)
    merged = (_sigmoid(gp_ref[...].astype(jnp.float32)) * pp
              + _sigmoid(ga_ref[...].astype(jnp.float32)) * pa)
    x1 = x_ref[...] + jnp.dot(merged.astype(MXU_DTYPE), wo_ref[...], preferred_element_type=jnp.float32)

    h = _rms(x1, n2_ref[...]).astype(MXU_DTYPE)
    d_ff = wg_ref.shape[1]
    out = x1
    for c0 in range(0, d_ff, FF_CHUNK):
        cw = min(FF_CHUNK, d_ff - c0)
        gate = jnp.dot(h, wg_ref[:, c0:c0 + cw], preferred_element_type=jnp.float32)
        up = jnp.dot(h, wu_ref[:, c0:c0 + cw], preferred_element_type=jnp.float32)
        act = (gate * _sigmoid(gate) * up).astype(MXU_DTYPE)
        out = out + jnp.dot(act, wd_ref[c0:c0 + cw, :], preferred_element_type=jnp.float32)
    o_ref[...] = _rms(out, nf_ref[...])


def _tail(proj2, attn2, x2d, pool_w, pool_scale, wpp, wpa, wo, norm2_w, wg, wu, wd, norm_f_w, seq_len):
    m, d = x2d.shape
    tm = min(ROW_TILE, seq_len)
    halo_per_tile = tm // POOL_HALO
    weights = (pool_w, pool_scale, wpp, wpa, wo, norm2_w, wg, wu, wd, norm_f_w)
    return pl.pallas_call(
        functools.partial(_tail_kernel, seq_len=seq_len),
        grid=(m // tm,),
        in_specs=[pl.BlockSpec((tm, POOL_WIDTH), lambda i: (i, COL_POOL // POOL_WIDTH)),
                  pl.BlockSpec((POOL_HALO, POOL_WIDTH),
                               lambda i: (jnp.maximum(i * halo_per_tile - 1, 0), COL_POOL // POOL_WIDTH)),
                  pl.BlockSpec((tm, d), lambda i: (i, COL_GPOOL // D_MODEL)),
                  pl.BlockSpec((tm, d), lambda i: (i, COL_GATTN // D_MODEL)),
                  pl.BlockSpec((tm, ATTN_Q_WIDTH), lambda i: (i, 0)),
                  pl.BlockSpec((tm, d), lambda i: (i, 0))] + [_const_spec(w.shape) for w in weights],
        out_specs=pl.BlockSpec((tm, d), lambda i: (i, 0)),
        out_shape=jax.ShapeDtypeStruct((m, d), jnp.float32),
        compiler_params=_cparams(1),
        name="tail",
    )(proj2, proj2, proj2, proj2, attn2, x2d, *weights)


def _rope_tables(seq_len):
    inv_freq = 1.0 / (ROPE_THETA ** (jnp.arange(0, ROPE_DIM, 2, dtype=jnp.float32) / ROPE_DIM))
    ang = inv_freq[:, None] * jnp.arange(seq_len).astype(jnp.float32)[None, :]
    cos_t, sin_t = jnp.cos(ang), jnp.sin(ang)
    rest = HEAD_DIM - ROPE_DIM
    cos_h = jnp.concatenate([cos_t, cos_t, jnp.ones((rest, seq_len), jnp.float32)], axis=0).T
    sin_h = jnp.concatenate([sin_t, sin_t, jnp.zeros((rest, seq_len), jnp.float32)], axis=0).T
    return cos_t, sin_t, cos_h, sin_h


def _rotate_half_matrix():
    rot = np.zeros((KV_WIDTH, KV_WIDTH), np.float32)
    for l in range(KV_WIDTH):
        d = l % HEAD_DIM
        if d < ROPE_HALF:
            rot[l + ROPE_HALF, l] = -1.0
        elif d < ROPE_DIM:
            rot[l - ROPE_HALF, l] = 1.0
    return jnp.asarray(rot, MXU_DTYPE)


def _layer(x, norm1_w, w_in, pool_w, pool_scale, cmp_pe_k, cmp_w1_k, cmp_b1_k, cmp_w2_k,
           cmp_pe_v, cmp_w1_v, cmp_b1_v, cmp_w2_v, w_proj_pool, w_proj_attn, w_out,
           norm2_w, w_ffn_gate, w_ffn_up, w_ffn_down, norm_f_w):
    b, s, d = x.shape
    m = b * s
    n_blk = s // SLC_BLOCK
    assert s % Q_TILE == 0 and s % ROW_TILE == 0 and (s & (s - 1)) == 0, s
    cd = MXU_DTYPE

    g0 = COL_GPOOL
    w_parts = (w_in[:, :g0].astype(cd), w_in[:, g0 + GATE_NSA:].astype(cd),
               jnp.pad(w_in[:, g0:g0 + GATE_NSA], ((0, 0), (0, LANE - GATE_NSA))).astype(cd))
    x2d = x.reshape(m, d)
    proj2, kvc = _inproj(x2d, norm1_w.reshape(1, d), w_parts)
    proj3 = proj2.reshape(b, s, PROJ_WIDTH)

    def pe8(pe):
        return jnp.broadcast_to(pe.reshape(1, -1), (8, pe.size)).astype(cd)

    kvc_rows = kvc.reshape(2 * N_KV_GROUPS, b, n_blk, SLC_BLOCK * HEAD_DIM)
    kc_r, vc_rt = _compress(kvc_rows,
                            pe8(cmp_pe_k), cmp_w1_k.astype(cd), cmp_b1_k.reshape(1, -1), cmp_w2_k.astype(cd),
                            pe8(cmp_pe_v), cmp_w1_v.astype(cd), cmp_b1_v.reshape(1, -1), cmp_w2_v.astype(cd))

    cos_q, sin_q, cos_k, sin_k = _rope_tables(s)
    kaug, vst, kwin, vwt, knorm = _kprep(proj3, cos_k, sin_k, _rotate_half_matrix())
    ocmp, bias = _cmpsel(proj3, kc_r, vc_rt)
    attn = _flash(proj3, cos_q, sin_q, bias, kaug, vst, kwin, vwt, knorm, ocmp)
    return _tail(proj2, attn.reshape(m, ATTN_Q_WIDTH), x2d, pool_w.astype(cd),
                 pool_scale.reshape(1, -1), w_proj_pool.astype(cd), w_proj_attn.astype(cd),
                 w_out.astype(cd), norm2_w.reshape(1, d), w_ffn_gate.astype(cd), w_ffn_up.astype(cd),
                 w_ffn_down.astype(cd), norm_f_w.reshape(1, d), s)


def kernel(x, norm1_w, w_in, pool_w, pool_scale, cmp_pe_k, cmp_w1_k, cmp_b1_k, cmp_w2_k, cmp_pe_v,
           cmp_w1_v, cmp_b1_v, cmp_w2_v, w_proj_pool, w_proj_attn, w_out, norm2_w, w_ffn_gate,
           w_ffn_up, w_ffn_down, norm_f_w):
    b, s, d = x.shape
    depth = w_in.shape[0]
    assert depth == 1, "the final norm is fused into the last layer's FFN kernel"
    out = _layer(x, norm1_w[0], w_in[0], pool_w[0], pool_scale[0], cmp_pe_k[0], cmp_w1_k[0], cmp_b1_k[0],
                 cmp_w2_k[0], cmp_pe_v[0], cmp_w1_v[0], cmp_b1_v[0], cmp_w2_v[0], w_proj_pool[0],
                 w_proj_attn[0], w_out[0], norm2_w[0], w_ffn_gate[0], w_ffn_up[0], w_ffn_down[0], norm_f_w)
    return out.reshape(b, s, d)
```

```python
import functools

import numpy as np
import jax
import jax.numpy as jnp
from jax import lax
from jax.experimental import pallas as pl
from jax.experimental.pallas import tpu as pltpu

D_MODEL = 1024
N_HEADS = 16
HEAD_DIM = 64
N_KV_GROUPS = 4
HEADS_PER_GROUP = N_HEADS // N_KV_GROUPS
ROPE_DIM = HEAD_DIM // 4
ROPE_HALF = ROPE_DIM // 2
ROPE_THETA = 500000.0
CMP_BLOCK = 32
CMP_STRIDE = 16
CMP_PER_SLC = 4
SLC_BLOCK = 64
SLC_TOPK = 16
WINDOW = 512
ATTN_Q_WIDTH = N_HEADS * HEAD_DIM
KV_WIDTH = N_KV_GROUPS * HEAD_DIM
POOL_WIDTH = D_MODEL // 2
POOL_WINDOWS = (2, 4, 8, 16)
POOL_GROUP = POOL_WIDTH // len(POOL_WINDOWS)
POOL_HALO = 16
RMS_EPS = 1e-6

LANE = 128
SUBLANE = 8
MXU_DTYPE = jnp.bfloat16
ACT_DTYPE = jnp.bfloat16
VMEM_LIMIT_BYTES = 56 * 1024 * 1024

GATE_NSA = N_HEADS * 3
COL_POOL = 0
COL_Q = COL_POOL + POOL_WIDTH
COL_KV = COL_Q + ATTN_Q_WIDTH
COL_GPOOL = COL_KV + 6 * KV_WIDTH
COL_GATTN = COL_GPOOL + D_MODEL
COL_GNSA = COL_GATTN + D_MODEL
PROJ_WIDTH = COL_GNSA + LANE
GATE_ROWS = 16

ROW_TILE = 512
Q_TILE = 256
SLC_SPAN = 2048
SLC_SUBTILE = 512
KPREP_TILE = 1024
FF_CHUNK = 512
SCORE_SCALE = HEAD_DIM ** -0.5
LOG2E = float(np.log2(np.e))
FINITE_LIMIT = 3e38
STAB_MARGIN = 1.03
L_MIN = 2.0 ** -64
MASK_BIAS = float(2 ** 30)
NEG_BIG = -1e30
IMP_BIG = 1e30
CHOSEN = -3e38
CHOSEN_LIMIT = -2e38
N_FORCED = 3


def _log2(n):
    k = int(n).bit_length() - 1
    assert 1 << k == n, n
    return k


def _cparams(n_grid):
    return pltpu.CompilerParams(dimension_semantics=("arbitrary",) * n_grid,
                                vmem_limit_bytes=VMEM_LIMIT_BYTES)


def _const_spec(shape):
    nd = len(shape)
    return pl.BlockSpec(shape, lambda *_: (0,) * nd, pipeline_mode=pl.Buffered(1))


def _sigmoid(x):
    return 1.0 / (1.0 + jnp.exp(-x))


def _lost_range(total, smallest):
    return jnp.logical_not(jnp.sum(total) < FINITE_LIMIT) | (jnp.min(smallest) < L_MIN)


def _rms(x, w):
    var = jnp.mean(x * x, axis=-1, keepdims=True)
    return x * lax.rsqrt(var + RMS_EPS) * w


def _inproj_kernel(x_ref, nw_ref, wa_ref, wb_ref, wc_ref, o_ref, kvc_ref):
    h = _rms(x_ref[...], nw_ref[...]).astype(MXU_DTYPE)
    n = o_ref.shape[1]
    for c0 in range(0, n, KV_WIDTH):
        cw = min(KV_WIDTH, n - c0)
        w_ref, w0 = ((wa_ref, 0) if c0 < COL_GPOOL else (wb_ref, COL_GPOOL) if c0 < COL_GNSA
                     else (wc_ref, COL_GNSA))
        res = jnp.dot(h, w_ref[:, c0 - w0:c0 - w0 + cw],
                      preferred_element_type=jnp.float32).astype(o_ref.dtype)
        o_ref[:, c0:c0 + cw] = res
        which = (c0 - COL_KV) // KV_WIDTH
        if c0 >= COL_KV and which < 2:
            for g in range(N_KV_GROUPS):
                kvc_ref[which * N_KV_GROUPS + g] = res[:, g * HEAD_DIM:(g + 1) * HEAD_DIM]


def _inproj(x2d, norm_w, w_parts):
    m, d = x2d.shape
    n = sum(w.shape[1] for w in w_parts)
    assert n == PROJ_WIDTH
    return pl.pallas_call(
        _inproj_kernel,
        grid=(m // ROW_TILE,),
        in_specs=[pl.BlockSpec((ROW_TILE, d), lambda i: (i, 0)),
                  _const_spec((1, d))] + [_const_spec(w.shape) for w in w_parts],
        out_specs=[pl.BlockSpec((ROW_TILE, n), lambda i: (i, 0)),
                   pl.BlockSpec((2 * N_KV_GROUPS, ROW_TILE, HEAD_DIM), lambda i: (0, i, 0))],
        out_shape=[jax.ShapeDtypeStruct((m, n), ACT_DTYPE),
                   jax.ShapeDtypeStruct((2 * N_KV_GROUPS, m, HEAD_DIM), ACT_DTYPE)],
        compiler_params=_cparams(1),
        name="inproj",
    )(x2d, norm_w, *w_parts)


def _gelu_tanh(x):
    return 0.5 * x * (1.0 + jnp.tanh(np.sqrt(2.0 / np.pi) * (x + 0.044715 * (x * x * x))))


def _compress_one(r_ref, pe_ref, w1_ref, b1_ref, w2_ref):
    half = w1_ref.shape[0] // 2
    n_blk = r_ref.shape[2]
    rows = [r_ref[0, 0, :, r * half:(r + 1) * half] for r in range(CMP_PER_SLC)]
    top = [jnp.dot(x, w1_ref[0:half, :], preferred_element_type=jnp.float32) for x in rows]
    bot = [jnp.dot(x, w1_ref[half:, :], preferred_element_type=jnp.float32) for x in rows]
    bias = jnp.dot(pe_ref[...], w1_ref[...], preferred_element_type=jnp.float32)[0:1, :] + b1_ref[...]
    nxt = bot[1:] + [pltpu.roll(bot[0], n_blk - 1, 0)]
    hid = jnp.concatenate([top[r] + nxt[r] for r in range(CMP_PER_SLC)], axis=0) + bias
    return jnp.dot(_gelu_tanh(hid).astype(MXU_DTYPE), w2_ref[...], preferred_element_type=jnp.float32)


def _compress_kernel(rk_ref, rv_ref, pek_ref, w1k_ref, b1k_ref, w2k_ref,
                     pev_ref, w1v_ref, b1v_ref, w2v_ref, ok_ref, ovt_ref):
    kc = _compress_one(rk_ref, pek_ref, w1k_ref, b1k_ref, w2k_ref)
    vc = _compress_one(rv_ref, pev_ref, w1v_ref, b1v_ref, w2v_ref)
    ok_ref[0, 0] = jnp.concatenate([kc * (SCORE_SCALE * LOG2E), jnp.zeros_like(kc)], axis=1).astype(ok_ref.dtype)
    ovt_ref[0, 0] = jnp.concatenate([kc, vc], axis=1).T[HEAD_DIM:, :].astype(ovt_ref.dtype)


def _compress(kvc_rows, pek, w1k, b1k, w2k, pev, w1v, b1v, w2v):
    _, b, n_blk, width = kvc_rows.shape
    dk = w2k.shape[1]
    n_rows = n_blk * CMP_PER_SLC
    w_specs = [_const_spec(a.shape) for a in (pek, w1k, b1k, w2k)]
    return pl.pallas_call(
        _compress_kernel,
        grid=(b, N_KV_GROUPS),
        in_specs=[pl.BlockSpec((1, 1, n_blk, width), lambda i, j: (j, i, 0, 0)),
                  pl.BlockSpec((1, 1, n_blk, width), lambda i, j: (N_KV_GROUPS + j, i, 0, 0))]
        + w_specs + w_specs,
        out_specs=[pl.BlockSpec((1, 1, n_rows, 2 * dk), lambda i, j: (i, j, 0, 0)),
                   pl.BlockSpec((1, 1, dk, n_rows), lambda i, j: (i, j, 0, 0))],
        out_shape=[jax.ShapeDtypeStruct((b, N_KV_GROUPS, n_rows, 2 * dk), MXU_DTYPE),
                   jax.ShapeDtypeStruct((b, N_KV_GROUPS, dk, n_rows), MXU_DTYPE)],
        compiler_params=_cparams(2),
        name="compress",
    )(kvc_rows, kvc_rows, pek, w1k, b1k, w2k, pev, w1v, b1v, w2v)


def _kprep_kernel(ks_ref, vs_ref, kw_ref, vw_ref, cos_ref, sin_ref, rot_ref,
                  kaug_ref, vst_ref, kwin_ref, vwt_ref, knorm_ref):
    ts = ks_ref.shape[1]
    t0 = pl.program_id(1) * ts
    cos = jnp.concatenate([cos_ref[...]] * N_KV_GROUPS, axis=1)
    sin = jnp.concatenate([sin_ref[...]] * N_KV_GROUPS, axis=1)
    rot = rot_ref[...]

    def rope(k_ref):
        k = k_ref[0]
        krot = jnp.dot(k.astype(MXU_DTYPE), rot, preferred_element_type=jnp.float32)
        return (k.astype(jnp.float32) * cos + krot * sin) * (SCORE_SCALE * LOG2E)

    n_blk = kaug_ref.shape[3] - 2 * HEAD_DIM
    blk = lax.broadcasted_iota(jnp.int32, (ts, n_blk), 1)
    key_blk = (t0 + lax.broadcasted_iota(jnp.int32, (ts, n_blk), 0)) >> _log2(SLC_BLOCK)
    onehot = jnp.where(blk == key_blk, 1.0, 0.0).astype(kaug_ref.dtype)
    ks_f = rope(ks_ref)
    kw_f = rope(kw_ref)
    ks = ks_f.astype(kaug_ref.dtype)
    kw = kw_f.astype(kwin_ref.dtype)

    seg_l = lax.broadcasted_iota(jnp.int32, (KV_WIDTH, LANE), 0) >> _log2(HEAD_DIM)
    seg_c = lax.broadcasted_iota(jnp.int32, (KV_WIDTH, LANE), 1)
    sq = (jnp.dot(ks_f * ks_f, jnp.where(seg_c == seg_l, 1.0, 0.0), preferred_element_type=jnp.float32)
          + jnp.dot(kw_f * kw_f, jnp.where(seg_c == seg_l + N_KV_GROUPS, 1.0, 0.0),
                    preferred_element_type=jnp.float32))
    tile_max = jnp.broadcast_to(jnp.max(sq, axis=0, keepdims=True), knorm_ref.shape[1:])

    @pl.when(pl.program_id(1) == 0)
    def _():
        knorm_ref[0] = tile_max

    @pl.when(pl.program_id(1) > 0)
    def _():
        knorm_ref[0] = jnp.maximum(knorm_ref[0], tile_max)

    vst = vs_ref[0].astype(jnp.float32).T.astype(vst_ref.dtype)
    vwt = vw_ref[0].astype(jnp.float32).T.astype(vwt_ref.dtype)
    ones_col = jnp.where(lax.broadcasted_iota(jnp.int32, (ts, HEAD_DIM), 1) == 0, 1.0, 0.0
                         ).astype(kaug_ref.dtype)
    for g in range(N_KV_GROUPS):
        sl = slice(g * HEAD_DIM, (g + 1) * HEAD_DIM)
        kaug_ref[0, g, :, 0:n_blk] = onehot
        kaug_ref[0, g, :, n_blk:n_blk + HEAD_DIM] = ks[:, sl]
        kaug_ref[0, g, :, n_blk + HEAD_DIM:] = ones_col
        kwin_ref[0, g, :, 0:HEAD_DIM] = kw[:, sl]
        kwin_ref[0, g, :, HEAD_DIM:] = ones_col
        vst_ref[0, g] = vst[sl, :]
        vwt_ref[0, g] = vwt[sl, :]


def _kprep(proj3, cos_k, sin_k, rot):
    b, s, _ = proj3.shape
    n_blk = s // SLC_BLOCK
    ts = min(KPREP_TILE, s)
    kvb = COL_KV // KV_WIDTH

    def col(j):
        return pl.BlockSpec((1, ts, KV_WIDTH), lambda i, t: (i, t, kvb + j))

    tab = pl.BlockSpec((ts, HEAD_DIM), lambda i, t: (t, 0))
    aug_w = n_blk + 2 * HEAD_DIM
    return pl.pallas_call(
        _kprep_kernel,
        grid=(b, s // ts),
        in_specs=[col(2), col(3), col(4), col(5), tab, tab, _const_spec(rot.shape)],
        out_specs=[pl.BlockSpec((1, N_KV_GROUPS, ts, aug_w), lambda i, t: (i, 0, t, 0)),
                   pl.BlockSpec((1, N_KV_GROUPS, HEAD_DIM, ts), lambda i, t: (i, 0, 0, t)),
                   pl.BlockSpec((1, N_KV_GROUPS, ts, 2 * HEAD_DIM), lambda i, t: (i, 0, t, 0)),
                   pl.BlockSpec((1, N_KV_GROUPS, HEAD_DIM, ts), lambda i, t: (i, 0, 0, t)),
                   pl.BlockSpec((1, SUBLANE, LANE), lambda i, t: (i, 0, 0))],
        out_shape=[jax.ShapeDtypeStruct((b, N_KV_GROUPS, s, aug_w), MXU_DTYPE),
                   jax.ShapeDtypeStruct((b, N_KV_GROUPS, HEAD_DIM, s), MXU_DTYPE),
                   jax.ShapeDtypeStruct((b, N_KV_GROUPS, s, 2 * HEAD_DIM), MXU_DTYPE),
                   jax.ShapeDtypeStruct((b, N_KV_GROUPS, HEAD_DIM, s), MXU_DTYPE),
                   jax.ShapeDtypeStruct((b, SUBLANE, LANE), jnp.float32)],
        compiler_params=_cparams(2),
        name="kprep",
    )(proj3, proj3, proj3, proj3, cos_k, sin_k, rot)


def _heads_to_lanes(xt):
    return jnp.concatenate([xt[h * HEAD_DIM:(h + 1) * HEAD_DIM, :] for h in range(HEADS_PER_GROUP)],
                           axis=1)


def _vis_rows(tq):
    levels = tq // CMP_STRIDE
    dt = np.arange(tq)
    v = (dt - (CMP_BLOCK - 1)) // CMP_STRIDE - (-(CMP_BLOCK - 1)) // CMP_STRIDE
    rows = np.zeros((HEAD_DIM, tq), np.float32)
    for c in range(1, levels + 1):
        rows[c] = np.where(v < c, -MASK_BIAS, 0.0)
    rows[levels + 1] = -MASK_BIAS
    return jnp.asarray(np.tile(rows, (1, HEADS_PER_GROUP)), MXU_DTYPE)


def _cmpsel_kernel(q_ref, kc_ref, vct_ref, vis_ref, ocmp_ref, bias_ref):
    tq = q_ref.shape[1]
    n_blk = bias_ref.shape[2]
    m_rows = HEADS_PER_GROUP * tq
    t0 = pl.program_id(1) * tq
    levels = tq // CMP_STRIDE

    def body(nr):
        n_tok = CMP_PER_SLC * nr
        j_idx = lax.broadcasted_iota(jnp.int32, (nr, LANE), 0)
        n_idx = jnp.concatenate([j_idx * CMP_PER_SLC + r for r in range(CMP_PER_SLC)], axis=0)
        n_first = (t0 - (CMP_BLOCK - 1)) >> _log2(CMP_STRIDE)
        u = jnp.clip(n_idx - n_first, 0, levels + 1)
        vis_cols = jnp.where(lax.broadcasted_iota(jnp.int32, (n_tok, LANE), 1) - HEAD_DIM == u,
                             1.0, 0.0).astype(MXU_DTYPE)
        vis_rows = vis_ref[...]
        t_row = t0 + (lax.broadcasted_iota(jnp.int32, (1, m_rows), 1) & (tq - 1))
        jrow = lax.broadcasted_iota(jnp.int32, (nr, tq), 0)
        jt = (t0 + lax.broadcasted_iota(jnp.int32, (nr, tq), 1)) >> _log2(SLC_BLOCK)
        causal_blk = jrow <= jt
        forced = (jrow == 0) | (jrow == jt) | (jrow == jt - 1)
        jf = jrow.astype(jnp.float32)
        n_extract = min(SLC_TOPK, n_blk) - N_FORCED
        jt_row = (t0 + lax.broadcasted_iota(jnp.int32, (1, tq), 1)) >> _log2(SLC_BLOCK)
        want = jnp.minimum(jt_row + 1, SLC_TOPK).astype(jnp.float32)

        def store_bias(pb, marks):
            bias_ref[pb, 0, 0:nr, :] = jnp.where(marks < CHOSEN_LIMIT, 0.0, -MASK_BIAS).astype(bias_ref.dtype)

        pending = []
        for pb in range(q_ref.shape[0]):
            kc = jnp.concatenate([kc_ref[pb, 0, r * n_blk:r * n_blk + nr, :] for r in range(CMP_PER_SLC)],
                                 axis=0)
            vct = jnp.concatenate([vct_ref[pb, 0, :, r * n_blk:r * n_blk + nr] for r in range(CMP_PER_SLC)],
                                  axis=1)
            qt = _heads_to_lanes(q_ref[pb].astype(jnp.float32).T).astype(MXU_DTYPE)
            s = jnp.dot(kc + vis_cols, jnp.concatenate([qt, vis_rows], axis=0),
                        preferred_element_type=jnp.float32)
            mx = jnp.max(s, axis=0, keepdims=True)
            p = jnp.exp2(s - mx)
            den = jnp.sum(p, axis=0, keepdims=True)
            inv = jnp.where(t_row >= CMP_BLOCK - 1, 1.0 / den, 0.0)
            ocmp_ref[pb, 0, 0] = inv * jnp.dot(vct, p.astype(MXU_DTYPE), preferred_element_type=jnp.float32)

            imp = None
            for h in range(HEADS_PER_GROUP):
                ph = p[:, h * tq:(h + 1) * tq]
                last = ph[3 * nr:4 * nr, :]
                prev = jnp.where(jrow == 0, 0.0, pltpu.roll(last, 1, 0))
                taps = ph[0:nr, :] + ph[nr:2 * nr, :] + ph[2 * nr:3 * nr, :] + last + prev
                imp_h = taps * inv[:, h * tq:(h + 1) * tq]
                imp = imp_h if imp is None else imp + imp_h

            val = jnp.where(forced, CHOSEN, jnp.where(causal_blk, imp, -IMP_BIG))
            marks = val
            for _ in range(n_extract):
                marks = jnp.where(marks == jnp.max(marks, axis=0, keepdims=True), CHOSEN, marks)
            store_bias(pb, marks)
            if nr < n_blk:
                bias_ref[pb, 0, nr:, :] = jnp.full((n_blk - nr, tq), -MASK_BIAS, bias_ref.dtype)
            count = jnp.sum(jnp.where((marks < CHOSEN_LIMIT) & causal_blk, 1.0, 0.0), axis=0, keepdims=True)
            pending.append((pb, val, jnp.max(jnp.abs(count - want)) > 0.5))

        for pb, val, tied in pending:
            @pl.when(tied)
            def _(val=val, pb=pb):
                v = val
                for _ in range(n_extract):
                    top = jnp.max(v, axis=0, keepdims=True)
                    first = jnp.min(jnp.where(v == top, jf, float(n_blk)), axis=0, keepdims=True)
                    v = jnp.where(jf == first, CHOSEN, v)
                store_bias(pb, v)

    sizes = [n_blk * f // 4 for f in (1, 2, 3, 4) if (n_blk * f // 4) % 16 == 0]
    lo = 0
    for nr in sizes:
        hi = nr * SLC_BLOCK if nr < n_blk else None
        end = t0 + tq
        cond = (end > lo) if hi is None else (end > lo) & (end <= hi)
        pl.when(cond)(functools.partial(body, nr))
        lo = hi


def _cmpsel(proj3, kc_r, vc_rt):
    b, s, _ = proj3.shape
    tq = min(Q_TILE, s)
    nq = s // tq
    n_cmp_pad = kc_r.shape[2]
    n_blk = s // SLC_BLOCK
    qb = COL_Q // KV_WIDTH
    m_rows = HEADS_PER_GROUP * tq
    return pl.pallas_call(
        _cmpsel_kernel,
        grid=(N_KV_GROUPS, nq),
        in_specs=[pl.BlockSpec((b, tq, KV_WIDTH), lambda g, q: (0, q, qb + g)),
                  pl.BlockSpec((b, 1, n_cmp_pad, LANE), lambda g, q: (0, g, 0, 0)),
                  pl.BlockSpec((b, 1, HEAD_DIM, n_cmp_pad), lambda g, q: (0, g, 0, 0)),
                  _const_spec((HEAD_DIM, m_rows))],
        out_specs=[pl.BlockSpec((b, 1, 1, HEAD_DIM, m_rows), lambda g, q: (0, g, q, 0, 0)),
                   pl.BlockSpec((b, 1, n_blk, tq), lambda g, q: (0, g, 0, q))],
        out_shape=[jax.ShapeDtypeStruct((b, N_KV_GROUPS, nq, HEAD_DIM, m_rows), jnp.float32),
                   jax.ShapeDtypeStruct((b, N_KV_GROUPS, n_blk, s), MXU_DTYPE)],
        compiler_params=_cparams(2),
        name="cmpsel",
    )(proj3, kc_r, vc_rt, _vis_rows(tq))


def _flash_kernel(q_ref, cos_ref, sin_ref, bias_ref, kaug_ref, vst_ref, kwin_ref, vwt_ref, knorm_ref,
                  wmask_ref, gate_ref, ocmp_ref, o_ref, m_ref, l_ref, acc_ref, lw_ref, accw_ref):
    nb = q_ref.shape[0]
    tq = q_ref.shape[1]
    m_rows = HEADS_PER_GROUP * tq
    grp = pl.program_id(0)
    qi = pl.program_id(1)
    t0 = qi * tq
    kd = pl.multiple_of(t0, tq)
    probs = range(nb)
    zpad = jnp.zeros((HEAD_DIM, m_rows), MXU_DTYPE)
    pad_row = lax.broadcasted_iota(jnp.int32, (HEAD_DIM, m_rows), 0)
    lane = lax.broadcasted_iota(jnp.int32, (1, LANE), 1)

    def stab_rows(neg_m):
        return jnp.where(pad_row == 0, neg_m, 0.0).astype(MXU_DTYPE)

    cos = cos_ref[...]
    sin = sin_ref[...]
    qr, bias4, qs_slc, qs_win = [], [], [], []
    for pb in probs:
        qt = q_ref[pb].astype(jnp.float32).T
        parts = []
        for h in range(HEADS_PER_GROUP):
            qh = qt[h * HEAD_DIM:(h + 1) * HEAD_DIM, :]
            x1 = qh[0:ROPE_HALF, :]
            x2 = qh[ROPE_HALF:ROPE_DIM, :]
            parts.append(jnp.concatenate([x1 * cos - x2 * sin, x2 * cos + x1 * sin, qh[ROPE_DIM:, :]],
                                         axis=0))
        qf = jnp.concatenate(parts, axis=1)
        q_sq = jnp.sum(qf * qf, axis=0, keepdims=True)
        k_sq = knorm_ref[pb, 0:1, :]
        ks_sq = jnp.max(jnp.where(lane == grp, k_sq, 0.0), axis=1, keepdims=True)
        kw_sq = jnp.max(jnp.where(lane == grp + N_KV_GROUPS, k_sq, 0.0), axis=1, keepdims=True)
        qr.append(qf.astype(MXU_DTYPE))
        bias4.append(jnp.concatenate([bias_ref[pb, 0]] * HEADS_PER_GROUP, axis=1))
        qs_slc.append(jnp.concatenate([bias4[pb], qr[pb], stab_rows(-STAB_MARGIN * jnp.sqrt(q_sq * ks_sq))],
                                      axis=0))
        qs_win.append(jnp.concatenate([qr[pb], stab_rows(-STAB_MARGIN * jnp.sqrt(q_sq * kw_sq))], axis=0))

    n_back = WINDOW // tq

    def win_tile(back):
        if back == 0:
            return kd, tq, wmask_ref[0]
        inside = wmask_ref[back] if back * tq + tq - 1 >= WINDOW else 0.0
        return (pl.multiple_of(jnp.maximum(t0 - back * tq, 0), tq), tq,
                jnp.where(qi >= back, inside, NEG_BIG))

    def fast_tiles(k_ref, v_ref, qs, tiles, l_out, acc_out, first=False):
        for pb in probs:
            l_add = acc_add = None
            for k0, tk, mask in tiles:
                s = jnp.dot(k_ref[pb, 0, pl.ds(k0, tk), :], qs[pb], preferred_element_type=jnp.float32)
                if mask is not None:
                    s = s + mask
                p = jnp.exp2(s)
                l_t = jnp.sum(p, axis=0, keepdims=True)
                acc_t = jnp.dot(v_ref[pb, 0, :, pl.ds(k0, tk)], p.astype(MXU_DTYPE),
                                preferred_element_type=jnp.float32)
                l_add = l_t if l_add is None else l_add + l_t
                acc_add = acc_t if acc_add is None else acc_add + acc_t
            if first:
                l_out[pb] = l_add
                acc_out[pb] = acc_add
            else:
                l_out[pb] += l_add
                acc_out[pb] += acc_add

    fast_tiles(kaug_ref, vst_ref, qs_slc, [(kd, tq, wmask_ref[0])], l_ref, acc_ref, first=True)
    fast_tiles(kwin_ref, vwt_ref, qs_win, [win_tile(back) for back in range(n_back + 1)],
               lw_ref, accw_ref, first=True)

    sub = SLC_SUBTILE
    n_sub = SLC_SPAN // sub

    def span_body(i, carry):
        base = i * SLC_SPAN
        fast_tiles(kaug_ref, vst_ref, qs_slc,
                   [(pl.multiple_of(base + j * sub, sub), sub, None) for j in range(n_sub)],
                   l_ref, acc_ref)
        return carry

    n_span = t0 // SLC_SPAN
    lax.fori_loop(0, n_span, span_body, 0)
    done = n_span * SLC_SPAN
    size = SLC_SPAN // 2
    while size >= tq:
        take = ((t0 // size) % 2) == 1

        @pl.when(take)
        def _(done=done, size=size):
            piece = min(size, sub)
            fast_tiles(kaug_ref, vst_ref, qs_slc,
                       [(pl.multiple_of(done + j * piece, piece), piece, None) for j in range(size // piece)],
                       l_ref, acc_ref)

        done = done + jnp.where(take, size, 0)
        size //= 2

    def exact_start(pb, s, vt, l_out, acc_out):
        mx = jnp.max(s, axis=0, keepdims=True)
        p = jnp.exp2(s - mx)
        m_ref[pb] = mx
        l_out[pb] = jnp.sum(p, axis=0, keepdims=True)
        acc_out[pb] = jnp.dot(vt, p.astype(MXU_DTYPE), preferred_element_type=jnp.float32)

    def exact_update(pb, s, vt, l_out, acc_out):
        m_old = m_ref[pb]
        mx = jnp.maximum(m_old, jnp.max(s, axis=0, keepdims=True))
        alpha = jnp.exp2(m_old - mx)
        p = jnp.exp2(s - mx)
        m_ref[pb] = mx
        l_out[pb] = alpha * l_out[pb] + jnp.sum(p, axis=0, keepdims=True)
        acc_out[pb] = alpha * acc_out[pb] + jnp.dot(vt, p.astype(MXU_DTYPE),
                                                    preferred_element_type=jnp.float32)

    check = l_low = None
    for pb in probs:
        c = (l_ref[pb] + lw_ref[pb] + jnp.sum(jnp.abs(acc_ref[pb]), axis=0, keepdims=True)
             + jnp.sum(jnp.abs(accw_ref[pb]), axis=0, keepdims=True))
        lo = jnp.minimum(l_ref[pb], lw_ref[pb])
        check = c if check is None else check + c
        l_low = lo if l_low is None else jnp.minimum(l_low, lo)

    @pl.when(_lost_range(check, l_low))
    def _():
        tcol = t0 + (lax.broadcasted_iota(jnp.int32, (tq, m_rows), 1) & (tq - 1))
        koff = lax.broadcasted_iota(jnp.int32, (tq, m_rows), 0)
        causal = t0 + koff <= tcol
        for pb in probs:
            q0 = jnp.concatenate([bias4[pb], qr[pb], zpad], axis=0)
            s = jnp.dot(kaug_ref[pb, 0, pl.ds(kd, tq), :], q0, preferred_element_type=jnp.float32)
            exact_start(pb, jnp.where(causal, s, NEG_BIG), vst_ref[pb, 0, :, pl.ds(kd, tq)], l_ref, acc_ref)

            def body(kt, carry, pb=pb, q0=q0):
                k0 = pl.multiple_of(kt * tq, tq)
                sk = jnp.dot(kaug_ref[pb, 0, pl.ds(k0, tq), :], q0, preferred_element_type=jnp.float32)
                exact_update(pb, sk, vst_ref[pb, 0, :, pl.ds(k0, tq)], l_ref, acc_ref)
                return carry

            lax.fori_loop(0, qi, body, 0)
            qw0 = jnp.concatenate([qr[pb], zpad], axis=0)
            for back in range(n_back + 1):
                kbase = t0 - back * tq
                k0 = pl.multiple_of(jnp.maximum(kbase, 0), tq)
                sw = jnp.dot(kwin_ref[pb, 0, pl.ds(k0, tq), :], qw0, preferred_element_type=jnp.float32)
                diff = tcol - (kbase + koff)
                sw = jnp.where((diff >= 0) & (diff < WINDOW) & (kbase + koff >= 0), sw, NEG_BIG)
                if back == 0:
                    exact_start(pb, sw, vwt_ref[pb, 0, :, pl.ds(k0, tq)], lw_ref, accw_ref)
                else:
                    exact_update(pb, sw, vwt_ref[pb, 0, :, pl.ds(k0, tq)], lw_ref, accw_ref)

    r_id = lax.broadcasted_iota(jnp.int32, (GATE_ROWS, LANE), 0)
    c_id = lax.broadcasted_iota(jnp.int32, (GATE_ROWS, LANE), 1)
    pick = jnp.where((c_id == grp * (HEADS_PER_GROUP * 3) + r_id) & (r_id < HEADS_PER_GROUP * 3),
                     1.0, 0.0).astype(gate_ref.dtype)
    for pb in probs:
        o_slc = acc_ref[pb] / l_ref[pb]
        o_win = accw_ref[pb] / lw_ref[pb]
        graw = lax.dot_general(pick, gate_ref[pb], (((1,), (1,)), ((), ())),
                               preferred_element_type=jnp.float32)
        gates = _sigmoid(graw)
        o_cmp = ocmp_ref[pb, 0, 0]
        mixed = []
        for h in range(HEADS_PER_GROUP):
            sl = slice(h * tq, (h + 1) * tq)
            mixed.append(gates[3 * h:3 * h + 1, :] * o_cmp[:, sl]
                         + gates[3 * h + 1:3 * h + 2, :] * o_slc[:, sl]
                         + gates[3 * h + 2:3 * h + 3, :] * o_win[:, sl])
        o_ref[pb] = jnp.concatenate(mixed, axis=0).T.astype(o_ref.dtype)


def _window_masks(tq):
    k = np.arange(tq)[:, None]
    t = np.arange(tq)[None, :]
    tiles = []
    for back in range(WINDOW // tq + 1):
        diff = t + back * tq - k
        tiles.append(np.tile(np.where((diff >= 0) & (diff < WINDOW), 0.0, NEG_BIG), (1, HEADS_PER_GROUP)))
    return jnp.asarray(np.stack(tiles), jnp.float32)


def _flash(proj3, cos_q, sin_q, bias, kaug, vst, kwin, vwt, knorm, ocmp):
    b, s, _ = proj3.shape
    tq = min(Q_TILE, s)
    nq = s // tq
    n_blk = s // SLC_BLOCK
    aug_w = kaug.shape[3]
    qb = COL_Q // KV_WIDTH
    m_rows = HEADS_PER_GROUP * tq

    wmask = _window_masks(tq)

    def resident(shape):
        return pl.BlockSpec((b, 1) + shape, lambda g, q: (0, g, 0, 0))

    return pl.pallas_call(
        _flash_kernel,
        grid=(N_KV_GROUPS, nq),
        in_specs=[pl.BlockSpec((b, tq, KV_WIDTH), lambda g, q: (0, q, qb + g)),
                  pl.BlockSpec((ROPE_HALF, tq), lambda g, q: (0, q)),
                  pl.BlockSpec((ROPE_HALF, tq), lambda g, q: (0, q)),
                  pl.BlockSpec((b, 1, n_blk, tq), lambda g, q: (0, g, 0, q)),
                  resident((s, aug_w)),
                  resident((HEAD_DIM, s)),
                  resident((s, 2 * HEAD_DIM)),
                  resident((HEAD_DIM, s)),
                  _const_spec((b, SUBLANE, LANE)),
                  _const_spec(wmask.shape),
                  pl.BlockSpec((b, tq, LANE), lambda g, q: (0, q, COL_GNSA // LANE)),
                  pl.BlockSpec((b, 1, 1, HEAD_DIM, m_rows), lambda g, q: (0, g, q, 0, 0))],
        out_specs=pl.BlockSpec((b, tq, KV_WIDTH), lambda g, q: (0, q, g)),
        out_shape=jax.ShapeDtypeStruct((b, s, ATTN_Q_WIDTH), ACT_DTYPE),
        scratch_shapes=[pltpu.VMEM((b, 1, m_rows), jnp.float32),
                        pltpu.VMEM((b, 1, m_rows), jnp.float32),
                        pltpu.VMEM((b, HEAD_DIM, m_rows), jnp.float32),
                        pltpu.VMEM((b, 1, m_rows), jnp.float32),
                        pltpu.VMEM((b, HEAD_DIM, m_rows), jnp.float32)],
        compiler_params=_cparams(2),
        name="flash",
    )(proj3, cos_q, sin_q, bias, kaug, vst, kwin, vwt, knorm, wmask, proj3, ocmp)


def _tail_kernel(u_ref, halo_ref, gp_ref, ga_ref, attn_ref, x_ref, pw_ref, ps_ref, wpp_ref, wpa_ref, wo_ref,
                 n2_ref, wg_ref, wu_ref, wd_ref, nf_ref, o_ref, *, seq_len):
    tm = u_ref.shape[0]
    t0 = (pl.program_id(0) * tm) & (seq_len - 1)

    u = u_ref[...].astype(jnp.float32)
    halo = jnp.where(t0 > 0, halo_ref[...].astype(jnp.float32), 0.0)
    ext = jnp.concatenate([halo, u], axis=0)
    t = t0 + lax.broadcasted_iota(jnp.int32, (tm, POOL_GROUP), 0)
    pooled = []
    for gi, w in enumerate(POOL_WINDOWS):
        sl = slice(gi * POOL_GROUP, (gi + 1) * POOL_GROUP)
        acc = ext[:, sl]
        span = 1
        while span < w:
            acc = acc + pltpu.roll(acc, span, 0)
            span *= 2
        cnt = jnp.minimum(t + 1, w).astype(jnp.float32)
        mean = acc[POOL_HALO:, :] / cnt
        mixed = jnp.dot((mean - u[:, sl]).astype(MXU_DTYPE), pw_ref[gi],
                        preferred_element_type=jnp.float32)
        pooled.append(mixed * ps_ref[:, sl])
    pool = jnp.concatenate(pooled, axis=1).astype(MXU_DTYPE)

    pp = jnp.dot(pool, wpp_ref[...], preferred_element_type=jnp.float32)
    pa = jnp.dot(attn_ref[...], wpa_ref[...], preferred_element_type=jnp.float32)
    merged = (_sigmoid(gp_ref[...].astype(jnp.float32)) * pp
              + _sigmoid(ga_ref[...].astype(jnp.float32)) * pa)
    x1 = x_ref[...] + jnp.dot(merged.astype(MXU_DTYPE), wo_ref[...], preferred_element_type=jnp.float32)

    h = _rms(x1, n2_ref[...]).astype(MXU_DTYPE)
    d_ff = wg_ref.shape[1]
    out = x1
    for c0 in range(0, d_ff, FF_CHUNK):
        cw = min(FF_CHUNK, d_ff - c0)
        gate = jnp.dot(h, wg_ref[:, c0:c0 + cw], preferred_element_type=jnp.float32)
        up = jnp.dot(h, wu_ref[:, c0:c0 + cw], preferred_element_type=jnp.float32)
        act = (gate * _sigmoid(gate) * up).astype(MXU_DTYPE)
        out = out + jnp.dot(act, wd_ref[c0:c0 + cw, :], preferred_element_type=jnp.float32)
    o_ref[...] = _rms(out, nf_ref[...])


def _tail(proj2, attn2, x2d, pool_w, pool_scale, wpp, wpa, wo, norm2_w, wg, wu, wd, norm_f_w, seq_len):
    m, d = x2d.shape
    tm = min(ROW_TILE, seq_len)
    halo_per_tile = tm // POOL_HALO
    weights = (pool_w, pool_scale, wpp, wpa, wo, norm2_w, wg, wu, wd, norm_f_w)
    return pl.pallas_call(
        functools.partial(_tail_kernel, seq_len=seq_len),
        grid=(m // tm,),
        in_specs=[pl.BlockSpec((tm, POOL_WIDTH), lambda i: (i, COL_POOL // POOL_WIDTH)),
                  pl.BlockSpec((POOL_HALO, POOL_WIDTH),
                               lambda i: (jnp.maximum(i * halo_per_tile - 1, 0), COL_POOL // POOL_WIDTH)),
                  pl.BlockSpec((tm, d), lambda i: (i, COL_GPOOL // D_MODEL)),
                  pl.BlockSpec((tm, d), lambda i: (i, COL_GATTN // D_MODEL)),
                  pl.BlockSpec((tm, ATTN_Q_WIDTH), lambda i: (i, 0)),
                  pl.BlockSpec((tm, d), lambda i: (i, 0))] + [_const_spec(w.shape) for w in weights],
        out_specs=pl.BlockSpec((tm, d), lambda i: (i, 0)),
        out_shape=jax.ShapeDtypeStruct((m, d), jnp.float32),
        compiler_params=_cparams(1),
        name="tail",
    )(proj2, proj2, proj2, proj2, attn2, x2d, *weights)


def _rope_tables(seq_len):
    inv_freq = 1.0 / (ROPE_THETA ** (jnp.arange(0, ROPE_DIM, 2, dtype=jnp.float32) / ROPE_DIM))
    ang = inv_freq[:, None] * jnp.arange(seq_len).astype(jnp.float32)[None, :]
    cos_t, sin_t = jnp.cos(ang), jnp.sin(ang)
    rest = HEAD_DIM - ROPE_DIM
    cos_h = jnp.concatenate([cos_t, cos_t, jnp.ones((rest, seq_len), jnp.float32)], axis=0).T
    sin_h = jnp.concatenate([sin_t, sin_t, jnp.zeros((rest, seq_len), jnp.float32)], axis=0).T
    return cos_t, sin_t, cos_h, sin_h


def _rotate_half_matrix():
    rot = np.zeros((KV_WIDTH, KV_WIDTH), np.float32)
    for l in range(KV_WIDTH):
        d = l % HEAD_DIM
        if d < ROPE_HALF:
            rot[l + ROPE_HALF, l] = -1.0
        elif d < ROPE_DIM:
            rot[l - ROPE_HALF, l] = 1.0
    return jnp.asarray(rot, MXU_DTYPE)


def _layer(x, norm1_w, w_in, pool_w, pool_scale, cmp_pe_k, cmp_w1_k, cmp_b1_k, cmp_w2_k,
           cmp_pe_v, cmp_w1_v, cmp_b1_v, cmp_w2_v, w_proj_pool, w_proj_attn, w_out,
           norm2_w, w_ffn_gate, w_ffn_up, w_ffn_down, norm_f_w):
    b, s, d = x.shape
    m = b * s
    n_blk = s // SLC_BLOCK
    assert s % Q_TILE == 0 and s % ROW_TILE == 0 and (s & (s - 1)) == 0, s
    cd = MXU_DTYPE

    g0 = COL_GPOOL
    w_parts = (w_in[:, :g0].astype(cd), w_in[:, g0 + GATE_NSA:].astype(cd),
               jnp.pad(w_in[:, g0:g0 + GATE_NSA], ((0, 0), (0, LANE - GATE_NSA))).astype(cd))
    x2d = x.reshape(m, d)
    proj2, kvc = _inproj(x2d, norm1_w.reshape(1, d), w_parts)
    proj3 = proj2.reshape(b, s, PROJ_WIDTH)

    def pe8(pe):
        return jnp.broadcast_to(pe.reshape(1, -1), (8, pe.size)).astype(cd)

    kvc_rows = kvc.reshape(2 * N_KV_GROUPS, b, n_blk, SLC_BLOCK * HEAD_DIM)
    kc_r, vc_rt = _compress(kvc_rows,
                            pe8(cmp_pe_k), cmp_w1_k.astype(cd), cmp_b1_k.reshape(1, -1), cmp_w2_k.astype(cd),
                            pe8(cmp_pe_v), cmp_w1_v.astype(cd), cmp_b1_v.reshape(1, -1), cmp_w2_v.astype(cd))

    cos_q, sin_q, cos_k, sin_k = _rope_tables(s)
    kaug, vst, kwin, vwt, knorm = _kprep(proj3, cos_k, sin_k, _rotate_half_matrix())
    ocmp, bias = _cmpsel(proj3, kc_r, vc_rt)
    attn = _flash(proj3, cos_q, sin_q, bias, kaug, vst, kwin, vwt, knorm, ocmp)
    return _tail(proj2, attn.reshape(m, ATTN_Q_WIDTH), x2d, pool_w.astype(cd),
                 pool_scale.reshape(1, -1), w_proj_pool.astype(cd), w_proj_attn.astype(cd),
                 w_out.astype(cd), norm2_w.reshape(1, d), w_ffn_gate.astype(cd), w_ffn_up.astype(cd),
                 w_ffn_down.astype(cd), norm_f_w.reshape(1, d), s)


def kernel(x, norm1_w, w_in, pool_w, pool_scale, cmp_pe_k, cmp_w1_k, cmp_b1_k, cmp_w2_k, cmp_pe_v,
           cmp_w1_v, cmp_b1_v, cmp_w2_v, w_proj_pool, w_proj_attn, w_out, norm2_w, w_ffn_gate,
           w_ffn_up, w_ffn_down, norm_f_w):
    b, s, d = x.shape
    depth = w_in.shape[0]
    assert depth == 1, "the final norm is fused into the last layer's FFN kernel"
    out = _layer(x, norm1_w[0], w_in[0], pool_w[0], pool_scale[0], cmp_pe_k[0], cmp_w1_k[0], cmp_b1_k[0],
                 cmp_w2_k[0], cmp_pe_v[0], cmp_w1_v[0], cmp_b1_v[0], cmp_w2_v[0], w_proj_pool[0],
                 w_proj_attn[0], w_out[0], norm2_w[0], w_ffn_gate[0], w_ffn_up[0], w_ffn_down[0], norm_f_w)
    return out.reshape(b, s, d)
```

```python
import functools

import numpy as np
import jax
import jax.numpy as jnp
from jax import lax
from jax.experimental import pallas as pl
from jax.experimental.pallas import tpu as pltpu

D_MODEL = 1024
N_HEADS = 16
HEAD_DIM = 64
N_KV_GROUPS = 4
HEADS_PER_GROUP = N_HEADS // N_KV_GROUPS
ROPE_DIM = HEAD_DIM // 4
ROPE_HALF = ROPE_DIM // 2
ROPE_THETA = 500000.0
CMP_BLOCK = 32
CMP_STRIDE = 16
CMP_PER_SLC = 4
SLC_BLOCK = 64
SLC_TOPK = 16
WINDOW = 512
ATTN_Q_WIDTH = N_HEADS * HEAD_DIM
KV_WIDTH = N_KV_GROUPS * HEAD_DIM
POOL_WIDTH = D_MODEL // 2
POOL_WINDOWS = (2, 4, 8, 16)
POOL_GROUP = POOL_WIDTH // len(POOL_WINDOWS)
POOL_HALO = 16
RMS_EPS = 1e-6

LANE = 128
SUBLANE = 8
MXU_DTYPE = jnp.bfloat16
ACT_DTYPE = jnp.bfloat16
VMEM_LIMIT_BYTES = 56 * 1024 * 1024

GATE_NSA = N_HEADS * 3
COL_POOL = 0
COL_Q = COL_POOL + POOL_WIDTH
COL_KV = COL_Q + ATTN_Q_WIDTH
COL_GPOOL = COL_KV + 6 * KV_WIDTH
COL_GATTN = COL_GPOOL + D_MODEL
COL_GNSA = COL_GATTN + D_MODEL
PROJ_WIDTH = COL_GNSA + LANE
GATE_ROWS = 16

ROW_TILE = 512
Q_TILE = 256
SLC_SPAN = 2048
SLC_SUBTILE = 2048
KPREP_TILE = 1024
FF_CHUNK = 512
SCORE_SCALE = HEAD_DIM ** -0.5
LOG2E = float(np.log2(np.e))
FINITE_LIMIT = 3e38
STAB_MARGIN = 1.03
L_MIN = 2.0 ** -64
MASK_BIAS = float(2 ** 30)
NEG_BIG = -1e30
IMP_BIG = 1e30
CHOSEN = -3e38
CHOSEN_LIMIT = -2e38
N_FORCED = 3


def _log2(n):
    k = int(n).bit_length() - 1
    assert 1 << k == n, n
    return k


def _cparams(n_grid):
    return pltpu.CompilerParams(dimension_semantics=("arbitrary",) * n_grid,
                                vmem_limit_bytes=VMEM_LIMIT_BYTES)


def _const_spec(shape):
    nd = len(shape)
    return pl.BlockSpec(shape, lambda *_: (0,) * nd, pipeline_mode=pl.Buffered(1))


def _sigmoid(x):
    return 1.0 / (1.0 + jnp.exp(-x))


def _lost_range(total, smallest):
    return jnp.logical_not(jnp.sum(total) < FINITE_LIMIT) | (jnp.min(smallest) < L_MIN)


def _rms(x, w):
    var = jnp.mean(x * x, axis=-1, keepdims=True)
    return x * lax.rsqrt(var + RMS_EPS) * w


def _inproj_kernel(x_ref, nw_ref, wa_ref, wb_ref, wc_ref, o_ref, kvc_ref):
    h = _rms(x_ref[...], nw_ref[...]).astype(MXU_DTYPE)
    n = o_ref.shape[1]
    for c0 in range(0, n, KV_WIDTH):
        cw = min(KV_WIDTH, n - c0)
        w_ref, w0 = ((wa_ref, 0) if c0 < COL_GPOOL else (wb_ref, COL_GPOOL) if c0 < COL_GNSA
                     else (wc_ref, COL_GNSA))
        res = jnp.dot(h, w_ref[:, c0 - w0:c0 - w0 + cw],
                      preferred_element_type=jnp.float32).astype(o_ref.dtype)
        o_ref[:, c0:c0 + cw] = res
        which = (c0 - COL_KV) // KV_WIDTH
        if c0 >= COL_KV and which < 2:
            for g in range(N_KV_GROUPS):
                kvc_ref[which * N_KV_GROUPS + g] = res[:, g * HEAD_DIM:(g + 1) * HEAD_DIM]


def _inproj(x2d, norm_w, w_parts):
    m, d = x2d.shape
    n = sum(w.shape[1] for w in w_parts)
    assert n == PROJ_WIDTH
    return pl.pallas_call(
        _inproj_kernel,
        grid=(m // ROW_TILE,),
        in_specs=[pl.BlockSpec((ROW_TILE, d), lambda i: (i, 0)),
                  _const_spec((1, d))] + [_const_spec(w.shape) for w in w_parts],
        out_specs=[pl.BlockSpec((ROW_TILE, n), lambda i: (i, 0)),
                   pl.BlockSpec((2 * N_KV_GROUPS, ROW_TILE, HEAD_DIM), lambda i: (0, i, 0))],
        out_shape=[jax.ShapeDtypeStruct((m, n), ACT_DTYPE),
                   jax.ShapeDtypeStruct((2 * N_KV_GROUPS, m, HEAD_DIM), ACT_DTYPE)],
        compiler_params=_cparams(1),
        name="inproj",
    )(x2d, norm_w, *w_parts)


def _gelu_tanh(x):
    return 0.5 * x * (1.0 + jnp.tanh(np.sqrt(2.0 / np.pi) * (x + 0.044715 * (x * x * x))))


def _compress_one(r_ref, pe_ref, w1_ref, b1_ref, w2_ref):
    half = w1_ref.shape[0] // 2
    n_blk = r_ref.shape[2]
    rows = [r_ref[0, 0, :, r * half:(r + 1) * half] for r in range(CMP_PER_SLC)]
    top = [jnp.dot(x, w1_ref[0:half, :], preferred_element_type=jnp.float32) for x in rows]
    bot = [jnp.dot(x, w1_ref[half:, :], preferred_element_type=jnp.float32) for x in rows]
    bias = jnp.dot(pe_ref[...], w1_ref[...], preferred_element_type=jnp.float32)[0:1, :] + b1_ref[...]
    nxt = bot[1:] + [pltpu.roll(bot[0], n_blk - 1, 0)]
    hid = jnp.concatenate([top[r] + nxt[r] for r in range(CMP_PER_SLC)], axis=0) + bias
    return jnp.dot(_gelu_tanh(hid).astype(MXU_DTYPE), w2_ref[...], preferred_element_type=jnp.float32)


def _compress_kernel(rk_ref, rv_ref, pek_ref, w1k_ref, b1k_ref, w2k_ref,
                     pev_ref, w1v_ref, b1v_ref, w2v_ref, ok_ref, ovt_ref):
    kc = _compress_one(rk_ref, pek_ref, w1k_ref, b1k_ref, w2k_ref)
    vc = _compress_one(rv_ref, pev_ref, w1v_ref, b1v_ref, w2v_ref)
    ok_ref[0, 0] = jnp.concatenate([kc * (SCORE_SCALE * LOG2E), jnp.zeros_like(kc)], axis=1).astype(ok_ref.dtype)
    ovt_ref[0, 0] = jnp.concatenate([kc, vc], axis=1).T[HEAD_DIM:, :].astype(ovt_ref.dtype)


def _compress(kvc_rows, pek, w1k, b1k, w2k, pev, w1v, b1v, w2v):
    _, b, n_blk, width = kvc_rows.shape
    dk = w2k.shape[1]
    n_rows = n_blk * CMP_PER_SLC
    w_specs = [_const_spec(a.shape) for a in (pek, w1k, b1k, w2k)]
    return pl.pallas_call(
        _compress_kernel,
        grid=(b, N_KV_GROUPS),
        in_specs=[pl.BlockSpec((1, 1, n_blk, width), lambda i, j: (j, i, 0, 0)),
                  pl.BlockSpec((1, 1, n_blk, width), lambda i, j: (N_KV_GROUPS + j, i, 0, 0))]
        + w_specs + w_specs,
        out_specs=[pl.BlockSpec((1, 1, n_rows, 2 * dk), lambda i, j: (i, j, 0, 0)),
                   pl.BlockSpec((1, 1, dk, n_rows), lambda i, j: (i, j, 0, 0))],
        out_shape=[jax.ShapeDtypeStruct((b, N_KV_GROUPS, n_rows, 2 * dk), MXU_DTYPE),
                   jax.ShapeDtypeStruct((b, N_KV_GROUPS, dk, n_rows), MXU_DTYPE)],
        compiler_params=_cparams(2),
        name="compress",
    )(kvc_rows, kvc_rows, pek, w1k, b1k, w2k, pev, w1v, b1v, w2v)


def _kprep_kernel(ks_ref, vs_ref, kw_ref, vw_ref, cos_ref, sin_ref, rot_ref,
                  kaug_ref, vst_ref, kwin_ref, vwt_ref, knorm_ref):
    ts = ks_ref.shape[1]
    t0 = pl.program_id(1) * ts
    cos = jnp.concatenate([cos_ref[...]] * N_KV_GROUPS, axis=1)
    sin = jnp.concatenate([sin_ref[...]] * N_KV_GROUPS, axis=1)
    rot = rot_ref[...]

    def rope(k_ref):
        k = k_ref[0]
        krot = jnp.dot(k.astype(MXU_DTYPE), rot, preferred_element_type=jnp.float32)
        return (k.astype(jnp.float32) * cos + krot * sin) * (SCORE_SCALE * LOG2E)

    n_blk = kaug_ref.shape[3] - 2 * HEAD_DIM
    blk = lax.broadcasted_iota(jnp.int32, (ts, n_blk), 1)
    key_blk = (t0 + lax.broadcasted_iota(jnp.int32, (ts, n_blk), 0)) >> _log2(SLC_BLOCK)
    onehot = jnp.where(blk == key_blk, 1.0, 0.0).astype(kaug_ref.dtype)
    ks_f = rope(ks_ref)
    kw_f = rope(kw_ref)
    ks = ks_f.astype(kaug_ref.dtype)
    kw = kw_f.astype(kwin_ref.dtype)

    seg_l = lax.broadcasted_iota(jnp.int32, (KV_WIDTH, LANE), 0) >> _log2(HEAD_DIM)
    seg_c = lax.broadcasted_iota(jnp.int32, (KV_WIDTH, LANE), 1)
    sq = (jnp.dot(ks_f * ks_f, jnp.where(seg_c == seg_l, 1.0, 0.0), preferred_element_type=jnp.float32)
          + jnp.dot(kw_f * kw_f, jnp.where(seg_c == seg_l + N_KV_GROUPS, 1.0, 0.0),
                    preferred_element_type=jnp.float32))
    tile_max = jnp.broadcast_to(jnp.max(sq, axis=0, keepdims=True), knorm_ref.shape[1:])

    @pl.when(pl.program_id(1) == 0)
    def _():
        knorm_ref[0] = tile_max

    @pl.when(pl.program_id(1) > 0)
    def _():
        knorm_ref[0] = jnp.maximum(knorm_ref[0], tile_max)

    vst = vs_ref[0].astype(jnp.float32).T.astype(vst_ref.dtype)
    vwt = vw_ref[0].astype(jnp.float32).T.astype(vwt_ref.dtype)
    ones_col = jnp.where(lax.broadcasted_iota(jnp.int32, (ts, HEAD_DIM), 1) == 0, 1.0, 0.0
                         ).astype(kaug_ref.dtype)
    for g in range(N_KV_GROUPS):
        sl = slice(g * HEAD_DIM, (g + 1) * HEAD_DIM)
        kaug_ref[0, g, :, 0:n_blk] = onehot
        kaug_ref[0, g, :, n_blk:n_blk + HEAD_DIM] = ks[:, sl]
        kaug_ref[0, g, :, n_blk + HEAD_DIM:] = ones_col
        kwin_ref[0, g, :, 0:HEAD_DIM] = kw[:, sl]
        kwin_ref[0, g, :, HEAD_DIM:] = ones_col
        vst_ref[0, g] = vst[sl, :]
        vwt_ref[0, g] = vwt[sl, :]


def _kprep(proj3, cos_k, sin_k, rot):
    b, s, _ = proj3.shape
    n_blk = s // SLC_BLOCK
    ts = min(KPREP_TILE, s)
    kvb = COL_KV // KV_WIDTH

    def col(j):
        return pl.BlockSpec((1, ts, KV_WIDTH), lambda i, t: (i, t, kvb + j))

    tab = pl.BlockSpec((ts, HEAD_DIM), lambda i, t: (t, 0))
    aug_w = n_blk + 2 * HEAD_DIM
    return pl.pallas_call(
        _kprep_kernel,
        grid=(b, s // ts),
        in_specs=[col(2), col(3), col(4), col(5), tab, tab, _const_spec(rot.shape)],
        out_specs=[pl.BlockSpec((1, N_KV_GROUPS, ts, aug_w), lambda i, t: (i, 0, t, 0)),
                   pl.BlockSpec((1, N_KV_GROUPS, HEAD_DIM, ts), lambda i, t: (i, 0, 0, t)),
                   pl.BlockSpec((1, N_KV_GROUPS, ts, 2 * HEAD_DIM), lambda i, t: (i, 0, t, 0)),
                   pl.BlockSpec((1, N_KV_GROUPS, HEAD_DIM, ts), lambda i, t: (i, 0, 0, t)),
                   pl.BlockSpec((1, SUBLANE, LANE), lambda i, t: (i, 0, 0))],
        out_shape=[jax.ShapeDtypeStruct((b, N_KV_GROUPS, s, aug_w), MXU_DTYPE),
                   jax.ShapeDtypeStruct((b, N_KV_GROUPS, HEAD_DIM, s), MXU_DTYPE),
                   jax.ShapeDtypeStruct((b, N_KV_GROUPS, s, 2 * HEAD_DIM), MXU_DTYPE),
                   jax.ShapeDtypeStruct((b, N_KV_GROUPS, HEAD_DIM, s), MXU_DTYPE),
                   jax.ShapeDtypeStruct((b, SUBLANE, LANE), jnp.float32)],
        compiler_params=_cparams(2),
        name="kprep",
    )(proj3, proj3, proj3, proj3, cos_k, sin_k, rot)


def _heads_to_lanes(xt):
    return jnp.concatenate([xt[h * HEAD_DIM:(h + 1) * HEAD_DIM, :] for h in range(HEADS_PER_GROUP)],
                           axis=1)


def _vis_rows(tq):
    levels = tq // CMP_STRIDE
    dt = np.arange(tq)
    v = (dt - (CMP_BLOCK - 1)) // CMP_STRIDE - (-(CMP_BLOCK - 1)) // CMP_STRIDE
    rows = np.zeros((HEAD_DIM, tq), np.float32)
    for c in range(1, levels + 1):
        rows[c] = np.where(v < c, -MASK_BIAS, 0.0)
    rows[levels + 1] = -MASK_BIAS
    return jnp.asarray(np.tile(rows, (1, HEADS_PER_GROUP)), MXU_DTYPE)


def _cmpsel_kernel(q_ref, kc_ref, vct_ref, vis_ref, ocmp_ref, bias_ref):
    tq = q_ref.shape[1]
    n_blk = bias_ref.shape[2]
    m_rows = HEADS_PER_GROUP * tq
    t0 = pl.program_id(1) * tq
    levels = tq // CMP_STRIDE

    def body(nr):
        n_tok = CMP_PER_SLC * nr
        j_idx = lax.broadcasted_iota(jnp.int32, (nr, LANE), 0)
        n_idx = jnp.concatenate([j_idx * CMP_PER_SLC + r for r in range(CMP_PER_SLC)], axis=0)
        n_first = (t0 - (CMP_BLOCK - 1)) >> _log2(CMP_STRIDE)
        u = jnp.clip(n_idx - n_first, 0, levels + 1)
        vis_cols = jnp.where(lax.broadcasted_iota(jnp.int32, (n_tok, LANE), 1) - HEAD_DIM == u,
                             1.0, 0.0).astype(MXU_DTYPE)
        vis_rows = vis_ref[...]
        t_row = t0 + (lax.broadcasted_iota(jnp.int32, (1, m_rows), 1) & (tq - 1))
        jrow = lax.broadcasted_iota(jnp.int32, (nr, tq), 0)
        jt = (t0 + lax.broadcasted_iota(jnp.int32, (nr, tq), 1)) >> _log2(SLC_BLOCK)
        causal_blk = jrow <= jt
        forced = (jrow == 0) | (jrow == jt) | (jrow == jt - 1)
        jf = jrow.astype(jnp.float32)
        n_extract = min(SLC_TOPK, n_blk) - N_FORCED
        jt_row = (t0 + lax.broadcasted_iota(jnp.int32, (1, tq), 1)) >> _log2(SLC_BLOCK)
        want = jnp.minimum(jt_row + 1, SLC_TOPK).astype(jnp.float32)

        def store_bias(pb, marks):
            bias_ref[pb, 0, 0:nr, :] = jnp.where(marks < CHOSEN_LIMIT, 0.0, -MASK_BIAS).astype(bias_ref.dtype)

        pending = []
        for pb in range(q_ref.shape[0]):
            kc = jnp.concatenate([kc_ref[pb, 0, r * n_blk:r * n_blk + nr, :] for r in range(CMP_PER_SLC)],
                                 axis=0)
            vct = jnp.concatenate([vct_ref[pb, 0, :, r * n_blk:r * n_blk + nr] for r in range(CMP_PER_SLC)],
                                  axis=1)
            qt = _heads_to_lanes(q_ref[pb].astype(jnp.float32).T).astype(MXU_DTYPE)
            s = jnp.dot(kc + vis_cols, jnp.concatenate([qt, vis_rows], axis=0),
                        preferred_element_type=jnp.float32)
            mx = jnp.max(s, axis=0, keepdims=True)
            p = jnp.exp2(s - mx)
            den = jnp.sum(p, axis=0, keepdims=True)
            inv = jnp.where(t_row >= CMP_BLOCK - 1, 1.0 / den, 0.0)
            ocmp_ref[pb, 0, 0] = inv * jnp.dot(vct, p.astype(MXU_DTYPE), preferred_element_type=jnp.float32)

            imp = None
            for h in range(HEADS_PER_GROUP):
                ph = p[:, h * tq:(h + 1) * tq]
                last = ph[3 * nr:4 * nr, :]
                prev = jnp.where(jrow == 0, 0.0, pltpu.roll(last, 1, 0))
                taps = ph[0:nr, :] + ph[nr:2 * nr, :] + ph[2 * nr:3 * nr, :] + last + prev
                imp_h = taps * inv[:, h * tq:(h + 1) * tq]
                imp = imp_h if imp is None else imp + imp_h

            val = jnp.where(forced, CHOSEN, jnp.where(causal_blk, imp, -IMP_BIG))
            marks = val
            for _ in range(n_extract):
                marks = jnp.where(marks == jnp.max(marks, axis=0, keepdims=True), CHOSEN, marks)
            store_bias(pb, marks)
            if nr < n_blk:
                bias_ref[pb, 0, nr:, :] = jnp.full((n_blk - nr, tq), -MASK_BIAS, bias_ref.dtype)
            count = jnp.sum(jnp.where((marks < CHOSEN_LIMIT) & causal_blk, 1.0, 0.0), axis=0, keepdims=True)
            pending.append((pb, val, jnp.max(jnp.abs(count - want)) > 0.5))

        for pb, val, tied in pending:
            @pl.when(tied)
            def _(val=val, pb=pb):
                v = val
                for _ in range(n_extract):
                    top = jnp.max(v, axis=0, keepdims=True)
                    first = jnp.min(jnp.where(v == top, jf, float(n_blk)), axis=0, keepdims=True)
                    v = jnp.where(jf == first, CHOSEN, v)
                store_bias(pb, v)

    sizes = [n_blk * f // 4 for f in (1, 2, 3, 4) if (n_blk * f // 4) % 16 == 0]
    lo = 0
    for nr in sizes:
        hi = nr * SLC_BLOCK if nr < n_blk else None
        end = t0 + tq
        cond = (end > lo) if hi is None else (end > lo) & (end <= hi)
        pl.when(cond)(functools.partial(body, nr))
        lo = hi


def _cmpsel(proj3, kc_r, vc_rt):
    b, s, _ = proj3.shape
    tq = min(Q_TILE, s)
    nq = s // tq
    n_cmp_pad = kc_r.shape[2]
    n_blk = s // SLC_BLOCK
    qb = COL_Q // KV_WIDTH
    m_rows = HEADS_PER_GROUP * tq
    return pl.pallas_call(
        _cmpsel_kernel,
        grid=(N_KV_GROUPS, nq),
        in_specs=[pl.BlockSpec((b, tq, KV_WIDTH), lambda g, q: (0, q, qb + g)),
                  pl.BlockSpec((b, 1, n_cmp_pad, LANE), lambda g, q: (0, g, 0, 0)),
                  pl.BlockSpec((b, 1, HEAD_DIM, n_cmp_pad), lambda g, q: (0, g, 0, 0)),
                  _const_spec((HEAD_DIM, m_rows))],
        out_specs=[pl.BlockSpec((b, 1, 1, HEAD_DIM, m_rows), lambda g, q: (0, g, q, 0, 0)),
                   pl.BlockSpec((b, 1, n_blk, tq), lambda g, q: (0, g, 0, q))],
        out_shape=[jax.ShapeDtypeStruct((b, N_KV_GROUPS, nq, HEAD_DIM, m_rows), jnp.float32),
                   jax.ShapeDtypeStruct((b, N_KV_GROUPS, n_blk, s), MXU_DTYPE)],
        compiler_params=_cparams(2),
        name="cmpsel",
    )(proj3, kc_r, vc_rt, _vis_rows(tq))


def _flash_kernel(q_ref, cos_ref, sin_ref, bias_ref, kaug_ref, vst_ref, kwin_ref, vwt_ref, knorm_ref,
                  wmask_ref, gate_ref, ocmp_ref, o_ref, m_ref, l_ref, acc_ref, lw_ref, accw_ref):
    nb = q_ref.shape[0]
    tq = q_ref.shape[1]
    m_rows = HEADS_PER_GROUP * tq
    grp = pl.program_id(0)
    qi = pl.program_id(1)
    t0 = qi * tq
    kd = pl.multiple_of(t0, tq)
    probs = range(nb)
    zpad = jnp.zeros((HEAD_DIM, m_rows), MXU_DTYPE)
    pad_row = lax.broadcasted_iota(jnp.int32, (HEAD_DIM, m_rows), 0)
    lane = lax.broadcasted_iota(jnp.int32, (1, LANE), 1)

    def stab_rows(neg_m):
        return jnp.where(pad_row == 0, neg_m, 0.0).astype(MXU_DTYPE)

    cos = cos_ref[...]
    sin = sin_ref[...]
    qr, bias4, qs_slc, qs_win = [], [], [], []
    for pb in probs:
        qt = q_ref[pb].astype(jnp.float32).T
        parts = []
        for h in range(HEADS_PER_GROUP):
            qh = qt[h * HEAD_DIM:(h + 1) * HEAD_DIM, :]
            x1 = qh[0:ROPE_HALF, :]
            x2 = qh[ROPE_HALF:ROPE_DIM, :]
            parts.append(jnp.concatenate([x1 * cos - x2 * sin, x2 * cos + x1 * sin, qh[ROPE_DIM:, :]],
                                         axis=0))
        qf = jnp.concatenate(parts, axis=1)
        q_sq = jnp.sum(qf * qf, axis=0, keepdims=True)
        k_sq = knorm_ref[pb, 0:1, :]
        ks_sq = jnp.max(jnp.where(lane == grp, k_sq, 0.0), axis=1, keepdims=True)
        kw_sq = jnp.max(jnp.where(lane == grp + N_KV_GROUPS, k_sq, 0.0), axis=1, keepdims=True)
        qr.append(qf.astype(MXU_DTYPE))
        bias4.append(jnp.concatenate([bias_ref[pb, 0]] * HEADS_PER_GROUP, axis=1))
        qs_slc.append(jnp.concatenate([bias4[pb], qr[pb], stab_rows(-STAB_MARGIN * jnp.sqrt(q_sq * ks_sq))],
                                      axis=0))
        qs_win.append(jnp.concatenate([qr[pb], stab_rows(-STAB_MARGIN * jnp.sqrt(q_sq * kw_sq))], axis=0))

    n_back = WINDOW // tq

    def win_tile(back):
        if back == 0:
            return kd, tq, wmask_ref[0]
        inside = wmask_ref[back] if back * tq + tq - 1 >= WINDOW else 0.0
        return (pl.multiple_of(jnp.maximum(t0 - back * tq, 0), tq), tq,
                jnp.where(qi >= back, inside, NEG_BIG))

    def fast_tiles(k_ref, v_ref, qs, tiles, l_out, acc_out, first=False):
        for pb in probs:
            l_add = acc_add = None
            for k0, tk, mask in tiles:
                s = jnp.dot(k_ref[pb, 0, pl.ds(k0, tk), :], qs[pb], preferred_element_type=jnp.float32)
                if mask is not None:
                    s = s + mask
                p = jnp.exp2(s)
                l_t = jnp.sum(p, axis=0, keepdims=True)
                acc_t = jnp.dot(v_ref[pb, 0, :, pl.ds(k0, tk)], p.astype(MXU_DTYPE),
                                preferred_element_type=jnp.float32)
                l_add = l_t if l_add is None else l_add + l_t
                acc_add = acc_t if acc_add is None else acc_add + acc_t
            if first:
                l_out[pb] = l_add
                acc_out[pb] = acc_add
            else:
                l_out[pb] += l_add
                acc_out[pb] += acc_add

    fast_tiles(kaug_ref, vst_ref, qs_slc, [(kd, tq, wmask_ref[0])], l_ref, acc_ref, first=True)
    fast_tiles(kwin_ref, vwt_ref, qs_win, [win_tile(back) for back in range(n_back + 1)],
               lw_ref, accw_ref, first=True)

    sub = SLC_SUBTILE
    n_sub = SLC_SPAN // sub

    def span_body(i, carry):
        base = i * SLC_SPAN
        fast_tiles(kaug_ref, vst_ref, qs_slc,
                   [(pl.multiple_of(base + j * sub, sub), sub, None) for j in range(n_sub)],
                   l_ref, acc_ref)
        return carry

    n_span = t0 // SLC_SPAN
    lax.fori_loop(0, n_span, span_body, 0)
    done = n_span * SLC_SPAN
    size = SLC_SPAN // 2
    while size >= tq:
        take = ((t0 // size) % 2) == 1

        @pl.when(take)
        def _(done=done, size=size):
            piece = min(size, sub)
            fast_tiles(kaug_ref, vst_ref, qs_slc,
                       [(pl.multiple_of(done + j * piece, piece), piece, None) for j in range(size // piece)],
                       l_ref, acc_ref)

        done = done + jnp.where(take, size, 0)
        size //= 2

    def exact_start(pb, s, vt, l_out, acc_out):
        mx = jnp.max(s, axis=0, keepdims=True)
        p = jnp.exp2(s - mx)
        m_ref[pb] = mx
        l_out[pb] = jnp.sum(p, axis=0, keepdims=True)
        acc_out[pb] = jnp.dot(vt, p.astype(MXU_DTYPE), preferred_element_type=jnp.float32)

    def exact_update(pb, s, vt, l_out, acc_out):
        m_old = m_ref[pb]
        mx = jnp.maximum(m_old, jnp.max(s, axis=0, keepdims=True))
        alpha = jnp.exp2(m_old - mx)
        p = jnp.exp2(s - mx)
        m_ref[pb] = mx
        l_out[pb] = alpha * l_out[pb] + jnp.sum(p, axis=0, keepdims=True)
        acc_out[pb] = alpha * acc_out[pb] + jnp.dot(vt, p.astype(MXU_DTYPE),
                                                    preferred_element_type=jnp.float32)

    check = l_low = None
    for pb in probs:
        c = (l_ref[pb] + lw_ref[pb] + jnp.sum(jnp.abs(acc_ref[pb]), axis=0, keepdims=True)
             + jnp.sum(jnp.abs(accw_ref[pb]), axis=0, keepdims=True))
        lo = jnp.minimum(l_ref[pb], lw_ref[pb])
        check = c if check is None else check + c
        l_low = lo if l_low is None else jnp.minimum(l_low, lo)

    @pl.when(_lost_range(check, l_low))
    def _():
        tcol = t0 + (lax.broadcasted_iota(jnp.int32, (tq, m_rows), 1) & (tq - 1))
        koff = lax.broadcasted_iota(jnp.int32, (tq, m_rows), 0)
        causal = t0 + koff <= tcol
        for pb in probs:
            q0 = jnp.concatenate([bias4[pb], qr[pb], zpad], axis=0)
            s = jnp.dot(kaug_ref[pb, 0, pl.ds(kd, tq), :], q0, preferred_element_type=jnp.float32)
            exact_start(pb, jnp.where(causal, s, NEG_BIG), vst_ref[pb, 0, :, pl.ds(kd, tq)], l_ref, acc_ref)

            def body(kt, carry, pb=pb, q0=q0):
                k0 = pl.multiple_of(kt * tq, tq)
                sk = jnp.dot(kaug_ref[pb, 0, pl.ds(k0, tq), :], q0, preferred_element_type=jnp.float32)
                exact_update(pb, sk, vst_ref[pb, 0, :, pl.ds(k0, tq)], l_ref, acc_ref)
                return carry

            lax.fori_loop(0, qi, body, 0)
            qw0 = jnp.concatenate([qr[pb], zpad], axis=0)
            for back in range(n_back + 1):
                kbase = t0 - back * tq
                k0 = pl.multiple_of(jnp.maximum(kbase, 0), tq)
                sw = jnp.dot(kwin_ref[pb, 0, pl.ds(k0, tq), :], qw0, preferred_element_type=jnp.float32)
                diff = tcol - (kbase + koff)
                sw = jnp.where((diff >= 0) & (diff < WINDOW) & (kbase + koff >= 0), sw, NEG_BIG)
                if back == 0:
                    exact_start(pb, sw, vwt_ref[pb, 0, :, pl.ds(k0, tq)], lw_ref, accw_ref)
                else:
                    exact_update(pb, sw, vwt_ref[pb, 0, :, pl.ds(k0, tq)], lw_ref, accw_ref)

    r_id = lax.broadcasted_iota(jnp.int32, (GATE_ROWS, LANE), 0)
    c_id = lax.broadcasted_iota(jnp.int32, (GATE_ROWS, LANE), 1)
    pick = jnp.where((c_id == grp * (HEADS_PER_GROUP * 3) + r_id) & (r_id < HEADS_PER_GROUP * 3),
                     1.0, 0.0).astype(gate_ref.dtype)
    for pb in probs:
        o_slc = acc_ref[pb] / l_ref[pb]
        o_win = accw_ref[pb] / lw_ref[pb]
        graw = lax.dot_general(pick, gate_ref[pb], (((1,), (1,)), ((), ())),
                               preferred_element_type=jnp.float32)
        gates = _sigmoid(graw)
        o_cmp = ocmp_ref[pb, 0, 0]
        mixed = []
        for h in range(HEADS_PER_GROUP):
            sl = slice(h * tq, (h + 1) * tq)
            mixed.append(gates[3 * h:3 * h + 1, :] * o_cmp[:, sl]
                         + gates[3 * h + 1:3 * h + 2, :] * o_slc[:, sl]
                         + gates[3 * h + 2:3 * h + 3, :] * o_win[:, sl])
        o_ref[pb] = jnp.concatenate(mixed, axis=0).T.astype(o_ref.dtype)


def _window_masks(tq):
    k = np.arange(tq)[:, None]
    t = np.arange(tq)[None, :]
    tiles = []
    for back in range(WINDOW // tq + 1):
        diff = t + back * tq - k
        tiles.append(np.tile(np.where((diff >= 0) & (diff < WINDOW), 0.0, NEG_BIG), (1, HEADS_PER_GROUP)))
    return jnp.asarray(np.stack(tiles), jnp.float32)


def _flash(proj3, cos_q, sin_q, bias, kaug, vst, kwin, vwt, knorm, ocmp):
    b, s, _ = proj3.shape
    tq = min(Q_TILE, s)
    nq = s // tq
    n_blk = s // SLC_BLOCK
    aug_w = kaug.shape[3]
    qb = COL_Q // KV_WIDTH
    m_rows = HEADS_PER_GROUP * tq

    wmask = _window_masks(tq)

    def resident(shape):
        return pl.BlockSpec((b, 1) + shape, lambda g, q: (0, g, 0, 0))

    return pl.pallas_call(
        _flash_kernel,
        grid=(N_KV_GROUPS, nq),
        in_specs=[pl.BlockSpec((b, tq, KV_WIDTH), lambda g, q: (0, q, qb + g)),
                  pl.BlockSpec((ROPE_HALF, tq), lambda g, q: (0, q)),
                  pl.BlockSpec((ROPE_HALF, tq), lambda g, q: (0, q)),
                  pl.BlockSpec((b, 1, n_blk, tq), lambda g, q: (0, g, 0, q)),
                  resident((s, aug_w)),
                  resident((HEAD_DIM, s)),
                  resident((s, 2 * HEAD_DIM)),
                  resident((HEAD_DIM, s)),
                  _const_spec((b, SUBLANE, LANE)),
                  _const_spec(wmask.shape),
                  pl.BlockSpec((b, tq, LANE), lambda g, q: (0, q, COL_GNSA // LANE)),
                  pl.BlockSpec((b, 1, 1, HEAD_DIM, m_rows), lambda g, q: (0, g, q, 0, 0))],
        out_specs=pl.BlockSpec((b, tq, KV_WIDTH), lambda g, q: (0, q, g)),
        out_shape=jax.ShapeDtypeStruct((b, s, ATTN_Q_WIDTH), ACT_DTYPE),
        scratch_shapes=[pltpu.VMEM((b, 1, m_rows), jnp.float32),
                        pltpu.VMEM((b, 1, m_rows), jnp.float32),
                        pltpu.VMEM((b, HEAD_DIM, m_rows), jnp.float32),
                        pltpu.VMEM((b, 1, m_rows), jnp.float32),
                        pltpu.VMEM((b, HEAD_DIM, m_rows), jnp.float32)],
        compiler_params=_cparams(2),
        name="flash",
    )(proj3, cos_q, sin_q, bias, kaug, vst, kwin, vwt, knorm, wmask, proj3, ocmp)


def _tail_kernel(u_ref, halo_ref, gp_ref, ga_ref, attn_ref, x_ref, pw_ref, ps_ref, wpp_ref, wpa_ref, wo_ref,
                 n2_ref, wg_ref, wu_ref, wd_ref, nf_ref, o_ref, *, seq_len):
    tm = u_ref.shape[0]
    t0 = (pl.program_id(0) * tm) & (seq_len - 1)

    u = u_ref[...].astype(jnp.float32)
    halo = jnp.where(t0 > 0, halo_ref[...].astype(jnp.float32), 0.0)
    ext = jnp.concatenate([halo, u], axis=0)
    t = t0 + lax.broadcasted_iota(jnp.int32, (tm, POOL_GROUP), 0)
    pooled = []
    for gi, w in enumerate(POOL_WINDOWS):
        sl = slice(gi * POOL_GROUP, (gi + 1) * POOL_GROUP)
        acc = ext[:, sl]
        span = 1
        while span < w:
            acc = acc + pltpu.roll(acc, span, 0)
            span *= 2
        cnt = jnp.minimum(t + 1, w).astype(jnp.float32)
        mean = acc[POOL_HALO:, :] / cnt
        mixed = jnp.dot((mean - u[:, sl]).astype(MXU_DTYPE), pw_ref[gi],
                        preferred_element_type=jnp.float32)
        pooled.append(mixed * ps_ref[:, sl])
    pool = jnp.concatenate(pooled, axis=1).astype(MXU_DTYPE)

    pp = jnp.dot(pool, wpp_ref[...], preferred_element_type=jnp.float32)
    pa = jnp.dot(attn_ref[...], wpa_ref[...], preferred_element_type=jnp.float32)
    merged = (_sigmoid(gp_ref[...].astype(jnp.float32)) * pp
              + _sigmoid(ga_ref[...].astype(jnp.float32)) * pa)
    x1 = x_ref[...] + jnp.dot(merged.astype(MXU_DTYPE), wo_ref[...], preferred_element_type=jnp.float32)

    h = _rms(x1, n2_ref[...]).astype(MXU_DTYPE)
    d_ff = wg_ref.shape[1]
    out = x1
    for c0 in range(0, d_ff, FF_CHUNK):
        cw = min(FF_CHUNK, d_ff - c0)
        gate = jnp.dot(h, wg_ref[:, c0:c0 + cw], preferred_element_type=jnp.float32)
        up = jnp.dot(h, wu_ref[:, c0:c0 + cw], preferred_element_type=jnp.float32)
        act = (gate * _sigmoid(gate) * up).astype(MXU_DTYPE)
        out = out + jnp.dot(act, wd_ref[c0:c0 + cw, :], preferred_element_type=jnp.float32)
    o_ref[...] = _rms(out, nf_ref[...])


def _tail(proj2, attn2, x2d, pool_w, pool_scale, wpp, wpa, wo, norm2_w, wg, wu, wd, norm_f_w, seq_len):
    m, d = x2d.shape
    tm = min(ROW_TILE, seq_len)
    halo_per_tile = tm // POOL_HALO
    weights = (pool_w, pool_scale, wpp, wpa, wo, norm2_w, wg, wu, wd, norm_f_w)
    return pl.pallas_call(
        functools.partial(_tail_kernel, seq_len=seq_len),
        grid=(m // tm,),
        in_specs=[pl.BlockSpec((tm, POOL_WIDTH), lambda i: (i, COL_POOL // POOL_WIDTH)),
                  pl.BlockSpec((POOL_HALO, POOL_WIDTH),
                               lambda i: (jnp.maximum(i * halo_per_tile - 1, 0), COL_POOL // POOL_WIDTH)),
                  pl.BlockSpec((tm, d), lambda i: (i, COL_GPOOL // D_MODEL)),
                  pl.BlockSpec((tm, d), lambda i: (i, COL_GATTN // D_MODEL)),
                  pl.BlockSpec((tm, ATTN_Q_WIDTH), lambda i: (i, 0)),
                  pl.BlockSpec((tm, d), lambda i: (i, 0))] + [_const_spec(w.shape) for w in weights],
        out_specs=pl.BlockSpec((tm, d), lambda i: (i, 0)),
        out_shape=jax.ShapeDtypeStruct((m, d), jnp.float32),
        compiler_params=_cparams(1),
        name="tail",
    )(proj2, proj2, proj2, proj2, attn2, x2d, *weights)


def _rope_tables(seq_len):
    inv_freq = 1.0 / (ROPE_THETA ** (jnp.arange(0, ROPE_DIM, 2, dtype=jnp.float32) / ROPE_DIM))
    ang = inv_freq[:, None] * jnp.arange(seq_len).astype(jnp.float32)[None, :]
    cos_t, sin_t = jnp.cos(ang), jnp.sin(ang)
    rest = HEAD_DIM - ROPE_DIM
    cos_h = jnp.concatenate([cos_t, cos_t, jnp.ones((rest, seq_len), jnp.float32)], axis=0).T
    sin_h = jnp.concatenate([sin_t, sin_t, jnp.zeros((rest, seq_len), jnp.float32)], axis=0).T
    return cos_t, sin_t, cos_h, sin_h


def _rotate_half_matrix():
    rot = np.zeros((KV_WIDTH, KV_WIDTH), np.float32)
    for l in range(KV_WIDTH):
        d = l % HEAD_DIM
        if d < ROPE_HALF:
            rot[l + ROPE_HALF, l] = -1.0
        elif d < ROPE_DIM:
            rot[l - ROPE_HALF, l] = 1.0
    return jnp.asarray(rot, MXU_DTYPE)


def _layer(x, norm1_w, w_in, pool_w, pool_scale, cmp_pe_k, cmp_w1_k, cmp_b1_k, cmp_w2_k,
           cmp_pe_v, cmp_w1_v, cmp_b1_v, cmp_w2_v, w_proj_pool, w_proj_attn, w_out,
           norm2_w, w_ffn_gate, w_ffn_up, w_ffn_down, norm_f_w):
    b, s, d = x.shape
    m = b * s
    n_blk = s // SLC_BLOCK
    assert s % Q_TILE == 0 and s % ROW_TILE == 0 and (s & (s - 1)) == 0, s
    cd = MXU_DTYPE

    g0 = COL_GPOOL
    w_parts = (w_in[:, :g0].astype(cd), w_in[:, g0 + GATE_NSA:].astype(cd),
               jnp.pad(w_in[:, g0:g0 + GATE_NSA], ((0, 0), (0, LANE - GATE_NSA))).astype(cd))
    x2d = x.reshape(m, d)
    proj2, kvc = _inproj(x2d, norm1_w.reshape(1, d), w_parts)
    proj3 = proj2.reshape(b, s, PROJ_WIDTH)

    def pe8(pe):
        return jnp.broadcast_to(pe.reshape(1, -1), (8, pe.size)).astype(cd)

    kvc_rows = kvc.reshape(2 * N_KV_GROUPS, b, n_blk, SLC_BLOCK * HEAD_DIM)
    kc_r, vc_rt = _compress(kvc_rows,
                            pe8(cmp_pe_k), cmp_w1_k.astype(cd), cmp_b1_k.reshape(1, -1), cmp_w2_k.astype(cd),
                            pe8(cmp_pe_v), cmp_w1_v.astype(cd), cmp_b1_v.reshape(1, -1), cmp_w2_v.astype(cd))

    cos_q, sin_q, cos_k, sin_k = _rope_tables(s)
    kaug, vst, kwin, vwt, knorm = _kprep(proj3, cos_k, sin_k, _rotate_half_matrix())
    ocmp, bias = _cmpsel(proj3, kc_r, vc_rt)
    attn = _flash(proj3, cos_q, sin_q, bias, kaug, vst, kwin, vwt, knorm, ocmp)
    return _tail(proj2, attn.reshape(m, ATTN_Q_WIDTH), x2d, pool_w.astype(cd),
                 pool_scale.reshape(1, -1), w_proj_pool.astype(cd), w_proj_attn.astype(cd),
                 w_out.astype(cd), norm2_w.reshape(1, d), w_ffn_gate.astype(cd), w_ffn_up.astype(cd),
                 w_ffn_down.astype(cd), norm_f_w.reshape(1, d), s)


def kernel(x, norm1_w, w_in, pool_w, pool_scale, cmp_pe_k, cmp_w1_k, cmp_b1_k, cmp_w2_k, cmp_pe_v,
           cmp_w1_v, cmp_b1_v, cmp_w2_v, w_proj_pool, w_proj_attn, w_out, norm2_w, w_ffn_gate,
           w_ffn_up, w_ffn_down, norm_f_w):
    b, s, d = x.shape
    depth = w_in.shape[0]
    assert depth == 1, "the final norm is fused into the last layer's FFN kernel"
    out = _layer(x, norm1_w[0], w_in[0], pool_w[0], pool_scale[0], cmp_pe_k[0], cmp_w1_k[0], cmp_b1_k[0],
                 cmp_w2_k[0], cmp_pe_v[0], cmp_w1_v[0], cmp_b1_v[0], cmp_w2_v[0], w_proj_pool[0],
                 w_proj_attn[0], w_out[0], norm2_w[0], w_ffn_gate[0], w_ffn_up[0], w_ffn_down[0], norm_f_w)
    return out.reshape(b, s, d)
```

```python
import functools

import numpy as np
import jax
import jax.numpy as jnp
from jax import lax
from jax.experimental import pallas as pl
from jax.experimental.pallas import tpu as pltpu

D_MODEL = 1024
N_HEADS = 16
HEAD_DIM = 64
N_KV_GROUPS = 4
HEADS_PER_GROUP = N_HEADS // N_KV_GROUPS
ROPE_DIM = HEAD_DIM // 4
ROPE_HALF = ROPE_DIM // 2
ROPE_THETA = 500000.0
CMP_BLOCK = 32
CMP_STRIDE = 16
CMP_PER_SLC = 4
SLC_BLOCK = 64
SLC_TOPK = 16
WINDOW = 512
ATTN_Q_WIDTH = N_HEADS * HEAD_DIM
KV_WIDTH = N_KV_GROUPS * HEAD_DIM
POOL_WIDTH = D_MODEL // 2
POOL_WINDOWS = (2, 4, 8, 16)
POOL_GROUP = POOL_WIDTH // len(POOL_WINDOWS)
POOL_HALO = 16
RMS_EPS = 1e-6

LANE = 128
SUBLANE = 8
MXU_DTYPE = jnp.bfloat16
ACT_DTYPE = jnp.bfloat16
VMEM_LIMIT_BYTES = 56 * 1024 * 1024

GATE_NSA = N_HEADS * 3
COL_POOL = 0
COL_Q = COL_POOL + POOL_WIDTH
COL_KV = COL_Q + ATTN_Q_WIDTH
COL_GPOOL = COL_KV + 6 * KV_WIDTH
COL_GATTN = COL_GPOOL + D_MODEL
COL_GNSA = COL_GATTN + D_MODEL
PROJ_WIDTH = COL_GNSA + LANE
GATE_ROWS = 16

ROW_TILE = 512
Q_TILE = 256
SLC_SPAN = 4096
SLC_SUBTILE = 4096
KPREP_TILE = 1024
FF_CHUNK = 512
SCORE_SCALE = HEAD_DIM ** -0.5
LOG2E = float(np.log2(np.e))
FINITE_LIMIT = 3e38
STAB_MARGIN = 1.03
L_MIN = 2.0 ** -64
MASK_BIAS = float(2 ** 30)
NEG_BIG = -1e30
IMP_BIG = 1e30
CHOSEN = -3e38
CHOSEN_LIMIT = -2e38
N_FORCED = 3


def _log2(n):
    k = int(n).bit_length() - 1
    assert 1 << k == n, n
    return k


def _cparams(n_grid):
    return pltpu.CompilerParams(dimension_semantics=("arbitrary",) * n_grid,
                                vmem_limit_bytes=VMEM_LIMIT_BYTES)


def _const_spec(shape):
    nd = len(shape)
    return pl.BlockSpec(shape, lambda *_: (0,) * nd, pipeline_mode=pl.Buffered(1))


def _sigmoid(x):
    return 1.0 / (1.0 + jnp.exp(-x))


def _lost_range(total, smallest):
    return jnp.logical_not(jnp.sum(total) < FINITE_LIMIT) | (jnp.min(smallest) < L_MIN)


def _rms(x, w):
    var = jnp.mean(x * x, axis=-1, keepdims=True)
    return x * lax.rsqrt(var + RMS_EPS) * w


def _inproj_kernel(x_ref, nw_ref, wa_ref, wb_ref, wc_ref, o_ref, kvc_ref):
    h = _rms(x_ref[...], nw_ref[...]).astype(MXU_DTYPE)
    n = o_ref.shape[1]
    for c0 in range(0, n, KV_WIDTH):
        cw = min(KV_WIDTH, n - c0)
        w_ref, w0 = ((wa_ref, 0) if c0 < COL_GPOOL else (wb_ref, COL_GPOOL) if c0 < COL_GNSA
                     else (wc_ref, COL_GNSA))
        res = jnp.dot(h, w_ref[:, c0 - w0:c0 - w0 + cw],
                      preferred_element_type=jnp.float32).astype(o_ref.dtype)
        o_ref[:, c0:c0 + cw] = res
        which = (c0 - COL_KV) // KV_WIDTH
        if c0 >= COL_KV and which < 2:
            for g in range(N_KV_GROUPS):
                kvc_ref[which * N_KV_GROUPS + g] = res[:, g * HEAD_DIM:(g + 1) * HEAD_DIM]


def _inproj(x2d, norm_w, w_parts):
    m, d = x2d.shape
    n = sum(w.shape[1] for w in w_parts)
    assert n == PROJ_WIDTH
    return pl.pallas_call(
        _inproj_kernel,
        grid=(m // ROW_TILE,),
        in_specs=[pl.BlockSpec((ROW_TILE, d), lambda i: (i, 0)),
                  _const_spec((1, d))] + [_const_spec(w.shape) for w in w_parts],
        out_specs=[pl.BlockSpec((ROW_TILE, n), lambda i: (i, 0)),
                   pl.BlockSpec((2 * N_KV_GROUPS, ROW_TILE, HEAD_DIM), lambda i: (0, i, 0))],
        out_shape=[jax.ShapeDtypeStruct((m, n), ACT_DTYPE),
                   jax.ShapeDtypeStruct((2 * N_KV_GROUPS, m, HEAD_DIM), ACT_DTYPE)],
        compiler_params=_cparams(1),
        name="inproj",
    )(x2d, norm_w, *w_parts)


def _gelu_tanh(x):
    return 0.5 * x * (1.0 + jnp.tanh(np.sqrt(2.0 / np.pi) * (x + 0.044715 * (x * x * x))))


def _compress_one(r_ref, pe_ref, w1_ref, b1_ref, w2_ref):
    half = w1_ref.shape[0] // 2
    n_blk = r_ref.shape[2]
    rows = [r_ref[0, 0, :, r * half:(r + 1) * half] for r in range(CMP_PER_SLC)]
    top = [jnp.dot(x, w1_ref[0:half, :], preferred_element_type=jnp.float32) for x in rows]
    bot = [jnp.dot(x, w1_ref[half:, :], preferred_element_type=jnp.float32) for x in rows]
    bias = jnp.dot(pe_ref[...], w1_ref[...], preferred_element_type=jnp.float32)[0:1, :] + b1_ref[...]
    nxt = bot[1:] + [pltpu.roll(bot[0], n_blk - 1, 0)]
    hid = jnp.concatenate([top[r] + nxt[r] for r in range(CMP_PER_SLC)], axis=0) + bias
    return jnp.dot(_gelu_tanh(hid).astype(MXU_DTYPE), w2_ref[...], preferred_element_type=jnp.float32)


def _compress_kernel(rk_ref, rv_ref, pek_ref, w1k_ref, b1k_ref, w2k_ref,
                     pev_ref, w1v_ref, b1v_ref, w2v_ref, ok_ref, ovt_ref):
    kc = _compress_one(rk_ref, pek_ref, w1k_ref, b1k_ref, w2k_ref)
    vc = _compress_one(rv_ref, pev_ref, w1v_ref, b1v_ref, w2v_ref)
    ok_ref[0, 0] = jnp.concatenate([kc * (SCORE_SCALE * LOG2E), jnp.zeros_like(kc)], axis=1).astype(ok_ref.dtype)
    ovt_ref[0, 0] = jnp.concatenate([kc, vc], axis=1).T[HEAD_DIM:, :].astype(ovt_ref.dtype)


def _compress(kvc_rows, pek, w1k, b1k, w2k, pev, w1v, b1v, w2v):
    _, b, n_blk, width = kvc_rows.shape
    dk = w2k.shape[1]
    n_rows = n_blk * CMP_PER_SLC
    w_specs = [_const_spec(a.shape) for a in (pek, w1k, b1k, w2k)]
    return pl.pallas_call(
        _compress_kernel,
        grid=(b, N_KV_GROUPS),
        in_specs=[pl.BlockSpec((1, 1, n_blk, width), lambda i, j: (j, i, 0, 0)),
                  pl.BlockSpec((1, 1, n_blk, width), lambda i, j: (N_KV_GROUPS + j, i, 0, 0))]
        + w_specs + w_specs,
        out_specs=[pl.BlockSpec((1, 1, n_rows, 2 * dk), lambda i, j: (i, j, 0, 0)),
                   pl.BlockSpec((1, 1, dk, n_rows), lambda i, j: (i, j, 0, 0))],
        out_shape=[jax.ShapeDtypeStruct((b, N_KV_GROUPS, n_rows, 2 * dk), MXU_DTYPE),
                   jax.ShapeDtypeStruct((b, N_KV_GROUPS, dk, n_rows), MXU_DTYPE)],
        compiler_params=_cparams(2),
        name="compress",
    )(kvc_rows, kvc_rows, pek, w1k, b1k, w2k, pev, w1v, b1v, w2v)


def _kprep_kernel(ks_ref, vs_ref, kw_ref, vw_ref, cos_ref, sin_ref, rot_ref,
                  kaug_ref, vst_ref, kwin_ref, vwt_ref, knorm_ref):
    ts = ks_ref.shape[1]
    t0 = pl.program_id(1) * ts
    cos = jnp.concatenate([cos_ref[...]] * N_KV_GROUPS, axis=1)
    sin = jnp.concatenate([sin_ref[...]] * N_KV_GROUPS, axis=1)
    rot = rot_ref[...]

    def rope(k_ref):
        k = k_ref[0]
        krot = jnp.dot(k.astype(MXU_DTYPE), rot, preferred_element_type=jnp.float32)
        return (k.astype(jnp.float32) * cos + krot * sin) * (SCORE_SCALE * LOG2E)

    n_blk = kaug_ref.shape[3] - 2 * HEAD_DIM
    blk = lax.broadcasted_iota(jnp.int32, (ts, n_blk), 1)
    key_blk = (t0 + lax.broadcasted_iota(jnp.int32, (ts, n_blk), 0)) >> _log2(SLC_BLOCK)
    onehot = jnp.where(blk == key_blk, 1.0, 0.0).astype(kaug_ref.dtype)
    ks_f = rope(ks_ref)
    kw_f = rope(kw_ref)
    ks = ks_f.astype(kaug_ref.dtype)
    kw = kw_f.astype(kwin_ref.dtype)

    seg_l = lax.broadcasted_iota(jnp.int32, (KV_WIDTH, LANE), 0) >> _log2(HEAD_DIM)
    seg_c = lax.broadcasted_iota(jnp.int32, (KV_WIDTH, LANE), 1)
    sq = (jnp.dot(ks_f * ks_f, jnp.where(seg_c == seg_l, 1.0, 0.0), preferred_element_type=jnp.float32)
          + jnp.dot(kw_f * kw_f, jnp.where(seg_c == seg_l + N_KV_GROUPS, 1.0, 0.0),
                    preferred_element_type=jnp.float32))
    tile_max = jnp.broadcast_to(jnp.max(sq, axis=0, keepdims=True), knorm_ref.shape[1:])

    @pl.when(pl.program_id(1) == 0)
    def _():
        knorm_ref[0] = tile_max

    @pl.when(pl.program_id(1) > 0)
    def _():
        knorm_ref[0] = jnp.maximum(knorm_ref[0], tile_max)

    vst = vs_ref[0].astype(jnp.float32).T.astype(vst_ref.dtype)
    vwt = vw_ref[0].astype(jnp.float32).T.astype(vwt_ref.dtype)
    ones_col = jnp.where(lax.broadcasted_iota(jnp.int32, (ts, HEAD_DIM), 1) == 0, 1.0, 0.0
                         ).astype(kaug_ref.dtype)
    for g in range(N_KV_GROUPS):
        sl = slice(g * HEAD_DIM, (g + 1) * HEAD_DIM)
        kaug_ref[0, g, :, 0:n_blk] = onehot
        kaug_ref[0, g, :, n_blk:n_blk + HEAD_DIM] = ks[:, sl]
        kaug_ref[0, g, :, n_blk + HEAD_DIM:] = ones_col
        kwin_ref[0, g, :, 0:HEAD_DIM] = kw[:, sl]
        kwin_ref[0, g, :, HEAD_DIM:] = ones_col
        vst_ref[0, g] = vst[sl, :]
        vwt_ref[0, g] = vwt[sl, :]


def _kprep(proj3, cos_k, sin_k, rot):
    b, s, _ = proj3.shape
    n_blk = s // SLC_BLOCK
    ts = min(KPREP_TILE, s)
    kvb = COL_KV // KV_WIDTH

    def col(j):
        return pl.BlockSpec((1, ts, KV_WIDTH), lambda i, t: (i, t, kvb + j))

    tab = pl.BlockSpec((ts, HEAD_DIM), lambda i, t: (t, 0))
    aug_w = n_blk + 2 * HEAD_DIM
    return pl.pallas_call(
        _kprep_kernel,
        grid=(b, s // ts),
        in_specs=[col(2), col(3), col(4), col(5), tab, tab, _const_spec(rot.shape)],
        out_specs=[pl.BlockSpec((1, N_KV_GROUPS, ts, aug_w), lambda i, t: (i, 0, t, 0)),
                   pl.BlockSpec((1, N_KV_GROUPS, HEAD_DIM, ts), lambda i, t: (i, 0, 0, t)),
                   pl.BlockSpec((1, N_KV_GROUPS, ts, 2 * HEAD_DIM), lambda i, t: (i, 0, t, 0)),
                   pl.BlockSpec((1, N_KV_GROUPS, HEAD_DIM, ts), lambda i, t: (i, 0, 0, t)),
                   pl.BlockSpec((1, SUBLANE, LANE), lambda i, t: (i, 0, 0))],
        out_shape=[jax.ShapeDtypeStruct((b, N_KV_GROUPS, s, aug_w), MXU_DTYPE),
                   jax.ShapeDtypeStruct((b, N_KV_GROUPS, HEAD_DIM, s), MXU_DTYPE),
                   jax.ShapeDtypeStruct((b, N_KV_GROUPS, s, 2 * HEAD_DIM), MXU_DTYPE),
                   jax.ShapeDtypeStruct((b, N_KV_GROUPS, HEAD_DIM, s), MXU_DTYPE),
                   jax.ShapeDtypeStruct((b, SUBLANE, LANE), jnp.float32)],
        compiler_params=_cparams(2),
        name="kprep",
    )(proj3, proj3, proj3, proj3, cos_k, sin_k, rot)


def _heads_to_lanes(xt):
    return jnp.concatenate([xt[h * HEAD_DIM:(h + 1) * HEAD_DIM, :] for h in range(HEADS_PER_GROUP)],
                           axis=1)


def _vis_rows(tq):
    levels = tq // CMP_STRIDE
    dt = np.arange(tq)
    v = (dt - (CMP_BLOCK - 1)) // CMP_STRIDE - (-(CMP_BLOCK - 1)) // CMP_STRIDE
    rows = np.zeros((HEAD_DIM, tq), np.float32)
    for c in range(1, levels + 1):
        rows[c] = np.where(v < c, -MASK_BIAS, 0.0)
    rows[levels + 1] = -MASK_BIAS
    return jnp.asarray(np.tile(rows, (1, HEADS_PER_GROUP)), MXU_DTYPE)


def _cmpsel_kernel(q_ref, kc_ref, vct_ref, vis_ref, ocmp_ref, bias_ref):
    tq = q_ref.shape[1]
    n_blk = bias_ref.shape[2]
    m_rows = HEADS_PER_GROUP * tq
    t0 = pl.program_id(1) * tq
    levels = tq // CMP_STRIDE

    def body(nr):
        n_tok = CMP_PER_SLC * nr
        j_idx = lax.broadcasted_iota(jnp.int32, (nr, LANE), 0)
        n_idx = jnp.concatenate([j_idx * CMP_PER_SLC + r for r in range(CMP_PER_SLC)], axis=0)
        n_first = (t0 - (CMP_BLOCK - 1)) >> _log2(CMP_STRIDE)
        u = jnp.clip(n_idx - n_first, 0, levels + 1)
        vis_cols = jnp.where(lax.broadcasted_iota(jnp.int32, (n_tok, LANE), 1) - HEAD_DIM == u,
                             1.0, 0.0).astype(MXU_DTYPE)
        vis_rows = vis_ref[...]
        t_row = t0 + (lax.broadcasted_iota(jnp.int32, (1, m_rows), 1) & (tq - 1))
        jrow = lax.broadcasted_iota(jnp.int32, (nr, tq), 0)
        jt = (t0 + lax.broadcasted_iota(jnp.int32, (nr, tq), 1)) >> _log2(SLC_BLOCK)
        causal_blk = jrow <= jt
        forced = (jrow == 0) | (jrow == jt) | (jrow == jt - 1)
        jf = jrow.astype(jnp.float32)
        n_extract = min(SLC_TOPK, n_blk) - N_FORCED
        jt_row = (t0 + lax.broadcasted_iota(jnp.int32, (1, tq), 1)) >> _log2(SLC_BLOCK)
        want = jnp.minimum(jt_row + 1, SLC_TOPK).astype(jnp.float32)

        def store_bias(pb, marks):
            bias_ref[pb, 0, 0:nr, :] = jnp.where(marks < CHOSEN_LIMIT, 0.0, -MASK_BIAS).astype(bias_ref.dtype)

        pending = []
        for pb in range(q_ref.shape[0]):
            kc = jnp.concatenate([kc_ref[pb, 0, r * n_blk:r * n_blk + nr, :] for r in range(CMP_PER_SLC)],
                                 axis=0)
            vct = jnp.concatenate([vct_ref[pb, 0, :, r * n_blk:r * n_blk + nr] for r in range(CMP_PER_SLC)],
                                  axis=1)
            qt = _heads_to_lanes(q_ref[pb].astype(jnp.float32).T).astype(MXU_DTYPE)
            s = jnp.dot(kc + vis_cols, jnp.concatenate([qt, vis_rows], axis=0),
                        preferred_element_type=jnp.float32)
            mx = jnp.max(s, axis=0, keepdims=True)
            p = jnp.exp2(s - mx)
            den = jnp.sum(p, axis=0, keepdims=True)
            inv = jnp.where(t_row >= CMP_BLOCK - 1, 1.0 / den, 0.0)
            ocmp_ref[pb, 0, 0] = inv * jnp.dot(vct, p.astype(MXU_DTYPE), preferred_element_type=jnp.float32)

            imp = None
            for h in range(HEADS_PER_GROUP):
                ph = p[:, h * tq:(h + 1) * tq]
                last = ph[3 * nr:4 * nr, :]
                prev = jnp.where(jrow == 0, 0.0, pltpu.roll(last, 1, 0))
                taps = ph[0:nr, :] + ph[nr:2 * nr, :] + ph[2 * nr:3 * nr, :] + last + prev
                imp_h = taps * inv[:, h * tq:(h + 1) * tq]
                imp = imp_h if imp is None else imp + imp_h

            val = jnp.where(forced, CHOSEN, jnp.where(causal_blk, imp, -IMP_BIG))
            marks = val
            for _ in range(n_extract):
                marks = jnp.where(marks == jnp.max(marks, axis=0, keepdims=True), CHOSEN, marks)
            store_bias(pb, marks)
            if nr < n_blk:
                bias_ref[pb, 0, nr:, :] = jnp.full((n_blk - nr, tq), -MASK_BIAS, bias_ref.dtype)
            count = jnp.sum(jnp.where((marks < CHOSEN_LIMIT) & causal_blk, 1.0, 0.0), axis=0, keepdims=True)
            pending.append((pb, val, jnp.max(jnp.abs(count - want)) > 0.5))

        for pb, val, tied in pending:
            @pl.when(tied)
            def _(val=val, pb=pb):
                v = val
                for _ in range(n_extract):
                    top = jnp.max(v, axis=0, keepdims=True)
                    first = jnp.min(jnp.where(v == top, jf, float(n_blk)), axis=0, keepdims=True)
                    v = jnp.where(jf == first, CHOSEN, v)
                store_bias(pb, v)

    sizes = [n_blk * f // 4 for f in (1, 2, 3, 4) if (n_blk * f // 4) % 16 == 0]
    lo = 0
    for nr in sizes:
        hi = nr * SLC_BLOCK if nr < n_blk else None
        end = t0 + tq
        cond = (end > lo) if hi is None else (end > lo) & (end <= hi)
        pl.when(cond)(functools.partial(body, nr))
        lo = hi


def _cmpsel(proj3, kc_r, vc_rt):
    b, s, _ = proj3.shape
    tq = min(Q_TILE, s)
    nq = s // tq
    n_cmp_pad = kc_r.shape[2]
    n_blk = s // SLC_BLOCK
    qb = COL_Q // KV_WIDTH
    m_rows = HEADS_PER_GROUP * tq
    return pl.pallas_call(
        _cmpsel_kernel,
        grid=(N_KV_GROUPS, nq),
        in_specs=[pl.BlockSpec((b, tq, KV_WIDTH), lambda g, q: (0, q, qb + g)),
                  pl.BlockSpec((b, 1, n_cmp_pad, LANE), lambda g, q: (0, g, 0, 0)),
                  pl.BlockSpec((b, 1, HEAD_DIM, n_cmp_pad), lambda g, q: (0, g, 0, 0)),
                  _const_spec((HEAD_DIM, m_rows))],
        out_specs=[pl.BlockSpec((b, 1, 1, HEAD_DIM, m_rows), lambda g, q: (0, g, q, 0, 0)),
                   pl.BlockSpec((b, 1, n_blk, tq), lambda g, q: (0, g, 0, q))],
        out_shape=[jax.ShapeDtypeStruct((b, N_KV_GROUPS, nq, HEAD_DIM, m_rows), jnp.float32),
                   jax.ShapeDtypeStruct((b, N_KV_GROUPS, n_blk, s), MXU_DTYPE)],
        compiler_params=_cparams(2),
        name="cmpsel",
    )(proj3, kc_r, vc_rt, _vis_rows(tq))


def _flash_kernel(q_ref, cos_ref, sin_ref, bias_ref, kaug_ref, vst_ref, kwin_ref, vwt_ref, knorm_ref,
                  wmask_ref, gate_ref, ocmp_ref, o_ref, m_ref, l_ref, acc_ref, lw_ref, accw_ref):
    nb = q_ref.shape[0]
    tq = q_ref.shape[1]
    m_rows = HEADS_PER_GROUP * tq
    grp = pl.program_id(0)
    qi = pl.program_id(1)
    t0 = qi * tq
    kd = pl.multiple_of(t0, tq)
    probs = range(nb)
    zpad = jnp.zeros((HEAD_DIM, m_rows), MXU_DTYPE)
    pad_row = lax.broadcasted_iota(jnp.int32, (HEAD_DIM, m_rows), 0)
    lane = lax.broadcasted_iota(jnp.int32, (1, LANE), 1)

    def stab_rows(neg_m):
        return jnp.where(pad_row == 0, neg_m, 0.0).astype(MXU_DTYPE)

    cos = cos_ref[...]
    sin = sin_ref[...]
    qr, bias4, qs_slc, qs_win = [], [], [], []
    for pb in probs:
        qt = q_ref[pb].astype(jnp.float32).T
        parts = []
        for h in range(HEADS_PER_GROUP):
            qh = qt[h * HEAD_DIM:(h + 1) * HEAD_DIM, :]
            x1 = qh[0:ROPE_HALF, :]
            x2 = qh[ROPE_HALF:ROPE_DIM, :]
            parts.append(jnp.concatenate([x1 * cos - x2 * sin, x2 * cos + x1 * sin, qh[ROPE_DIM:, :]],
                                         axis=0))
        qf = jnp.concatenate(parts, axis=1)
        q_sq = jnp.sum(qf * qf, axis=0, keepdims=True)
        k_sq = knorm_ref[pb, 0:1, :]
        ks_sq = jnp.max(jnp.where(lane == grp, k_sq, 0.0), axis=1, keepdims=True)
        kw_sq = jnp.max(jnp.where(lane == grp + N_KV_GROUPS, k_sq, 0.0), axis=1, keepdims=True)
        qr.append(qf.astype(MXU_DTYPE))
        bias4.append(jnp.concatenate([bias_ref[pb, 0]] * HEADS_PER_GROUP, axis=1))
        qs_slc.append(jnp.concatenate([bias4[pb], qr[pb], stab_rows(-STAB_MARGIN * jnp.sqrt(q_sq * ks_sq))],
                                      axis=0))
        qs_win.append(jnp.concatenate([qr[pb], stab_rows(-STAB_MARGIN * jnp.sqrt(q_sq * kw_sq))], axis=0))

    n_back = WINDOW // tq

    def win_tile(back):
        if back == 0:
            return kd, tq, wmask_ref[0]
        inside = wmask_ref[back] if back * tq + tq - 1 >= WINDOW else 0.0
        return (pl.multiple_of(jnp.maximum(t0 - back * tq, 0), tq), tq,
                jnp.where(qi >= back, inside, NEG_BIG))

    def fast_tiles(k_ref, v_ref, qs, tiles, l_out, acc_out, first=False):
        for pb in probs:
            l_add = acc_add = None
            for k0, tk, mask in tiles:
                s = jnp.dot(k_ref[pb, 0, pl.ds(k0, tk), :], qs[pb], preferred_element_type=jnp.float32)
                if mask is not None:
                    s = s + mask
                p = jnp.exp2(s)
                l_t = jnp.sum(p, axis=0, keepdims=True)
                acc_t = jnp.dot(v_ref[pb, 0, :, pl.ds(k0, tk)], p.astype(MXU_DTYPE),
                                preferred_element_type=jnp.float32)
                l_add = l_t if l_add is None else l_add + l_t
                acc_add = acc_t if acc_add is None else acc_add + acc_t
            if first:
                l_out[pb] = l_add
                acc_out[pb] = acc_add
            else:
                l_out[pb] += l_add
                acc_out[pb] += acc_add

    fast_tiles(kaug_ref, vst_ref, qs_slc, [(kd, tq, wmask_ref[0])], l_ref, acc_ref, first=True)
    fast_tiles(kwin_ref, vwt_ref, qs_win, [win_tile(back) for back in range(n_back + 1)],
               lw_ref, accw_ref, first=True)

    sub = SLC_SUBTILE
    n_sub = SLC_SPAN // sub

    def span_body(i, carry):
        base = i * SLC_SPAN
        fast_tiles(kaug_ref, vst_ref, qs_slc,
                   [(pl.multiple_of(base + j * sub, sub), sub, None) for j in range(n_sub)],
                   l_ref, acc_ref)
        return carry

    n_span = t0 // SLC_SPAN
    lax.fori_loop(0, n_span, span_body, 0)
    done = n_span * SLC_SPAN
    size = SLC_SPAN // 2
    while size >= tq:
        take = ((t0 // size) % 2) == 1

        @pl.when(take)
        def _(done=done, size=size):
            piece = min(size, sub)
            fast_tiles(kaug_ref, vst_ref, qs_slc,
                       [(pl.multiple_of(done + j * piece, piece), piece, None) for j in range(size // piece)],
                       l_ref, acc_ref)

        done = done + jnp.where(take, size, 0)
        size //= 2

    def exact_start(pb, s, vt, l_out, acc_out):
        mx = jnp.max(s, axis=0, keepdims=True)
        p = jnp.exp2(s - mx)
        m_ref[pb] = mx
        l_out[pb] = jnp.sum(p, axis=0, keepdims=True)
        acc_out[pb] = jnp.dot(vt, p.astype(MXU_DTYPE), preferred_element_type=jnp.float32)

    def exact_update(pb, s, vt, l_out, acc_out):
        m_old = m_ref[pb]
        mx = jnp.maximum(m_old, jnp.max(s, axis=0, keepdims=True))
        alpha = jnp.exp2(m_old - mx)
        p = jnp.exp2(s - mx)
        m_ref[pb] = mx
        l_out[pb] = alpha * l_out[pb] + jnp.sum(p, axis=0, keepdims=True)
        acc_out[pb] = alpha * acc_out[pb] + jnp.dot(vt, p.astype(MXU_DTYPE),
                                                    preferred_element_type=jnp.float32)

    check = l_low = None
    for pb in probs:
        c = (l_ref[pb] + lw_ref[pb] + jnp.sum(jnp.abs(acc_ref[pb]), axis=0, keepdims=True)
             + jnp.sum(jnp.abs(accw_ref[pb]), axis=0, keepdims=True))
        lo = jnp.minimum(l_ref[pb], lw_ref[pb])
        check = c if check is None else check + c
        l_low = lo if l_low is None else jnp.minimum(l_low, lo)

    @pl.when(_lost_range(check, l_low))
    def _():
        tcol = t0 + (lax.broadcasted_iota(jnp.int32, (tq, m_rows), 1) & (tq - 1))
        koff = lax.broadcasted_iota(jnp.int32, (tq, m_rows), 0)
        causal = t0 + koff <= tcol
        for pb in probs:
            q0 = jnp.concatenate([bias4[pb], qr[pb], zpad], axis=0)
            s = jnp.dot(kaug_ref[pb, 0, pl.ds(kd, tq), :], q0, preferred_element_type=jnp.float32)
            exact_start(pb, jnp.where(causal, s, NEG_BIG), vst_ref[pb, 0, :, pl.ds(kd, tq)], l_ref, acc_ref)

            def body(kt, carry, pb=pb, q0=q0):
                k0 = pl.multiple_of(kt * tq, tq)
                sk = jnp.dot(kaug_ref[pb, 0, pl.ds(k0, tq), :], q0, preferred_element_type=jnp.float32)
                exact_update(pb, sk, vst_ref[pb, 0, :, pl.ds(k0, tq)], l_ref, acc_ref)
                return carry

            lax.fori_loop(0, qi, body, 0)
            qw0 = jnp.concatenate([qr[pb], zpad], axis=0)
            for back in range(n_back + 1):
                kbase = t0 - back * tq
                k0 = pl.multiple_of(jnp.maximum(kbase, 0), tq)
                sw = jnp.dot(kwin_ref[pb, 0, pl.ds(k0, tq), :], qw0, preferred_element_type=jnp.float32)
                diff = tcol - (kbase + koff)
                sw = jnp.where((diff >= 0) & (diff < WINDOW) & (kbase + koff >= 0), sw, NEG_BIG)
                if back == 0:
                    exact_start(pb, sw, vwt_ref[pb, 0, :, pl.ds(k0, tq)], lw_ref, accw_ref)
                else:
                    exact_update(pb, sw, vwt_ref[pb, 0, :, pl.ds(k0, tq)], lw_ref, accw_ref)

    r_id = lax.broadcasted_iota(jnp.int32, (GATE_ROWS, LANE), 0)
    c_id = lax.broadcasted_iota(jnp.int32, (GATE_ROWS, LANE), 1)
    pick = jnp.where((c_id == grp * (HEADS_PER_GROUP * 3) + r_id) & (r_id < HEADS_PER_GROUP * 3),
                     1.0, 0.0).astype(gate_ref.dtype)
    for pb in probs:
        o_slc = acc_ref[pb] / l_ref[pb]
        o_win = accw_ref[pb] / lw_ref[pb]
        graw = lax.dot_general(pick, gate_ref[pb], (((1,), (1,)), ((), ())),
                               preferred_element_type=jnp.float32)
        gates = _sigmoid(graw)
        o_cmp = ocmp_ref[pb, 0, 0]
        mixed = []
        for h in range(HEADS_PER_GROUP):
            sl = slice(h * tq, (h + 1) * tq)
            mixed.append(gates[3 * h:3 * h + 1, :] * o_cmp[:, sl]
                         + gates[3 * h + 1:3 * h + 2, :] * o_slc[:, sl]
                         + gates[3 * h + 2:3 * h + 3, :] * o_win[:, sl])
        o_ref[pb] = jnp.concatenate(mixed, axis=0).T.astype(o_ref.dtype)


def _window_masks(tq):
    k = np.arange(tq)[:, None]
    t = np.arange(tq)[None, :]
    tiles = []
    for back in range(WINDOW // tq + 1):
        diff = t + back * tq - k
        tiles.append(np.tile(np.where((diff >= 0) & (diff < WINDOW), 0.0, NEG_BIG), (1, HEADS_PER_GROUP)))
    return jnp.asarray(np.stack(tiles), jnp.float32)


def _flash(proj3, cos_q, sin_q, bias, kaug, vst, kwin, vwt, knorm, ocmp):
    b, s, _ = proj3.shape
    tq = min(Q_TILE, s)
    nq = s // tq
    n_blk = s // SLC_BLOCK
    aug_w = kaug.shape[3]
    qb = COL_Q // KV_WIDTH
    m_rows = HEADS_PER_GROUP * tq

    wmask = _window_masks(tq)

    def resident(shape):
        return pl.BlockSpec((b, 1) + shape, lambda g, q: (0, g, 0, 0))

    return pl.pallas_call(
        _flash_kernel,
        grid=(N_KV_GROUPS, nq),
        in_specs=[pl.BlockSpec((b, tq, KV_WIDTH), lambda g, q: (0, q, qb + g)),
                  pl.BlockSpec((ROPE_HALF, tq), lambda g, q: (0, q)),
                  pl.BlockSpec((ROPE_HALF, tq), lambda g, q: (0, q)),
                  pl.BlockSpec((b, 1, n_blk, tq), lambda g, q: (0, g, 0, q)),
                  resident((s, aug_w)),
                  resident((HEAD_DIM, s)),
                  resident((s, 2 * HEAD_DIM)),
                  resident((HEAD_DIM, s)),
                  _const_spec((b, SUBLANE, LANE)),
                  _const_spec(wmask.shape),
                  pl.BlockSpec((b, tq, LANE), lambda g, q: (0, q, COL_GNSA // LANE)),
                  pl.BlockSpec((b, 1, 1, HEAD_DIM, m_rows), lambda g, q: (0, g, q, 0, 0))],
        out_specs=pl.BlockSpec((b, tq, KV_WIDTH), lambda g, q: (0, q, g)),
        out_shape=jax.ShapeDtypeStruct((b, s, ATTN_Q_WIDTH), ACT_DTYPE),
        scratch_shapes=[pltpu.VMEM((b, 1, m_rows), jnp.float32),
                        pltpu.VMEM((b, 1, m_rows), jnp.float32),
                        pltpu.VMEM((b, HEAD_DIM, m_rows), jnp.float32),
                        pltpu.VMEM((b, 1, m_rows), jnp.float32),
                        pltpu.VMEM((b, HEAD_DIM, m_rows), jnp.float32)],
        compiler_params=_cparams(2),
        name="flash",
    )(proj3, cos_q, sin_q, bias, kaug, vst, kwin, vwt, knorm, wmask, proj3, ocmp)


def _tail_kernel(u_ref, halo_ref, gp_ref, ga_ref, attn_ref, x_ref, pw_ref, ps_ref, wpp_ref, wpa_ref, wo_ref,
                 n2_ref, wg_ref, wu_ref, wd_ref, nf_ref, o_ref, *, seq_len):
    tm = u_ref.shape[0]
    t0 = (pl.program_id(0) * tm) & (seq_len - 1)

    u = u_ref[...].astype(jnp.float32)
    halo = jnp.where(t0 > 0, halo_ref[...].astype(jnp.float32), 0.0)
    ext = jnp.concatenate([halo, u], axis=0)
    t = t0 + lax.broadcasted_iota(jnp.int32, (tm, POOL_GROUP), 0)
    pooled = []
    for gi, w in enumerate(POOL_WINDOWS):
        sl = slice(gi * POOL_GROUP, (gi + 1) * POOL_GROUP)
        acc = ext[:, sl]
        span = 1
        while span < w:
            acc = acc + pltpu.roll(acc, span, 0)
            span *= 2
        cnt = jnp.minimum(t + 1, w).astype(jnp.float32)
        mean = acc[POOL_HALO:, :] / cnt
        mixed = jnp.dot((mean - u[:, sl]).astype(MXU_DTYPE), pw_ref[gi],
                        preferred_element_type=jnp.float32)
        pooled.append(mixed * ps_ref[:, sl])
    pool = jnp.concatenate(pooled, axis=1).astype(MXU_DTYPE)

    pp = jnp.dot(pool, wpp_ref[...], preferred_element_type=jnp.float32)
    pa = jnp.dot(attn_ref[...], wpa_ref[...], preferred_element_type=jnp.float32)
    merged = (_sigmoid(gp_ref[...].astype(jnp.float32)) * pp
              + _sigmoid(ga_ref[...].astype(jnp.float32)) * pa)
    x1 = x_ref[...] + jnp.dot(merged.astype(MXU_DTYPE), wo_ref[...], preferred_element_type=jnp.float32)

    h = _rms(x1, n2_ref[...]).astype(MXU_DTYPE)
    d_ff = wg_ref.shape[1]
    out = x1
    for c0 in range(0, d_ff, FF_CHUNK):
        cw = min(FF_CHUNK, d_ff - c0)
        gate = jnp.dot(h, wg_ref[:, c0:c0 + cw], preferred_element_type=jnp.float32)
        up = jnp.dot(h, wu_ref[:, c0:c0 + cw], preferred_element_type=jnp.float32)
        act = (gate * _sigmoid(gate) * up).astype(MXU_DTYPE)
        out = out + jnp.dot(act, wd_ref[c0:c0 + cw, :], preferred_element_type=jnp.float32)
    o_ref[...] = _rms(out, nf_ref[...])


def _tail(proj2, attn2, x2d, pool_w, pool_scale, wpp, wpa, wo, norm2_w, wg, wu, wd, norm_f_w, seq_len):
    m, d = x2d.shape
    tm = min(ROW_TILE, seq_len)
    halo_per_tile = tm // POOL_HALO
    weights = (pool_w, pool_scale, wpp, wpa, wo, norm2_w, wg, wu, wd, norm_f_w)
    return pl.pallas_call(
        functools.partial(_tail_kernel, seq_len=seq_len),
        grid=(m // tm,),
        in_specs=[pl.BlockSpec((tm, POOL_WIDTH), lambda i: (i, COL_POOL // POOL_WIDTH)),
                  pl.BlockSpec((POOL_HALO, POOL_WIDTH),
                               lambda i: (jnp.maximum(i * halo_per_tile - 1, 0), COL_POOL // POOL_WIDTH)),
                  pl.BlockSpec((tm, d), lambda i: (i, COL_GPOOL // D_MODEL)),
                  pl.BlockSpec((tm, d), lambda i: (i, COL_GATTN // D_MODEL)),
                  pl.BlockSpec((tm, ATTN_Q_WIDTH), lambda i: (i, 0)),
                  pl.BlockSpec((tm, d), lambda i: (i, 0))] + [_const_spec(w.shape) for w in weights],
        out_specs=pl.BlockSpec((tm, d), lambda i: (i, 0)),
        out_shape=jax.ShapeDtypeStruct((m, d), jnp.float32),
        compiler_params=_cparams(1),
        name="tail",
    )(proj2, proj2, proj2, proj2, attn2, x2d, *weights)


def _rope_tables(seq_len):
    inv_freq = 1.0 / (ROPE_THETA ** (jnp.arange(0, ROPE_DIM, 2, dtype=jnp.float32) / ROPE_DIM))
    ang = inv_freq[:, None] * jnp.arange(seq_len).astype(jnp.float32)[None, :]
    cos_t, sin_t = jnp.cos(ang), jnp.sin(ang)
    rest = HEAD_DIM - ROPE_DIM
    cos_h = jnp.concatenate([cos_t, cos_t, jnp.ones((rest, seq_len), jnp.float32)], axis=0).T
    sin_h = jnp.concatenate([sin_t, sin_t, jnp.zeros((rest, seq_len), jnp.float32)], axis=0).T
    return cos_t, sin_t, cos_h, sin_h


def _rotate_half_matrix():
    rot = np.zeros((KV_WIDTH, KV_WIDTH), np.float32)
    for l in range(KV_WIDTH):
        d = l % HEAD_DIM
        if d < ROPE_HALF:
            rot[l + ROPE_HALF, l] = -1.0
        elif d < ROPE_DIM:
            rot[l - ROPE_HALF, l] = 1.0
    return jnp.asarray(rot, MXU_DTYPE)


def _layer(x, norm1_w, w_in, pool_w, pool_scale, cmp_pe_k, cmp_w1_k, cmp_b1_k, cmp_w2_k,
           cmp_pe_v, cmp_w1_v, cmp_b1_v, cmp_w2_v, w_proj_pool, w_proj_attn, w_out,
           norm2_w, w_ffn_gate, w_ffn_up, w_ffn_down, norm_f_w):
    b, s, d = x.shape
    m = b * s
    n_blk = s // SLC_BLOCK
    assert s % Q_TILE == 0 and s % ROW_TILE == 0 and (s & (s - 1)) == 0, s
    cd = MXU_DTYPE

    g0 = COL_GPOOL
    w_parts = (w_in[:, :g0].astype(cd), w_in[:, g0 + GATE_NSA:].astype(cd),
               jnp.pad(w_in[:, g0:g0 + GATE_NSA], ((0, 0), (0, LANE - GATE_NSA))).astype(cd))
    x2d = x.reshape(m, d)
    proj2, kvc = _inproj(x2d, norm1_w.reshape(1, d), w_parts)
    proj3 = proj2.reshape(b, s, PROJ_WIDTH)

    def pe8(pe):
        return jnp.broadcast_to(pe.reshape(1, -1), (8, pe.size)).astype(cd)

    kvc_rows = kvc.reshape(2 * N_KV_GROUPS, b, n_blk, SLC_BLOCK * HEAD_DIM)
    kc_r, vc_rt = _compress(kvc_rows,
                            pe8(cmp_pe_k), cmp_w1_k.astype(cd), cmp_b1_k.reshape(1, -1), cmp_w2_k.astype(cd),
                            pe8(cmp_pe_v), cmp_w1_v.astype(cd), cmp_b1_v.reshape(1, -1), cmp_w2_v.astype(cd))

    cos_q, sin_q, cos_k, sin_k = _rope_tables(s)
    kaug, vst, kwin, vwt, knorm = _kprep(proj3, cos_k, sin_k, _rotate_half_matrix())
    ocmp, bias = _cmpsel(proj3, kc_r, vc_rt)
    attn = _flash(proj3, cos_q, sin_q, bias, kaug, vst, kwin, vwt, knorm, ocmp)
    return _tail(proj2, attn.reshape(m, ATTN_Q_WIDTH), x2d, pool_w.astype(cd),
                 pool_scale.reshape(1, -1), w_proj_pool.astype(cd), w_proj_attn.astype(cd),
                 w_out.astype(cd), norm2_w.reshape(1, d), w_ffn_gate.astype(cd), w_ffn_up.astype(cd),
                 w_ffn_down.astype(cd), norm_f_w.reshape(1, d), s)


def kernel(x, norm1_w, w_in, pool_w, pool_scale, cmp_pe_k, cmp_w1_k, cmp_b1_k, cmp_w2_k, cmp_pe_v,
           cmp_w1_v, cmp_b1_v, cmp_w2_v, w_proj_pool, w_proj_attn, w_out, norm2_w, w_ffn_gate,
           w_ffn_up, w_ffn_down, norm_f_w):
    b, s, d = x.shape
    depth = w_in.shape[0]
    assert depth == 1, "the final norm is fused into the last layer's FFN kernel"
    out = _layer(x, norm1_w[0], w_in[0], pool_w[0], pool_scale[0], cmp_pe_k[0], cmp_w1_k[0], cmp_b1_k[0],
                 cmp_w2_k[0], cmp_pe_v[0], cmp_w1_v[0], cmp_b1_v[0], cmp_w2_v[0], w_proj_pool[0],
                 w_proj_attn[0], w_out[0], norm2_w[0], w_ffn_gate[0], w_ffn_up[0], w_ffn_down[0], norm_f_w)
    return out.reshape(b, s, d)
```

```python
import functools

import numpy as np
import jax
import jax.numpy as jnp
from jax import lax
from jax.experimental import pallas as pl
from jax.experimental.pallas import tpu as pltpu

D_MODEL = 1024
N_HEADS = 16
HEAD_DIM = 64
N_KV_GROUPS = 4
HEADS_PER_GROUP = N_HEADS // N_KV_GROUPS
ROPE_DIM = HEAD_DIM // 4
ROPE_HALF = ROPE_DIM // 2
ROPE_THETA = 500000.0
CMP_BLOCK = 32
CMP_STRIDE = 16
CMP_PER_SLC = 4
SLC_BLOCK = 64
SLC_TOPK = 16
WINDOW = 512
ATTN_Q_WIDTH = N_HEADS * HEAD_DIM
KV_WIDTH = N_KV_GROUPS * HEAD_DIM
POOL_WIDTH = D_MODEL // 2
POOL_WINDOWS = (2, 4, 8, 16)
POOL_GROUP = POOL_WIDTH // len(POOL_WINDOWS)
POOL_HALO = 16
RMS_EPS = 1e-6

LANE = 128
SUBLANE = 8
MXU_DTYPE = jnp.bfloat16
ACT_DTYPE = jnp.bfloat16
VMEM_LIMIT_BYTES = 56 * 1024 * 1024

GATE_NSA = N_HEADS * 3
COL_POOL = 0
COL_Q = COL_POOL + POOL_WIDTH
COL_KV = COL_Q + ATTN_Q_WIDTH
COL_GPOOL = COL_KV + 6 * KV_WIDTH
COL_GATTN = COL_GPOOL + D_MODEL
COL_GNSA = COL_GATTN + D_MODEL
PROJ_WIDTH = COL_GNSA + LANE
GATE_ROWS = 16

ROW_TILE = 512
Q_TILE = 256
SLC_SPAN = 4096
SLC_SUBTILE = 4096
KPREP_TILE = 1024
FF_CHUNK = 512
SCORE_SCALE = HEAD_DIM ** -0.5
LOG2E = float(np.log2(np.e))
FINITE_LIMIT = 3e38
STAB_MARGIN = 1.03
L_MIN = 2.0 ** -64
MASK_BIAS = float(2 ** 30)
NEG_BIG = -1e30
IMP_BIG = 1e30
CHOSEN = -3e38
CHOSEN_LIMIT = -2e38
N_FORCED = 3


def _log2(n):
    k = int(n).bit_length() - 1
    assert 1 << k == n, n
    return k


def _cparams(n_grid):
    return pltpu.CompilerParams(dimension_semantics=("arbitrary",) * n_grid,
                                vmem_limit_bytes=VMEM_LIMIT_BYTES)


def _const_spec(shape):
    nd = len(shape)
    return pl.BlockSpec(shape, lambda *_: (0,) * nd, pipeline_mode=pl.Buffered(1))


def _sigmoid(x):
    return 1.0 / (1.0 + jnp.exp(-x))


def _lost_range(total, smallest):
    return jnp.logical_not(jnp.sum(total) < FINITE_LIMIT) | (jnp.min(smallest) < L_MIN)


def _rms(x, w):
    var = jnp.mean(x * x, axis=-1, keepdims=True)
    return x * lax.rsqrt(var + RMS_EPS) * w


def _inproj_kernel(x_ref, nw_ref, wa_ref, wb_ref, wc_ref, o_ref, kvc_ref):
    h = _rms(x_ref[...], nw_ref[...]).astype(MXU_DTYPE)
    n = o_ref.shape[1]
    for c0 in range(0, n, KV_WIDTH):
        cw = min(KV_WIDTH, n - c0)
        w_ref, w0 = ((wa_ref, 0) if c0 < COL_GPOOL else (wb_ref, COL_GPOOL) if c0 < COL_GNSA
                     else (wc_ref, COL_GNSA))
        res = jnp.dot(h, w_ref[:, c0 - w0:c0 - w0 + cw],
                      preferred_element_type=jnp.float32).astype(o_ref.dtype)
        o_ref[:, c0:c0 + cw] = res
        which = (c0 - COL_KV) // KV_WIDTH
        if c0 >= COL_KV and which < 2:
            for g in range(N_KV_GROUPS):
                kvc_ref[which * N_KV_GROUPS + g] = res[:, g * HEAD_DIM:(g + 1) * HEAD_DIM]


def _inproj(x2d, norm_w, w_parts):
    m, d = x2d.shape
    n = sum(w.shape[1] for w in w_parts)
    assert n == PROJ_WIDTH
    return pl.pallas_call(
        _inproj_kernel,
        grid=(m // ROW_TILE,),
        in_specs=[pl.BlockSpec((ROW_TILE, d), lambda i: (i, 0)),
                  _const_spec((1, d))] + [_const_spec(w.shape) for w in w_parts],
        out_specs=[pl.BlockSpec((ROW_TILE, n), lambda i: (i, 0)),
                   pl.BlockSpec((2 * N_KV_GROUPS, ROW_TILE, HEAD_DIM), lambda i: (0, i, 0))],
        out_shape=[jax.ShapeDtypeStruct((m, n), ACT_DTYPE),
                   jax.ShapeDtypeStruct((2 * N_KV_GROUPS, m, HEAD_DIM), ACT_DTYPE)],
        compiler_params=_cparams(1),
        name="inproj",
    )(x2d, norm_w, *w_parts)


def _gelu_tanh(x):
    return 0.5 * x * (1.0 + jnp.tanh(np.sqrt(2.0 / np.pi) * (x + 0.044715 * (x * x * x))))


def _compress_one(r_ref, pe_ref, w1_ref, b1_ref, w2_ref):
    half = w1_ref.shape[0] // 2
    n_blk = r_ref.shape[2]
    rows = [r_ref[0, 0, :, r * half:(r + 1) * half] for r in range(CMP_PER_SLC)]
    top = [jnp.dot(x, w1_ref[0:half, :], preferred_element_type=jnp.float32) for x in rows]
    bot = [jnp.dot(x, w1_ref[half:, :], preferred_element_type=jnp.float32) for x in rows]
    bias = jnp.dot(pe_ref[...], w1_ref[...], preferred_element_type=jnp.float32)[0:1, :] + b1_ref[...]
    nxt = bot[1:] + [pltpu.roll(bot[0], n_blk - 1, 0)]
    hid = jnp.concatenate([top[r] + nxt[r] for r in range(CMP_PER_SLC)], axis=0) + bias
    return jnp.dot(_gelu_tanh(hid).astype(MXU_DTYPE), w2_ref[...], preferred_element_type=jnp.float32)


def _compress_kernel(rk_ref, rv_ref, pek_ref, w1k_ref, b1k_ref, w2k_ref,
                     pev_ref, w1v_ref, b1v_ref, w2v_ref, ok_ref, ovt_ref):
    kc = _compress_one(rk_ref, pek_ref, w1k_ref, b1k_ref, w2k_ref)
    vc = _compress_one(rv_ref, pev_ref, w1v_ref, b1v_ref, w2v_ref)
    ok_ref[0, 0] = jnp.concatenate([kc * (SCORE_SCALE * LOG2E), jnp.zeros_like(kc)], axis=1).astype(ok_ref.dtype)
    ovt_ref[0, 0] = jnp.concatenate([kc, vc], axis=1).T[HEAD_DIM:, :].astype(ovt_ref.dtype)


def _compress(kvc_rows, pek, w1k, b1k, w2k, pev, w1v, b1v, w2v):
    _, b, n_blk, width = kvc_rows.shape
    dk = w2k.shape[1]
    n_rows = n_blk * CMP_PER_SLC
    w_specs = [_const_spec(a.shape) for a in (pek, w1k, b1k, w2k)]
    return pl.pallas_call(
        _compress_kernel,
        grid=(b, N_KV_GROUPS),
        in_specs=[pl.BlockSpec((1, 1, n_blk, width), lambda i, j: (j, i, 0, 0)),
                  pl.BlockSpec((1, 1, n_blk, width), lambda i, j: (N_KV_GROUPS + j, i, 0, 0))]
        + w_specs + w_specs,
        out_specs=[pl.BlockSpec((1, 1, n_rows, 2 * dk), lambda i, j: (i, j, 0, 0)),
                   pl.BlockSpec((1, 1, dk, n_rows), lambda i, j: (i, j, 0, 0))],
        out_shape=[jax.ShapeDtypeStruct((b, N_KV_GROUPS, n_rows, 2 * dk), MXU_DTYPE),
                   jax.ShapeDtypeStruct((b, N_KV_GROUPS, dk, n_rows), MXU_DTYPE)],
        compiler_params=_cparams(2),
        name="compress",
    )(kvc_rows, kvc_rows, pek, w1k, b1k, w2k, pev, w1v, b1v, w2v)


def _kprep_kernel(ks_ref, vs_ref, kw_ref, vw_ref, cos_ref, sin_ref, rot_ref,
                  kaug_ref, vst_ref, kwin_ref, vwt_ref, knorm_ref):
    ts = ks_ref.shape[1]
    t0 = pl.program_id(1) * ts
    cos = jnp.concatenate([cos_ref[...]] * N_KV_GROUPS, axis=1)
    sin = jnp.concatenate([sin_ref[...]] * N_KV_GROUPS, axis=1)
    rot = rot_ref[...]

    def rope(k_ref):
        k = k_ref[0]
        krot = jnp.dot(k.astype(MXU_DTYPE), rot, preferred_element_type=jnp.float32)
        return (k.astype(jnp.float32) * cos + krot * sin) * (SCORE_SCALE * LOG2E)

    n_blk = kaug_ref.shape[3] - 2 * HEAD_DIM
    blk = lax.broadcasted_iota(jnp.int32, (ts, n_blk), 1)
    key_blk = (t0 + lax.broadcasted_iota(jnp.int32, (ts, n_blk), 0)) >> _log2(SLC_BLOCK)
    onehot = jnp.where(blk == key_blk, 1.0, 0.0).astype(kaug_ref.dtype)
    ks_f = rope(ks_ref)
    kw_f = rope(kw_ref)
    ks = ks_f.astype(kaug_ref.dtype)
    kw = kw_f.astype(kwin_ref.dtype)

    seg_l = lax.broadcasted_iota(jnp.int32, (KV_WIDTH, LANE), 0) >> _log2(HEAD_DIM)
    seg_c = lax.broadcasted_iota(jnp.int32, (KV_WIDTH, LANE), 1)
    sq = (jnp.dot(ks_f * ks_f, jnp.where(seg_c == seg_l, 1.0, 0.0), preferred_element_type=jnp.float32)
          + jnp.dot(kw_f * kw_f, jnp.where(seg_c == seg_l + N_KV_GROUPS, 1.0, 0.0),
                    preferred_element_type=jnp.float32))
    tile_max = jnp.broadcast_to(jnp.max(sq, axis=0, keepdims=True), knorm_ref.shape[1:])

    @pl.when(pl.program_id(1) == 0)
    def _():
        knorm_ref[0] = tile_max

    @pl.when(pl.program_id(1) > 0)
    def _():
        knorm_ref[0] = jnp.maximum(knorm_ref[0], tile_max)

    vst = vs_ref[0].astype(jnp.float32).T.astype(vst_ref.dtype)
    vwt = vw_ref[0].astype(jnp.float32).T.astype(vwt_ref.dtype)
    ones_col = jnp.where(lax.broadcasted_iota(jnp.int32, (ts, HEAD_DIM), 1) == 0, 1.0, 0.0
                         ).astype(kaug_ref.dtype)
    for g in range(N_KV_GROUPS):
        sl = slice(g * HEAD_DIM, (g + 1) * HEAD_DIM)
        kaug_ref[0, g, :, 0:n_blk] = onehot
        kaug_ref[0, g, :, n_blk:n_blk + HEAD_DIM] = ks[:, sl]
        kaug_ref[0, g, :, n_blk + HEAD_DIM:] = ones_col
        kwin_ref[0, g, :, 0:HEAD_DIM] = kw[:, sl]
        kwin_ref[0, g, :, HEAD_DIM:] = ones_col
        vst_ref[0, g] = vst[sl, :]
        vwt_ref[0, g] = vwt[sl, :]


def _kprep(proj3, cos_k, sin_k, rot):
    b, s, _ = proj3.shape
    n_blk = s // SLC_BLOCK
    ts = min(KPREP_TILE, s)
    kvb = COL_KV // KV_WIDTH

    def col(j):
        return pl.BlockSpec((1, ts, KV_WIDTH), lambda i, t: (i, t, kvb + j))

    tab = pl.BlockSpec((ts, HEAD_DIM), lambda i, t: (t, 0))
    aug_w = n_blk + 2 * HEAD_DIM
    return pl.pallas_call(
        _kprep_kernel,
        grid=(b, s // ts),
        in_specs=[col(2), col(3), col(4), col(5), tab, tab, _const_spec(rot.shape)],
        out_specs=[pl.BlockSpec((1, N_KV_GROUPS, ts, aug_w), lambda i, t: (i, 0, t, 0)),
                   pl.BlockSpec((1, N_KV_GROUPS, HEAD_DIM, ts), lambda i, t: (i, 0, 0, t)),
                   pl.BlockSpec((1, N_KV_GROUPS, ts, 2 * HEAD_DIM), lambda i, t: (i, 0, t, 0)),
                   pl.BlockSpec((1, N_KV_GROUPS, HEAD_DIM, ts), lambda i, t: (i, 0, 0, t)),
                   pl.BlockSpec((1, SUBLANE, LANE), lambda i, t: (i, 0, 0))],
        out_shape=[jax.ShapeDtypeStruct((b, N_KV_GROUPS, s, aug_w), MXU_DTYPE),
                   jax.ShapeDtypeStruct((b, N_KV_GROUPS, HEAD_DIM, s), MXU_DTYPE),
                   jax.ShapeDtypeStruct((b, N_KV_GROUPS, s, 2 * HEAD_DIM), MXU_DTYPE),
                   jax.ShapeDtypeStruct((b, N_KV_GROUPS, HEAD_DIM, s), MXU_DTYPE),
                   jax.ShapeDtypeStruct((b, SUBLANE, LANE), jnp.float32)],
        compiler_params=_cparams(2),
        name="kprep",
    )(proj3, proj3, proj3, proj3, cos_k, sin_k, rot)


def _heads_to_lanes(xt):
    return jnp.concatenate([xt[h * HEAD_DIM:(h + 1) * HEAD_DIM, :] for h in range(HEADS_PER_GROUP)],
                           axis=1)


def _vis_rows(tq):
    levels = tq // CMP_STRIDE
    dt = np.arange(tq)
    v = (dt - (CMP_BLOCK - 1)) // CMP_STRIDE - (-(CMP_BLOCK - 1)) // CMP_STRIDE
    rows = np.zeros((HEAD_DIM, tq), np.float32)
    for c in range(1, levels + 1):
        rows[c] = np.where(v < c, -MASK_BIAS, 0.0)
    rows[levels + 1] = -MASK_BIAS
    return jnp.asarray(np.tile(rows, (1, HEADS_PER_GROUP)), MXU_DTYPE)


def _cmpsel_kernel(q_ref, kc_ref, vct_ref, vis_ref, ocmp_ref, bias_ref):
    tq = q_ref.shape[1]
    n_blk = bias_ref.shape[2]
    m_rows = HEADS_PER_GROUP * tq
    t0 = pl.program_id(1) * tq
    levels = tq // CMP_STRIDE

    def body(nr):
        n_tok = CMP_PER_SLC * nr
        j_idx = lax.broadcasted_iota(jnp.int32, (nr, LANE), 0)
        n_idx = jnp.concatenate([j_idx * CMP_PER_SLC + r for r in range(CMP_PER_SLC)], axis=0)
        n_first = (t0 - (CMP_BLOCK - 1)) >> _log2(CMP_STRIDE)
        u = jnp.clip(n_idx - n_first, 0, levels + 1)
        vis_cols = jnp.where(lax.broadcasted_iota(jnp.int32, (n_tok, LANE), 1) - HEAD_DIM == u,
                             1.0, 0.0).astype(MXU_DTYPE)
        vis_rows = vis_ref[...]
        t_row = t0 + (lax.broadcasted_iota(jnp.int32, (1, m_rows), 1) & (tq - 1))
        jrow = lax.broadcasted_iota(jnp.int32, (nr, tq), 0)
        jt = (t0 + lax.broadcasted_iota(jnp.int32, (nr, tq), 1)) >> _log2(SLC_BLOCK)
        causal_blk = jrow <= jt
        forced = (jrow == 0) | (jrow == jt) | (jrow == jt - 1)
        jf = jrow.astype(jnp.float32)
        n_extract = min(SLC_TOPK, n_blk) - N_FORCED
        jt_row = (t0 + lax.broadcasted_iota(jnp.int32, (1, tq), 1)) >> _log2(SLC_BLOCK)
        want = jnp.minimum(jt_row + 1, SLC_TOPK).astype(jnp.float32)

        def store_bias(pb, marks):
            bias_ref[pb, 0, 0:nr, :] = jnp.where(marks < CHOSEN_LIMIT, 0.0, -MASK_BIAS).astype(bias_ref.dtype)

        pending = []
        for pb in range(q_ref.shape[0]):
            kc = jnp.concatenate([kc_ref[pb, 0, r * n_blk:r * n_blk + nr, :] for r in range(CMP_PER_SLC)],
                                 axis=0)
            vct = jnp.concatenate([vct_ref[pb, 0, :, r * n_blk:r * n_blk + nr] for r in range(CMP_PER_SLC)],
                                  axis=1)
            qt = _heads_to_lanes(q_ref[pb].astype(jnp.float32).T).astype(MXU_DTYPE)
            s = jnp.dot(kc + vis_cols, jnp.concatenate([qt, vis_rows], axis=0),
                        preferred_element_type=jnp.float32)
            mx = jnp.max(s, axis=0, keepdims=True)
            p = jnp.exp2(s - mx)
            den = jnp.sum(p, axis=0, keepdims=True)
            inv = jnp.where(t_row >= CMP_BLOCK - 1, 1.0 / den, 0.0)
            ocmp_ref[pb, 0, 0] = inv * jnp.dot(vct, p.astype(MXU_DTYPE), preferred_element_type=jnp.float32)

            imp = None
            for h in range(HEADS_PER_GROUP):
                ph = p[:, h * tq:(h + 1) * tq]
                last = ph[3 * nr:4 * nr, :]
                prev = jnp.where(jrow == 0, 0.0, pltpu.roll(last, 1, 0))
                taps = ph[0:nr, :] + ph[nr:2 * nr, :] + ph[2 * nr:3 * nr, :] + last + prev
                imp_h = taps * inv[:, h * tq:(h + 1) * tq]
                imp = imp_h if imp is None else imp + imp_h

            val = jnp.where(forced, CHOSEN, jnp.where(causal_blk, imp, -IMP_BIG))
            marks = val
            for _ in range(n_extract):
                marks = jnp.where(marks == jnp.max(marks, axis=0, keepdims=True), CHOSEN, marks)
            store_bias(pb, marks)
            if nr < n_blk:
                bias_ref[pb, 0, nr:, :] = jnp.full((n_blk - nr, tq), -MASK_BIAS, bias_ref.dtype)
            count = jnp.sum(jnp.where((marks < CHOSEN_LIMIT) & causal_blk, 1.0, 0.0), axis=0, keepdims=True)
            pending.append((pb, val, jnp.max(jnp.abs(count - want)) > 0.5))

        for pb, val, tied in pending:
            @pl.when(tied)
            def _(val=val, pb=pb):
                v = val
                for _ in range(n_extract):
                    top = jnp.max(v, axis=0, keepdims=True)
                    first = jnp.min(jnp.where(v == top, jf, float(n_blk)), axis=0, keepdims=True)
                    v = jnp.where(jf == first, CHOSEN, v)
                store_bias(pb, v)

    sizes = [n_blk * f // 4 for f in (1, 2, 3, 4) if (n_blk * f // 4) % 16 == 0]
    lo = 0
    for nr in sizes:
        hi = nr * SLC_BLOCK if nr < n_blk else None
        end = t0 + tq
        cond = (end > lo) if hi is None else (end > lo) & (end <= hi)
        pl.when(cond)(functools.partial(body, nr))
        lo = hi


def _cmpsel(proj3, kc_r, vc_rt):
    b, s, _ = proj3.shape
    tq = min(Q_TILE, s)
    nq = s // tq
    n_cmp_pad = kc_r.shape[2]
    n_blk = s // SLC_BLOCK
    qb = COL_Q // KV_WIDTH
    m_rows = HEADS_PER_GROUP * tq
    return pl.pallas_call(
        _cmpsel_kernel,
        grid=(N_KV_GROUPS, nq),
        in_specs=[pl.BlockSpec((b, tq, KV_WIDTH), lambda g, q: (0, q, qb + g)),
                  pl.BlockSpec((b, 1, n_cmp_pad, LANE), lambda g, q: (0, g, 0, 0)),
                  pl.BlockSpec((b, 1, HEAD_DIM, n_cmp_pad), lambda g, q: (0, g, 0, 0)),
                  _const_spec((HEAD_DIM, m_rows))],
        out_specs=[pl.BlockSpec((b, 1, 1, HEAD_DIM, m_rows), lambda g, q: (0, g, q, 0, 0)),
                   pl.BlockSpec((b, 1, n_blk, tq), lambda g, q: (0, g, 0, q))],
        out_shape=[jax.ShapeDtypeStruct((b, N_KV_GROUPS, nq, HEAD_DIM, m_rows), jnp.float32),
                   jax.ShapeDtypeStruct((b, N_KV_GROUPS, n_blk, s), MXU_DTYPE)],
        compiler_params=_cparams(2),
        name="cmpsel",
    )(proj3, kc_r, vc_rt, _vis_rows(tq))


def _flash_kernel(q_ref, cos_ref, sin_ref, bias_ref, kaug_ref, vst_ref, kwin_ref, vwt_ref, knorm_ref,
                  wmask_ref, gate_ref, ocmp_ref, o_ref, m_ref, l_ref, acc_ref, lw_ref, accw_ref):
    nb = q_ref.shape[0]
    tq = q_ref.shape[1]
    m_rows = HEADS_PER_GROUP * tq
    grp = pl.program_id(0)
    qi = pl.program_id(1)
    t0 = qi * tq
    kd = pl.multiple_of(t0, tq)
    probs = range(nb)
    zpad = jnp.zeros((HEAD_DIM, m_rows), MXU_DTYPE)
    pad_row = lax.broadcasted_iota(jnp.int32, (HEAD_DIM, m_rows), 0)
    lane = lax.broadcasted_iota(jnp.int32, (1, LANE), 1)

    def stab_rows(neg_m):
        return jnp.where(pad_row == 0, neg_m, 0.0).astype(MXU_DTYPE)

    cos = cos_ref[...]
    sin = sin_ref[...]
    qr, bias4, qs_slc, qs_win = [], [], [], []
    for pb in probs:
        qt = q_ref[pb].astype(jnp.float32).T
        parts = []
        for h in range(HEADS_PER_GROUP):
            qh = qt[h * HEAD_DIM:(h + 1) * HEAD_DIM, :]
            x1 = qh[0:ROPE_HALF, :]
            x2 = qh[ROPE_HALF:ROPE_DIM, :]
            parts.append(jnp.concatenate([x1 * cos - x2 * sin, x2 * cos + x1 * sin, qh[ROPE_DIM:, :]],
                                         axis=0))
        qf = jnp.concatenate(parts, axis=1)
        q_sq = jnp.sum(qf * qf, axis=0, keepdims=True)
        k_sq = knorm_ref[pb, 0:1, :]
        ks_sq = jnp.max(jnp.where(lane == grp, k_sq, 0.0), axis=1, keepdims=True)
        kw_sq = jnp.max(jnp.where(lane == grp + N_KV_GROUPS, k_sq, 0.0), axis=1, keepdims=True)
        qr.append(qf.astype(MXU_DTYPE))
        bias4.append(jnp.concatenate([bias_ref[pb, 0]] * HEADS_PER_GROUP, axis=1))
        qs_slc.append(jnp.concatenate([bias4[pb], qr[pb], stab_rows(-STAB_MARGIN * jnp.sqrt(q_sq * ks_sq))],
                                      axis=0))
        qs_win.append(jnp.concatenate([qr[pb], stab_rows(-STAB_MARGIN * jnp.sqrt(q_sq * kw_sq))], axis=0))

    n_back = WINDOW // tq

    def win_tile(back):
        if back == 0:
            return kd, tq, wmask_ref[0]
        inside = wmask_ref[back] if back * tq + tq - 1 >= WINDOW else 0.0
        return (pl.multiple_of(jnp.maximum(t0 - back * tq, 0), tq), tq,
                jnp.where(qi >= back, inside, NEG_BIG))

    def fast_tiles(k_ref, v_ref, qs, tiles, l_out, acc_out, first=False):
        for pb in probs:
            l_add = acc_add = None
            for k0, tk, mask in tiles:
                s = jnp.dot(k_ref[pb, 0, pl.ds(k0, tk), :], qs[pb], preferred_element_type=jnp.float32)
                if mask is not None:
                    s = s + mask
                p = jnp.exp2(s)
                l_t = jnp.sum(p, axis=0, keepdims=True)
                acc_t = jnp.dot(v_ref[pb, 0, :, pl.ds(k0, tk)], p.astype(MXU_DTYPE),
                                preferred_element_type=jnp.float32)
                l_add = l_t if l_add is None else l_add + l_t
                acc_add = acc_t if acc_add is None else acc_add + acc_t
            if first:
                l_out[pb] = l_add
                acc_out[pb] = acc_add
            else:
                l_out[pb] += l_add
                acc_out[pb] += acc_add

    fast_tiles(kaug_ref, vst_ref, qs_slc, [(kd, tq, wmask_ref[0])], l_ref, acc_ref, first=True)
    fast_tiles(kwin_ref, vwt_ref, qs_win, [win_tile(back) for back in range(n_back + 1)],
               lw_ref, accw_ref, first=True)

    sub = SLC_SUBTILE
    n_sub = SLC_SPAN // sub

    def span_body(i, carry):
        base = i * SLC_SPAN
        fast_tiles(kaug_ref, vst_ref, qs_slc,
                   [(pl.multiple_of(base + j * sub, sub), sub, None) for j in range(n_sub)],
                   l_ref, acc_ref)
        return carry

    n_span = t0 // SLC_SPAN
    lax.fori_loop(0, n_span, span_body, 0)
    done = n_span * SLC_SPAN
    size = SLC_SPAN // 2
    while size >= tq:
        take = ((t0 // size) % 2) == 1

        @pl.when(take)
        def _(done=done, size=size):
            piece = min(size, sub)
            fast_tiles(kaug_ref, vst_ref, qs_slc,
                       [(pl.multiple_of(done + j * piece, piece), piece, None) for j in range(size // piece)],
                       l_ref, acc_ref)

        done = done + jnp.where(take, size, 0)
        size //= 2

    def exact_start(pb, s, vt, l_out, acc_out):
        mx = jnp.max(s, axis=0, keepdims=True)
        p = jnp.exp2(s - mx)
        m_ref[pb] = mx
        l_out[pb] = jnp.sum(p, axis=0, keepdims=True)
        acc_out[pb] = jnp.dot(vt, p.astype(MXU_DTYPE), preferred_element_type=jnp.float32)

    def exact_update(pb, s, vt, l_out, acc_out):
        m_old = m_ref[pb]
        mx = jnp.maximum(m_old, jnp.max(s, axis=0, keepdims=True))
        alpha = jnp.exp2(m_old - mx)
        p = jnp.exp2(s - mx)
        m_ref[pb] = mx
        l_out[pb] = alpha * l_out[pb] + jnp.sum(p, axis=0, keepdims=True)
        acc_out[pb] = alpha * acc_out[pb] + jnp.dot(vt, p.astype(MXU_DTYPE),
                                                    preferred_element_type=jnp.float32)

    r_id = lax.broadcasted_iota(jnp.int32, (GATE_ROWS, LANE), 0)
    c_id = lax.broadcasted_iota(jnp.int32, (GATE_ROWS, LANE), 1)
    pick = jnp.where((c_id == grp * (HEADS_PER_GROUP * 3) + r_id) & (r_id < HEADS_PER_GROUP * 3),
                     1.0, 0.0).astype(gate_ref.dtype)

    def finish():
        total = None
        for pb in probs:
            o_slc = acc_ref[pb] / l_ref[pb]
            o_win = accw_ref[pb] / lw_ref[pb]
            graw = lax.dot_general(pick, gate_ref[pb], (((1,), (1,)), ((), ())),
                                   preferred_element_type=jnp.float32)
            gates = _sigmoid(graw)
            o_cmp = ocmp_ref[pb, 0, 0]
            mixed = []
            for h in range(HEADS_PER_GROUP):
                sl = slice(h * tq, (h + 1) * tq)
                mixed.append(gates[3 * h:3 * h + 1, :] * o_cmp[:, sl]
                             + gates[3 * h + 1:3 * h + 2, :] * o_slc[:, sl]
                             + gates[3 * h + 2:3 * h + 3, :] * o_win[:, sl])
            out = jnp.concatenate(mixed, axis=0)
            o_ref[pb] = out.T.astype(o_ref.dtype)
            part = jnp.sum(jnp.abs(out), axis=0, keepdims=True)
            total = part if total is None else total + part
        return total

    check = finish()
    l_low = None
    for pb in probs:
        l_both = l_ref[pb] + lw_ref[pb]
        for h in range(HEADS_PER_GROUP):
            check = check + l_both[:, h * tq:(h + 1) * tq]
        lo = jnp.minimum(l_ref[pb], lw_ref[pb])
        l_low = lo if l_low is None else jnp.minimum(l_low, lo)

    @pl.when(_lost_range(check, l_low))
    def _():
        tcol = t0 + (lax.broadcasted_iota(jnp.int32, (tq, m_rows), 1) & (tq - 1))
        koff = lax.broadcasted_iota(jnp.int32, (tq, m_rows), 0)
        causal = t0 + koff <= tcol
        for pb in probs:
            q0 = jnp.concatenate([bias4[pb], qr[pb], zpad], axis=0)
            s = jnp.dot(kaug_ref[pb, 0, pl.ds(kd, tq), :], q0, preferred_element_type=jnp.float32)
            exact_start(pb, jnp.where(causal, s, NEG_BIG), vst_ref[pb, 0, :, pl.ds(kd, tq)], l_ref, acc_ref)

            def body(kt, carry, pb=pb, q0=q0):
                k0 = pl.multiple_of(kt * tq, tq)
                sk = jnp.dot(kaug_ref[pb, 0, pl.ds(k0, tq), :], q0, preferred_element_type=jnp.float32)
                exact_update(pb, sk, vst_ref[pb, 0, :, pl.ds(k0, tq)], l_ref, acc_ref)
                return carry

            lax.fori_loop(0, qi, body, 0)
            qw0 = jnp.concatenate([qr[pb], zpad], axis=0)
            for back in range(n_back + 1):
                kbase = t0 - back * tq
                k0 = pl.multiple_of(jnp.maximum(kbase, 0), tq)
                sw = jnp.dot(kwin_ref[pb, 0, pl.ds(k0, tq), :], qw0, preferred_element_type=jnp.float32)
                diff = tcol - (kbase + koff)
                sw = jnp.where((diff >= 0) & (diff < WINDOW) & (kbase + koff >= 0), sw, NEG_BIG)
                if back == 0:
                    exact_start(pb, sw, vwt_ref[pb, 0, :, pl.ds(k0, tq)], lw_ref, accw_ref)
                else:
                    exact_update(pb, sw, vwt_ref[pb, 0, :, pl.ds(k0, tq)], lw_ref, accw_ref)
        finish()


def _window_masks(tq):
    k = np.arange(tq)[:, None]
    t = np.arange(tq)[None, :]
    tiles = []
    for back in range(WINDOW // tq + 1):
        diff = t + back * tq - k
        tiles.append(np.tile(np.where((diff >= 0) & (diff < WINDOW), 0.0, NEG_BIG), (1, HEADS_PER_GROUP)))
    return jnp.asarray(np.stack(tiles), jnp.float32)


def _flash(proj3, cos_q, sin_q, bias, kaug, vst, kwin, vwt, knorm, ocmp):
    b, s, _ = proj3.shape
    tq = min(Q_TILE, s)
    nq = s // tq
    n_blk = s // SLC_BLOCK
    aug_w = kaug.shape[3]
    qb = COL_Q // KV_WIDTH
    m_rows = HEADS_PER_GROUP * tq

    wmask = _window_masks(tq)

    def resident(shape):
        return pl.BlockSpec((b, 1) + shape, lambda g, q: (0, g, 0, 0))

    return pl.pallas_call(
        _flash_kernel,
        grid=(N_KV_GROUPS, nq),
        in_specs=[pl.BlockSpec((b, tq, KV_WIDTH), lambda g, q: (0, q, qb + g)),
                  pl.BlockSpec((ROPE_HALF, tq), lambda g, q: (0, q)),
                  pl.BlockSpec((ROPE_HALF, tq), lambda g, q: (0, q)),
                  pl.BlockSpec((b, 1, n_blk, tq), lambda g, q: (0, g, 0, q)),
                  resident((s, aug_w)),
                  resident((HEAD_DIM, s)),
                  resident((s, 2 * HEAD_DIM)),
                  resident((HEAD_DIM, s)),
                  _const_spec((b, SUBLANE, LANE)),
                  _const_spec(wmask.shape),
                  pl.BlockSpec((b, tq, LANE), lambda g, q: (0, q, COL_GNSA // LANE)),
                  pl.BlockSpec((b, 1, 1, HEAD_DIM, m_rows), lambda g, q: (0, g, q, 0, 0))],
        out_specs=pl.BlockSpec((b, tq, KV_WIDTH), lambda g, q: (0, q, g)),
        out_shape=jax.ShapeDtypeStruct((b, s, ATTN_Q_WIDTH), ACT_DTYPE),
        scratch_shapes=[pltpu.VMEM((b, 1, m_rows), jnp.float32),
                        pltpu.VMEM((b, 1, m_rows), jnp.float32),
                        pltpu.VMEM((b, HEAD_DIM, m_rows), jnp.float32),
                        pltpu.VMEM((b, 1, m_rows), jnp.float32),
                        pltpu.VMEM((b, HEAD_DIM, m_rows), jnp.float32)],
        compiler_params=_cparams(2),
        name="flash",
    )(proj3, cos_q, sin_q, bias, kaug, vst, kwin, vwt, knorm, wmask, proj3, ocmp)


def _tail_kernel(u_ref, halo_ref, gp_ref, ga_ref, attn_ref, x_ref, pw_ref, ps_ref, wpp_ref, wpa_ref, wo_ref,
                 n2_ref, wg_ref, wu_ref, wd_ref, nf_ref, o_ref, *, seq_len):
    tm = u_ref.shape[0]
    t0 = (pl.program_id(0) * tm) & (seq_len - 1)

    u = u_ref[...].astype(jnp.float32)
    halo = jnp.where(t0 > 0, halo_ref[...].astype(jnp.float32), 0.0)
    ext = jnp.concatenate([halo, u], axis=0)
    t = t0 + lax.broadcasted_iota(jnp.int32, (tm, POOL_GROUP), 0)
    pooled = []
    for gi, w in enumerate(POOL_WINDOWS):
        sl = slice(gi * POOL_GROUP, (gi + 1) * POOL_GROUP)
        acc = ext[:, sl]
        span = 1
        while span < w:
            acc = acc + pltpu.roll(acc, span, 0)
            span *= 2
        cnt = jnp.minimum(t + 1, w).astype(jnp.float32)
        mean = acc[POOL_HALO:, :] / cnt
        mixed = jnp.dot((mean - u[:, sl]).astype(MXU_DTYPE), pw_ref[gi],
                        preferred_element_type=jnp.float32)
        pooled.append(mixed * ps_ref[:, sl])
    pool = jnp.concatenate(pooled, axis=1).astype(MXU_DTYPE)

    pp = jnp.dot(pool, wpp_ref[...], preferred_element_type=jnp.float32)
    pa = jnp.dot(attn_ref[...], wpa_ref[...], preferred_element_type=jnp.float32)
    merged = (_sigmoid(gp_ref[...].astype(jnp.float32)) * pp
              + _sigmoid(ga_ref[...].astype(jnp.float32)) * pa)
    x1 = x_ref[...] + jnp.dot(merged.astype(MXU_DTYPE), wo_ref[...], preferred_element_type=jnp.float32)

    h = _rms(x1, n2_ref[...]).astype(MXU_DTYPE)
    d_ff = wg_ref.shape[1]
    out = x1
    for c0 in range(0, d_ff, FF_CHUNK):
        cw = min(FF_CHUNK, d_ff - c0)
        gate = jnp.dot(h, wg_ref[:, c0:c0 + cw], preferred_element_type=jnp.float32)
        up = jnp.dot(h, wu_ref[:, c0:c0 + cw], preferred_element_type=jnp.float32)
        act = (gate * _sigmoid(gate) * up).astype(MXU_DTYPE)
        out = out + jnp.dot(act, wd_ref[c0:c0 + cw, :], preferred_element_type=jnp.float32)
    o_ref[...] = _rms(out, nf_ref[...])


def _tail(proj2, attn2, x2d, pool_w, pool_scale, wpp, wpa, wo, norm2_w, wg, wu, wd, norm_f_w, seq_len):
    m, d = x2d.shape
    tm = min(ROW_TILE, seq_len)
    halo_per_tile = tm // POOL_HALO
    weights = (pool_w, pool_scale, wpp, wpa, wo, norm2_w, wg, wu, wd, norm_f_w)
    return pl.pallas_call(
        functools.partial(_tail_kernel, seq_len=seq_len),
        grid=(m // tm,),
        in_specs=[pl.BlockSpec((tm, POOL_WIDTH), lambda i: (i, COL_POOL // POOL_WIDTH)),
                  pl.BlockSpec((POOL_HALO, POOL_WIDTH),
                               lambda i: (jnp.maximum(i * halo_per_tile - 1, 0), COL_POOL // POOL_WIDTH)),
                  pl.BlockSpec((tm, d), lambda i: (i, COL_GPOOL // D_MODEL)),
                  pl.BlockSpec((tm, d), lambda i: (i, COL_GATTN // D_MODEL)),
                  pl.BlockSpec((tm, ATTN_Q_WIDTH), lambda i: (i, 0)),
                  pl.BlockSpec((tm, d), lambda i: (i, 0))] + [_const_spec(w.shape) for w in weights],
        out_specs=pl.BlockSpec((tm, d), lambda i: (i, 0)),
        out_shape=jax.ShapeDtypeStruct((m, d), jnp.float32),
        compiler_params=_cparams(1),
        name="tail",
    )(proj2, proj2, proj2, proj2, attn2, x2d, *weights)


def _rope_tables(seq_len):
    inv_freq = 1.0 / (ROPE_THETA ** (jnp.arange(0, ROPE_DIM, 2, dtype=jnp.float32) / ROPE_DIM))
    ang = inv_freq[:, None] * jnp.arange(seq_len).astype(jnp.float32)[None, :]
    cos_t, sin_t = jnp.cos(ang), jnp.sin(ang)
    rest = HEAD_DIM - ROPE_DIM
    cos_h = jnp.concatenate([cos_t, cos_t, jnp.ones((rest, seq_len), jnp.float32)], axis=0).T
    sin_h = jnp.concatenate([sin_t, sin_t, jnp.zeros((rest, seq_len), jnp.float32)], axis=0).T
    return cos_t, sin_t, cos_h, sin_h


def _rotate_half_matrix():
    rot = np.zeros((KV_WIDTH, KV_WIDTH), np.float32)
    for l in range(KV_WIDTH):
        d = l % HEAD_DIM
        if d < ROPE_HALF:
            rot[l + ROPE_HALF, l] = -1.0
        elif d < ROPE_DIM:
            rot[l - ROPE_HALF, l] = 1.0
    return jnp.asarray(rot, MXU_DTYPE)


def _layer(x, norm1_w, w_in, pool_w, pool_scale, cmp_pe_k, cmp_w1_k, cmp_b1_k, cmp_w2_k,
           cmp_pe_v, cmp_w1_v, cmp_b1_v, cmp_w2_v, w_proj_pool, w_proj_attn, w_out,
           norm2_w, w_ffn_gate, w_ffn_up, w_ffn_down, norm_f_w):
    b, s, d = x.shape
    m = b * s
    n_blk = s // SLC_BLOCK
    assert s % Q_TILE == 0 and s % ROW_TILE == 0 and (s & (s - 1)) == 0, s
    cd = MXU_DTYPE

    g0 = COL_GPOOL
    w_parts = (w_in[:, :g0].astype(cd), w_in[:, g0 + GATE_NSA:].astype(cd),
               jnp.pad(w_in[:, g0:g0 + GATE_NSA], ((0, 0), (0, LANE - GATE_NSA))).astype(cd))
    x2d = x.reshape(m, d)
    proj2, kvc = _inproj(x2d, norm1_w.reshape(1, d), w_parts)
    proj3 = proj2.reshape(b, s, PROJ_WIDTH)

    def pe8(pe):
        return jnp.broadcast_to(pe.reshape(1, -1), (8, pe.size)).astype(cd)

    kvc_rows = kvc.reshape(2 * N_KV_GROUPS, b, n_blk, SLC_BLOCK * HEAD_DIM)
    kc_r, vc_rt = _compress(kvc_rows,
                            pe8(cmp_pe_k), cmp_w1_k.astype(cd), cmp_b1_k.reshape(1, -1), cmp_w2_k.astype(cd),
                            pe8(cmp_pe_v), cmp_w1_v.astype(cd), cmp_b1_v.reshape(1, -1), cmp_w2_v.astype(cd))

    cos_q, sin_q, cos_k, sin_k = _rope_tables(s)
    kaug, vst, kwin, vwt, knorm = _kprep(proj3, cos_k, sin_k, _rotate_half_matrix())
    ocmp, bias = _cmpsel(proj3, kc_r, vc_rt)
    attn = _flash(proj3, cos_q, sin_q, bias, kaug, vst, kwin, vwt, knorm, ocmp)
    return _tail(proj2, attn.reshape(m, ATTN_Q_WIDTH), x2d, pool_w.astype(cd),
                 pool_scale.reshape(1, -1), w_proj_pool.astype(cd), w_proj_attn.astype(cd),
                 w_out.astype(cd), norm2_w.reshape(1, d), w_ffn_gate.astype(cd), w_ffn_up.astype(cd),
                 w_ffn_down.astype(cd), norm_f_w.reshape(1, d), s)


def kernel(x, norm1_w, w_in, pool_w, pool_scale, cmp_pe_k, cmp_w1_k, cmp_b1_k, cmp_w2_k, cmp_pe_v,
           cmp_w1_v, cmp_b1_v, cmp_w2_v, w_proj_pool, w_proj_attn, w_out, norm2_w, w_ffn_gate,
           w_ffn_up, w_ffn_down, norm_f_w):
    b, s, d = x.shape
    depth = w_in.shape[0]
    assert depth == 1, "the final norm is fused into the last layer's FFN kernel"
    out = _layer(x, norm1_w[0], w_in[0], pool_w[0], pool_scale[0], cmp_pe_k[0], cmp_w1_k[0], cmp_b1_k[0],
                 cmp_w2_k[0], cmp_pe_v[0], cmp_w1_v[0], cmp_b1_v[0], cmp_w2_v[0], w_proj_pool[0],
                 w_proj_attn[0], w_out[0], norm2_w[0], w_ffn_gate[0], w_ffn_up[0], w_ffn_down[0], norm_f_w)
    return out.reshape(b, s, d)
```

```python
import functools

import numpy as np
import jax
import jax.numpy as jnp
from jax import lax
from jax.experimental import pallas as pl
from jax.experimental.pallas import tpu as pltpu

D_MODEL = 1024
N_HEADS = 16
HEAD_DIM = 64
N_KV_GROUPS = 4
HEADS_PER_GROUP = N_HEADS // N_KV_GROUPS
ROPE_DIM = HEAD_DIM // 4
ROPE_HALF = ROPE_DIM // 2
ROPE_THETA = 500000.0
CMP_BLOCK = 32
CMP_STRIDE = 16
CMP_PER_SLC = 4
SLC_BLOCK = 64
SLC_TOPK = 16
WINDOW = 512
ATTN_Q_WIDTH = N_HEADS * HEAD_DIM
KV_WIDTH = N_KV_GROUPS * HEAD_DIM
POOL_WIDTH = D_MODEL // 2
POOL_WINDOWS = (2, 4, 8, 16)
POOL_GROUP = POOL_WIDTH // len(POOL_WINDOWS)
POOL_HALO = 16
RMS_EPS = 1e-6

LANE = 128
SUBLANE = 8
MXU_DTYPE = jnp.bfloat16
ACT_DTYPE = jnp.bfloat16
VMEM_LIMIT_BYTES = 56 * 1024 * 1024

GATE_NSA = N_HEADS * 3
COL_POOL = 0
COL_Q = COL_POOL + POOL_WIDTH
COL_KV = COL_Q + ATTN_Q_WIDTH
COL_GPOOL = COL_KV + 6 * KV_WIDTH
COL_GATTN = COL_GPOOL + D_MODEL
COL_GNSA = COL_GATTN + D_MODEL
PROJ_WIDTH = COL_GNSA + LANE
GATE_ROWS = 16

ROW_TILE = 512
Q_TILE = 256
SLC_SPAN = 4096
SLC_SUBTILE = 4096
KPREP_TILE = 1024
FF_CHUNK = 512
SCORE_SCALE = HEAD_DIM ** -0.5
LOG2E = float(np.log2(np.e))
FINITE_LIMIT = 3e38
STAB_MARGIN = 1.03
L_MIN = 2.0 ** -64
MASK_BIAS = float(2 ** 30)
NEG_BIG = -1e30
IMP_BIG = 1e30
CHOSEN = -3e38
CHOSEN_LIMIT = -2e38
N_FORCED = 3


def _log2(n):
    k = int(n).bit_length() - 1
    assert 1 << k == n, n
    return k


def _cparams(n_grid):
    return pltpu.CompilerParams(dimension_semantics=("arbitrary",) * n_grid,
                                vmem_limit_bytes=VMEM_LIMIT_BYTES)


def _const_spec(shape):
    nd = len(shape)
    return pl.BlockSpec(shape, lambda *_: (0,) * nd, pipeline_mode=pl.Buffered(1))


def _sigmoid(x):
    return 1.0 / (1.0 + jnp.exp(-x))


def _lost_range(total, smallest):
    return jnp.logical_not(jnp.sum(total) < FINITE_LIMIT) | (jnp.min(smallest) < L_MIN)


def _rms(x, w):
    var = jnp.mean(x * x, axis=-1, keepdims=True)
    return x * lax.rsqrt(var + RMS_EPS) * w


def _inproj_kernel(x_ref, nw_ref, wa_ref, wb_ref, wc_ref, o_ref, kvc_ref):
    h = _rms(x_ref[...], nw_ref[...]).astype(MXU_DTYPE)
    n = o_ref.shape[1]
    for c0 in range(0, n, KV_WIDTH):
        cw = min(KV_WIDTH, n - c0)
        w_ref, w0 = ((wa_ref, 0) if c0 < COL_GPOOL else (wb_ref, COL_GPOOL) if c0 < COL_GNSA
                     else (wc_ref, COL_GNSA))
        res = jnp.dot(h, w_ref[:, c0 - w0:c0 - w0 + cw],
                      preferred_element_type=jnp.float32).astype(o_ref.dtype)
        o_ref[:, c0:c0 + cw] = res
        which = (c0 - COL_KV) // KV_WIDTH
        if c0 >= COL_KV and which < 2:
            for g in range(N_KV_GROUPS):
                kvc_ref[which * N_KV_GROUPS + g] = res[:, g * HEAD_DIM:(g + 1) * HEAD_DIM]


def _inproj(x2d, norm_w, w_parts):
    m, d = x2d.shape
    n = sum(w.shape[1] for w in w_parts)
    assert n == PROJ_WIDTH
    return pl.pallas_call(
        _inproj_kernel,
        grid=(m // ROW_TILE,),
        in_specs=[pl.BlockSpec((ROW_TILE, d), lambda i: (i, 0)),
                  _const_spec((1, d))] + [_const_spec(w.shape) for w in w_parts],
        out_specs=[pl.BlockSpec((ROW_TILE, n), lambda i: (i, 0)),
                   pl.BlockSpec((2 * N_KV_GROUPS, ROW_TILE, HEAD_DIM), lambda i: (0, i, 0))],
        out_shape=[jax.ShapeDtypeStruct((m, n), ACT_DTYPE),
                   jax.ShapeDtypeStruct((2 * N_KV_GROUPS, m, HEAD_DIM), ACT_DTYPE)],
        compiler_params=_cparams(1),
        name="inproj",
    )(x2d, norm_w, *w_parts)


def _gelu_tanh(x):
    return 0.5 * x * (1.0 + jnp.tanh(np.sqrt(2.0 / np.pi) * (x + 0.044715 * (x * x * x))))


def _compress_one(r_ref, pe_ref, w1_ref, b1_ref, w2_ref):
    half = w1_ref.shape[0] // 2
    n_blk = r_ref.shape[2]
    rows = [r_ref[0, 0, :, r * half:(r + 1) * half] for r in range(CMP_PER_SLC)]
    top = [jnp.dot(x, w1_ref[0:half, :], preferred_element_type=jnp.float32) for x in rows]
    bot = [jnp.dot(x, w1_ref[half:, :], preferred_element_type=jnp.float32) for x in rows]
    bias = jnp.dot(pe_ref[...], w1_ref[...], preferred_element_type=jnp.float32)[0:1, :] + b1_ref[...]
    nxt = bot[1:] + [pltpu.roll(bot[0], n_blk - 1, 0)]
    hid = jnp.concatenate([top[r] + nxt[r] for r in range(CMP_PER_SLC)], axis=0) + bias
    return jnp.dot(_gelu_tanh(hid).astype(MXU_DTYPE), w2_ref[...], preferred_element_type=jnp.float32)


def _compress_kernel(rk_ref, rv_ref, pek_ref, w1k_ref, b1k_ref, w2k_ref,
                     pev_ref, w1v_ref, b1v_ref, w2v_ref, ok_ref, ovt_ref):
    kc = _compress_one(rk_ref, pek_ref, w1k_ref, b1k_ref, w2k_ref)
    vc = _compress_one(rv_ref, pev_ref, w1v_ref, b1v_ref, w2v_ref)
    ok_ref[0, 0] = jnp.concatenate([kc * (SCORE_SCALE * LOG2E), jnp.zeros_like(kc)], axis=1).astype(ok_ref.dtype)
    ovt_ref[0, 0] = jnp.concatenate([kc, vc], axis=1).T[HEAD_DIM:, :].astype(ovt_ref.dtype)


def _compress(kvc_rows, pek, w1k, b1k, w2k, pev, w1v, b1v, w2v):
    _, b, n_blk, width = kvc_rows.shape
    dk = w2k.shape[1]
    n_rows = n_blk * CMP_PER_SLC
    w_specs = [_const_spec(a.shape) for a in (pek, w1k, b1k, w2k)]
    return pl.pallas_call(
        _compress_kernel,
        grid=(b, N_KV_GROUPS),
        in_specs=[pl.BlockSpec((1, 1, n_blk, width), lambda i, j: (j, i, 0, 0)),
                  pl.BlockSpec((1, 1, n_blk, width), lambda i, j: (N_KV_GROUPS + j, i, 0, 0))]
        + w_specs + w_specs,
        out_specs=[pl.BlockSpec((1, 1, n_rows, 2 * dk), lambda i, j: (i, j, 0, 0)),
                   pl.BlockSpec((1, 1, dk, n_rows), lambda i, j: (i, j, 0, 0))],
        out_shape=[jax.ShapeDtypeStruct((b, N_KV_GROUPS, n_rows, 2 * dk), MXU_DTYPE),
                   jax.ShapeDtypeStruct((b, N_KV_GROUPS, dk, n_rows), MXU_DTYPE)],
        compiler_params=_cparams(2),
        name="compress",
    )(kvc_rows, kvc_rows, pek, w1k, b1k, w2k, pev, w1v, b1v, w2v)


def _kprep_kernel(ks_ref, vs_ref, kw_ref, vw_ref, cos_ref, sin_ref, rot_ref,
                  kaug_ref, vst_ref, kwin_ref, vwt_ref, knorm_ref):
    ts = ks_ref.shape[1]
    t0 = pl.program_id(1) * ts
    cos = jnp.concatenate([cos_ref[...]] * N_KV_GROUPS, axis=1)
    sin = jnp.concatenate([sin_ref[...]] * N_KV_GROUPS, axis=1)
    rot = rot_ref[...]

    def rope(k_ref):
        k = k_ref[0]
        krot = jnp.dot(k.astype(MXU_DTYPE), rot, preferred_element_type=jnp.float32)
        return (k.astype(jnp.float32) * cos + krot * sin) * (SCORE_SCALE * LOG2E)

    n_blk = kaug_ref.shape[3] - 2 * HEAD_DIM
    blk = lax.broadcasted_iota(jnp.int32, (ts, n_blk), 1)
    key_blk = (t0 + lax.broadcasted_iota(jnp.int32, (ts, n_blk), 0)) >> _log2(SLC_BLOCK)
    onehot = jnp.where(blk == key_blk, 1.0, 0.0).astype(kaug_ref.dtype)
    ks_f = rope(ks_ref)
    kw_f = rope(kw_ref)
    ks = ks_f.astype(kaug_ref.dtype)
    kw = kw_f.astype(kwin_ref.dtype)

    seg_l = lax.broadcasted_iota(jnp.int32, (KV_WIDTH, LANE), 0) >> _log2(HEAD_DIM)
    seg_c = lax.broadcasted_iota(jnp.int32, (KV_WIDTH, LANE), 1)
    sq = (jnp.dot(ks_f * ks_f, jnp.where(seg_c == seg_l, 1.0, 0.0), preferred_element_type=jnp.float32)
          + jnp.dot(kw_f * kw_f, jnp.where(seg_c == seg_l + N_KV_GROUPS, 1.0, 0.0),
                    preferred_element_type=jnp.float32))
    tile_max = jnp.broadcast_to(jnp.max(sq, axis=0, keepdims=True), knorm_ref.shape[1:])

    @pl.when(pl.program_id(1) == 0)
    def _():
        knorm_ref[0] = tile_max

    @pl.when(pl.program_id(1) > 0)
    def _():
        knorm_ref[0] = jnp.maximum(knorm_ref[0], tile_max)

    vst = vs_ref[0].astype(jnp.float32).T.astype(vst_ref.dtype)
    vwt = vw_ref[0].astype(jnp.float32).T.astype(vwt_ref.dtype)
    ones_col = jnp.where(lax.broadcasted_iota(jnp.int32, (ts, HEAD_DIM), 1) == 0, 1.0, 0.0
                         ).astype(kaug_ref.dtype)
    for g in range(N_KV_GROUPS):
        sl = slice(g * HEAD_DIM, (g + 1) * HEAD_DIM)
        kaug_ref[0, g, :, 0:n_blk] = onehot
        kaug_ref[0, g, :, n_blk:n_blk + HEAD_DIM] = ks[:, sl]
        kaug_ref[0, g, :, n_blk + HEAD_DIM:] = ones_col
        kwin_ref[0, g, :, 0:HEAD_DIM] = kw[:, sl]
        kwin_ref[0, g, :, HEAD_DIM:] = ones_col
        vst_ref[0, g] = vst[sl, :]
        vwt_ref[0, g] = vwt[sl, :]


def _kprep(proj3, cos_k, sin_k, rot):
    b, s, _ = proj3.shape
    n_blk = s // SLC_BLOCK
    ts = min(KPREP_TILE, s)
    kvb = COL_KV // KV_WIDTH

    def col(j):
        return pl.BlockSpec((1, ts, KV_WIDTH), lambda i, t: (i, t, kvb + j))

    tab = pl.BlockSpec((ts, HEAD_DIM), lambda i, t: (t, 0))
    aug_w = n_blk + 2 * HEAD_DIM
    return pl.pallas_call(
        _kprep_kernel,
        grid=(b, s // ts),
        in_specs=[col(2), col(3), col(4), col(5), tab, tab, _const_spec(rot.shape)],
        out_specs=[pl.BlockSpec((1, N_KV_GROUPS, ts, aug_w), lambda i, t: (i, 0, t, 0)),
                   pl.BlockSpec((1, N_KV_GROUPS, HEAD_DIM, ts), lambda i, t: (i, 0, 0, t)),
                   pl.BlockSpec((1, N_KV_GROUPS, ts, 2 * HEAD_DIM), lambda i, t: (i, 0, t, 0)),
                   pl.BlockSpec((1, N_KV_GROUPS, HEAD_DIM, ts), lambda i, t: (i, 0, 0, t)),
                   pl.BlockSpec((1, SUBLANE, LANE), lambda i, t: (i, 0, 0))],
        out_shape=[jax.ShapeDtypeStruct((b, N_KV_GROUPS, s, aug_w), MXU_DTYPE),
                   jax.ShapeDtypeStruct((b, N_KV_GROUPS, HEAD_DIM, s), MXU_DTYPE),
                   jax.ShapeDtypeStruct((b, N_KV_GROUPS, s, 2 * HEAD_DIM), MXU_DTYPE),
                   jax.ShapeDtypeStruct((b, N_KV_GROUPS, HEAD_DIM, s), MXU_DTYPE),
                   jax.ShapeDtypeStruct((b, SUBLANE, LANE), jnp.float32)],
        compiler_params=_cparams(2),
        name="kprep",
    )(proj3, proj3, proj3, proj3, cos_k, sin_k, rot)


def _heads_to_lanes(xt):
    return jnp.concatenate([xt[h * HEAD_DIM:(h + 1) * HEAD_DIM, :] for h in range(HEADS_PER_GROUP)],
                           axis=1)


def _vis_rows(tq):
    levels = tq // CMP_STRIDE
    dt = np.arange(tq)
    v = (dt - (CMP_BLOCK - 1)) // CMP_STRIDE - (-(CMP_BLOCK - 1)) // CMP_STRIDE
    rows = np.zeros((HEAD_DIM, tq), np.float32)
    for c in range(1, levels + 1):
        rows[c] = np.where(v < c, -MASK_BIAS, 0.0)
    rows[levels + 1] = -MASK_BIAS
    return jnp.asarray(np.tile(rows, (1, HEADS_PER_GROUP)), MXU_DTYPE)


def _cmpsel_kernel(q_ref, kc_ref, vct_ref, vis_ref, ocmp_ref, bias_ref):
    tq = q_ref.shape[1]
    n_blk = bias_ref.shape[2]
    m_rows = HEADS_PER_GROUP * tq
    t0 = pl.program_id(1) * tq
    levels = tq // CMP_STRIDE

    def body(nr):
        n_tok = CMP_PER_SLC * nr
        j_idx = lax.broadcasted_iota(jnp.int32, (nr, LANE), 0)
        n_idx = jnp.concatenate([j_idx * CMP_PER_SLC + r for r in range(CMP_PER_SLC)], axis=0)
        n_first = (t0 - (CMP_BLOCK - 1)) >> _log2(CMP_STRIDE)
        u = jnp.clip(n_idx - n_first, 0, levels + 1)
        vis_cols = jnp.where(lax.broadcasted_iota(jnp.int32, (n_tok, LANE), 1) - HEAD_DIM == u,
                             1.0, 0.0).astype(MXU_DTYPE)
        vis_rows = vis_ref[...]
        t_row = t0 + (lax.broadcasted_iota(jnp.int32, (1, m_rows), 1) & (tq - 1))
        jrow = lax.broadcasted_iota(jnp.int32, (nr, tq), 0)
        jt = (t0 + lax.broadcasted_iota(jnp.int32, (nr, tq), 1)) >> _log2(SLC_BLOCK)
        causal_blk = jrow <= jt
        forced = (jrow == 0) | (jrow == jt) | (jrow == jt - 1)
        jf = jrow.astype(jnp.float32)
        n_extract = min(SLC_TOPK, n_blk) - N_FORCED
        jt_row = (t0 + lax.broadcasted_iota(jnp.int32, (1, tq), 1)) >> _log2(SLC_BLOCK)
        want = jnp.minimum(jt_row + 1, SLC_TOPK).astype(jnp.float32)

        def store_bias(pb, marks):
            bias_ref[pb, 0, 0:nr, :] = jnp.where(marks < CHOSEN_LIMIT, 0.0, -MASK_BIAS).astype(bias_ref.dtype)

        pending = []
        for pb in range(q_ref.shape[0]):
            kc = jnp.concatenate([kc_ref[pb, 0, r * n_blk:r * n_blk + nr, :] for r in range(CMP_PER_SLC)],
                                 axis=0)
            vct = jnp.concatenate([vct_ref[pb, 0, :, r * n_blk:r * n_blk + nr] for r in range(CMP_PER_SLC)],
                                  axis=1)
            qt = _heads_to_lanes(q_ref[pb].astype(jnp.float32).T).astype(MXU_DTYPE)
            s = jnp.dot(kc + vis_cols, jnp.concatenate([qt, vis_rows], axis=0),
                        preferred_element_type=jnp.float32)
            mx = jnp.max(s, axis=0, keepdims=True)
            p = jnp.exp2(s - mx)
            den = jnp.sum(p, axis=0, keepdims=True)
            inv = jnp.where(t_row >= CMP_BLOCK - 1, 1.0 / den, 0.0)
            ocmp_ref[pb, 0, 0] = inv * jnp.dot(vct, p.astype(MXU_DTYPE), preferred_element_type=jnp.float32)

            imp = None
            for h in range(HEADS_PER_GROUP):
                ph = p[:, h * tq:(h + 1) * tq]
                last = ph[3 * nr:4 * nr, :]
                prev = jnp.where(jrow == 0, 0.0, pltpu.roll(last, 1, 0))
                taps = ph[0:nr, :] + ph[nr:2 * nr, :] + ph[2 * nr:3 * nr, :] + last + prev
                imp_h = taps * inv[:, h * tq:(h + 1) * tq]
                imp = imp_h if imp is None else imp + imp_h

            val = jnp.where(forced, CHOSEN, jnp.where(causal_blk, imp, -IMP_BIG))
            marks = val
            for _ in range(n_extract):
                marks = jnp.where(marks == jnp.max(marks, axis=0, keepdims=True), CHOSEN, marks)
            store_bias(pb, marks)
            if nr < n_blk:
                bias_ref[pb, 0, nr:, :] = jnp.full((n_blk - nr, tq), -MASK_BIAS, bias_ref.dtype)
            count = jnp.sum(jnp.where((marks < CHOSEN_LIMIT) & causal_blk, 1.0, 0.0), axis=0, keepdims=True)
            pending.append((pb, val, jnp.max(jnp.abs(count - want)) > 0.5))

        for pb, val, tied in pending:
            @pl.when(tied)
            def _(val=val, pb=pb):
                v = val
                for _ in range(n_extract):
                    top = jnp.max(v, axis=0, keepdims=True)
                    first = jnp.min(jnp.where(v == top, jf, float(n_blk)), axis=0, keepdims=True)
                    v = jnp.where(jf == first, CHOSEN, v)
                store_bias(pb, v)

    sizes = [n_blk * f // 8 for f in range(1, 9) if (n_blk * f // 8) % 16 == 0]
    lo = 0
    for nr in sizes:
        hi = nr * SLC_BLOCK if nr < n_blk else None
        end = t0 + tq
        cond = (end > lo) if hi is None else (end > lo) & (end <= hi)
        pl.when(cond)(functools.partial(body, nr))
        lo = hi


def _cmpsel(proj3, kc_r, vc_rt):
    b, s, _ = proj3.shape
    tq = min(Q_TILE, s)
    nq = s // tq
    n_cmp_pad = kc_r.shape[2]
    n_blk = s // SLC_BLOCK
    qb = COL_Q // KV_WIDTH
    m_rows = HEADS_PER_GROUP * tq
    return pl.pallas_call(
        _cmpsel_kernel,
        grid=(N_KV_GROUPS, nq),
        in_specs=[pl.BlockSpec((b, tq, KV_WIDTH), lambda g, q: (0, q, qb + g)),
                  pl.BlockSpec((b, 1, n_cmp_pad, LANE), lambda g, q: (0, g, 0, 0)),
                  pl.BlockSpec((b, 1, HEAD_DIM, n_cmp_pad), lambda g, q: (0, g, 0, 0)),
                  _const_spec((HEAD_DIM, m_rows))],
        out_specs=[pl.BlockSpec((b, 1, 1, HEAD_DIM, m_rows), lambda g, q: (0, g, q, 0, 0)),
                   pl.BlockSpec((b, 1, n_blk, tq), lambda g, q: (0, g, 0, q))],
        out_shape=[jax.ShapeDtypeStruct((b, N_KV_GROUPS, nq, HEAD_DIM, m_rows), jnp.float32),
                   jax.ShapeDtypeStruct((b, N_KV_GROUPS, n_blk, s), MXU_DTYPE)],
        compiler_params=_cparams(2),
        name="cmpsel",
    )(proj3, kc_r, vc_rt, _vis_rows(tq))


def _flash_kernel(q_ref, cos_ref, sin_ref, bias_ref, kaug_ref, vst_ref, kwin_ref, vwt_ref, knorm_ref,
                  wmask_ref, gate_ref, ocmp_ref, o_ref, m_ref, l_ref, acc_ref, lw_ref, accw_ref):
    nb = q_ref.shape[0]
    tq = q_ref.shape[1]
    m_rows = HEADS_PER_GROUP * tq
    grp = pl.program_id(0)
    qi = pl.program_id(1)
    t0 = qi * tq
    kd = pl.multiple_of(t0, tq)
    probs = range(nb)
    zpad = jnp.zeros((HEAD_DIM, m_rows), MXU_DTYPE)
    pad_row = lax.broadcasted_iota(jnp.int32, (HEAD_DIM, m_rows), 0)
    lane = lax.broadcasted_iota(jnp.int32, (1, LANE), 1)

    def stab_rows(neg_m):
        return jnp.where(pad_row == 0, neg_m, 0.0).astype(MXU_DTYPE)

    cos = cos_ref[...]
    sin = sin_ref[...]
    qr, bias4, qs_slc, qs_win = [], [], [], []
    for pb in probs:
        qt = q_ref[pb].astype(jnp.float32).T
        parts = []
        for h in range(HEADS_PER_GROUP):
            qh = qt[h * HEAD_DIM:(h + 1) * HEAD_DIM, :]
            x1 = qh[0:ROPE_HALF, :]
            x2 = qh[ROPE_HALF:ROPE_DIM, :]
            parts.append(jnp.concatenate([x1 * cos - x2 * sin, x2 * cos + x1 * sin, qh[ROPE_DIM:, :]],
                                         axis=0))
        qf = jnp.concatenate(parts, axis=1)
        q_sq = jnp.sum(qf * qf, axis=0, keepdims=True)
        k_sq = knorm_ref[pb, 0:1, :]
        ks_sq = jnp.max(jnp.where(lane == grp, k_sq, 0.0), axis=1, keepdims=True)
        kw_sq = jnp.max(jnp.where(lane == grp + N_KV_GROUPS, k_sq, 0.0), axis=1, keepdims=True)
        qr.append(qf.astype(MXU_DTYPE))
        bias4.append(jnp.concatenate([bias_ref[pb, 0]] * HEADS_PER_GROUP, axis=1))
        qs_slc.append(jnp.concatenate([bias4[pb], qr[pb], stab_rows(-STAB_MARGIN * jnp.sqrt(q_sq * ks_sq))],
                                      axis=0))
        qs_win.append(jnp.concatenate([qr[pb], stab_rows(-STAB_MARGIN * jnp.sqrt(q_sq * kw_sq))], axis=0))

    n_back = WINDOW // tq

    def win_tile(back):
        if back == 0:
            return kd, tq, wmask_ref[0]
        inside = wmask_ref[back] if back * tq + tq - 1 >= WINDOW else 0.0
        return (pl.multiple_of(jnp.maximum(t0 - back * tq, 0), tq), tq,
                jnp.where(qi >= back, inside, NEG_BIG))

    def fast_tiles(k_ref, v_ref, qs, tiles, l_out, acc_out, first=False):
        for pb in probs:
            l_add = acc_add = None
            for k0, tk, mask in tiles:
                s = jnp.dot(k_ref[pb, 0, pl.ds(k0, tk), :], qs[pb], preferred_element_type=jnp.float32)
                if mask is not None:
                    s = s + mask
                p = jnp.exp2(s)
                l_t = jnp.sum(p, axis=0, keepdims=True)
                acc_t = jnp.dot(v_ref[pb, 0, :, pl.ds(k0, tk)], p.astype(MXU_DTYPE),
                                preferred_element_type=jnp.float32)
                l_add = l_t if l_add is None else l_add + l_t
                acc_add = acc_t if acc_add is None else acc_add + acc_t
            if first:
                l_out[pb] = l_add
                acc_out[pb] = acc_add
            else:
                l_out[pb] += l_add
                acc_out[pb] += acc_add

    fast_tiles(kaug_ref, vst_ref, qs_slc, [(kd, tq, wmask_ref[0])], l_ref, acc_ref, first=True)
    fast_tiles(kwin_ref, vwt_ref, qs_win, [win_tile(back) for back in range(n_back + 1)],
               lw_ref, accw_ref, first=True)

    sub = SLC_SUBTILE
    n_sub = SLC_SPAN // sub

    def span_body(i, carry):
        base = i * SLC_SPAN
        fast_tiles(kaug_ref, vst_ref, qs_slc,
                   [(pl.multiple_of(base + j * sub, sub), sub, None) for j in range(n_sub)],
                   l_ref, acc_ref)
        return carry

    n_span = t0 // SLC_SPAN
    lax.fori_loop(0, n_span, span_body, 0)
    done = n_span * SLC_SPAN
    size = SLC_SPAN // 2
    while size >= tq:
        take = ((t0 // size) % 2) == 1

        @pl.when(take)
        def _(done=done, size=size):
            piece = min(size, sub)
            fast_tiles(kaug_ref, vst_ref, qs_slc,
                       [(pl.multiple_of(done + j * piece, piece), piece, None) for j in range(size // piece)],
                       l_ref, acc_ref)

        done = done + jnp.where(take, size, 0)
        size //= 2

    def exact_start(pb, s, vt, l_out, acc_out):
        mx = jnp.max(s, axis=0, keepdims=True)
        p = jnp.exp2(s - mx)
        m_ref[pb] = mx
        l_out[pb] = jnp.sum(p, axis=0, keepdims=True)
        acc_out[pb] = jnp.dot(vt, p.astype(MXU_DTYPE), preferred_element_type=jnp.float32)

    def exact_update(pb, s, vt, l_out, acc_out):
        m_old = m_ref[pb]
        mx = jnp.maximum(m_old, jnp.max(s, axis=0, keepdims=True))
        alpha = jnp.exp2(m_old - mx)
        p = jnp.exp2(s - mx)
        m_ref[pb] = mx
        l_out[pb] = alpha * l_out[pb] + jnp.sum(p, axis=0, keepdims=True)
        acc_out[pb] = alpha * acc_out[pb] + jnp.dot(vt, p.astype(MXU_DTYPE),
                                                    preferred_element_type=jnp.float32)

    r_id = lax.broadcasted_iota(jnp.int32, (GATE_ROWS, LANE), 0)
    c_id = lax.broadcasted_iota(jnp.int32, (GATE_ROWS, LANE), 1)
    pick = jnp.where((c_id == grp * (HEADS_PER_GROUP * 3) + r_id) & (r_id < HEADS_PER_GROUP * 3),
                     1.0, 0.0).astype(gate_ref.dtype)

    def finish():
        total = None
        for pb in probs:
            o_slc = acc_ref[pb] / l_ref[pb]
            o_win = accw_ref[pb] / lw_ref[pb]
            graw = lax.dot_general(pick, gate_ref[pb], (((1,), (1,)), ((), ())),
                                   preferred_element_type=jnp.float32)
            gates = _sigmoid(graw)
            o_cmp = ocmp_ref[pb, 0, 0]
            mixed = []
            for h in range(HEADS_PER_GROUP):
                sl = slice(h * tq, (h + 1) * tq)
                mixed.append(gates[3 * h:3 * h + 1, :] * o_cmp[:, sl]
                             + gates[3 * h + 1:3 * h + 2, :] * o_slc[:, sl]
                             + gates[3 * h + 2:3 * h + 3, :] * o_win[:, sl])
            out = jnp.concatenate(mixed, axis=0)
            o_ref[pb] = out.T.astype(o_ref.dtype)
            part = jnp.sum(jnp.abs(out), axis=0, keepdims=True)
            total = part if total is None else total + part
        return total

    check = finish()
    l_low = None
    for pb in probs:
        l_both = l_ref[pb] + lw_ref[pb]
        for h in range(HEADS_PER_GROUP):
            check = check + l_both[:, h * tq:(h + 1) * tq]
        lo = jnp.minimum(l_ref[pb], lw_ref[pb])
        l_low = lo if l_low is None else jnp.minimum(l_low, lo)

    @pl.when(_lost_range(check, l_low))
    def _():
        tcol = t0 + (lax.broadcasted_iota(jnp.int32, (tq, m_rows), 1) & (tq - 1))
        koff = lax.broadcasted_iota(jnp.int32, (tq, m_rows), 0)
        causal = t0 + koff <= tcol
        for pb in probs:
            q0 = jnp.concatenate([bias4[pb], qr[pb], zpad], axis=0)
            s = jnp.dot(kaug_ref[pb, 0, pl.ds(kd, tq), :], q0, preferred_element_type=jnp.float32)
            exact_start(pb, jnp.where(causal, s, NEG_BIG), vst_ref[pb, 0, :, pl.ds(kd, tq)], l_ref, acc_ref)

            def body(kt, carry, pb=pb, q0=q0):
                k0 = pl.multiple_of(kt * tq, tq)
                sk = jnp.dot(kaug_ref[pb, 0, pl.ds(k0, tq), :], q0, preferred_element_type=jnp.float32)
                exact_update(pb, sk, vst_ref[pb, 0, :, pl.ds(k0, tq)], l_ref, acc_ref)
                return carry

            lax.fori_loop(0, qi, body, 0)
            qw0 = jnp.concatenate([qr[pb], zpad], axis=0)
            for back in range(n_back + 1):
                kbase = t0 - back * tq
                k0 = pl.multiple_of(jnp.maximum(kbase, 0), tq)
                sw = jnp.dot(kwin_ref[pb, 0, pl.ds(k0, tq), :], qw0, preferred_element_type=jnp.float32)
                diff = tcol - (kbase + koff)
                sw = jnp.where((diff >= 0) & (diff < WINDOW) & (kbase + koff >= 0), sw, NEG_BIG)
                if back == 0:
                    exact_start(pb, sw, vwt_ref[pb, 0, :, pl.ds(k0, tq)], lw_ref, accw_ref)
                else:
                    exact_update(pb, sw, vwt_ref[pb, 0, :, pl.ds(k0, tq)], lw_ref, accw_ref)
        finish()


def _window_masks(tq):
    k = np.arange(tq)[:, None]
    t = np.arange(tq)[None, :]
    tiles = []
    for back in range(WINDOW // tq + 1):
        diff = t + back * tq - k
        tiles.append(np.tile(np.where((diff >= 0) & (diff < WINDOW), 0.0, NEG_BIG), (1, HEADS_PER_GROUP)))
    return jnp.asarray(np.stack(tiles), jnp.float32)


def _flash(proj3, cos_q, sin_q, bias, kaug, vst, kwin, vwt, knorm, ocmp):
    b, s, _ = proj3.shape
    tq = min(Q_TILE, s)
    nq = s // tq
    n_blk = s // SLC_BLOCK
    aug_w = kaug.shape[3]
    qb = COL_Q // KV_WIDTH
    m_rows = HEADS_PER_GROUP * tq

    wmask = _window_masks(tq)

    def resident(shape):
        return pl.BlockSpec((b, 1) + shape, lambda g, q: (0, g, 0, 0))

    return pl.pallas_call(
        _flash_kernel,
        grid=(N_KV_GROUPS, nq),
        in_specs=[pl.BlockSpec((b, tq, KV_WIDTH), lambda g, q: (0, q, qb + g)),
                  pl.BlockSpec((ROPE_HALF, tq), lambda g, q: (0, q)),
                  pl.BlockSpec((ROPE_HALF, tq), lambda g, q: (0, q)),
                  pl.BlockSpec((b, 1, n_blk, tq), lambda g, q: (0, g, 0, q)),
                  resident((s, aug_w)),
                  resident((HEAD_DIM, s)),
                  resident((s, 2 * HEAD_DIM)),
                  resident((HEAD_DIM, s)),
                  _const_spec((b, SUBLANE, LANE)),
                  _const_spec(wmask.shape),
                  pl.BlockSpec((b, tq, LANE), lambda g, q: (0, q, COL_GNSA // LANE)),
                  pl.BlockSpec((b, 1, 1, HEAD_DIM, m_rows), lambda g, q: (0, g, q, 0, 0))],
        out_specs=pl.BlockSpec((b, tq, KV_WIDTH), lambda g, q: (0, q, g)),
        out_shape=jax.ShapeDtypeStruct((b, s, ATTN_Q_WIDTH), ACT_DTYPE),
        scratch_shapes=[pltpu.VMEM((b, 1, m_rows), jnp.float32),
                        pltpu.VMEM((b, 1, m_rows), jnp.float32),
                        pltpu.VMEM((b, HEAD_DIM, m_rows), jnp.float32),
                        pltpu.VMEM((b, 1, m_rows), jnp.float32),
                        pltpu.VMEM((b, HEAD_DIM, m_rows), jnp.float32)],
        compiler_params=_cparams(2),
        name="flash",
    )(proj3, cos_q, sin_q, bias, kaug, vst, kwin, vwt, knorm, wmask, proj3, ocmp)


def _tail_kernel(u_ref, halo_ref, gp_ref, ga_ref, attn_ref, x_ref, pw_ref, ps_ref, wpp_ref, wpa_ref, wo_ref,
                 n2_ref, wg_ref, wu_ref, wd_ref, nf_ref, o_ref, *, seq_len):
    tm = u_ref.shape[0]
    t0 = (pl.program_id(0) * tm) & (seq_len - 1)

    u = u_ref[...].astype(jnp.float32)
    halo = jnp.where(t0 > 0, halo_ref[...].astype(jnp.float32), 0.0)
    ext = jnp.concatenate([halo, u], axis=0)
    t = t0 + lax.broadcasted_iota(jnp.int32, (tm, POOL_GROUP), 0)
    pooled = []
    for gi, w in enumerate(POOL_WINDOWS):
        sl = slice(gi * POOL_GROUP, (gi + 1) * POOL_GROUP)
        acc = ext[:, sl]
        span = 1
        while span < w:
            acc = acc + pltpu.roll(acc, span, 0)
            span *= 2
        cnt = jnp.minimum(t + 1, w).astype(jnp.float32)
        mean = acc[POOL_HALO:, :] / cnt
        mixed = jnp.dot((mean - u[:, sl]).astype(MXU_DTYPE), pw_ref[gi],
                        preferred_element_type=jnp.float32)
        pooled.append(mixed * ps_ref[:, sl])
    pool = jnp.concatenate(pooled, axis=1).astype(MXU_DTYPE)

    pp = jnp.dot(pool, wpp_ref[...], preferred_element_type=jnp.float32)
    pa = jnp.dot(attn_ref[...], wpa_ref[...], preferred_element_type=jnp.float32)
    merged = (_sigmoid(gp_ref[...].astype(jnp.float32)) * pp
              + _sigmoid(ga_ref[...].astype(jnp.float32)) * pa)
    x1 = x_ref[...] + jnp.dot(merged.astype(MXU_DTYPE), wo_ref[...], preferred_element_type=jnp.float32)

    h = _rms(x1, n2_ref[...]).astype(MXU_DTYPE)
    d_ff = wg_ref.shape[1]
    out = x1
    for c0 in range(0, d_ff, FF_CHUNK):
        cw = min(FF_CHUNK, d_ff - c0)
        gate = jnp.dot(h, wg_ref[:, c0:c0 + cw], preferred_element_type=jnp.float32)
        up = jnp.dot(h, wu_ref[:, c0:c0 + cw], preferred_element_type=jnp.float32)
        act = (gate * _sigmoid(gate) * up).astype(MXU_DTYPE)
        out = out + jnp.dot(act, wd_ref[c0:c0 + cw, :], preferred_element_type=jnp.float32)
    o_ref[...] = _rms(out, nf_ref[...])


def _tail(proj2, attn2, x2d, pool_w, pool_scale, wpp, wpa, wo, norm2_w, wg, wu, wd, norm_f_w, seq_len):
    m, d = x2d.shape
    tm = min(ROW_TILE, seq_len)
    halo_per_tile = tm // POOL_HALO
    weights = (pool_w, pool_scale, wpp, wpa, wo, norm2_w, wg, wu, wd, norm_f_w)
    return pl.pallas_call(
        functools.partial(_tail_kernel, seq_len=seq_len),
        grid=(m // tm,),
        in_specs=[pl.BlockSpec((tm, POOL_WIDTH), lambda i: (i, COL_POOL // POOL_WIDTH)),
                  pl.BlockSpec((POOL_HALO, POOL_WIDTH),
                               lambda i: (jnp.maximum(i * halo_per_tile - 1, 0), COL_POOL // POOL_WIDTH)),
                  pl.BlockSpec((tm, d), lambda i: (i, COL_GPOOL // D_MODEL)),
                  pl.BlockSpec((tm, d), lambda i: (i, COL_GATTN // D_MODEL)),
                  pl.BlockSpec((tm, ATTN_Q_WIDTH), lambda i: (i, 0)),
                  pl.BlockSpec((tm, d), lambda i: (i, 0))] + [_const_spec(w.shape) for w in weights],
        out_specs=pl.BlockSpec((tm, d), lambda i: (i, 0)),
        out_shape=jax.ShapeDtypeStruct((m, d), jnp.float32),
        compiler_params=_cparams(1),
        name="tail",
    )(proj2, proj2, proj2, proj2, attn2, x2d, *weights)


def _rope_tables(seq_len):
    inv_freq = 1.0 / (ROPE_THETA ** (jnp.arange(0, ROPE_DIM, 2, dtype=jnp.float32) / ROPE_DIM))
    ang = inv_freq[:, None] * jnp.arange(seq_len).astype(jnp.float32)[None, :]
    cos_t, sin_t = jnp.cos(ang), jnp.sin(ang)
    rest = HEAD_DIM - ROPE_DIM
    cos_h = jnp.concatenate([cos_t, cos_t, jnp.ones((rest, seq_len), jnp.float32)], axis=0).T
    sin_h = jnp.concatenate([sin_t, sin_t, jnp.zeros((rest, seq_len), jnp.float32)], axis=0).T
    return cos_t, sin_t, cos_h, sin_h


def _rotate_half_matrix():
    rot = np.zeros((KV_WIDTH, KV_WIDTH), np.float32)
    for l in range(KV_WIDTH):
        d = l % HEAD_DIM
        if d < ROPE_HALF:
            rot[l + ROPE_HALF, l] = -1.0
        elif d < ROPE_DIM:
            rot[l - ROPE_HALF, l] = 1.0
    return jnp.asarray(rot, MXU_DTYPE)


def _layer(x, norm1_w, w_in, pool_w, pool_scale, cmp_pe_k, cmp_w1_k, cmp_b1_k, cmp_w2_k,
           cmp_pe_v, cmp_w1_v, cmp_b1_v, cmp_w2_v, w_proj_pool, w_proj_attn, w_out,
           norm2_w, w_ffn_gate, w_ffn_up, w_ffn_down, norm_f_w):
    b, s, d = x.shape
    m = b * s
    n_blk = s // SLC_BLOCK
    assert s % Q_TILE == 0 and s % ROW_TILE == 0 and (s & (s - 1)) == 0, s
    cd = MXU_DTYPE

    g0 = COL_GPOOL
    w_parts = (w_in[:, :g0].astype(cd), w_in[:, g0 + GATE_NSA:].astype(cd),
               jnp.pad(w_in[:, g0:g0 + GATE_NSA], ((0, 0), (0, LANE - GATE_NSA))).astype(cd))
    x2d = x.reshape(m, d)
    proj2, kvc = _inproj(x2d, norm1_w.reshape(1, d), w_parts)
    proj3 = proj2.reshape(b, s, PROJ_WIDTH)

    def pe8(pe):
        return jnp.broadcast_to(pe.reshape(1, -1), (8, pe.size)).astype(cd)

    kvc_rows = kvc.reshape(2 * N_KV_GROUPS, b, n_blk, SLC_BLOCK * HEAD_DIM)
    kc_r, vc_rt = _compress(kvc_rows,
                            pe8(cmp_pe_k), cmp_w1_k.astype(cd), cmp_b1_k.reshape(1, -1), cmp_w2_k.astype(cd),
                            pe8(cmp_pe_v), cmp_w1_v.astype(cd), cmp_b1_v.reshape(1, -1), cmp_w2_v.astype(cd))

    cos_q, sin_q, cos_k, sin_k = _rope_tables(s)
    kaug, vst, kwin, vwt, knorm = _kprep(proj3, cos_k, sin_k, _rotate_half_matrix())
    ocmp, bias = _cmpsel(proj3, kc_r, vc_rt)
    attn = _flash(proj3, cos_q, sin_q, bias, kaug, vst, kwin, vwt, knorm, ocmp)
    return _tail(proj2, attn.reshape(m, ATTN_Q_WIDTH), x2d, pool_w.astype(cd),
                 pool_scale.reshape(1, -1), w_proj_pool.astype(cd), w_proj_attn.astype(cd),
                 w_out.astype(cd), norm2_w.reshape(1, d), w_ffn_gate.astype(cd), w_ffn_up.astype(cd),
                 w_ffn_down.astype(cd), norm_f_w.reshape(1, d), s)


def kernel(x, norm1_w, w_in, pool_w, pool_scale, cmp_pe_k, cmp_w1_k, cmp_b1_k, cmp_w2_k, cmp_pe_v,
           cmp_w1_v, cmp_b1_v, cmp_w2_v, w_proj_pool, w_proj_attn, w_out, norm2_w, w_ffn_gate,
           w_ffn_up, w_ffn_down, norm_f_w):
    b, s, d = x.shape
    depth = w_in.shape[0]
    assert depth == 1, "the final norm is fused into the last layer's FFN kernel"
    out = _layer(x, norm1_w[0], w_in[0], pool_w[0], pool_scale[0], cmp_pe_k[0], cmp_w1_k[0], cmp_b1_k[0],
                 cmp_w2_k[0], cmp_pe_v[0], cmp_w1_v[0], cmp_b1_v[0], cmp_w2_v[0], w_proj_pool[0],
                 w_proj_attn[0], w_out[0], norm2_w[0], w_ffn_gate[0], w_ffn_up[0], w_ffn_down[0], norm_f_w)
    return out.reshape(b, s, d)
```

```python
import functools

import numpy as np
import jax
import jax.numpy as jnp
from jax import lax
from jax.experimental import pallas as pl
from jax.experimental.pallas import tpu as pltpu

D_MODEL = 1024
N_HEADS = 16
HEAD_DIM = 64
N_KV_GROUPS = 4
HEADS_PER_GROUP = N_HEADS // N_KV_GROUPS
ROPE_DIM = HEAD_DIM // 4
ROPE_HALF = ROPE_DIM // 2
ROPE_THETA = 500000.0
CMP_BLOCK = 32
CMP_STRIDE = 16
CMP_PER_SLC = 4
SLC_BLOCK = 64
SLC_TOPK = 16
WINDOW = 512
ATTN_Q_WIDTH = N_HEADS * HEAD_DIM
KV_WIDTH = N_KV_GROUPS * HEAD_DIM
POOL_WIDTH = D_MODEL // 2
POOL_WINDOWS = (2, 4, 8, 16)
POOL_GROUP = POOL_WIDTH // len(POOL_WINDOWS)
POOL_HALO = 16
RMS_EPS = 1e-6

LANE = 128
SUBLANE = 8
MXU_DTYPE = jnp.bfloat16
ACT_DTYPE = jnp.bfloat16
VMEM_LIMIT_BYTES = 56 * 1024 * 1024

GATE_NSA = N_HEADS * 3
COL_POOL = 0
COL_Q = COL_POOL + POOL_WIDTH
COL_KV = COL_Q + ATTN_Q_WIDTH
COL_GPOOL = COL_KV + 6 * KV_WIDTH
COL_GATTN = COL_GPOOL + D_MODEL
COL_GNSA = COL_GATTN + D_MODEL
PROJ_WIDTH = COL_GNSA + LANE
GATE_ROWS = 16
ONES_ROWS = 16

ROW_TILE = 512
Q_TILE = 256
SLC_SPAN = 4096
SLC_SUBTILE = 4096
KPREP_TILE = 1024
FF_CHUNK = 512
SCORE_SCALE = HEAD_DIM ** -0.5
LOG2E = float(np.log2(np.e))
FINITE_LIMIT = 3e38
STAB_MARGIN = 1.03
L_MIN = 2.0 ** -64
MASK_BIAS = float(2 ** 30)
NEG_BIG = -1e30
IMP_BIG = 1e30
CHOSEN = -3e38
CHOSEN_LIMIT = -2e38
N_FORCED = 3


def _log2(n):
    k = int(n).bit_length() - 1
    assert 1 << k == n, n
    return k


def _cparams(n_grid):
    return pltpu.CompilerParams(dimension_semantics=("arbitrary",) * n_grid,
                                vmem_limit_bytes=VMEM_LIMIT_BYTES)


def _const_spec(shape):
    nd = len(shape)
    return pl.BlockSpec(shape, lambda *_: (0,) * nd, pipeline_mode=pl.Buffered(1))


def _sigmoid(x):
    return 1.0 / (1.0 + jnp.exp(-x))


def _lost_range(total, smallest):
    return jnp.logical_not(jnp.sum(total) < FINITE_LIMIT) | (jnp.min(smallest) < L_MIN)


def _rms(x, w):
    var = jnp.mean(x * x, axis=-1, keepdims=True)
    return x * lax.rsqrt(var + RMS_EPS) * w


def _inproj_kernel(x_ref, nw_ref, wa_ref, wb_ref, wc_ref, o_ref, kvc_ref):
    h = _rms(x_ref[...], nw_ref[...]).astype(MXU_DTYPE)
    n = o_ref.shape[1]
    for c0 in range(0, n, KV_WIDTH):
        cw = min(KV_WIDTH, n - c0)
        w_ref, w0 = ((wa_ref, 0) if c0 < COL_GPOOL else (wb_ref, COL_GPOOL) if c0 < COL_GNSA
                     else (wc_ref, COL_GNSA))
        res = jnp.dot(h, w_ref[:, c0 - w0:c0 - w0 + cw],
                      preferred_element_type=jnp.float32).astype(o_ref.dtype)
        o_ref[:, c0:c0 + cw] = res
        which = (c0 - COL_KV) // KV_WIDTH
        if c0 >= COL_KV and which < 2:
            for g in range(N_KV_GROUPS):
                kvc_ref[which * N_KV_GROUPS + g] = res[:, g * HEAD_DIM:(g + 1) * HEAD_DIM]


def _inproj(x2d, norm_w, w_parts):
    m, d = x2d.shape
    n = sum(w.shape[1] for w in w_parts)
    assert n == PROJ_WIDTH
    return pl.pallas_call(
        _inproj_kernel,
        grid=(m // ROW_TILE,),
        in_specs=[pl.BlockSpec((ROW_TILE, d), lambda i: (i, 0)),
                  _const_spec((1, d))] + [_const_spec(w.shape) for w in w_parts],
        out_specs=[pl.BlockSpec((ROW_TILE, n), lambda i: (i, 0)),
                   pl.BlockSpec((2 * N_KV_GROUPS, ROW_TILE, HEAD_DIM), lambda i: (0, i, 0))],
        out_shape=[jax.ShapeDtypeStruct((m, n), ACT_DTYPE),
                   jax.ShapeDtypeStruct((2 * N_KV_GROUPS, m, HEAD_DIM), ACT_DTYPE)],
        compiler_params=_cparams(1),
        name="inproj",
    )(x2d, norm_w, *w_parts)


def _gelu_tanh(x):
    return 0.5 * x * (1.0 + jnp.tanh(np.sqrt(2.0 / np.pi) * (x + 0.044715 * (x * x * x))))


def _compress_one(r_ref, pe_ref, w1_ref, b1_ref, w2_ref):
    half = w1_ref.shape[0] // 2
    n_blk = r_ref.shape[2]
    rows = [r_ref[0, 0, :, r * half:(r + 1) * half] for r in range(CMP_PER_SLC)]
    top = [jnp.dot(x, w1_ref[0:half, :], preferred_element_type=jnp.float32) for x in rows]
    bot = [jnp.dot(x, w1_ref[half:, :], preferred_element_type=jnp.float32) for x in rows]
    bias = jnp.dot(pe_ref[...], w1_ref[...], preferred_element_type=jnp.float32)[0:1, :] + b1_ref[...]
    nxt = bot[1:] + [pltpu.roll(bot[0], n_blk - 1, 0)]
    hid = jnp.concatenate([top[r] + nxt[r] for r in range(CMP_PER_SLC)], axis=0) + bias
    return jnp.dot(_gelu_tanh(hid).astype(MXU_DTYPE), w2_ref[...], preferred_element_type=jnp.float32)


def _compress_kernel(rk_ref, rv_ref, pek_ref, w1k_ref, b1k_ref, w2k_ref,
                     pev_ref, w1v_ref, b1v_ref, w2v_ref, ok_ref, ovt_ref):
    kc = _compress_one(rk_ref, pek_ref, w1k_ref, b1k_ref, w2k_ref)
    vc = _compress_one(rv_ref, pev_ref, w1v_ref, b1v_ref, w2v_ref)
    ok_ref[0, 0] = jnp.concatenate([kc * (SCORE_SCALE * LOG2E), jnp.zeros_like(kc)], axis=1).astype(ok_ref.dtype)
    vct = jnp.concatenate([kc, vc], axis=1).T[HEAD_DIM:, :]
    ovt_ref[0, 0] = jnp.concatenate([vct, jnp.ones((ONES_ROWS, vct.shape[1]), vct.dtype)],
                                    axis=0).astype(ovt_ref.dtype)


def _compress(kvc_rows, pek, w1k, b1k, w2k, pev, w1v, b1v, w2v):
    _, b, n_blk, width = kvc_rows.shape
    dk = w2k.shape[1]
    n_rows = n_blk * CMP_PER_SLC
    w_specs = [_const_spec(a.shape) for a in (pek, w1k, b1k, w2k)]
    return pl.pallas_call(
        _compress_kernel,
        grid=(b, N_KV_GROUPS),
        in_specs=[pl.BlockSpec((1, 1, n_blk, width), lambda i, j: (j, i, 0, 0)),
                  pl.BlockSpec((1, 1, n_blk, width), lambda i, j: (N_KV_GROUPS + j, i, 0, 0))]
        + w_specs + w_specs,
        out_specs=[pl.BlockSpec((1, 1, n_rows, 2 * dk), lambda i, j: (i, j, 0, 0)),
                   pl.BlockSpec((1, 1, dk + ONES_ROWS, n_rows), lambda i, j: (i, j, 0, 0))],
        out_shape=[jax.ShapeDtypeStruct((b, N_KV_GROUPS, n_rows, 2 * dk), MXU_DTYPE),
                   jax.ShapeDtypeStruct((b, N_KV_GROUPS, dk + ONES_ROWS, n_rows), MXU_DTYPE)],
        compiler_params=_cparams(2),
        name="compress",
    )(kvc_rows, kvc_rows, pek, w1k, b1k, w2k, pev, w1v, b1v, w2v)


def _kprep_kernel(ks_ref, vs_ref, kw_ref, vw_ref, cos_ref, sin_ref, rot_ref,
                  kaug_ref, vst_ref, kwin_ref, vwt_ref, knorm_ref):
    ts = ks_ref.shape[1]
    t0 = pl.program_id(1) * ts
    cos = jnp.concatenate([cos_ref[...]] * N_KV_GROUPS, axis=1)
    sin = jnp.concatenate([sin_ref[...]] * N_KV_GROUPS, axis=1)
    rot = rot_ref[...]

    def rope(k_ref):
        k = k_ref[0]
        krot = jnp.dot(k.astype(MXU_DTYPE), rot, preferred_element_type=jnp.float32)
        return (k.astype(jnp.float32) * cos + krot * sin) * (SCORE_SCALE * LOG2E)

    n_blk = kaug_ref.shape[3] - 2 * HEAD_DIM
    blk = lax.broadcasted_iota(jnp.int32, (ts, n_blk), 1)
    key_blk = (t0 + lax.broadcasted_iota(jnp.int32, (ts, n_blk), 0)) >> _log2(SLC_BLOCK)
    onehot = jnp.where(blk == key_blk, 1.0, 0.0).astype(kaug_ref.dtype)
    ks_f = rope(ks_ref)
    kw_f = rope(kw_ref)
    ks = ks_f.astype(kaug_ref.dtype)
    kw = kw_f.astype(kwin_ref.dtype)

    seg_l = lax.broadcasted_iota(jnp.int32, (KV_WIDTH, LANE), 0) >> _log2(HEAD_DIM)
    seg_c = lax.broadcasted_iota(jnp.int32, (KV_WIDTH, LANE), 1)
    sq = (jnp.dot(ks_f * ks_f, jnp.where(seg_c == seg_l, 1.0, 0.0), preferred_element_type=jnp.float32)
          + jnp.dot(kw_f * kw_f, jnp.where(seg_c == seg_l + N_KV_GROUPS, 1.0, 0.0),
                    preferred_element_type=jnp.float32))
    tile_max = jnp.broadcast_to(jnp.max(sq, axis=0, keepdims=True), knorm_ref.shape[1:])

    @pl.when(pl.program_id(1) == 0)
    def _():
        knorm_ref[0] = tile_max

    @pl.when(pl.program_id(1) > 0)
    def _():
        knorm_ref[0] = jnp.maximum(knorm_ref[0], tile_max)

    vst = vs_ref[0].astype(jnp.float32).T.astype(vst_ref.dtype)
    vwt = vw_ref[0].astype(jnp.float32).T.astype(vwt_ref.dtype)
    ones_col = jnp.where(lax.broadcasted_iota(jnp.int32, (ts, HEAD_DIM), 1) == 0, 1.0, 0.0
                         ).astype(kaug_ref.dtype)
    for g in range(N_KV_GROUPS):
        sl = slice(g * HEAD_DIM, (g + 1) * HEAD_DIM)
        kaug_ref[0, g, :, 0:n_blk] = onehot
        kaug_ref[0, g, :, n_blk:n_blk + HEAD_DIM] = ks[:, sl]
        kaug_ref[0, g, :, n_blk + HEAD_DIM:] = ones_col
        kwin_ref[0, g, :, 0:HEAD_DIM] = kw[:, sl]
        kwin_ref[0, g, :, HEAD_DIM:] = ones_col
        vst_ref[0, g] = vst[sl, :]
        vwt_ref[0, g] = vwt[sl, :]


def _kprep(proj3, cos_k, sin_k, rot):
    b, s, _ = proj3.shape
    n_blk = s // SLC_BLOCK
    ts = min(KPREP_TILE, s)
    kvb = COL_KV // KV_WIDTH

    def col(j):
        return pl.BlockSpec((1, ts, KV_WIDTH), lambda i, t: (i, t, kvb + j))

    tab = pl.BlockSpec((ts, HEAD_DIM), lambda i, t: (t, 0))
    aug_w = n_blk + 2 * HEAD_DIM
    return pl.pallas_call(
        _kprep_kernel,
        grid=(b, s // ts),
        in_specs=[col(2), col(3), col(4), col(5), tab, tab, _const_spec(rot.shape)],
        out_specs=[pl.BlockSpec((1, N_KV_GROUPS, ts, aug_w), lambda i, t: (i, 0, t, 0)),
                   pl.BlockSpec((1, N_KV_GROUPS, HEAD_DIM, ts), lambda i, t: (i, 0, 0, t)),
                   pl.BlockSpec((1, N_KV_GROUPS, ts, 2 * HEAD_DIM), lambda i, t: (i, 0, t, 0)),
                   pl.BlockSpec((1, N_KV_GROUPS, HEAD_DIM, ts), lambda i, t: (i, 0, 0, t)),
                   pl.BlockSpec((1, SUBLANE, LANE), lambda i, t: (i, 0, 0))],
        out_shape=[jax.ShapeDtypeStruct((b, N_KV_GROUPS, s, aug_w), MXU_DTYPE),
                   jax.ShapeDtypeStruct((b, N_KV_GROUPS, HEAD_DIM, s), MXU_DTYPE),
                   jax.ShapeDtypeStruct((b, N_KV_GROUPS, s, 2 * HEAD_DIM), MXU_DTYPE),
                   jax.ShapeDtypeStruct((b, N_KV_GROUPS, HEAD_DIM, s), MXU_DTYPE),
                   jax.ShapeDtypeStruct((b, SUBLANE, LANE), jnp.float32)],
        compiler_params=_cparams(2),
        name="kprep",
    )(proj3, proj3, proj3, proj3, cos_k, sin_k, rot)


def _heads_to_lanes(xt):
    return jnp.concatenate([xt[h * HEAD_DIM:(h + 1) * HEAD_DIM, :] for h in range(HEADS_PER_GROUP)],
                           axis=1)


def _vis_rows(tq):
    levels = tq // CMP_STRIDE
    dt = np.arange(tq)
    v = (dt - (CMP_BLOCK - 1)) // CMP_STRIDE - (-(CMP_BLOCK - 1)) // CMP_STRIDE
    rows = np.zeros((HEAD_DIM, tq), np.float32)
    for c in range(1, levels + 1):
        rows[c] = np.where(v < c, -MASK_BIAS, 0.0)
    rows[levels + 1] = -MASK_BIAS
    return jnp.asarray(np.tile(rows, (1, HEADS_PER_GROUP)), MXU_DTYPE)


def _cmpsel_kernel(q_ref, kc_ref, vct_ref, vis_ref, ocmp_ref, bias_ref):
    tq = q_ref.shape[1]
    n_blk = bias_ref.shape[2]
    m_rows = HEADS_PER_GROUP * tq
    t0 = pl.program_id(1) * tq
    levels = tq // CMP_STRIDE

    def body(nr):
        n_tok = CMP_PER_SLC * nr
        j_idx = lax.broadcasted_iota(jnp.int32, (nr, LANE), 0)
        n_idx = jnp.concatenate([j_idx * CMP_PER_SLC + r for r in range(CMP_PER_SLC)], axis=0)
        n_first = (t0 - (CMP_BLOCK - 1)) >> _log2(CMP_STRIDE)
        u = jnp.clip(n_idx - n_first, 0, levels + 1)
        vis_cols = jnp.where(lax.broadcasted_iota(jnp.int32, (n_tok, LANE), 1) - HEAD_DIM == u,
                             1.0, 0.0).astype(MXU_DTYPE)
        vis_rows = vis_ref[...]
        t_row = t0 + (lax.broadcasted_iota(jnp.int32, (1, m_rows), 1) & (tq - 1))
        jrow = lax.broadcasted_iota(jnp.int32, (nr, tq), 0)
        jt = (t0 + lax.broadcasted_iota(jnp.int32, (nr, tq), 1)) >> _log2(SLC_BLOCK)
        causal_blk = jrow <= jt
        forced = (jrow == 0) | (jrow == jt) | (jrow == jt - 1)
        jf = jrow.astype(jnp.float32)
        n_extract = min(SLC_TOPK, n_blk) - N_FORCED
        jt_row = (t0 + lax.broadcasted_iota(jnp.int32, (1, tq), 1)) >> _log2(SLC_BLOCK)
        want = jnp.minimum(jt_row + 1, SLC_TOPK).astype(jnp.float32)

        def store_bias(pb, marks):
            bias_ref[pb, 0, 0:nr, :] = jnp.where(marks < CHOSEN_LIMIT, 0.0, -MASK_BIAS).astype(bias_ref.dtype)

        pending = []
        for pb in range(q_ref.shape[0]):
            kc = jnp.concatenate([kc_ref[pb, 0, r * n_blk:r * n_blk + nr, :] for r in range(CMP_PER_SLC)],
                                 axis=0)
            vct = jnp.concatenate([vct_ref[pb, 0, :, r * n_blk:r * n_blk + nr] for r in range(CMP_PER_SLC)],
                                  axis=1)
            qt = _heads_to_lanes(q_ref[pb].astype(jnp.float32).T).astype(MXU_DTYPE)
            s = jnp.dot(kc + vis_cols, jnp.concatenate([qt, vis_rows], axis=0),
                        preferred_element_type=jnp.float32)
            mx = jnp.max(s, axis=0, keepdims=True)
            p = jnp.exp2(s - mx)
            pv = jnp.dot(vct, p.astype(MXU_DTYPE), preferred_element_type=jnp.float32)
            den = pv[HEAD_DIM:HEAD_DIM + 1, :]
            inv = jnp.where(t_row >= CMP_BLOCK - 1, 1.0 / den, 0.0)
            ocmp_ref[pb, 0, 0] = inv * pv[0:HEAD_DIM, :]

            imp = None
            for h in range(HEADS_PER_GROUP):
                ph = p[:, h * tq:(h + 1) * tq]
                last = ph[3 * nr:4 * nr, :]
                prev = jnp.where(jrow == 0, 0.0, pltpu.roll(last, 1, 0))
                taps = ph[0:nr, :] + ph[nr:2 * nr, :] + ph[2 * nr:3 * nr, :] + last + prev
                imp_h = taps * inv[:, h * tq:(h + 1) * tq]
                imp = imp_h if imp is None else imp + imp_h

            val = jnp.where(forced, CHOSEN, jnp.where(causal_blk, imp, -IMP_BIG))
            marks = val
            for _ in range(n_extract):
                marks = jnp.where(marks == jnp.max(marks, axis=0, keepdims=True), CHOSEN, marks)
            store_bias(pb, marks)
            if nr < n_blk:
                bias_ref[pb, 0, nr:, :] = jnp.full((n_blk - nr, tq), -MASK_BIAS, bias_ref.dtype)
            count = jnp.sum(jnp.where((marks < CHOSEN_LIMIT) & causal_blk, 1.0, 0.0), axis=0, keepdims=True)
            pending.append((pb, val, jnp.max(jnp.abs(count - want)) > 0.5))

        for pb, val, tied in pending:
            @pl.when(tied)
            def _(val=val, pb=pb):
                v = val
                for _ in range(n_extract):
                    top = jnp.max(v, axis=0, keepdims=True)
                    first = jnp.min(jnp.where(v == top, jf, float(n_blk)), axis=0, keepdims=True)
                    v = jnp.where(jf == first, CHOSEN, v)
                store_bias(pb, v)

    sizes = [n_blk * f // 8 for f in range(1, 9) if (n_blk * f // 8) % 16 == 0]
    lo = 0
    for nr in sizes:
        hi = nr * SLC_BLOCK if nr < n_blk else None
        end = t0 + tq
        cond = (end > lo) if hi is None else (end > lo) & (end <= hi)
        pl.when(cond)(functools.partial(body, nr))
        lo = hi


def _cmpsel(proj3, kc_r, vc_rt):
    b, s, _ = proj3.shape
    tq = min(Q_TILE, s)
    nq = s // tq
    n_cmp_pad = kc_r.shape[2]
    n_blk = s // SLC_BLOCK
    qb = COL_Q // KV_WIDTH
    m_rows = HEADS_PER_GROUP * tq
    return pl.pallas_call(
        _cmpsel_kernel,
        grid=(N_KV_GROUPS, nq),
        in_specs=[pl.BlockSpec((b, tq, KV_WIDTH), lambda g, q: (0, q, qb + g)),
                  pl.BlockSpec((b, 1, n_cmp_pad, LANE), lambda g, q: (0, g, 0, 0)),
                  pl.BlockSpec((b, 1, HEAD_DIM + ONES_ROWS, n_cmp_pad), lambda g, q: (0, g, 0, 0)),
                  _const_spec((HEAD_DIM, m_rows))],
        out_specs=[pl.BlockSpec((b, 1, 1, HEAD_DIM, m_rows), lambda g, q: (0, g, q, 0, 0)),
                   pl.BlockSpec((b, 1, n_blk, tq), lambda g, q: (0, g, 0, q))],
        out_shape=[jax.ShapeDtypeStruct((b, N_KV_GROUPS, nq, HEAD_DIM, m_rows), jnp.float32),
                   jax.ShapeDtypeStruct((b, N_KV_GROUPS, n_blk, s), MXU_DTYPE)],
        compiler_params=_cparams(2),
        name="cmpsel",
    )(proj3, kc_r, vc_rt, _vis_rows(tq))


def _flash_kernel(q_ref, cos_ref, sin_ref, bias_ref, kaug_ref, vst_ref, kwin_ref, vwt_ref, knorm_ref,
                  wmask_ref, gate_ref, ocmp_ref, o_ref, m_ref, l_ref, acc_ref, lw_ref, accw_ref):
    nb = q_ref.shape[0]
    tq = q_ref.shape[1]
    m_rows = HEADS_PER_GROUP * tq
    grp = pl.program_id(0)
    qi = pl.program_id(1)
    t0 = qi * tq
    kd = pl.multiple_of(t0, tq)
    probs = range(nb)
    zpad = jnp.zeros((HEAD_DIM, m_rows), MXU_DTYPE)
    pad_row = lax.broadcasted_iota(jnp.int32, (HEAD_DIM, m_rows), 0)
    lane = lax.broadcasted_iota(jnp.int32, (1, LANE), 1)

    def stab_rows(neg_m):
        return jnp.where(pad_row == 0, neg_m, 0.0).astype(MXU_DTYPE)

    cos = cos_ref[...]
    sin = sin_ref[...]
    qr, bias4, qs_slc, qs_win = [], [], [], []
    for pb in probs:
        qt = q_ref[pb].astype(jnp.float32).T
        parts = []
        for h in range(HEADS_PER_GROUP):
            qh = qt[h * HEAD_DIM:(h + 1) * HEAD_DIM, :]
            x1 = qh[0:ROPE_HALF, :]
            x2 = qh[ROPE_HALF:ROPE_DIM, :]
            parts.append(jnp.concatenate([x1 * cos - x2 * sin, x2 * cos + x1 * sin, qh[ROPE_DIM:, :]],
                                         axis=0))
        qf = jnp.concatenate(parts, axis=1)
        q_sq = jnp.sum(qf * qf, axis=0, keepdims=True)
        k_sq = knorm_ref[pb, 0:1, :]
        ks_sq = jnp.max(jnp.where(lane == grp, k_sq, 0.0), axis=1, keepdims=True)
        kw_sq = jnp.max(jnp.where(lane == grp + N_KV_GROUPS, k_sq, 0.0), axis=1, keepdims=True)
        qr.append(qf.astype(MXU_DTYPE))
        bias4.append(jnp.concatenate([bias_ref[pb, 0]] * HEADS_PER_GROUP, axis=1))
        qs_slc.append(jnp.concatenate([bias4[pb], qr[pb], stab_rows(-STAB_MARGIN * jnp.sqrt(q_sq * ks_sq))],
                                      axis=0))
        qs_win.append(jnp.concatenate([qr[pb], stab_rows(-STAB_MARGIN * jnp.sqrt(q_sq * kw_sq))], axis=0))

    n_back = WINDOW // tq

    def win_tile(back):
        if back == 0:
            return kd, tq, wmask_ref[0]
        inside = wmask_ref[back] if back * tq + tq - 1 >= WINDOW else 0.0
        return (pl.multiple_of(jnp.maximum(t0 - back * tq, 0), tq), tq,
                jnp.where(qi >= back, inside, NEG_BIG))

    def fast_tiles(k_ref, v_ref, qs, tiles, l_out, acc_out, first=False):
        for pb in probs:
            l_add = acc_add = None
            for k0, tk, mask in tiles:
                s = jnp.dot(k_ref[pb, 0, pl.ds(k0, tk), :], qs[pb], preferred_element_type=jnp.float32)
                if mask is not None:
                    s = s + mask
                p = jnp.exp2(s)
                l_t = jnp.sum(p, axis=0, keepdims=True)
                acc_t = jnp.dot(v_ref[pb, 0, :, pl.ds(k0, tk)], p.astype(MXU_DTYPE),
                                preferred_element_type=jnp.float32)
                l_add = l_t if l_add is None else l_add + l_t
                acc_add = acc_t if acc_add is None else acc_add + acc_t
            if first:
                l_out[pb] = l_add
                acc_out[pb] = acc_add
            else:
                l_out[pb] += l_add
                acc_out[pb] += acc_add

    fast_tiles(kaug_ref, vst_ref, qs_slc, [(kd, tq, wmask_ref[0])], l_ref, acc_ref, first=True)
    fast_tiles(kwin_ref, vwt_ref, qs_win, [win_tile(back) for back in range(n_back + 1)],
               lw_ref, accw_ref, first=True)

    sub = SLC_SUBTILE
    n_sub = SLC_SPAN // sub

    def span_body(i, carry):
        base = i * SLC_SPAN
        fast_tiles(kaug_ref, vst_ref, qs_slc,
                   [(pl.multiple_of(base + j * sub, sub), sub, None) for j in range(n_sub)],
                   l_ref, acc_ref)
        return carry

    n_span = t0 // SLC_SPAN
    lax.fori_loop(0, n_span, span_body, 0)
    done = n_span * SLC_SPAN
    size = SLC_SPAN // 2
    while size >= tq:
        take = ((t0 // size) % 2) == 1

        @pl.when(take)
        def _(done=done, size=size):
            piece = min(size, sub)
            fast_tiles(kaug_ref, vst_ref, qs_slc,
                       [(pl.multiple_of(done + j * piece, piece), piece, None) for j in range(size // piece)],
                       l_ref, acc_ref)

        done = done + jnp.where(take, size, 0)
        size //= 2

    def exact_start(pb, s, vt, l_out, acc_out):
        mx = jnp.max(s, axis=0, keepdims=True)
        p = jnp.exp2(s - mx)
        m_ref[pb] = mx
        l_out[pb] = jnp.sum(p, axis=0, keepdims=True)
        acc_out[pb] = jnp.dot(vt, p.astype(MXU_DTYPE), preferred_element_type=jnp.float32)

    def exact_update(pb, s, vt, l_out, acc_out):
        m_old = m_ref[pb]
        mx = jnp.maximum(m_old, jnp.max(s, axis=0, keepdims=True))
        alpha = jnp.exp2(m_old - mx)
        p = jnp.exp2(s - mx)
        m_ref[pb] = mx
        l_out[pb] = alpha * l_out[pb] + jnp.sum(p, axis=0, keepdims=True)
        acc_out[pb] = alpha * acc_out[pb] + jnp.dot(vt, p.astype(MXU_DTYPE),
                                                    preferred_element_type=jnp.float32)

    r_id = lax.broadcasted_iota(jnp.int32, (GATE_ROWS, LANE), 0)
    c_id = lax.broadcasted_iota(jnp.int32, (GATE_ROWS, LANE), 1)
    pick = jnp.where((c_id == grp * (HEADS_PER_GROUP * 3) + r_id) & (r_id < HEADS_PER_GROUP * 3),
                     1.0, 0.0).astype(gate_ref.dtype)

    def finish():
        total = None
        for pb in probs:
            o_slc = acc_ref[pb] / l_ref[pb]
            o_win = accw_ref[pb] / lw_ref[pb]
            graw = lax.dot_general(pick, gate_ref[pb], (((1,), (1,)), ((), ())),
                                   preferred_element_type=jnp.float32)
            gates = _sigmoid(graw)
            o_cmp = ocmp_ref[pb, 0, 0]
            mixed = []
            for h in range(HEADS_PER_GROUP):
                sl = slice(h * tq, (h + 1) * tq)
                mixed.append(gates[3 * h:3 * h + 1, :] * o_cmp[:, sl]
                             + gates[3 * h + 1:3 * h + 2, :] * o_slc[:, sl]
                             + gates[3 * h + 2:3 * h + 3, :] * o_win[:, sl])
            out = jnp.concatenate(mixed, axis=0)
            o_ref[pb] = out.T.astype(o_ref.dtype)
            part = jnp.sum(jnp.abs(out), axis=0, keepdims=True)
            total = part if total is None else total + part
        return total

    check = finish()
    l_low = None
    for pb in probs:
        l_both = l_ref[pb] + lw_ref[pb]
        for h in range(HEADS_PER_GROUP):
            check = check + l_both[:, h * tq:(h + 1) * tq]
        lo = jnp.minimum(l_ref[pb], lw_ref[pb])
        l_low = lo if l_low is None else jnp.minimum(l_low, lo)

    @pl.when(_lost_range(check, l_low))
    def _():
        tcol = t0 + (lax.broadcasted_iota(jnp.int32, (tq, m_rows), 1) & (tq - 1))
        koff = lax.broadcasted_iota(jnp.int32, (tq, m_rows), 0)
        causal = t0 + koff <= tcol
        for pb in probs:
            q0 = jnp.concatenate([bias4[pb], qr[pb], zpad], axis=0)
            s = jnp.dot(kaug_ref[pb, 0, pl.ds(kd, tq), :], q0, preferred_element_type=jnp.float32)
            exact_start(pb, jnp.where(causal, s, NEG_BIG), vst_ref[pb, 0, :, pl.ds(kd, tq)], l_ref, acc_ref)

            def body(kt, carry, pb=pb, q0=q0):
                k0 = pl.multiple_of(kt * tq, tq)
                sk = jnp.dot(kaug_ref[pb, 0, pl.ds(k0, tq), :], q0, preferred_element_type=jnp.float32)
                exact_update(pb, sk, vst_ref[pb, 0, :, pl.ds(k0, tq)], l_ref, acc_ref)
                return carry

            lax.fori_loop(0, qi, body, 0)
            qw0 = jnp.concatenate([qr[pb], zpad], axis=0)
            for back in range(n_back + 1):
                kbase = t0 - back * tq
                k0 = pl.multiple_of(jnp.maximum(kbase, 0), tq)
                sw = jnp.dot(kwin_ref[pb, 0, pl.ds(k0, tq), :], qw0, preferred_element_type=jnp.float32)
                diff = tcol - (kbase + koff)
                sw = jnp.where((diff >= 0) & (diff < WINDOW) & (kbase + koff >= 0), sw, NEG_BIG)
                if back == 0:
                    exact_start(pb, sw, vwt_ref[pb, 0, :, pl.ds(k0, tq)], lw_ref, accw_ref)
                else:
                    exact_update(pb, sw, vwt_ref[pb, 0, :, pl.ds(k0, tq)], lw_ref, accw_ref)
        finish()


def _window_masks(tq):
    k = np.arange(tq)[:, None]
    t = np.arange(tq)[None, :]
    tiles = []
    for back in range(WINDOW // tq + 1):
        diff = t + back * tq - k
        tiles.append(np.tile(np.where((diff >= 0) & (diff < WINDOW), 0.0, NEG_BIG), (1, HEADS_PER_GROUP)))
    return jnp.asarray(np.stack(tiles), jnp.float32)


def _flash(proj3, cos_q, sin_q, bias, kaug, vst, kwin, vwt, knorm, ocmp):
    b, s, _ = proj3.shape
    tq = min(Q_TILE, s)
    nq = s // tq
    n_blk = s // SLC_BLOCK
    aug_w = kaug.shape[3]
    qb = COL_Q // KV_WIDTH
    m_rows = HEADS_PER_GROUP * tq

    wmask = _window_masks(tq)

    def resident(shape):
        return pl.BlockSpec((b, 1) + shape, lambda g, q: (0, g, 0, 0))

    return pl.pallas_call(
        _flash_kernel,
        grid=(N_KV_GROUPS, nq),
        in_specs=[pl.BlockSpec((b, tq, KV_WIDTH), lambda g, q: (0, q, qb + g)),
                  pl.BlockSpec((ROPE_HALF, tq), lambda g, q: (0, q)),
                  pl.BlockSpec((ROPE_HALF, tq), lambda g, q: (0, q)),
                  pl.BlockSpec((b, 1, n_blk, tq), lambda g, q: (0, g, 0, q)),
                  resident((s, aug_w)),
                  resident((HEAD_DIM, s)),
                  resident((s, 2 * HEAD_DIM)),
                  resident((HEAD_DIM, s)),
                  _const_spec((b, SUBLANE, LANE)),
                  _const_spec(wmask.shape),
                  pl.BlockSpec((b, tq, LANE), lambda g, q: (0, q, COL_GNSA // LANE)),
                  pl.BlockSpec((b, 1, 1, HEAD_DIM, m_rows), lambda g, q: (0, g, q, 0, 0))],
        out_specs=pl.BlockSpec((b, tq, KV_WIDTH), lambda g, q: (0, q, g)),
        out_shape=jax.ShapeDtypeStruct((b, s, ATTN_Q_WIDTH), ACT_DTYPE),
        scratch_shapes=[pltpu.VMEM((b, 1, m_rows), jnp.float32),
                        pltpu.VMEM((b, 1, m_rows), jnp.float32),
                        pltpu.VMEM((b, HEAD_DIM, m_rows), jnp.float32),
                        pltpu.VMEM((b, 1, m_rows), jnp.float32),
                        pltpu.VMEM((b, HEAD_DIM, m_rows), jnp.float32)],
        compiler_params=_cparams(2),
        name="flash",
    )(proj3, cos_q, sin_q, bias, kaug, vst, kwin, vwt, knorm, wmask, proj3, ocmp)


def _tail_kernel(u_ref, halo_ref, gp_ref, ga_ref, attn_ref, x_ref, pw_ref, ps_ref, wpp_ref, wpa_ref, wo_ref,
                 n2_ref, wg_ref, wu_ref, wd_ref, nf_ref, o_ref, *, seq_len):
    tm = u_ref.shape[0]
    t0 = (pl.program_id(0) * tm) & (seq_len - 1)

    u = u_ref[...].astype(jnp.float32)
    halo = jnp.where(t0 > 0, halo_ref[...].astype(jnp.float32), 0.0)
    ext = jnp.concatenate([halo, u], axis=0)
    t = t0 + lax.broadcasted_iota(jnp.int32, (tm, POOL_GROUP), 0)
    pooled = []
    for gi, w in enumerate(POOL_WINDOWS):
        sl = slice(gi * POOL_GROUP, (gi + 1) * POOL_GROUP)
        acc = ext[:, sl]
        span = 1
        while span < w:
            acc = acc + pltpu.roll(acc, span, 0)
            span *= 2
        cnt = jnp.minimum(t + 1, w).astype(jnp.float32)
        mean = acc[POOL_HALO:, :] / cnt
        mixed = jnp.dot((mean - u[:, sl]).astype(MXU_DTYPE), pw_ref[gi],
                        preferred_element_type=jnp.float32)
        pooled.append(mixed * ps_ref[:, sl])
    pool = jnp.concatenate(pooled, axis=1).astype(MXU_DTYPE)

    pp = jnp.dot(pool, wpp_ref[...], preferred_element_type=jnp.float32)
    pa = jnp.dot(attn_ref[...], wpa_ref[...], preferred_element_type=jnp.float32)
    merged = (_sigmoid(gp_ref[...].astype(jnp.float32)) * pp
              + _sigmoid(ga_ref[...].astype(jnp.float32)) * pa)
    x1 = x_ref[...] + jnp.dot(merged.astype(MXU_DTYPE), wo_ref[...], preferred_element_type=jnp.float32)

    h = _rms(x1, n2_ref[...]).astype(MXU_DTYPE)
    d_ff = wg_ref.shape[1]
    out = x1
    for c0 in range(0, d_ff, FF_CHUNK):
        cw = min(FF_CHUNK, d_ff - c0)
        gate = jnp.dot(h, wg_ref[:, c0:c0 + cw], preferred_element_type=jnp.float32)
        up = jnp.dot(h, wu_ref[:, c0:c0 + cw], preferred_element_type=jnp.float32)
        act = (gate * _sigmoid(gate) * up).astype(MXU_DTYPE)
        out = out + jnp.dot(act, wd_ref[c0:c0 + cw, :], preferred_element_type=jnp.float32)
    o_ref[...] = _rms(out, nf_ref[...])


def _tail(proj2, attn2, x2d, pool_w, pool_scale, wpp, wpa, wo, norm2_w, wg, wu, wd, norm_f_w, seq_len):
    m, d = x2d.shape
    tm = min(ROW_TILE, seq_len)
    halo_per_tile = tm // POOL_HALO
    weights = (pool_w, pool_scale, wpp, wpa, wo, norm2_w, wg, wu, wd, norm_f_w)
    return pl.pallas_call(
        functools.partial(_tail_kernel, seq_len=seq_len),
        grid=(m // tm,),
        in_specs=[pl.BlockSpec((tm, POOL_WIDTH), lambda i: (i, COL_POOL // POOL_WIDTH)),
                  pl.BlockSpec((POOL_HALO, POOL_WIDTH),
                               lambda i: (jnp.maximum(i * halo_per_tile - 1, 0), COL_POOL // POOL_WIDTH)),
                  pl.BlockSpec((tm, d), lambda i: (i, COL_GPOOL // D_MODEL)),
                  pl.BlockSpec((tm, d), lambda i: (i, COL_GATTN // D_MODEL)),
                  pl.BlockSpec((tm, ATTN_Q_WIDTH), lambda i: (i, 0)),
                  pl.BlockSpec((tm, d), lambda i: (i, 0))] + [_const_spec(w.shape) for w in weights],
        out_specs=pl.BlockSpec((tm, d), lambda i: (i, 0)),
        out_shape=jax.ShapeDtypeStruct((m, d), jnp.float32),
        compiler_params=_cparams(1),
        name="tail",
    )(proj2, proj2, proj2, proj2, attn2, x2d, *weights)


def _rope_tables(seq_len):
    inv_freq = 1.0 / (ROPE_THETA ** (jnp.arange(0, ROPE_DIM, 2, dtype=jnp.float32) / ROPE_DIM))
    ang = inv_freq[:, None] * jnp.arange(seq_len).astype(jnp.float32)[None, :]
    cos_t, sin_t = jnp.cos(ang), jnp.sin(ang)
    rest = HEAD_DIM - ROPE_DIM
    cos_h = jnp.concatenate([cos_t, cos_t, jnp.ones((rest, seq_len), jnp.float32)], axis=0).T
    sin_h = jnp.concatenate([sin_t, sin_t, jnp.zeros((rest, seq_len), jnp.float32)], axis=0).T
    return cos_t, sin_t, cos_h, sin_h


def _rotate_half_matrix():
    rot = np.zeros((KV_WIDTH, KV_WIDTH), np.float32)
    for l in range(KV_WIDTH):
        d = l % HEAD_DIM
        if d < ROPE_HALF:
            rot[l + ROPE_HALF, l] = -1.0
        elif d < ROPE_DIM:
            rot[l - ROPE_HALF, l] = 1.0
    return jnp.asarray(rot, MXU_DTYPE)


def _layer(x, norm1_w, w_in, pool_w, pool_scale, cmp_pe_k, cmp_w1_k, cmp_b1_k, cmp_w2_k,
           cmp_pe_v, cmp_w1_v, cmp_b1_v, cmp_w2_v, w_proj_pool, w_proj_attn, w_out,
           norm2_w, w_ffn_gate, w_ffn_up, w_ffn_down, norm_f_w):
    b, s, d = x.shape
    m = b * s
    n_blk = s // SLC_BLOCK
    assert s % Q_TILE == 0 and s % ROW_TILE == 0 and (s & (s - 1)) == 0, s
    cd = MXU_DTYPE

    g0 = COL_GPOOL
    w_parts = (w_in[:, :g0].astype(cd), w_in[:, g0 + GATE_NSA:].astype(cd),
               jnp.pad(w_in[:, g0:g0 + GATE_NSA], ((0, 0), (0, LANE - GATE_NSA))).astype(cd))
    x2d = x.reshape(m, d)
    proj2, kvc = _inproj(x2d, norm1_w.reshape(1, d), w_parts)
    proj3 = proj2.reshape(b, s, PROJ_WIDTH)

    def pe8(pe):
        return jnp.broadcast_to(pe.reshape(1, -1), (8, pe.size)).astype(cd)

    kvc_rows = kvc.reshape(2 * N_KV_GROUPS, b, n_blk, SLC_BLOCK * HEAD_DIM)
    kc_r, vc_rt = _compress(kvc_rows,
                            pe8(cmp_pe_k), cmp_w1_k.astype(cd), cmp_b1_k.reshape(1, -1), cmp_w2_k.astype(cd),
                            pe8(cmp_pe_v), cmp_w1_v.astype(cd), cmp_b1_v.reshape(1, -1), cmp_w2_v.astype(cd))

    cos_q, sin_q, cos_k, sin_k = _rope_tables(s)
    kaug, vst, kwin, vwt, knorm = _kprep(proj3, cos_k, sin_k, _rotate_half_matrix())
    ocmp, bias = _cmpsel(proj3, kc_r, vc_rt)
    attn = _flash(proj3, cos_q, sin_q, bias, kaug, vst, kwin, vwt, knorm, ocmp)
    return _tail(proj2, attn.reshape(m, ATTN_Q_WIDTH), x2d, pool_w.astype(cd),
                 pool_scale.reshape(1, -1), w_proj_pool.astype(cd), w_proj_attn.astype(cd),
                 w_out.astype(cd), norm2_w.reshape(1, d), w_ffn_gate.astype(cd), w_ffn_up.astype(cd),
                 w_ffn_down.astype(cd), norm_f_w.reshape(1, d), s)


def kernel(x, norm1_w, w_in, pool_w, pool_scale, cmp_pe_k, cmp_w1_k, cmp_b1_k, cmp_w2_k, cmp_pe_v,
           cmp_w1_v, cmp_b1_v, cmp_w2_v, w_proj_pool, w_proj_attn, w_out, norm2_w, w_ffn_gate,
           w_ffn_up, w_ffn_down, norm_f_w):
    b, s, d = x.shape
    depth = w_in.shape[0]
    assert depth == 1, "the final norm is fused into the last layer's FFN kernel"
    out = _layer(x, norm1_w[0], w_in[0], pool_w[0], pool_scale[0], cmp_pe_k[0], cmp_w1_k[0], cmp_b1_k[0],
                 cmp_w2_k[0], cmp_pe_v[0], cmp_w1_v[0], cmp_b1_v[0], cmp_w2_v[0], w_proj_pool[0],
                 w_proj_attn[0], w_out[0], norm2_w[0], w_ffn_gate[0], w_ffn_up[0], w_ffn_down[0], norm_f_w)
    return out.reshape(b, s, d)
```
